```python
import math
import jax, jax.numpy as jnp
from jax import lax
import numpy as np


D_MODEL = 1024
BATCH = 8
SEQ = 8192
DEPTH = 4

CTX_LEN = 256
GRID_W = 64
D_FF = 2816
MLA_HEADS = 8
MLA_NOPE = 64
MLA_ROPE = 32
MLA_V = 64
MLA_Q_RANK = 384
MLA_KV_RANK = 256
SSM_WIDTH = 512
SSM_GROUP = 16
SSM_GROUPS = SSM_WIDTH // SSM_GROUP
SSM_STATE = 64
DT_MIN = 1e-3
DT_MAX = 1e-1
GQA_HEADS = 8
GQA_KV_HEADS = 2
GQA_HEAD_DIM = 64
WINDOW = 128
BLOCK = 128
N_BRANCH = 3
N_MOD = 9
ROPE_BASE = 10000.0
EPS = 1e-6
NEG_INF = -1e30
IN_SPLITS = (MLA_Q_RANK, MLA_KV_RANK, MLA_ROPE, SSM_WIDTH, GQA_HEADS * GQA_HEAD_DIM, GQA_KV_HEADS * GQA_HEAD_DIM, GQA_KV_HEADS * GQA_HEAD_DIM, N_BRANCH * D_MODEL)
IN_DIM = sum(IN_SPLITS)

kernel_name = 'hybrid_mla_s5_swa_dit_block'


def _offsets():
    return np.cumsum(IN_SPLITS)[:-1].tolist()


def bcast(t):
    return t[..., None, :]


def rmsnorm(x, g):
    x32 = x.astype(jnp.float32)
    y = x32 * lax.rsqrt(jnp.mean(x32 * x32, axis=-1, keepdims=True) + EPS)
    return (y * g.astype(jnp.float32)).astype(x.dtype)


def modulate(x, shift, scale):
    return x * (1 + bcast(scale)) + bcast(shift)


def swiglu(x, w13, w2):
    a, b = jnp.split(x @ w13, 2, axis=-1)
    return (jax.nn.silu(a) * b) @ w2


def rope_1d(x, pos):
    n = x.shape[-1]
    inv = ROPE_BASE ** (-jnp.arange(0, n, 2, dtype=jnp.float32) / n)
    ang = pos.astype(jnp.float32)[:, None, None] * inv
    cos, sin = jnp.cos(ang), jnp.sin(ang)
    x32 = x.astype(jnp.float32)
    x1, x2 = x32[..., : n // 2], x32[..., n // 2:]
    return jnp.concatenate([x1 * cos - x2 * sin, x1 * sin + x2 * cos], axis=-1).astype(x.dtype)


def axial_rope(x, row, col):
    half = x.shape[-1] // 2
    return jnp.concatenate([rope_1d(x[..., :half], row), rope_1d(x[..., half:], col)], axis=-1)


def sink_softmax(score_list, sink_logit):
    m = sink_logit
    for s in score_list:
        m = jnp.maximum(m, s.max(axis=-1, keepdims=True))
    e = [jnp.exp(s - m) for s in score_list]
    denom = jnp.exp(sink_logit - m)
    for t in e:
        denom = denom + t.sum(axis=-1, keepdims=True)
    return [t / denom for t in e]


def dense_attention_blocks(q, k, v):
    B, T, H, dk = q.shape
    dv = v.shape[-1]
    nb = T // BLOCK
    scale = dk ** -0.5
    qb = jnp.moveaxis(q.reshape(B, nb, BLOCK, H, dk), 1, 0)

    def one_block(qi):
        s = jnp.einsum('bqhd,bkhd->bhqk', qi, k, preferred_element_type=jnp.float32) * scale
        p = jax.nn.softmax(s, axis=-1).astype(v.dtype)
        return jnp.einsum('bhqk,bkhd->bqhd', p, v)

    o = lax.map(one_block, qb)
    return jnp.moveaxis(o, 0, 1).reshape(B, T, H * dv)


def mla_queries(cq, q_norm, w_uq, row, col):
    B, T, _ = cq.shape
    q = (rmsnorm(cq, q_norm) @ w_uq).reshape(B, T, MLA_HEADS, MLA_NOPE + MLA_ROPE)
    q_nope, q_rope = q[..., :MLA_NOPE], q[..., MLA_NOPE:]
    if row is not None:
        q_rope = axial_rope(q_rope, row, col)
    return jnp.concatenate([q_nope, q_rope], axis=-1)


def mla_keys_values(ckv, kr, kv_norm, w_ukv, row, col):
    B, T, _ = ckv.shape
    kv = (rmsnorm(ckv, kv_norm) @ w_ukv).reshape(B, T, MLA_HEADS, MLA_NOPE + MLA_V)
    k_nope, v = kv[..., :MLA_NOPE], kv[..., MLA_NOPE:]
    kr = kr[:, :, None, :]
    if row is not None:
        kr = axial_rope(kr, row, col)
    k = jnp.concatenate([k_nope, jnp.broadcast_to(kr, (B, T, MLA_HEADS, MLA_ROPE))], axis=-1)
    return k, v


def window_gqa(q, k, v, kc, vc, sink):
    B, T, H, d = q.shape
    G = H // GQA_KV_HEADS
    nb = T // BLOCK
    scale = d ** -0.5
    qb = q.reshape(B, nb, BLOCK, GQA_KV_HEADS, G, d)

    def band(t):
        tb = t.reshape(B, nb, BLOCK, GQA_KV_HEADS, d)
        tp = jnp.pad(tb, ((0, 0), (1, 1), (0, 0), (0, 0), (0, 0)))
        return jnp.concatenate([tp[:, :-2], tp[:, 1:-1], tp[:, 2:]], axis=2)

    kb, vb = band(k), band(v)
    s_band = jnp.einsum('bnqhgd,bnkhd->bhgnqk', qb, kb, preferred_element_type=jnp.float32) * scale
    blk = jnp.arange(nb)[:, None, None]
    qpos = blk * BLOCK + jnp.arange(BLOCK)[None, :, None]
    kpos = (blk - 1) * BLOCK + jnp.arange(3 * BLOCK)[None, None, :]
    valid = (jnp.abs(qpos - kpos) <= WINDOW) & (kpos >= 0) & (kpos < T)
    s_band = jnp.where(valid, s_band, NEG_INF)
    s_ctx = jnp.einsum('bnqhgd,bchd->bhgnqc', qb, kc, preferred_element_type=jnp.float32) * scale
    sk = sink.astype(jnp.float32).reshape(GQA_KV_HEADS, G)[None, :, :, None, None, None]
    p_band, p_ctx = sink_softmax([s_band, s_ctx], sk)
    o = (jnp.einsum('bhgnqk,bnkhd->bnqhgd', p_band.astype(v.dtype), vb)
         + jnp.einsum('bhgnqc,bchd->bnqhgd', p_ctx.astype(vc.dtype), vc))
    return o.reshape(B, T, H * d)


def context_gqa(qc, kc, vc, sink):
    B, C, H, d = qc.shape
    G = H // GQA_KV_HEADS
    qg = qc.reshape(B, C, GQA_KV_HEADS, G, d)
    s = jnp.einsum('bqhgd,bkhd->bhgqk', qg, kc, preferred_element_type=jnp.float32) * d ** -0.5
    sk = sink.astype(jnp.float32).reshape(GQA_KV_HEADS, G)[None, :, :, None, None]
    (p,) = sink_softmax([s], sk)
    o = jnp.einsum('bhgqk,bkhd->bqhgd', p.astype(vc.dtype), vc)
    return o.reshape(B, C, H * d)


def cmul(ar, ai, br, bi):
    return ar * br - ai * bi, ar * bi + ai * br


def ssm_discretize(lam_re, lam_im, log_dt, b_re, b_im):
    lr, li = lam_re.astype(jnp.float32), lam_im.astype(jnp.float32)
    dt = jnp.exp(log_dt.astype(jnp.float32))[:, None]
    mag = jnp.exp(lr * dt)
    a_re, a_im = mag * jnp.cos(li * dt), mag * jnp.sin(li * dt)
    den = lr * lr + li * li
    w_re = ((a_re - 1) * lr + a_im * li) / den
    w_im = (a_im * lr - (a_re - 1) * li) / den
    bb_re, bb_im = cmul(w_re[..., None], w_im[..., None], b_re.astype(jnp.float32), b_im.astype(jnp.float32))
    return a_re, a_im, bb_re, bb_im


def diag_scan(a_re, a_im, b_re, b_im, h0, reverse):
    if reverse:
        b_re, b_im = jnp.flip(b_re, axis=1), jnp.flip(b_im, axis=1)
    if h0 is not None:
        i_re, i_im = cmul(a_re, a_im, h0[0], h0[1])
        b_re = b_re.at[:, 0].add(i_re)
        b_im = b_im.at[:, 0].add(i_im)
    T = b_re.shape[1]
    ar = jnp.broadcast_to(a_re, (1, T) + a_re.shape)
    ai = jnp.broadcast_to(a_im, (1, T) + a_im.shape)

    def combine(e1, e2):
        a1r, a1i, b1r, b1i = e1
        a2r, a2i, b2r, b2i = e2
        nar, nai = cmul(a2r, a2i, a1r, a1i)
        nbr, nbi = cmul(a2r, a2i, b1r, b1i)
        return nar, nai, nbr + b2r, nbi + b2i

    _, _, s_re, s_im = lax.associative_scan(combine, (ar, ai, b_re, b_im), axis=1)
    if reverse:
        s_re, s_im = jnp.flip(s_re, axis=1), jnp.flip(s_im, axis=1)
    return s_re, s_im


def ssm_branch(u_lat, u_ctx, lam_re, lam_im, log_dt, b_re, b_im, c_re, c_im, d_skip, w_glu, ctx_out):
    dtype = u_lat.dtype

    def to_groups(u):
        return u.astype(jnp.float32).reshape(u.shape[0], u.shape[1], SSM_GROUPS, SSM_GROUP)

    ul, uc = to_groups(u_lat), to_groups(u_ctx)
    d_g = d_skip.astype(jnp.float32).reshape(SSM_GROUPS, SSM_GROUP)
    y_lat = ul * d_g
    y_ctx = uc * d_g if ctx_out else None
    for direction in range(2):
        reverse = direction == 1
        a_re, a_im, bb_re, bb_im = ssm_discretize(lam_re[direction], lam_im[direction], log_dt[direction], b_re[direction], b_im[direction])
        c_r, c_i = c_re[direction].astype(jnp.float32), c_im[direction].astype(jnp.float32)

        def drive(u):
            return jnp.einsum('btgm,gpm->btgp', u, bb_re), jnp.einsum('btgm,gpm->btgp', u, bb_im)

        def readout(sr, si):
            return jnp.einsum('btgp,gmp->btgm', sr, c_r) - jnp.einsum('btgp,gmp->btgm', si, c_i)

        sc_re, sc_im = diag_scan(a_re, a_im, *drive(uc), None, reverse)
        end = 0 if reverse else -1
        sl_re, sl_im = diag_scan(a_re, a_im, *drive(ul), (sc_re[:, end], sc_im[:, end]), reverse)
        y_lat = y_lat + readout(sl_re, sl_im)
        if ctx_out:
            y_ctx = y_ctx + readout(sc_re, sc_im)

    def glu(y):
        y = jax.nn.gelu(y).reshape(y.shape[0], y.shape[1], SSM_WIDTH).astype(dtype)
        a, g = jnp.split(y @ w_glu, 2, axis=-1)
        return a * jax.nn.sigmoid(g)

    return glu(y_lat), (glu(y_ctx) if ctx_out else None)


def mixing_sublayer(xl, xc, row, col, ctx_out, w_in, mla_q_norm, mla_kv_norm, mla_w_uq, mla_w_ukv, mla_w_o,
                    lam_re, lam_im, log_dt, b_re, b_im, c_re, c_im, d_skip, w_glu, sink, gqa_w_o, w_out):
    B, T, _ = xl.shape
    C = xc.shape[1]
    offs = _offsets()
    cq, ckv, kr, u, gq, gk, gv, gates = jnp.split(xl @ w_in, offs, axis=-1)
    w_parts = jnp.split(w_in, offs, axis=1)
    ckv_c, kr_c, u_c, gk_c, gv_c = (xc @ w_parts[i] for i in (1, 2, 3, 5, 6))
    k_c, v_c = mla_keys_values(ckv_c, kr_c, mla_kv_norm, mla_w_ukv, None, None)
    gk_c = gk_c.reshape(B, C, GQA_KV_HEADS, GQA_HEAD_DIM)
    gv_c = gv_c.reshape(B, C, GQA_KV_HEADS, GQA_HEAD_DIM)
    q = mla_queries(cq, mla_q_norm, mla_w_uq, row, col)
    k, v = mla_keys_values(ckv, kr, mla_kv_norm, mla_w_ukv, row, col)
    mla_lat = dense_attention_blocks(q, jnp.concatenate([k_c, k], axis=1), jnp.concatenate([v_c, v], axis=1)) @ mla_w_o
    ssm_lat, ssm_ctx = ssm_branch(u, u_c, lam_re, lam_im, log_dt, b_re, b_im, c_re, c_im, d_skip, w_glu, ctx_out)
    gq = axial_rope(gq.reshape(B, T, GQA_HEADS, GQA_HEAD_DIM), row, col)
    gk = axial_rope(gk.reshape(B, T, GQA_KV_HEADS, GQA_HEAD_DIM), row, col)
    gv = gv.reshape(B, T, GQA_KV_HEADS, GQA_HEAD_DIM)
    gqa_lat = window_gqa(gq, gk, gv, gk_c, gv_c, sink) @ gqa_w_o

    def merge(gate_logits, b0, b1, b2):
        g0, g1, g2 = jnp.split(jax.nn.sigmoid(gate_logits), N_BRANCH, axis=-1)
        return (g0 * b0 + g1 * b1 + g2 * b2) @ w_out

    out_lat = merge(gates, mla_lat, ssm_lat, gqa_lat)
    if not ctx_out:
        return out_lat, None
    cq_c, gq_c, gates_c = (xc @ w_parts[i] for i in (0, 4, 7))
    q_c = mla_queries(cq_c, mla_q_norm, mla_w_uq, None, None)
    mla_ctx = dense_attention_blocks(q_c, k_c, v_c) @ mla_w_o
    gqa_ctx = context_gqa(gq_c.reshape(B, C, GQA_HEADS, GQA_HEAD_DIM), gk_c, gv_c, sink) @ gqa_w_o
    out_ctx = merge(gates_c, mla_ctx, ssm_ctx, gqa_ctx)
    return out_lat, out_ctx


def _fwd_setup_inputs(seed: int = 0) -> dict:
    key = jax.random.key(seed)
    ks = list(jax.random.split(key, 32))
    f32 = jnp.float32

    def nrm(shape, scale):
        return jax.random.normal(ks.pop(), shape, f32) * scale

    D, F, Lr = D_MODEL, D_FF, DEPTH
    G, P, M = SSM_GROUPS, SSM_STATE, SSM_GROUP
    inp = {}
    inp['x'] = nrm((BATCH, SEQ, D), 1.0)
    inp['c'] = nrm((BATCH, D), 1.0)
    inp['ctx'] = nrm((BATCH, CTX_LEN, D), 1.0)
    inp['c_ctx'] = nrm((D,), 1.0)
    inp['ada_w'] = nrm((Lr, D, N_MOD * D), 0.5 * D ** -0.5)
    inp['ada_b'] = nrm((Lr, N_MOD * D), 0.01)
    inp['norm_ffn1'] = 1.0 + nrm((Lr, D), 0.01)
    inp['norm_mix'] = 1.0 + nrm((Lr, D), 0.01)
    inp['norm_ffn2'] = 1.0 + nrm((Lr, D), 0.01)
    inp['ffn1_w13'] = nrm((Lr, D, 2 * F), D ** -0.5)
    inp['ffn1_w2'] = nrm((Lr, F, D), F ** -0.5)
    inp['ffn2_w13'] = nrm((Lr, D, 2 * F), D ** -0.5)
    inp['ffn2_w2'] = nrm((Lr, F, D), F ** -0.5)
    inp['w_in'] = nrm((Lr, D, IN_DIM), D ** -0.5)
    inp['mla_q_norm'] = 1.0 + nrm((Lr, MLA_Q_RANK), 0.01)
    inp['mla_kv_norm'] = 1.0 + nrm((Lr, MLA_KV_RANK), 0.01)
    inp['mla_w_uq'] = nrm((Lr, MLA_Q_RANK, MLA_HEADS * (MLA_NOPE + MLA_ROPE)), MLA_Q_RANK ** -0.5)
    inp['mla_w_ukv'] = nrm((Lr, MLA_KV_RANK, MLA_HEADS * (MLA_NOPE + MLA_V)), MLA_KV_RANK ** -0.5)
    inp['mla_w_o'] = nrm((Lr, MLA_HEADS * MLA_V, D), (MLA_HEADS * MLA_V) ** -0.5)
    inp['ssm_lambda_re'] = -0.5 + nrm((Lr, 2, G, P), 0.01)
    inp['ssm_lambda_im'] = jnp.pi * jnp.arange(P, dtype=f32) + nrm((Lr, 2, G, P), 0.01)
    inp['ssm_log_dt'] = jax.random.uniform(ks.pop(), (Lr, 2, G), f32, minval=math.log(DT_MIN), maxval=math.log(DT_MAX))
    inp['ssm_b_re'] = nrm((Lr, 2, G, P, M), (2 * M) ** -0.5)
    inp['ssm_b_im'] = nrm((Lr, 2, G, P, M), (2 * M) ** -0.5)
    inp['ssm_c_re'] = nrm((Lr, 2, G, M, P), 0.5)
    inp['ssm_c_im'] = nrm((Lr, 2, G, M, P), 0.5)
    inp['ssm_d'] = nrm((Lr, SSM_WIDTH), 1.0)
    inp['ssm_w_glu'] = nrm((Lr, SSM_WIDTH, 2 * D), SSM_WIDTH ** -0.5)
    inp['gqa_sink'] = nrm((Lr, GQA_HEADS), 0.5)
    inp['gqa_w_o'] = nrm((Lr, GQA_HEADS * GQA_HEAD_DIM, D), (GQA_HEADS * GQA_HEAD_DIM) ** -0.5)
    inp['w_out'] = nrm((Lr, D, D), D ** -0.5)
    inp['final_norm'] = 1.0 + nrm((D,), 0.01)
    return inp


def _fwd_reference(x, c, ctx, c_ctx, ada_w, ada_b, norm_ffn1, norm_mix, norm_ffn2, ffn1_w13, ffn1_w2, ffn2_w13, ffn2_w2,
              w_in, mla_q_norm, mla_kv_norm, mla_w_uq, mla_w_ukv, mla_w_o, ssm_lambda_re, ssm_lambda_im, ssm_log_dt,
              ssm_b_re, ssm_b_im, ssm_c_re, ssm_c_im, ssm_d, ssm_w_glu, gqa_sink, gqa_w_o, w_out, final_norm):
    L = x.shape[1]
    ROWS = L // GRID_W
    row = jnp.repeat(jnp.arange(ROWS, dtype=jnp.int32), GRID_W)
    col = jnp.tile(jnp.arange(GRID_W, dtype=jnp.int32), ROWS)
    h, hc = x, ctx
    for layer in range(DEPTH):
        ctx_out = layer < DEPTH - 1
        mod = jnp.split(jax.nn.silu(c) @ ada_w[layer] + ada_b[layer], N_MOD, axis=-1)
        mod_c = jnp.split(jax.nn.silu(c_ctx) @ ada_w[layer] + ada_b[layer], N_MOD, axis=-1)
        h = h + 0.5 * bcast(mod[2]) * swiglu(modulate(rmsnorm(h, norm_ffn1[layer]), mod[0], mod[1]), ffn1_w13[layer], ffn1_w2[layer])
        hc = hc + 0.5 * bcast(mod_c[2]) * swiglu(modulate(rmsnorm(hc, norm_ffn1[layer]), mod_c[0], mod_c[1]), ffn1_w13[layer], ffn1_w2[layer])
        mix_lat, mix_ctx = mixing_sublayer(
            modulate(rmsnorm(h, norm_mix[layer]), mod[3], mod[4]),
            modulate(rmsnorm(hc, norm_mix[layer]), mod_c[3], mod_c[4]),
            row, col, ctx_out, w_in[layer], mla_q_norm[layer], mla_kv_norm[layer], mla_w_uq[layer], mla_w_ukv[layer],
            mla_w_o[layer], ssm_lambda_re[layer], ssm_lambda_im[layer], ssm_log_dt[layer], ssm_b_re[layer], ssm_b_im[layer],
            ssm_c_re[layer], ssm_c_im[layer], ssm_d[layer], ssm_w_glu[layer], gqa_sink[layer], gqa_w_o[layer], w_out[layer])
        h = h + bcast(mod[5]) * mix_lat
        h = h + 0.5 * bcast(mod[8]) * swiglu(modulate(rmsnorm(h, norm_ffn2[layer]), mod[6], mod[7]), ffn2_w13[layer], ffn2_w2[layer])
        if ctx_out:
            hc = hc + bcast(mod_c[5]) * mix_ctx
            hc = hc + 0.5 * bcast(mod_c[8]) * swiglu(modulate(rmsnorm(hc, norm_ffn2[layer]), mod_c[6], mod_c[7]), ffn2_w13[layer], ffn2_w2[layer])
    return rmsnorm(h, final_norm)


import jax as _jax
import jax.numpy as _jnp

TWIN_FORMAT = 'train_step'
FWD_PARAMS = ['x', 'c', 'ctx', 'c_ctx', 'ada_w', 'ada_b', 'norm_ffn1', 'norm_mix', 'norm_ffn2', 'ffn1_w13', 'ffn1_w2', 'ffn2_w13', 'ffn2_w2', 'w_in', 'mla_q_norm', 'mla_kv_norm', 'mla_w_uq', 'mla_w_ukv', 'mla_w_o', 'ssm_lambda_re', 'ssm_lambda_im', 'ssm_log_dt', 'ssm_b_re', 'ssm_b_im', 'ssm_c_re', 'ssm_c_im', 'ssm_d', 'ssm_w_glu', 'gqa_sink', 'gqa_w_o', 'w_out', 'final_norm']
TWIN_WEIGHTS = ['c_ctx', 'ada_w', 'ada_b', 'norm_ffn1', 'norm_mix', 'norm_ffn2', 'ffn1_w13', 'ffn1_w2', 'ffn2_w13', 'ffn2_w2', 'w_in', 'mla_q_norm', 'mla_kv_norm', 'mla_w_uq', 'mla_w_ukv', 'mla_w_o', 'ssm_lambda_re', 'ssm_lambda_im', 'ssm_log_dt', 'ssm_b_re', 'ssm_b_im', 'ssm_c_re', 'ssm_c_im', 'ssm_d', 'ssm_w_glu', 'gqa_sink', 'gqa_w_o', 'w_out', 'final_norm']
TWIN_DIFF_INPUT = 'x'
TWIN_INPUTS = ['x', 'c', 'ctx', 'c_ctx', 'ada_w', 'ada_b', 'norm_ffn1', 'norm_mix', 'norm_ffn2', 'ffn1_w13', 'ffn1_w2', 'ffn2_w13', 'ffn2_w2', 'w_in', 'mla_q_norm', 'mla_kv_norm', 'mla_w_uq', 'mla_w_ukv', 'mla_w_o', 'ssm_lambda_re', 'ssm_lambda_im', 'ssm_log_dt', 'ssm_b_re', 'ssm_b_im', 'ssm_c_re', 'ssm_c_im', 'ssm_d', 'ssm_w_glu', 'gqa_sink', 'gqa_w_o', 'w_out', 'final_norm', 'loss_target', 'm_c_ctx', 'm_ada_w', 'm_ada_b', 'm_norm_ffn1', 'm_norm_mix', 'm_norm_ffn2', 'm_ffn1_w13', 'm_ffn1_w2', 'm_ffn2_w13', 'm_ffn2_w2', 'm_w_in', 'm_mla_q_norm', 'm_mla_kv_norm', 'm_mla_w_uq', 'm_mla_w_ukv', 'm_mla_w_o', 'm_ssm_lambda_re', 'm_ssm_lambda_im', 'm_ssm_log_dt', 'm_ssm_b_re', 'm_ssm_b_im', 'm_ssm_c_re', 'm_ssm_c_im', 'm_ssm_d', 'm_ssm_w_glu', 'm_gqa_sink', 'm_gqa_w_o', 'm_w_out', 'm_final_norm', 'v_c_ctx', 'v_ada_w', 'v_ada_b', 'v_norm_ffn1', 'v_norm_mix', 'v_norm_ffn2', 'v_ffn1_w13', 'v_ffn1_w2', 'v_ffn2_w13', 'v_ffn2_w2', 'v_w_in', 'v_mla_q_norm', 'v_mla_kv_norm', 'v_mla_w_uq', 'v_mla_w_ukv', 'v_mla_w_o', 'v_ssm_lambda_re', 'v_ssm_lambda_im', 'v_ssm_log_dt', 'v_ssm_b_re', 'v_ssm_b_im', 'v_ssm_c_re', 'v_ssm_c_im', 'v_ssm_d', 'v_ssm_w_glu', 'v_gqa_sink', 'v_gqa_w_o', 'v_w_out', 'v_final_norm']
TWIN_OUTPUTS = ['loss', 'grad_x', 'grad_c_ctx', 'grad_ada_w', 'grad_ada_b', 'grad_norm_ffn1', 'grad_norm_mix', 'grad_norm_ffn2', 'grad_ffn1_w13', 'grad_ffn1_w2', 'grad_ffn2_w13', 'grad_ffn2_w2', 'grad_w_in', 'grad_mla_q_norm', 'grad_mla_kv_norm', 'grad_mla_w_uq', 'grad_mla_w_ukv', 'grad_mla_w_o', 'grad_ssm_lambda_re', 'grad_ssm_lambda_im', 'grad_ssm_log_dt', 'grad_ssm_b_re', 'grad_ssm_b_im', 'grad_ssm_c_re', 'grad_ssm_c_im', 'grad_ssm_d', 'grad_ssm_w_glu', 'grad_gqa_sink', 'grad_gqa_w_o', 'grad_w_out', 'grad_final_norm', 'delta_c_ctx', 'delta_ada_w', 'delta_ada_b', 'delta_norm_ffn1', 'delta_norm_mix', 'delta_norm_ffn2', 'delta_ffn1_w13', 'delta_ffn1_w2', 'delta_ffn2_w13', 'delta_ffn2_w2', 'delta_w_in', 'delta_mla_q_norm', 'delta_mla_kv_norm', 'delta_mla_w_uq', 'delta_mla_w_ukv', 'delta_mla_w_o', 'delta_ssm_lambda_re', 'delta_ssm_lambda_im', 'delta_ssm_log_dt', 'delta_ssm_b_re', 'delta_ssm_b_im', 'delta_ssm_c_re', 'delta_ssm_c_im', 'delta_ssm_d', 'delta_ssm_w_glu', 'delta_gqa_sink', 'delta_gqa_w_o', 'delta_w_out', 'delta_final_norm', 'new_m_c_ctx', 'new_m_ada_w', 'new_m_ada_b', 'new_m_norm_ffn1', 'new_m_norm_mix', 'new_m_norm_ffn2', 'new_m_ffn1_w13', 'new_m_ffn1_w2', 'new_m_ffn2_w13', 'new_m_ffn2_w2', 'new_m_w_in', 'new_m_mla_q_norm', 'new_m_mla_kv_norm', 'new_m_mla_w_uq', 'new_m_mla_w_ukv', 'new_m_mla_w_o', 'new_m_ssm_lambda_re', 'new_m_ssm_lambda_im', 'new_m_ssm_log_dt', 'new_m_ssm_b_re', 'new_m_ssm_b_im', 'new_m_ssm_c_re', 'new_m_ssm_c_im', 'new_m_ssm_d', 'new_m_ssm_w_glu', 'new_m_gqa_sink', 'new_m_gqa_w_o', 'new_m_w_out', 'new_m_final_norm', 'new_v_c_ctx', 'new_v_ada_w', 'new_v_ada_b', 'new_v_norm_ffn1', 'new_v_norm_mix', 'new_v_norm_ffn2', 'new_v_ffn1_w13', 'new_v_ffn1_w2', 'new_v_ffn2_w13', 'new_v_ffn2_w2', 'new_v_w_in', 'new_v_mla_q_norm', 'new_v_mla_kv_norm', 'new_v_mla_w_uq', 'new_v_mla_w_ukv', 'new_v_mla_w_o', 'new_v_ssm_lambda_re', 'new_v_ssm_lambda_im', 'new_v_ssm_log_dt', 'new_v_ssm_b_re', 'new_v_ssm_b_im', 'new_v_ssm_c_re', 'new_v_ssm_c_im', 'new_v_ssm_d', 'new_v_ssm_w_glu', 'new_v_gqa_sink', 'new_v_gqa_w_o', 'new_v_w_out', 'new_v_final_norm']
TWIN_LEAF_KINDS = {'loss': 'loss', 'grad_x': 'grad_x', 'grad_c_ctx': 'grad_w', 'grad_ada_w': 'grad_w', 'grad_ada_b': 'grad_w', 'grad_norm_ffn1': 'grad_w', 'grad_norm_mix': 'grad_w', 'grad_norm_ffn2': 'grad_w', 'grad_ffn1_w13': 'grad_w', 'grad_ffn1_w2': 'grad_w', 'grad_ffn2_w13': 'grad_w', 'grad_ffn2_w2': 'grad_w', 'grad_w_in': 'grad_w', 'grad_mla_q_norm': 'grad_w', 'grad_mla_kv_norm': 'grad_w', 'grad_mla_w_uq': 'grad_w', 'grad_mla_w_ukv': 'grad_w', 'grad_mla_w_o': 'grad_w', 'grad_ssm_lambda_re': 'grad_w', 'grad_ssm_lambda_im': 'grad_w', 'grad_ssm_log_dt': 'grad_w', 'grad_ssm_b_re': 'grad_w', 'grad_ssm_b_im': 'grad_w', 'grad_ssm_c_re': 'grad_w', 'grad_ssm_c_im': 'grad_w', 'grad_ssm_d': 'grad_w', 'grad_ssm_w_glu': 'grad_w', 'grad_gqa_sink': 'grad_w', 'grad_gqa_w_o': 'grad_w', 'grad_w_out': 'grad_w', 'grad_final_norm': 'grad_w', 'delta_c_ctx': 'delta_w', 'delta_ada_w': 'delta_w', 'delta_ada_b': 'delta_w', 'delta_norm_ffn1': 'delta_w', 'delta_norm_mix': 'delta_w', 'delta_norm_ffn2': 'delta_w', 'delta_ffn1_w13': 'delta_w', 'delta_ffn1_w2': 'delta_w', 'delta_ffn2_w13': 'delta_w', 'delta_ffn2_w2': 'delta_w', 'delta_w_in': 'delta_w', 'delta_mla_q_norm': 'delta_w', 'delta_mla_kv_norm': 'delta_w', 'delta_mla_w_uq': 'delta_w', 'delta_mla_w_ukv': 'delta_w', 'delta_mla_w_o': 'delta_w', 'delta_ssm_lambda_re': 'delta_w', 'delta_ssm_lambda_im': 'delta_w', 'delta_ssm_log_dt': 'delta_w', 'delta_ssm_b_re': 'delta_w', 'delta_ssm_b_im': 'delta_w', 'delta_ssm_c_re': 'delta_w', 'delta_ssm_c_im': 'delta_w', 'delta_ssm_d': 'delta_w', 'delta_ssm_w_glu': 'delta_w', 'delta_gqa_sink': 'delta_w', 'delta_gqa_w_o': 'delta_w', 'delta_w_out': 'delta_w', 'delta_final_norm': 'delta_w', 'new_m_c_ctx': 'new_m', 'new_m_ada_w': 'new_m', 'new_m_ada_b': 'new_m', 'new_m_norm_ffn1': 'new_m', 'new_m_norm_mix': 'new_m', 'new_m_norm_ffn2': 'new_m', 'new_m_ffn1_w13': 'new_m', 'new_m_ffn1_w2': 'new_m', 'new_m_ffn2_w13': 'new_m', 'new_m_ffn2_w2': 'new_m', 'new_m_w_in': 'new_m', 'new_m_mla_q_norm': 'new_m', 'new_m_mla_kv_norm': 'new_m', 'new_m_mla_w_uq': 'new_m', 'new_m_mla_w_ukv': 'new_m', 'new_m_mla_w_o': 'new_m', 'new_m_ssm_lambda_re': 'new_m', 'new_m_ssm_lambda_im': 'new_m', 'new_m_ssm_log_dt': 'new_m', 'new_m_ssm_b_re': 'new_m', 'new_m_ssm_b_im': 'new_m', 'new_m_ssm_c_re': 'new_m', 'new_m_ssm_c_im': 'new_m', 'new_m_ssm_d': 'new_m', 'new_m_ssm_w_glu': 'new_m', 'new_m_gqa_sink': 'new_m', 'new_m_gqa_w_o': 'new_m', 'new_m_w_out': 'new_m', 'new_m_final_norm': 'new_m', 'new_v_c_ctx': 'new_v', 'new_v_ada_w': 'new_v', 'new_v_ada_b': 'new_v', 'new_v_norm_ffn1': 'new_v', 'new_v_norm_mix': 'new_v', 'new_v_norm_ffn2': 'new_v', 'new_v_ffn1_w13': 'new_v', 'new_v_ffn1_w2': 'new_v', 'new_v_ffn2_w13': 'new_v', 'new_v_ffn2_w2': 'new_v', 'new_v_w_in': 'new_v', 'new_v_mla_q_norm': 'new_v', 'new_v_mla_kv_norm': 'new_v', 'new_v_mla_w_uq': 'new_v', 'new_v_mla_w_ukv': 'new_v', 'new_v_mla_w_o': 'new_v', 'new_v_ssm_lambda_re': 'new_v', 'new_v_ssm_lambda_im': 'new_v', 'new_v_ssm_log_dt': 'new_v', 'new_v_ssm_b_re': 'new_v', 'new_v_ssm_b_im': 'new_v', 'new_v_ssm_c_re': 'new_v', 'new_v_ssm_c_im': 'new_v', 'new_v_ssm_d': 'new_v', 'new_v_ssm_w_glu': 'new_v', 'new_v_gqa_sink': 'new_v', 'new_v_gqa_w_o': 'new_v', 'new_v_w_out': 'new_v', 'new_v_final_norm': 'new_v'}


def _forward(args):
    return _fwd_reference(*[args[k] for k in FWD_PARAMS])


def _output_shape():
    def fwd():
        inp = _fwd_setup_inputs(0)
        return _fwd_reference(*[inp[k] for k in FWD_PARAMS])
    out = _jax.eval_shape(fwd)
    return out.shape, out.dtype

N_MICROBATCH = 1
ADAM_LR = 0.001
ADAM_B1 = 0.9
ADAM_B2 = 0.999
ADAM_EPS = 1e-08
ADAM_WD = 0.01
ADAM_STEP = 10
PER_EXAMPLE_BATCH_AXIS = {'x': 0, 'c': 0, 'ctx': 0, 'loss_target': 0}
SHARED_INPUTS = []
_WEIGHT_DTYPES = {'c_ctx': _jnp.float32, 'ada_w': _jnp.float32, 'ada_b': _jnp.float32, 'norm_ffn1': _jnp.float32, 'norm_mix': _jnp.float32, 'norm_ffn2': _jnp.float32, 'ffn1_w13': _jnp.float32, 'ffn1_w2': _jnp.float32, 'ffn2_w13': _jnp.float32, 'ffn2_w2': _jnp.float32, 'w_in': _jnp.float32, 'mla_q_norm': _jnp.float32, 'mla_kv_norm': _jnp.float32, 'mla_w_uq': _jnp.float32, 'mla_w_ukv': _jnp.float32, 'mla_w_o': _jnp.float32, 'ssm_lambda_re': _jnp.float32, 'ssm_lambda_im': _jnp.float32, 'ssm_log_dt': _jnp.float32, 'ssm_b_re': _jnp.float32, 'ssm_b_im': _jnp.float32, 'ssm_c_re': _jnp.float32, 'ssm_c_im': _jnp.float32, 'ssm_d': _jnp.float32, 'ssm_w_glu': _jnp.float32, 'gqa_sink': _jnp.float32, 'gqa_w_o': _jnp.float32, 'w_out': _jnp.float32, 'final_norm': _jnp.float32}
MOMENT_SCALE = {'c_ctx': 2.062616e-02, 'ada_w': 3.576045e-02, 'ada_b': 6.098491e-02, 'norm_ffn1': 3.747880e-02, 'norm_mix': 2.409519e-02, 'norm_ffn2': 3.718482e-02, 'ffn1_w13': 1.643304e-02, 'ffn1_w2': 2.676135e-02, 'ffn2_w13': 1.622913e-02, 'ffn2_w2': 2.651641e-02, 'w_in': 1.296924e-02, 'mla_q_norm': 6.507348e-03, 'mla_kv_norm': 2.203624e-02, 'mla_w_uq': 4.570427e-03, 'mla_w_ukv': 1.178204e-02, 'mla_w_o': 1.161573e-02, 'ssm_lambda_re': 8.959830e-03, 'ssm_lambda_im': 9.912148e-03, 'ssm_log_dt': 4.374552e+00, 'ssm_b_re': 5.686029e-03, 'ssm_b_im': 5.776009e-03, 'ssm_c_re': 1.932289e-03, 'ssm_c_im': 1.964823e-03, 'ssm_d': 2.735465e-02, 'ssm_w_glu': 1.454870e-02, 'gqa_sink': 2.257081e-04, 'gqa_w_o': 1.065325e-02, 'w_out': 2.516006e-02, 'final_norm': 6.390095e+01}


def _to_microbatches(a, axis):
    t = _jnp.moveaxis(a, axis, 0)
    t = t.reshape((N_MICROBATCH, t.shape[0] // N_MICROBATCH) + t.shape[1:])
    return _jnp.moveaxis(t, 1, axis + 1)


def setup_inputs(seed: int = 0) -> dict:
    inp = _fwd_setup_inputs(seed)
    key = _jax.random.fold_in(_jax.random.key(seed), 7919)
    shape, _ = _output_shape()
    out = dict(inp)
    out["loss_target"] = _jax.random.normal(_jax.random.fold_in(key, 0), shape, _jnp.float32)
    for i, name in enumerate(TWIN_WEIGHTS):
        w = inp[name].astype(_jnp.float32)
        if MOMENT_SCALE is None:
            s = _jnp.sqrt(_jnp.mean(_jnp.square(w)) + 1e-30)
        else:
            s = MOMENT_SCALE[name]
        km, kv = _jax.random.split(_jax.random.fold_in(key, i + 1))
        out[name] = w
        out["m_" + name] = s * _jax.random.normal(km, w.shape, _jnp.float32)
        out["v_" + name] = (s * s) * _jax.random.uniform(kv, w.shape, _jnp.float32, 0.5, 1.5)
    if N_MICROBATCH > 1:
        for name, axis in PER_EXAMPLE_BATCH_AXIS.items():
            out[name] = _to_microbatches(out[name], axis)
    return {'x': out['x'], 'c': out['c'], 'ctx': out['ctx'], 'c_ctx': out['c_ctx'], 'ada_w': out['ada_w'], 'ada_b': out['ada_b'], 'norm_ffn1': out['norm_ffn1'], 'norm_mix': out['norm_mix'], 'norm_ffn2': out['norm_ffn2'], 'ffn1_w13': out['ffn1_w13'], 'ffn1_w2': out['ffn1_w2'], 'ffn2_w13': out['ffn2_w13'], 'ffn2_w2': out['ffn2_w2'], 'w_in': out['w_in'], 'mla_q_norm': out['mla_q_norm'], 'mla_kv_norm': out['mla_kv_norm'], 'mla_w_uq': out['mla_w_uq'], 'mla_w_ukv': out['mla_w_ukv'], 'mla_w_o': out['mla_w_o'], 'ssm_lambda_re': out['ssm_lambda_re'], 'ssm_lambda_im': out['ssm_lambda_im'], 'ssm_log_dt': out['ssm_log_dt'], 'ssm_b_re': out['ssm_b_re'], 'ssm_b_im': out['ssm_b_im'], 'ssm_c_re': out['ssm_c_re'], 'ssm_c_im': out['ssm_c_im'], 'ssm_d': out['ssm_d'], 'ssm_w_glu': out['ssm_w_glu'], 'gqa_sink': out['gqa_sink'], 'gqa_w_o': out['gqa_w_o'], 'w_out': out['w_out'], 'final_norm': out['final_norm'], 'loss_target': out['loss_target'], 'm_c_ctx': out['m_c_ctx'], 'm_ada_w': out['m_ada_w'], 'm_ada_b': out['m_ada_b'], 'm_norm_ffn1': out['m_norm_ffn1'], 'm_norm_mix': out['m_norm_mix'], 'm_norm_ffn2': out['m_norm_ffn2'], 'm_ffn1_w13': out['m_ffn1_w13'], 'm_ffn1_w2': out['m_ffn1_w2'], 'm_ffn2_w13': out['m_ffn2_w13'], 'm_ffn2_w2': out['m_ffn2_w2'], 'm_w_in': out['m_w_in'], 'm_mla_q_norm': out['m_mla_q_norm'], 'm_mla_kv_norm': out['m_mla_kv_norm'], 'm_mla_w_uq': out['m_mla_w_uq'], 'm_mla_w_ukv': out['m_mla_w_ukv'], 'm_mla_w_o': out['m_mla_w_o'], 'm_ssm_lambda_re': out['m_ssm_lambda_re'], 'm_ssm_lambda_im': out['m_ssm_lambda_im'], 'm_ssm_log_dt': out['m_ssm_log_dt'], 'm_ssm_b_re': out['m_ssm_b_re'], 'm_ssm_b_im': out['m_ssm_b_im'], 'm_ssm_c_re': out['m_ssm_c_re'], 'm_ssm_c_im': out['m_ssm_c_im'], 'm_ssm_d': out['m_ssm_d'], 'm_ssm_w_glu': out['m_ssm_w_glu'], 'm_gqa_sink': out['m_gqa_sink'], 'm_gqa_w_o': out['m_gqa_w_o'], 'm_w_out': out['m_w_out'], 'm_final_norm': out['m_final_norm'], 'v_c_ctx': out['v_c_ctx'], 'v_ada_w': out['v_ada_w'], 'v_ada_b': out['v_ada_b'], 'v_norm_ffn1': out['v_norm_ffn1'], 'v_norm_mix': out['v_norm_mix'], 'v_norm_ffn2': out['v_norm_ffn2'], 'v_ffn1_w13': out['v_ffn1_w13'], 'v_ffn1_w2': out['v_ffn1_w2'], 'v_ffn2_w13': out['v_ffn2_w13'], 'v_ffn2_w2': out['v_ffn2_w2'], 'v_w_in': out['v_w_in'], 'v_mla_q_norm': out['v_mla_q_norm'], 'v_mla_kv_norm': out['v_mla_kv_norm'], 'v_mla_w_uq': out['v_mla_w_uq'], 'v_mla_w_ukv': out['v_mla_w_ukv'], 'v_mla_w_o': out['v_mla_w_o'], 'v_ssm_lambda_re': out['v_ssm_lambda_re'], 'v_ssm_lambda_im': out['v_ssm_lambda_im'], 'v_ssm_log_dt': out['v_ssm_log_dt'], 'v_ssm_b_re': out['v_ssm_b_re'], 'v_ssm_b_im': out['v_ssm_b_im'], 'v_ssm_c_re': out['v_ssm_c_re'], 'v_ssm_c_im': out['v_ssm_c_im'], 'v_ssm_d': out['v_ssm_d'], 'v_ssm_w_glu': out['v_ssm_w_glu'], 'v_gqa_sink': out['v_gqa_sink'], 'v_gqa_w_o': out['v_gqa_w_o'], 'v_w_out': out['v_w_out'], 'v_final_norm': out['v_final_norm']}


def _loss(weights, diff, rest, loss_target):
    with _jax.named_scope("forward"):
        args = {**rest, TWIN_DIFF_INPUT: diff, **{k: w.astype(_WEIGHT_DTYPES[k]) for k, w in weights.items()}}
        y = _forward(args)
    with _jax.named_scope("loss_head"):
        err = _jnp.square(y.astype(_jnp.float32) - loss_target)
        return 0.5 * _jnp.sum(_jnp.mean(err, axis=-1)) if err.ndim else 0.5 * err


def _adamw(w, g, m, v):
    m = ADAM_B1 * m + (1.0 - ADAM_B1) * g
    v = ADAM_B2 * v + (1.0 - ADAM_B2) * _jnp.square(g)
    m_hat = m / (1.0 - ADAM_B1 ** ADAM_STEP)
    v_hat = v / (1.0 - ADAM_B2 ** ADAM_STEP)
    delta = -ADAM_LR * (m_hat / (_jnp.sqrt(v_hat) + ADAM_EPS) + ADAM_WD * w)
    return delta, m, v


def reference(x, c, ctx, c_ctx, ada_w, ada_b, norm_ffn1, norm_mix, norm_ffn2, ffn1_w13, ffn1_w2, ffn2_w13, ffn2_w2, w_in, mla_q_norm, mla_kv_norm, mla_w_uq, mla_w_ukv, mla_w_o, ssm_lambda_re, ssm_lambda_im, ssm_log_dt, ssm_b_re, ssm_b_im, ssm_c_re, ssm_c_im, ssm_d, ssm_w_glu, gqa_sink, gqa_w_o, w_out, final_norm, loss_target, m_c_ctx, m_ada_w, m_ada_b, m_norm_ffn1, m_norm_mix, m_norm_ffn2, m_ffn1_w13, m_ffn1_w2, m_ffn2_w13, m_ffn2_w2, m_w_in, m_mla_q_norm, m_mla_kv_norm, m_mla_w_uq, m_mla_w_ukv, m_mla_w_o, m_ssm_lambda_re, m_ssm_lambda_im, m_ssm_log_dt, m_ssm_b_re, m_ssm_b_im, m_ssm_c_re, m_ssm_c_im, m_ssm_d, m_ssm_w_glu, m_gqa_sink, m_gqa_w_o, m_w_out, m_final_norm, v_c_ctx, v_ada_w, v_ada_b, v_norm_ffn1, v_norm_mix, v_norm_ffn2, v_ffn1_w13, v_ffn1_w2, v_ffn2_w13, v_ffn2_w2, v_w_in, v_mla_q_norm, v_mla_kv_norm, v_mla_w_uq, v_mla_w_ukv, v_mla_w_o, v_ssm_lambda_re, v_ssm_lambda_im, v_ssm_log_dt, v_ssm_b_re, v_ssm_b_im, v_ssm_c_re, v_ssm_c_im, v_ssm_d, v_ssm_w_glu, v_gqa_sink, v_gqa_w_o, v_w_out, v_final_norm):
    given = dict(x=x, c=c, ctx=ctx, c_ctx=c_ctx, ada_w=ada_w, ada_b=ada_b, norm_ffn1=norm_ffn1, norm_mix=norm_mix, norm_ffn2=norm_ffn2, ffn1_w13=ffn1_w13, ffn1_w2=ffn1_w2, ffn2_w13=ffn2_w13, ffn2_w2=ffn2_w2, w_in=w_in, mla_q_norm=mla_q_norm, mla_kv_norm=mla_kv_norm, mla_w_uq=mla_w_uq, mla_w_ukv=mla_w_ukv, mla_w_o=mla_w_o, ssm_lambda_re=ssm_lambda_re, ssm_lambda_im=ssm_lambda_im, ssm_log_dt=ssm_log_dt, ssm_b_re=ssm_b_re, ssm_b_im=ssm_b_im, ssm_c_re=ssm_c_re, ssm_c_im=ssm_c_im, ssm_d=ssm_d, ssm_w_glu=ssm_w_glu, gqa_sink=gqa_sink, gqa_w_o=gqa_w_o, w_out=w_out, final_norm=final_norm, loss_target=loss_target, m_c_ctx=m_c_ctx, m_ada_w=m_ada_w, m_ada_b=m_ada_b, m_norm_ffn1=m_norm_ffn1, m_norm_mix=m_norm_mix, m_norm_ffn2=m_norm_ffn2, m_ffn1_w13=m_ffn1_w13, m_ffn1_w2=m_ffn1_w2, m_ffn2_w13=m_ffn2_w13, m_ffn2_w2=m_ffn2_w2, m_w_in=m_w_in, m_mla_q_norm=m_mla_q_norm, m_mla_kv_norm=m_mla_kv_norm, m_mla_w_uq=m_mla_w_uq, m_mla_w_ukv=m_mla_w_ukv, m_mla_w_o=m_mla_w_o, m_ssm_lambda_re=m_ssm_lambda_re, m_ssm_lambda_im=m_ssm_lambda_im, m_ssm_log_dt=m_ssm_log_dt, m_ssm_b_re=m_ssm_b_re, m_ssm_b_im=m_ssm_b_im, m_ssm_c_re=m_ssm_c_re, m_ssm_c_im=m_ssm_c_im, m_ssm_d=m_ssm_d, m_ssm_w_glu=m_ssm_w_glu, m_gqa_sink=m_gqa_sink, m_gqa_w_o=m_gqa_w_o, m_w_out=m_w_out, m_final_norm=m_final_norm, v_c_ctx=v_c_ctx, v_ada_w=v_ada_w, v_ada_b=v_ada_b, v_norm_ffn1=v_norm_ffn1, v_norm_mix=v_norm_mix, v_norm_ffn2=v_norm_ffn2, v_ffn1_w13=v_ffn1_w13, v_ffn1_w2=v_ffn1_w2, v_ffn2_w13=v_ffn2_w13, v_ffn2_w2=v_ffn2_w2, v_w_in=v_w_in, v_mla_q_norm=v_mla_q_norm, v_mla_kv_norm=v_mla_kv_norm, v_mla_w_uq=v_mla_w_uq, v_mla_w_ukv=v_mla_w_ukv, v_mla_w_o=v_mla_w_o, v_ssm_lambda_re=v_ssm_lambda_re, v_ssm_lambda_im=v_ssm_lambda_im, v_ssm_log_dt=v_ssm_log_dt, v_ssm_b_re=v_ssm_b_re, v_ssm_b_im=v_ssm_b_im, v_ssm_c_re=v_ssm_c_re, v_ssm_c_im=v_ssm_c_im, v_ssm_d=v_ssm_d, v_ssm_w_glu=v_ssm_w_glu, v_gqa_sink=v_gqa_sink, v_gqa_w_o=v_gqa_w_o, v_w_out=v_w_out, v_final_norm=v_final_norm)
    weights = {n: given[n] for n in TWIN_WEIGHTS}
    shared = {n: given[n] for n in SHARED_INPUTS}
    per_example = {n: given[n] for n in ['x', 'c', 'ctx']}
    grad_fn = _jax.value_and_grad(_loss, argnums=(0, 1))

    def one_microbatch(ex, loss_target):
        ex = dict(ex)
        diff = ex.pop(TWIN_DIFF_INPUT)
        return grad_fn(weights, diff, {**shared, **ex}, loss_target)

    if N_MICROBATCH == 1:
        loss, (grad_w, grad_x) = one_microbatch(per_example, given["loss_target"])
    else:
        def body(carry, xs):
            loss_sum, grad_sum = carry
            l_k, (gw_k, gx_k) = one_microbatch(xs[0], xs[1])
            with _jax.named_scope("update"):
                return (loss_sum + l_k, _jax.tree.map(_jnp.add, grad_sum, gw_k)), gx_k

        init = (_jnp.zeros((), _jnp.float32), _jax.tree.map(_jnp.zeros_like, weights))
        (loss, grad_w), grad_x = _jax.lax.scan(body, init, (per_example, given["loss_target"]))
    with _jax.named_scope("update"):
        delta_w, new_m, new_v = {}, {}, {}
        for n in TWIN_WEIGHTS:
            delta_w[n], new_m[n], new_v[n] = _adamw(weights[n], grad_w[n], given["m_" + n], given["v_" + n])
    return (loss, grad_x, *[grad_w[n] for n in TWIN_WEIGHTS], *[delta_w[n] for n in TWIN_WEIGHTS],
            *[new_m[n] for n in TWIN_WEIGHTS], *[new_v[n] for n in TWIN_WEIGHTS])
```

```python
import functools
import math

import jax
import jax.numpy as jnp
from jax import lax
from jax.experimental import pallas as pl
from jax.experimental.pallas import tpu as pltpu

F32 = jnp.float32
MXU_DTYPE = jnp.bfloat16
WIRE_DTYPE = jnp.bfloat16

GRID_W = 64
MLA_HEADS, MLA_NOPE, MLA_ROPE, MLA_V = 8, 64, 32, 64
MLA_Q_RANK, MLA_KV_RANK = 384, 256
SSM_WIDTH, SSM_GROUP, SSM_STATE = 512, 16, 64
SSM_GROUPS = SSM_WIDTH // SSM_GROUP
SSM_CHUNK_GROUPS = 8
SSM_CHUNKS = SSM_GROUPS // SSM_CHUNK_GROUPS
SSM_CW = SSM_CHUNK_GROUPS * SSM_STATE
GQA_HEADS, GQA_KV_HEADS, GQA_HEAD_DIM = 8, 2, 64
WINDOW, BLOCK = 128, 128
N_BRANCH, N_MOD = 3, 9
ROPE_BASE = 10000.0
EPS = 1e-6
NEG_INF = -1e30
LANES = 128
FLAT_W = 1024

ADAM_LR, ADAM_B1, ADAM_B2, ADAM_EPS, ADAM_WD, ADAM_STEP = 0.001, 0.9, 0.999, 1e-08, 0.01, 10

VMEM_LIMIT = 48 * 1024 * 1024

ARG_NAMES = ['x', 'c', 'ctx', 'c_ctx', 'ada_w', 'ada_b', 'norm_ffn1', 'norm_mix', 'norm_ffn2', 'ffn1_w13', 'ffn1_w2', 'ffn2_w13', 'ffn2_w2', 'w_in', 'mla_q_norm', 'mla_kv_norm', 'mla_w_uq', 'mla_w_ukv', 'mla_w_o', 'ssm_lambda_re', 'ssm_lambda_im', 'ssm_log_dt', 'ssm_b_re', 'ssm_b_im', 'ssm_c_re', 'ssm_c_im', 'ssm_d', 'ssm_w_glu', 'gqa_sink', 'gqa_w_o', 'w_out', 'final_norm']
WEIGHTS = ARG_NAMES[3:]
BIG = (('ada_w', 2), ('ffn1_w13', 2), ('ffn1_w2', 1), ('ffn2_w13', 2), ('ffn2_w2', 1), ('w_in', 2), ('mla_w_uq', 2),
       ('mla_w_ukv', 2), ('mla_w_o', 2), ('ssm_w_glu', 2), ('gqa_w_o', 2), ('w_out', 1))
BIG_NAMES = tuple(n for n, _ in BIG)
SMALL = tuple(n for n in WEIGHTS if n not in BIG_NAMES)
N_CHIPS = 4


def _pick(n, prefs):
    for p in prefs:
        if n % p == 0:
            return p
    return n


def _params(sem=None):
    return pltpu.CompilerParams(dimension_semantics=sem, vmem_limit_bytes=VMEM_LIMIT)


def _mm_call(a, b, *, grid, a_spec, b_spec, o_spec, o_shape, acc_shape, ta, tb, name, out_dtype=F32):
    nk = grid[2]
    dn = (((0 if ta else 1,), (1 if tb else 0,)), ((), ()))

    def body(a_ref, b_ref, o_ref, acc_ref):
        k = pl.program_id(2)

        @pl.when(k == 0)
        def _():
            acc_ref[...] = jnp.zeros_like(acc_ref)

        acc_ref[...] += lax.dot_general(a_ref[...].astype(MXU_DTYPE), b_ref[...].astype(MXU_DTYPE), dn,
                                        preferred_element_type=F32)

        @pl.when(k == nk - 1)
        def _():
            o_ref[...] = acc_ref[...].astype(o_ref.dtype)

    return pl.pallas_call(
        body, name=name, grid=grid, in_specs=[a_spec, b_spec], out_specs=o_spec,
        out_shape=jax.ShapeDtypeStruct(o_shape, out_dtype), scratch_shapes=[pltpu.VMEM(acc_shape, F32)],
        compiler_params=_params(("parallel", "parallel", "arbitrary")))(a, b)


_TM = (512, 256, 128, 64, 32, 16, 8)
_TN = (512, 256, 128)
_TK = (512, 256, 128, 64, 32, 16, 8)


def _mm_nn(x, w, name):
    M, K = x.shape
    N = w.shape[1]
    tm, tn, tk = _pick(M, _TM), _pick(N, _TN), _pick(K, _TK)
    return _mm_call(x, w, grid=(M // tm, N // tn, K // tk),
                    a_spec=pl.BlockSpec((tm, tk), lambda i, j, k: (i, k)),
                    b_spec=pl.BlockSpec((tk, tn), lambda i, j, k: (k, j)),
                    o_spec=pl.BlockSpec((tm, tn), lambda i, j, k: (i, j)),
                    o_shape=(M, N), acc_shape=(tm, tn), ta=False, tb=False, name=name)


def _mm_nt(dy, w, name):
    M, N = dy.shape
    K = w.shape[0]
    tm, tn, tk = _pick(M, _TM), _pick(K, _TN), _pick(N, _TK)
    return _mm_call(dy, w, grid=(M // tm, K // tn, N // tk),
                    a_spec=pl.BlockSpec((tm, tk), lambda i, j, k: (i, k)),
                    b_spec=pl.BlockSpec((tn, tk), lambda i, j, k: (j, k)),
                    o_spec=pl.BlockSpec((tm, tn), lambda i, j, k: (i, j)),
                    o_shape=(M, K), acc_shape=(tm, tn), ta=False, tb=True, name=name)


def _mm_tn(x, dy, name):
    M, K = x.shape
    N = dy.shape[1]
    tm, tn, tk = _pick(K, _TN), _pick(N, _TN), _pick(M, _TK)
    return _mm_call(x, dy, grid=(K // tm, N // tn, M // tk),
                    a_spec=pl.BlockSpec((tk, tm), lambda i, j, k: (k, i)),
                    b_spec=pl.BlockSpec((tk, tn), lambda i, j, k: (k, j)),
                    o_spec=pl.BlockSpec((tm, tn), lambda i, j, k: (i, j)),
                    o_shape=(K, N), acc_shape=(tm, tn), ta=True, tb=False, name=name)


@functools.partial(jax.custom_vjp, nondiff_argnums=(3,))
def linear(x, w, wz, name):
    return _mm_nn(x, w, name)


def _linear_fwd(x, w, wz, name):
    return _mm_nn(x, w, name), (x, w)


def _linear_bwd(name, res, dy):
    x, w = res
    return _mm_nt(dy, w, name + "_dx"), jnp.zeros_like(w), _mm_tn(x, dy, name + "_dw")


linear.defvjp(_linear_fwd, _linear_bwd)


def _bd_nn(x, w, nblk, name):
    M = x.shape[0]
    kj, nj = x.shape[1] // nblk, w.shape[1]
    tm = _pick(M, _TM)
    return _mm_call(x, w, grid=(M // tm, nblk, 1),
                    a_spec=pl.BlockSpec((tm, kj), lambda i, j, k: (i, j)),
                    b_spec=pl.BlockSpec((kj, nj), lambda i, j, k: (j, 0)),
                    o_spec=pl.BlockSpec((tm, nj), lambda i, j, k: (i, j)),
                    o_shape=(M, nblk * nj), acc_shape=(tm, nj), ta=False, tb=False, name=name)


def _bd_nt(dy, w, nblk, name):
    M = dy.shape[0]
    nj = dy.shape[1] // nblk
    kj = w.shape[0] // nblk
    tm = _pick(M, _TM)
    return _mm_call(dy, w, grid=(M // tm, nblk, 1),
                    a_spec=pl.BlockSpec((tm, nj), lambda i, j, k: (i, j)),
                    b_spec=pl.BlockSpec((kj, nj), lambda i, j, k: (j, 0)),
                    o_spec=pl.BlockSpec((tm, kj), lambda i, j, k: (i, j)),
                    o_shape=(M, nblk * kj), acc_shape=(tm, kj), ta=False, tb=True, name=name)


def _bd_tn(x, dy, nblk, name):
    M = x.shape[0]
    kj, nj = x.shape[1] // nblk, dy.shape[1] // nblk
    tk = _pick(M, _TK)
    return _mm_call(x, dy, grid=(nblk, 1, M // tk),
                    a_spec=pl.BlockSpec((tk, kj), lambda i, j, k: (k, i)),
                    b_spec=pl.BlockSpec((tk, nj), lambda i, j, k: (k, i)),
                    o_spec=pl.BlockSpec((kj, nj), lambda i, j, k: (i, 0)),
                    o_shape=(nblk * kj, nj), acc_shape=(kj, nj), ta=True, tb=False, name=name)


@functools.partial(jax.custom_vjp, nondiff_argnums=(2, 3))
def bd_linear(x, w, nblk, name):
    return _bd_nn(x, w, nblk, name)


def _bd_fwd(x, w, nblk, name):
    return _bd_nn(x, w, nblk, name), (x, w)


def _bd_bwd(nblk, name, res, dy):
    x, w = res
    return _bd_nt(dy, w, nblk, name + "_dx"), _bd_tn(x, dy, nblk, name + "_dw")


bd_linear.defvjp(_bd_fwd, _bd_bwd)


def _row_tile(n_ctx, n_all):
    return _pick(math.gcd(n_ctx, n_all), (256, 128, 64, 32, 16, 8))


def _by_group(ref, is_ctx):
    return jnp.where(is_ctx, ref[0:1, :], ref[1:2, :])


def _acc_by_group(ref, is_ctx, part):
    ref[0:1, :] += jnp.where(is_ctx, part, 0.0)
    ref[1:2, :] += jnp.where(is_ctx, 0.0, part)


def _norm_fwd_call(x, g, shift, scale, n_ctx, name):
    M, D = x.shape
    has_mod = shift is not None
    tm = _row_tile(n_ctx, M) if has_mod else _pick(M, (256, 128, 64, 32, 16, 8))
    nct = n_ctx // tm

    def body(*refs):
        if has_mod:
            x_ref, g_ref, sh_ref, sc_ref, o_ref = refs
        else:
            x_ref, g_ref, o_ref = refs
        xv = x_ref[...]
        r = lax.rsqrt(jnp.mean(xv * xv, axis=-1, keepdims=True) + EPS)
        y = xv * r * g_ref[...]
        if has_mod:
            is_ctx = pl.program_id(0) < nct
            y = y * (1.0 + _by_group(sc_ref, is_ctx)) + _by_group(sh_ref, is_ctx)
        o_ref[...] = y

    row = pl.BlockSpec((tm, D), lambda i: (i, 0))
    vec = pl.BlockSpec((1, D), lambda i: (0, 0))
    two = pl.BlockSpec((2, D), lambda i: (0, 0))
    args = (x, g) + ((shift, scale) if has_mod else ())
    return pl.pallas_call(body, name=name, grid=(M // tm,), in_specs=[row, vec] + ([two, two] if has_mod else []),
                          out_specs=row, out_shape=jax.ShapeDtypeStruct((M, D), F32),
                          compiler_params=_params(("parallel",)))(*args)


def _norm_bwd_call(x, g, shift, scale, dy, n_ctx, name):
    M, D = x.shape
    has_mod = shift is not None
    tm = _row_tile(n_ctx, M) if has_mod else _pick(M, (256, 128, 64, 32, 16, 8))
    nct = n_ctx // tm

    def body(*refs):
        if has_mod:
            x_ref, g_ref, sc_ref, dy_ref, dx_ref, dg_ref, dsh_ref, dsc_ref = refs
        else:
            x_ref, g_ref, dy_ref, dx_ref, dg_ref = refs
        i = pl.program_id(0)

        @pl.when(i == 0)
        def _():
            dg_ref[...] = jnp.zeros_like(dg_ref)
            if has_mod:
                dsh_ref[...] = jnp.zeros_like(dsh_ref)
                dsc_ref[...] = jnp.zeros_like(dsc_ref)

        xv, gv, dyv = x_ref[...], g_ref[...], dy_ref[...]
        r = lax.rsqrt(jnp.mean(xv * xv, axis=-1, keepdims=True) + EPS)
        xhat = xv * r
        if has_mod:
            is_ctx = i < nct
            dy0 = dyv * (1.0 + _by_group(sc_ref, is_ctx))
            _acc_by_group(dsc_ref, is_ctx, jnp.sum(dyv * xhat * gv, axis=0, keepdims=True))
            _acc_by_group(dsh_ref, is_ctx, jnp.sum(dyv, axis=0, keepdims=True))
        else:
            dy0 = dyv
        dg_ref[...] += jnp.sum(dy0 * xhat, axis=0, keepdims=True)
        dxhat = dy0 * gv
        dx_ref[...] = r * (dxhat - xhat * jnp.mean(dxhat * xhat, axis=-1, keepdims=True))

    row = pl.BlockSpec((tm, D), lambda i: (i, 0))
    vec = pl.BlockSpec((1, D), lambda i: (0, 0))
    two = pl.BlockSpec((2, D), lambda i: (0, 0))
    if has_mod:
        args, in_specs = (x, g, scale, dy), [row, vec, two, row]
        out_specs = [row, vec, two, two]
        out_shape = [jax.ShapeDtypeStruct((M, D), F32), jax.ShapeDtypeStruct((1, D), F32),
                     jax.ShapeDtypeStruct((2, D), F32), jax.ShapeDtypeStruct((2, D), F32)]
    else:
        args, in_specs = (x, g, dy), [row, vec, row]
        out_specs = [row, vec]
        out_shape = [jax.ShapeDtypeStruct((M, D), F32), jax.ShapeDtypeStruct((1, D), F32)]
    return pl.pallas_call(body, name=name, grid=(M // tm,), in_specs=in_specs, out_specs=out_specs, out_shape=out_shape,
                          compiler_params=_params(("arbitrary",)))(*args)


@functools.partial(jax.custom_vjp, nondiff_argnums=(4, 5))
def norm_mod(x, g, shift, scale, n_ctx, name):
    return _norm_fwd_call(x, g, shift, scale, n_ctx, name)


def _norm_mod_fwd(x, g, shift, scale, n_ctx, name):
    return _norm_fwd_call(x, g, shift, scale, n_ctx, name), (x, g, shift, scale)


def _norm_mod_bwd(n_ctx, name, res, dy):
    x, g, shift, scale = res
    dx, dg, dsh, dsc = _norm_bwd_call(x, g, shift, scale, dy, n_ctx, name + "_bwd")
    return dx, dg, dsh, dsc


norm_mod.defvjp(_norm_mod_fwd, _norm_mod_bwd)


@functools.partial(jax.custom_vjp, nondiff_argnums=(2,))
def rmsnorm(x, g, name):
    return _norm_fwd_call(x, g, None, None, 0, name)


def _rmsnorm_fwd(x, g, name):
    return _norm_fwd_call(x, g, None, None, 0, name), (x, g)


def _rmsnorm_bwd(name, res, dy):
    x, g = res
    dx, dg = _norm_bwd_call(x, g, None, None, dy, 0, name + "_bwd")
    return dx, dg


rmsnorm.defvjp(_rmsnorm_fwd, _rmsnorm_bwd)


def _gres_fwd_call(h, o, gate, coef, n_ctx, name):
    M, D = h.shape
    tm = _row_tile(n_ctx, M)
    nct = n_ctx // tm

    def body(h_ref, o_ref, g_ref, out_ref):
        is_ctx = pl.program_id(0) < nct
        out_ref[...] = h_ref[...] + coef * _by_group(g_ref, is_ctx) * o_ref[...]

    row = pl.BlockSpec((tm, D), lambda i: (i, 0))
    two = pl.BlockSpec((2, D), lambda i: (0, 0))
    return pl.pallas_call(body, name=name, grid=(M // tm,), in_specs=[row, row, two], out_specs=row,
                          out_shape=jax.ShapeDtypeStruct((M, D), F32), compiler_params=_params(("parallel",)))(h, o, gate)


def _gres_bwd_call(o, gate, d, coef, n_ctx, name):
    M, D = o.shape
    tm = _row_tile(n_ctx, M)
    nct = n_ctx // tm

    def body(o_ref, g_ref, d_ref, do_ref, dg_ref):
        i = pl.program_id(0)
        is_ctx = i < nct

        @pl.when(i == 0)
        def _():
            dg_ref[...] = jnp.zeros_like(dg_ref)

        dv = d_ref[...]
        do_ref[...] = coef * _by_group(g_ref, is_ctx) * dv
        _acc_by_group(dg_ref, is_ctx, coef * jnp.sum(dv * o_ref[...], axis=0, keepdims=True))

    row = pl.BlockSpec((tm, D), lambda i: (i, 0))
    two = pl.BlockSpec((2, D), lambda i: (0, 0))
    return pl.pallas_call(body, name=name, grid=(M // tm,), in_specs=[row, two, row], out_specs=[row, two],
                          out_shape=[jax.ShapeDtypeStruct((M, D), F32), jax.ShapeDtypeStruct((2, D), F32)],
                          compiler_params=_params(("arbitrary",)))(o, gate, d)


@functools.partial(jax.custom_vjp, nondiff_argnums=(3, 4, 5))
def gated_residual(h, o, gate, coef, n_ctx, name):
    return _gres_fwd_call(h, o, gate, coef, n_ctx, name)


def _gres_fwd(h, o, gate, coef, n_ctx, name):
    return _gres_fwd_call(h, o, gate, coef, n_ctx, name), (o, gate)


def _gres_bwd(coef, n_ctx, name, res, d):
    o, gate = res
    do, dg = _gres_bwd_call(o, gate, d, coef, n_ctx, name + "_bwd")
    return d, do, dg


gated_residual.defvjp(_gres_fwd, _gres_bwd)


def _swiglu_fwd_call(ab, name):
    M, F2 = ab.shape
    Fh = F2 // 2
    tm, tn = _pick(M, (256, 128, 64, 32, 16, 8)), _pick(Fh, (512, 256, 128))
    nf = Fh // tn

    def body(a_ref, b_ref, o_ref):
        a = a_ref[...]
        o_ref[...] = a * jax.nn.sigmoid(a) * b_ref[...]

    return pl.pallas_call(body, name=name, grid=(M // tm, nf),
                          in_specs=[pl.BlockSpec((tm, tn), lambda i, j: (i, j)), pl.BlockSpec((tm, tn), lambda i, j: (i, j + nf))],
                          out_specs=pl.BlockSpec((tm, tn), lambda i, j: (i, j)),
                          out_shape=jax.ShapeDtypeStruct((M, Fh), F32), compiler_params=_params(("parallel", "parallel")))(ab, ab)


def _swiglu_bwd_call(ab, dact, name):
    M, F2 = ab.shape
    Fh = F2 // 2
    tm, tn = _pick(M, (256, 128, 64, 32, 16, 8)), _pick(Fh, (512, 256, 128))
    nf = Fh // tn

    def body(a_ref, b_ref, d_ref, o_ref):
        a, b, d = a_ref[...], b_ref[...], d_ref[...]
        sig = jax.nn.sigmoid(a)
        da = d * b * sig * (1.0 + a * (1.0 - sig))
        db = d * a * sig
        o_ref[...] = jnp.where(pl.program_id(1) < nf, da, db)

    return pl.pallas_call(body, name=name, grid=(M // tm, 2 * nf),
                          in_specs=[pl.BlockSpec((tm, tn), lambda i, j: (i, j % nf)),
                                    pl.BlockSpec((tm, tn), lambda i, j: (i, j % nf + nf)),
                                    pl.BlockSpec((tm, tn), lambda i, j: (i, j % nf))],
                          out_specs=pl.BlockSpec((tm, tn), lambda i, j: (i, j)),
                          out_shape=jax.ShapeDtypeStruct((M, F2), F32), compiler_params=_params(("parallel", "parallel")))(ab, ab, dact)


@functools.partial(jax.custom_vjp, nondiff_argnums=(1,))
def swiglu_act(ab, name):
    return _swiglu_fwd_call(ab, name)


def _swiglu_fwd(ab, name):
    return _swiglu_fwd_call(ab, name), (ab,)


def _swiglu_bwd(name, res, d):
    return (_swiglu_bwd_call(res[0], d, name + "_bwd"),)


swiglu_act.defvjp(_swiglu_fwd, _swiglu_bwd)


_DN_NT = (((1,), (1,)), ((), ()))
_DN_TN = (((0,), (0,)), ((), ()))
_DN_NN = (((1,), (0,)), ((), ()))


def _dot(a, b, dn):
    return lax.dot_general(a.astype(MXU_DTYPE), b.astype(MXU_DTYPE), dn, preferred_element_type=F32)


_TQ = (1024, 512, 256, 128, 64, 32, 16, 8)


def _flash_fwd_call(q, k1, v1, k2, v2, scale, name):
    H, Tq, dk = q.shape
    T1, dv = k1.shape[1], v1.shape[2]
    has2 = k2 is not None
    tq, tk = _pick(Tq, _TQ), _pick(T1, _TQ)
    off = 1 if has2 else 0
    nkv = T1 // tk + off

    def body(*refs):
        if has2:
            q_ref, k1_ref, v1_ref, k2_ref, v2_ref, o_ref, lse_ref, m_s, l_s, acc_s = refs
        else:
            q_ref, k1_ref, v1_ref, o_ref, lse_ref, m_s, l_s, acc_s = refs
        j = pl.program_id(2)

        @pl.when(j == 0)
        def _():
            m_s[...] = jnp.full_like(m_s, NEG_INF)
            l_s[...] = jnp.zeros_like(l_s)
            acc_s[...] = jnp.zeros_like(acc_s)

        def step(k, v):
            s = _dot(q_ref[0], k, _DN_NT) * scale
            m_prev = m_s[...]
            m_new = jnp.maximum(m_prev, jnp.max(s, axis=-1, keepdims=True))
            alpha = jnp.exp(m_prev - m_new)
            p = jnp.exp(s - m_new)
            l_s[...] = alpha * l_s[...] + jnp.sum(p, axis=-1, keepdims=True)
            acc_s[...] = alpha * acc_s[...] + _dot(p, v, _DN_NN)
            m_s[...] = m_new

        if has2:
            @pl.when(j == 0)
            def _():
                step(k2_ref[0], v2_ref[0])

            @pl.when(j > 0)
            def _():
                step(k1_ref[0], v1_ref[0])
        else:
            step(k1_ref[0], v1_ref[0])

        @pl.when(j == nkv - 1)
        def _():
            o_ref[0] = acc_s[...] / l_s[...]
            lse_ref[0] = m_s[...] + jnp.log(l_s[...])

    qs = pl.BlockSpec((1, tq, dk), lambda h, i, j: (h, i, 0))
    k1s = pl.BlockSpec((1, tk, dk), lambda h, i, j: (h, jnp.maximum(j - off, 0), 0))
    v1s = pl.BlockSpec((1, tk, dv), lambda h, i, j: (h, jnp.maximum(j - off, 0), 0))
    in_specs, args = [qs, k1s, v1s], [q, k1, v1]
    if has2:
        C = k2.shape[1]
        in_specs += [pl.BlockSpec((1, C, dk), lambda h, i, j: (h, 0, 0)), pl.BlockSpec((1, C, dv), lambda h, i, j: (h, 0, 0))]
        args += [k2, v2]
    return pl.pallas_call(
        body, name=name, grid=(H, Tq // tq, nkv), in_specs=in_specs,
        out_specs=[pl.BlockSpec((1, tq, dv), lambda h, i, j: (h, i, 0)), pl.BlockSpec((1, tq, 1), lambda h, i, j: (h, i, 0))],
        out_shape=[jax.ShapeDtypeStruct((H, Tq, dv), F32), jax.ShapeDtypeStruct((H, Tq, 1), F32)],
        scratch_shapes=[pltpu.VMEM((tq, 1), F32), pltpu.VMEM((tq, 1), F32), pltpu.VMEM((tq, dv), F32)],
        compiler_params=_params(("parallel", "parallel", "arbitrary")))(*args)


def _flash_bwd_call(q, k1, v1, k2, v2, o, lse, do, scale, name):
    H, Tq, dk = q.shape
    T1, dv = k1.shape[1], v1.shape[2]
    has2 = k2 is not None
    tq, tk = _pick(Tq, _TQ), _pick(T1, _TQ)
    off = 1 if has2 else 0
    nkv, nq = T1 // tk + off, Tq // tq
    C = k2.shape[1] if has2 else 0
    rows = max(tk, C)

    def body(*refs):
        if has2:
            (q_ref, k1_ref, v1_ref, k2_ref, v2_ref, o_ref, lse_ref, do_ref,
             dq_ref, dk1_ref, dv1_ref, dk2_ref, dv2_ref, dk_s, dv_s) = refs
        else:
            q_ref, k1_ref, v1_ref, o_ref, lse_ref, do_ref, dq_ref, dk1_ref, dv1_ref, dk_s, dv_s = refs
        j, i = pl.program_id(1), pl.program_id(2)

        @pl.when((j == 0) & (i == 0))
        def _():
            dq_ref[...] = jnp.zeros_like(dq_ref)

        @pl.when(i == 0)
        def _():
            dk_s[...] = jnp.zeros_like(dk_s)
            dv_s[...] = jnp.zeros_like(dv_s)

        def step(k, v, n):
            qb, dob = q_ref[0], do_ref[0]
            p = jnp.exp(_dot(qb, k, _DN_NT) * scale - lse_ref[0])
            delta = jnp.sum(dob * o_ref[0], axis=-1, keepdims=True)
            dv_s[0:n, :] += _dot(p, dob, _DN_TN)
            ds = p * (_dot(dob, v, _DN_NT) - delta) * scale
            dk_s[0:n, :] += _dot(ds, qb, _DN_TN)
            r0 = pl.multiple_of(i * tq, tq)
            dq_ref[0, pl.ds(r0, tq), :] += _dot(ds, k, _DN_NN)

        if has2:
            @pl.when(j == 0)
            def _():
                step(k2_ref[0], v2_ref[0], C)

            @pl.when(j > 0)
            def _():
                step(k1_ref[0], v1_ref[0], tk)

            @pl.when((i == nq - 1) & (j == 0))
            def _():
                dk2_ref[0] = dk_s[0:C, :]
                dv2_ref[0] = dv_s[0:C, :]

            @pl.when((i == nq - 1) & (j > 0))
            def _():
                dk1_ref[0] = dk_s[0:tk, :]
                dv1_ref[0] = dv_s[0:tk, :]
        else:
            step(k1_ref[0], v1_ref[0], tk)

            @pl.when(i == nq - 1)
            def _():
                dk1_ref[0] = dk_s[...]
                dv1_ref[0] = dv_s[...]

    qs = pl.BlockSpec((1, tq, dk), lambda h, j, i: (h, i, 0))
    os_ = pl.BlockSpec((1, tq, dv), lambda h, j, i: (h, i, 0))
    ls = pl.BlockSpec((1, tq, 1), lambda h, j, i: (h, i, 0))
    k1s = pl.BlockSpec((1, tk, dk), lambda h, j, i: (h, jnp.maximum(j - off, 0), 0))
    v1s = pl.BlockSpec((1, tk, dv), lambda h, j, i: (h, jnp.maximum(j - off, 0), 0))
    in_specs, args = [qs, k1s, v1s], [q, k1, v1]
    out_specs = [pl.BlockSpec((1, Tq, dk), lambda h, j, i: (h, 0, 0)), k1s, v1s]
    out_shape = [jax.ShapeDtypeStruct((H, Tq, dk), F32), jax.ShapeDtypeStruct((H, T1, dk), F32),
                 jax.ShapeDtypeStruct((H, T1, dv), F32)]
    if has2:
        k2s = pl.BlockSpec((1, C, dk), lambda h, j, i: (h, 0, 0))
        v2s = pl.BlockSpec((1, C, dv), lambda h, j, i: (h, 0, 0))
        in_specs += [k2s, v2s]
        args += [k2, v2]
        out_specs += [k2s, v2s]
        out_shape += [jax.ShapeDtypeStruct((H, C, dk), F32), jax.ShapeDtypeStruct((H, C, dv), F32)]
    in_specs += [os_, ls, os_]
    args += [o, lse, do]
    return pl.pallas_call(
        body, name=name, grid=(H, nkv, nq), in_specs=in_specs, out_specs=out_specs, out_shape=out_shape,
        scratch_shapes=[pltpu.VMEM((rows, dk), F32), pltpu.VMEM((rows, dv), F32)],
        compiler_params=_params(("parallel", "arbitrary", "arbitrary")))(*args)


@functools.partial(jax.custom_vjp, nondiff_argnums=(5, 6))
def flash2(q, k1, v1, k2, v2, scale, name):
    return _flash_fwd_call(q, k1, v1, k2, v2, scale, name)[0]


def _flash2_fwd(q, k1, v1, k2, v2, scale, name):
    o, lse = _flash_fwd_call(q, k1, v1, k2, v2, scale, name)
    return o, (q, k1, v1, k2, v2, o, lse)


def _flash2_bwd(scale, name, res, do):
    q, k1, v1, k2, v2, o, lse = res
    return tuple(_flash_bwd_call(q, k1, v1, k2, v2, o, lse, do, scale, name + "_bwd"))


flash2.defvjp(_flash2_fwd, _flash2_bwd)


@functools.partial(jax.custom_vjp, nondiff_argnums=(3, 4))
def flash1(q, k, v, scale, name):
    return _flash_fwd_call(q, k, v, None, None, scale, name)[0]


def _flash1_fwd(q, k, v, scale, name):
    o, lse = _flash_fwd_call(q, k, v, None, None, scale, name)
    return o, (q, k, v, o, lse)


def _flash1_bwd(scale, name, res, do):
    q, k, v, o, lse = res
    return tuple(_flash_bwd_call(q, k, v, None, None, o, lse, do, scale, name + "_bwd"))


flash1.defvjp(_flash1_fwd, _flash1_bwd)


def _scan_call(a, x, s, *, rev, adj, n_ctx, name):
    M, W = x.shape
    cw = SSM_CW
    J = W // (2 * cw)
    L = _row_tile(n_ctx, M)
    nt, nc = M // L, n_ctx // L
    asc = rev == adj
    n_steps = int(math.log2(L))
    assert 1 << n_steps == L

    def tile(t):
        if not rev:
            return nt - 1 - t if adj else t
        if not adj:
            return jnp.where(t < nc, nc - 1 - t, nt - 1 - (t - nc))
        return jnp.where(t < nt - nc, nc + t, t - (nt - nc))

    def body(*refs):
        if adj:
            a_ref, x_ref, s_ref, o_ref, da_ref, car_ref = refs
        else:
            a_ref, x_ref, o_ref, car_ref = refs
        t = pl.program_id(1)

        @pl.when(t == 0)
        def _():
            car_ref[...] = jnp.zeros_like(car_ref)
            if adj:
                da_ref[...] = jnp.zeros_like(da_ref)

        ar, ai = a_ref[:, 0:cw], a_ref[:, cw:2 * cw]
        cr, ci = car_ref[:, 0:cw], car_ref[:, cw:2 * cw]
        xr, xi = x_ref[:, 0:cw], x_ref[:, cw:2 * cw]
        row = lax.broadcasted_iota(jnp.int32, (L, cw), 0)
        first = 0 if asc else L - 1
        xr = xr + jnp.where(row == first, ar * cr - ai * ci, 0.0)
        xi = xi + jnp.where(row == first, ar * ci + ai * cr, 0.0)
        pr, pi = ar, ai
        k = 1
        for _ in range(n_steps):
            if asc:
                sr, si, keep = pltpu.roll(xr, k, 0), pltpu.roll(xi, k, 0), row >= k
            else:
                sr, si, keep = pltpu.roll(xr, L - k, 0), pltpu.roll(xi, L - k, 0), row < L - k
            sr, si = jnp.where(keep, sr, 0.0), jnp.where(keep, si, 0.0)
            xr, xi = xr + pr * sr - pi * si, xi + pr * si + pi * sr
            pr, pi = pr * pr - pi * pi, 2.0 * pr * pi
            k *= 2
        o_ref[:, 0:cw] = xr
        o_ref[:, cw:2 * cw] = xi
        if adj:
            if asc:
                gr, gi = pltpu.roll(xr, 1, 0), pltpu.roll(xi, 1, 0)
            else:
                gr, gi = pltpu.roll(xr, L - 1, 0), pltpu.roll(xi, L - 1, 0)
            gr, gi = jnp.where(row == first, cr, gr), jnp.where(row == first, ci, gi)
            sr, si = s_ref[:, 0:cw], s_ref[:, cw:2 * cw]
            da_ref[:, 0:cw] += jnp.sum(sr * gr + si * gi, axis=0, keepdims=True)
            da_ref[:, cw:2 * cw] += jnp.sum(sr * gi - si * gr, axis=0, keepdims=True)
        last = L - 1 if asc else 0
        car_ref[...] = o_ref[pl.ds(last, 1), :]

    blk = pl.BlockSpec((L, 2 * cw), lambda j, t: (tile(t), j))
    vec = pl.BlockSpec((1, 2 * cw), lambda j, t: (0, j))
    if adj:
        in_specs, args = [vec, blk, blk], (a, x, s)
        out_specs = [blk, vec]
        out_shape = [jax.ShapeDtypeStruct((M, W), F32), jax.ShapeDtypeStruct((1, W), F32)]
    else:
        in_specs, args = [vec, blk], (a, x)
        out_specs = blk
        out_shape = jax.ShapeDtypeStruct((M, W), F32)
    return pl.pallas_call(body, name=name, grid=(J, nt), in_specs=in_specs, out_specs=out_specs, out_shape=out_shape,
                          scratch_shapes=[pltpu.VMEM((1, 2 * cw), F32)],
                          compiler_params=_params(("parallel", "arbitrary")))(*args)


def _conj_layout(a):
    cw = SSM_CW
    J = a.shape[1] // (2 * cw)
    a4 = a.reshape(1, J, 2, cw)
    return jnp.concatenate([a4[:, :, 0:1], -a4[:, :, 1:2]], axis=2).reshape(a.shape)


@functools.partial(jax.custom_vjp, nondiff_argnums=(2, 3, 4))
def diag_scan(a, x, rev, n_ctx, name):
    return _scan_call(a, x, None, rev=rev, adj=False, n_ctx=n_ctx, name=name)


def _diag_scan_fwd(a, x, rev, n_ctx, name):
    s = _scan_call(a, x, None, rev=rev, adj=False, n_ctx=n_ctx, name=name)
    return s, (a, s)


def _diag_scan_bwd(rev, n_ctx, name, res, ds):
    a, s = res
    g, da = _scan_call(_conj_layout(a), ds, s, rev=rev, adj=True, n_ctx=n_ctx, name=name + "_adj")
    return da, g


diag_scan.defvjp(_diag_scan_fwd, _diag_scan_bwd)


def _loss_call(h, g, target, name):
    M, D = h.shape
    tm = _pick(M, (256, 128, 64, 32, 16, 8))

    def body(h_ref, g_ref, t_ref, loss_ref, dh_ref, dg_ref):
        i = pl.program_id(0)

        @pl.when(i == 0)
        def _():
            loss_ref[...] = jnp.zeros_like(loss_ref)
            dg_ref[...] = jnp.zeros_like(dg_ref)

        xv, gv = h_ref[...], g_ref[...]
        r = lax.rsqrt(jnp.mean(xv * xv, axis=-1, keepdims=True) + EPS)
        xhat = xv * r
        err = xhat * gv - t_ref[...]
        loss_ref[...] += 0.5 * jnp.sum(jnp.mean(err * err, axis=-1, keepdims=True), axis=0, keepdims=True)
        dy = err * (1.0 / D)
        dg_ref[...] += jnp.sum(dy * xhat, axis=0, keepdims=True)
        dxhat = dy * gv
        dh_ref[...] = r * (dxhat - xhat * jnp.mean(dxhat * xhat, axis=-1, keepdims=True))

    row = pl.BlockSpec((tm, D), lambda i: (i, 0))
    vec = pl.BlockSpec((1, D), lambda i: (0, 0))
    one = pl.BlockSpec((1, 1), lambda i: (0, 0))
    return pl.pallas_call(body, name=name, grid=(M // tm,), in_specs=[row, vec, row], out_specs=[one, row, vec],
                          out_shape=[jax.ShapeDtypeStruct((1, 1), F32), jax.ShapeDtypeStruct((M, D), F32),
                                     jax.ShapeDtypeStruct((1, D), F32)],
                          compiler_params=_params(("arbitrary",)))(h, g, target)


@functools.partial(jax.custom_vjp, nondiff_argnums=(3,))
def loss_head(h, g, target, name):
    return _loss_call(h, g, target, name)[0][0, 0]


def _loss_head_fwd(h, g, target, name):
    loss, dh, dg = _loss_call(h, g, target, name)
    return loss[0, 0], (dh, dg, target)


def _loss_head_bwd(name, res, ct):
    dh, dg, target = res
    return ct * dh, ct * dg, jnp.zeros_like(target)


loss_head.defvjp(_loss_head_fwd, _loss_head_bwd)


def _adamw_call(w, gstack, m, v, name):
    R, Cn = w.shape
    n = gstack.shape[0]
    tr = _pick(R, (64, 32, 16, 8))

    def body(w_ref, g_ref, m_ref, v_ref, go_ref, d_ref, mo_ref, vo_ref):
        g = g_ref[0]
        for s in range(1, n):
            g = g + g_ref[s]
        mn = ADAM_B1 * m_ref[...] + (1.0 - ADAM_B1) * g
        vn = ADAM_B2 * v_ref[...] + (1.0 - ADAM_B2) * (g * g)
        m_hat = mn / (1.0 - ADAM_B1 ** ADAM_STEP)
        v_hat = vn / (1.0 - ADAM_B2 ** ADAM_STEP)
        go_ref[...] = g
        d_ref[...] = -ADAM_LR * (m_hat / (jnp.sqrt(v_hat) + ADAM_EPS) + ADAM_WD * w_ref[...])
        mo_ref[...] = mn
        vo_ref[...] = vn

    blk = pl.BlockSpec((tr, Cn), lambda i: (i, 0))
    gblk = pl.BlockSpec((n, tr, Cn), lambda i: (0, i, 0))
    sds = jax.ShapeDtypeStruct((R, Cn), F32)
    return pl.pallas_call(body, name=name, grid=(R // tr,), in_specs=[blk, gblk, blk, blk], out_specs=[blk] * 4,
                          out_shape=[sds] * 4, compiler_params=_params(("parallel",)))(w, gstack, m, v)


MESH = pl.DeviceIdType.MESH
ANY = pl.BlockSpec(memory_space=pl.ANY)


def _place():
    return lax.axis_index("x"), lax.axis_index("y"), lax.axis_index("c")


def _sibling_exchange(v, name):
    n = v.shape[0]

    def body(v_ref, o_ref, send_sems, recv_sems):
        x, y, c = _place()
        copies = [pltpu.make_async_remote_copy(src_ref=v_ref.at[k], dst_ref=o_ref.at[k], send_sem=send_sems.at[k],
                                               recv_sem=recv_sems.at[k], device_id=(x, y, 1 - c), device_id_type=MESH)
                  for k in range(n)]
        for cp in copies:
            cp.start()
        for cp in copies:
            cp.wait()

    return pl.pallas_call(body, name=name, in_specs=[ANY], out_specs=ANY, out_shape=jax.ShapeDtypeStruct(v.shape, v.dtype),
                          scratch_shapes=[pltpu.SemaphoreType.DMA((n,)), pltpu.SemaphoreType.DMA((n,))])(v)


def _sibling_exchange_half(v, name):
    n = v.shape[0]

    def body(v_ref, o_ref, send_sems, recv_sems):
        x, y, c = _place()
        copies = [pltpu.make_async_remote_copy(src_ref=v_ref.at[k, pl.ds(1 - c, 1)], dst_ref=o_ref.at[k],
                                               send_sem=send_sems.at[k], recv_sem=recv_sems.at[k],
                                               device_id=(x, y, 1 - c), device_id_type=MESH)
                  for k in range(n)]
        for cp in copies:
            cp.start()
        for cp in copies:
            cp.wait()

    return pl.pallas_call(body, name=name, in_specs=[ANY], out_specs=ANY,
                          out_shape=jax.ShapeDtypeStruct((n, 1) + v.shape[2:], v.dtype),
                          scratch_shapes=[pltpu.SemaphoreType.DMA((n,)), pltpu.SemaphoreType.DMA((n,))])(v)


def _chip_exchange(v, same, name):
    out_shape = (N_CHIPS,) + (v.shape if same else v.shape[1:])

    def body(v_ref, o_ref, send_sems, recv_sems, local_sem):
        x, y, c = _place()
        me = 2 * x + y
        own = pltpu.make_async_copy(v_ref if same else v_ref.at[me], o_ref.at[me], local_sem)
        own.start()
        copies = []
        for k, (fx, fy) in enumerate(((1, 0), (0, 1), (1, 1))):
            px, py = jnp.where(fx == 1, 1 - x, x), jnp.where(fy == 1, 1 - y, y)
            src = v_ref if same else v_ref.at[2 * px + py]
            copies.append(pltpu.make_async_remote_copy(src_ref=src, dst_ref=o_ref.at[me], send_sem=send_sems.at[k],
                                                       recv_sem=recv_sems.at[k], device_id=(px, py, c), device_id_type=MESH))
        for cp in copies:
            cp.start()
        for cp in copies:
            cp.wait()
        own.wait()

    return pl.pallas_call(body, name=name, in_specs=[ANY], out_specs=ANY, out_shape=jax.ShapeDtypeStruct(out_shape, v.dtype),
                          scratch_shapes=[pltpu.SemaphoreType.DMA((3,)), pltpu.SemaphoreType.DMA((3,)), pltpu.SemaphoreType.DMA])(v)


def _all_gather(v, name):
    def body(v_ref, o_ref, send_sems, recv_sems, local_sem):
        x, y, c = _place()
        me = 4 * x + 2 * y + c
        own = pltpu.make_async_copy(v_ref, o_ref.at[me], local_sem)
        own.start()
        copies = []
        for k in range(1, 8):
            fx, fy, fc = (k >> 2) & 1, (k >> 1) & 1, k & 1
            peer = (jnp.where(fx == 1, 1 - x, x), jnp.where(fy == 1, 1 - y, y), jnp.where(fc == 1, 1 - c, c))
            copies.append(pltpu.make_async_remote_copy(src_ref=v_ref, dst_ref=o_ref.at[me], send_sem=send_sems.at[k - 1],
                                                       recv_sem=recv_sems.at[k - 1], device_id=peer, device_id_type=MESH))
        for cp in copies:
            cp.start()
        for cp in copies:
            cp.wait()
        own.wait()

    return pl.pallas_call(body, name=name, in_specs=[ANY], out_specs=ANY,
                          out_shape=jax.ShapeDtypeStruct((8,) + v.shape, v.dtype),
                          scratch_shapes=[pltpu.SemaphoreType.DMA((7,)), pltpu.SemaphoreType.DMA((7,)), pltpu.SemaphoreType.DMA])(v)


def _add_own_half(g, r, c, name):
    n, _, R, W = g.shape
    tr = _pick(R, (256, 128, 64, 32, 16, 8))

    def body(c_ref, g_ref, r_ref, o_ref):
        o_ref[...] = g_ref[0] + r_ref[0]

    grid_spec = pltpu.PrefetchScalarGridSpec(
        num_scalar_prefetch=1, grid=(n, R // tr),
        in_specs=[pl.BlockSpec((1, 1, tr, W), lambda p, i, c_ref: (p, c_ref[0], i, 0)),
                  pl.BlockSpec((1, 1, tr, W), lambda p, i, c_ref: (p, 0, i, 0))],
        out_specs=pl.BlockSpec((1, tr, W), lambda p, i, c_ref: (p, i, 0)))
    return pl.pallas_call(body, name=name, grid_spec=grid_spec, out_shape=jax.ShapeDtypeStruct((n, R, W), F32),
                          compiler_params=_params(("parallel", "parallel")))(c.reshape(1).astype(jnp.int32), g, r)


def _sum_stack(v, name):
    n, R, W = v.shape
    tr = _pick(R, (256, 128, 64, 32, 16, 8))

    def body(v_ref, o_ref):
        acc = v_ref[0]
        for s in range(1, n):
            acc = acc + v_ref[s]
        o_ref[...] = acc

    return pl.pallas_call(body, name=name, grid=(R // tr,), in_specs=[pl.BlockSpec((n, tr, W), lambda i: (0, i, 0))],
                          out_specs=pl.BlockSpec((tr, W), lambda i: (i, 0)), out_shape=jax.ShapeDtypeStruct((R, W), F32),
                          compiler_params=_params(("parallel",)))(v)


def _flat_rows(n, mult):
    rows = -(-n // FLAT_W)
    return -(-rows // mult) * mult


def _by_core(a, b, c):
    return lax.dynamic_index_in_dim(jnp.stack([a, b]), c, axis=0, keepdims=False)


def gather_weights(shards):
    _, _, c = _place()
    flat = jnp.concatenate([shards[n].astype(WIRE_DTYPE).reshape(-1) for n in BIG_NAMES])
    n_flat = flat.shape[0]
    rh = _flat_rows(n_flat, 32) // 2
    flat = jnp.pad(flat, (0, 2 * rh * FLAT_W - n_flat)).reshape(2, rh, FLAT_W)
    mine = lax.dynamic_index_in_dim(flat, c, axis=0, keepdims=False)
    got = _chip_exchange(mine, True, "gather_chips")
    other = _sibling_exchange(got, "gather_sibling")
    halves = jnp.stack([_by_core(got, other, c), _by_core(other, got, c)], axis=1)
    allflat = halves.reshape(N_CHIPS, 2 * rh * FLAT_W)
    out, off = {}, 0
    for name, axis in BIG:
        shp = shards[name].shape
        size = math.prod(shp)
        parts = allflat[:, off:off + size].reshape((N_CHIPS,) + shp)
        out[name] = jnp.concatenate([parts[p] for p in range(N_CHIPS)], axis=axis)
        off += size
    return out


def scatter_gradients(grads, shards):
    _, _, c = _place()
    cols = []
    for name, axis in BIG:
        cols.append(jnp.stack(jnp.split(grads[name], N_CHIPS, axis=axis)).reshape(N_CHIPS, -1))
    flat = jnp.concatenate(cols, axis=1)
    n_flat = flat.shape[1]
    rh = _flat_rows(n_flat, 16) // 2
    flat = jnp.pad(flat, ((0, 0), (0, 2 * rh * FLAT_W - n_flat))).reshape(N_CHIPS, 2, rh, FLAT_W)
    theirs = _sibling_exchange_half(flat, "scatter_sibling")
    pair = _add_own_half(flat, theirs, c, "scatter_pair_sum")
    got = _chip_exchange(pair, False, "scatter_chips")
    mine = _sum_stack(got, "scatter_chip_sum")
    other = _sibling_exchange(mine.reshape(1, rh, FLAT_W), "scatter_halves").reshape(rh, FLAT_W)
    full = jnp.stack([_by_core(mine, other, c), _by_core(other, mine, c)]).reshape(-1)
    out, off = {}, 0
    for name, _ in BIG:
        shp = shards[name].shape
        size = math.prod(shp)
        out[name] = full[off:off + size].reshape(shp)
        off += size
    return out


def _rope_tables(n_ctx, n_lat, n):
    t = jnp.arange(n_lat, dtype=jnp.int32)
    zero = jnp.zeros((n_ctx,), jnp.int32)
    row = jnp.concatenate([zero, t // GRID_W]).astype(F32)
    col = jnp.concatenate([zero, t % GRID_W]).astype(F32)
    half = n // 2
    inv = ROPE_BASE ** (-jnp.arange(0, half, 2, dtype=F32) / half)
    ang_r, ang_c = row[:, None, None] * inv, col[:, None, None] * inv
    return (jnp.cos(ang_r), jnp.sin(ang_r)), (jnp.cos(ang_c), jnp.sin(ang_c))


def _rot(x, cs):
    cos, sin = cs
    h = x.shape[-1] // 2
    x1, x2 = x[..., :h], x[..., h:]
    return jnp.concatenate([x1 * cos - x2 * sin, x1 * sin + x2 * cos], axis=-1)


def _axial_rope(x, tables):
    h = x.shape[-1] // 2
    return jnp.concatenate([_rot(x[..., :h], tables[0]), _rot(x[..., h:], tables[1])], axis=-1)


def _cmul(ar, ai, br, bi):
    return ar * br - ai * bi, ar * bi + ai * br


def _ssm_discretize(lam_re, lam_im, log_dt, b_re, b_im):
    dt = jnp.exp(log_dt)[:, None]
    mag = jnp.exp(lam_re * dt)
    a_re, a_im = mag * jnp.cos(lam_im * dt), mag * jnp.sin(lam_im * dt)
    den = lam_re * lam_re + lam_im * lam_im
    w_re = ((a_re - 1) * lam_re + a_im * lam_im) / den
    w_im = (a_im * lam_re - (a_re - 1) * lam_im) / den
    bb_re, bb_im = _cmul(w_re[..., None], w_im[..., None], b_re, b_im)
    return a_re, a_im, bb_re, bb_im


def _ssm_layouts(a_re, a_im, bb_re, bb_im, c_re, c_im):
    J, g8, P, Mg = SSM_CHUNKS, SSM_CHUNK_GROUPS, SSM_STATE, SSM_GROUP
    eye = jnp.eye(g8, dtype=F32)
    a = jnp.stack([a_re.reshape(J, g8 * P), a_im.reshape(J, g8 * P)], axis=1).reshape(1, J * 2 * g8 * P)
    bb = jnp.stack([bb_re, bb_im]).reshape(2, J, g8, P, Mg)
    w_drive = jnp.einsum('rjgpm,gh->jgmrhp', bb, eye).reshape(J * g8 * Mg, 2 * g8 * P)
    cc = jnp.stack([c_re, -c_im]).reshape(2, J, g8, Mg, P)
    w_read = jnp.einsum('rjgmp,gh->jrhpgm', cc, eye).reshape(J * 2 * g8 * P, g8 * Mg)
    return a, w_drive, w_read


def _sink_softmax(score_list, sink_logit):
    m = sink_logit
    for s in score_list:
        m = jnp.maximum(m, s.max(axis=-1, keepdims=True))
    e = [jnp.exp(s - m) for s in score_list]
    denom = jnp.exp(sink_logit - m)
    for t in e:
        denom = denom + t.sum(axis=-1, keepdims=True)
    return [t / denom for t in e]


def _window_gqa(q, k, v, kc, vc, sink):
    T, H, d = q.shape
    G = H // GQA_KV_HEADS
    nb = T // BLOCK
    scale = d ** -0.5
    qb = q.reshape(nb, BLOCK, GQA_KV_HEADS, G, d)

    def band(t):
        tb = t.reshape(nb, BLOCK, GQA_KV_HEADS, d)
        tp = jnp.pad(tb, ((1, 1), (0, 0), (0, 0), (0, 0)))
        return jnp.concatenate([tp[:-2], tp[1:-1], tp[2:]], axis=1)

    kb, vb = band(k), band(v)
    s_band = jnp.einsum('nqhgd,nkhd->hgnqk', qb, kb, preferred_element_type=F32) * scale
    blk = jnp.arange(nb)[:, None, None]
    qpos = blk * BLOCK + jnp.arange(BLOCK)[None, :, None]
    kpos = (blk - 1) * BLOCK + jnp.arange(3 * BLOCK)[None, None, :]
    valid = (jnp.abs(qpos - kpos) <= WINDOW) & (kpos >= 0) & (kpos < T)
    s_band = jnp.where(valid, s_band, NEG_INF)
    s_ctx = jnp.einsum('nqhgd,chd->hgnqc', qb, kc, preferred_element_type=F32) * scale
    sk = sink.reshape(GQA_KV_HEADS, G)[:, :, None, None, None]
    p_band, p_ctx = _sink_softmax([s_band, s_ctx], sk)
    o = jnp.einsum('hgnqk,nkhd->nqhgd', p_band, vb) + jnp.einsum('hgnqc,chd->nqhgd', p_ctx, vc)
    return o.reshape(T, H * d)


def _context_gqa(qc, kc, vc, sink):
    C, H, d = qc.shape
    G = H // GQA_KV_HEADS
    qg = qc.reshape(C, GQA_KV_HEADS, G, d)
    s = jnp.einsum('qhgd,khd->hgqk', qg, kc, preferred_element_type=F32) * d ** -0.5
    sk = sink.reshape(GQA_KV_HEADS, G)[:, :, None, None]
    (p,) = _sink_softmax([s], sk)
    return jnp.einsum('hgqk,khd->qhgd', p, vc).reshape(C, H * d)


def _w_in_layout(d_model):
    widths = (("cq", MLA_Q_RANK), ("ckv", MLA_KV_RANK), ("kr", MLA_ROPE), ("u", SSM_WIDTH), ("gq", GQA_HEADS * GQA_HEAD_DIM),
              ("gk", GQA_KV_HEADS * GQA_HEAD_DIM), ("gv", GQA_KV_HEADS * GQA_HEAD_DIM), ("gates", N_BRANCH * d_model))
    out, src, dst = [], 0, 0
    for name, w in widths:
        out.append((name, src, dst, w))
        src += w
        dst += -(-w // LANES) * LANES
    return out, src, dst


def _pad_w_in(w, d_model):
    lay, _, _ = _w_in_layout(d_model)
    parts = []
    for _, src, _, wd in lay:
        seg = w[..., src:src + wd]
        pad = -(-wd // LANES) * LANES - wd
        parts.append(jnp.pad(seg, [(0, 0)] * (w.ndim - 1) + [(0, pad)]) if pad else seg)
    return jnp.concatenate(parts, axis=-1)


def _unpad_w_in(w, d_model):
    lay, _, _ = _w_in_layout(d_model)
    return jnp.concatenate([w[..., dst:dst + wd] for _, _, dst, wd in lay], axis=-1)


def _layer(hall, lw, lz, sp, cs8, n_ctx, ropes):
    M, D = hall.shape
    C = n_ctx
    rope32, rope64 = ropes

    def lin(x, name):
        return linear(x, lw[name], lz[name], name)

    mod_all = lin(cs8, "ada_w")[0:2] + sp["ada_b"][None, :]
    mod = [mod_all[:, i * D:(i + 1) * D] for i in range(N_MOD)]

    def ffn(h, tag, norm_g, sh, sc, gate):
        hn = norm_mod(h, norm_g[None, :], sh, sc, C, tag + "_norm")
        act = swiglu_act(lin(hn, tag + "_w13"), tag + "_act")
        return gated_residual(h, lin(act, tag + "_w2"), gate, 0.5, C, tag + "_res")

    hall = ffn(hall, "ffn1", sp["norm_ffn1"], mod[0], mod[1], mod[2])

    xm = norm_mod(hall, sp["norm_mix"][None, :], mod[3], mod[4], C, "mix_norm")
    proj = lin(xm, "w_in")
    lay, _, _ = _w_in_layout(D)
    seg = {name: proj[:, dst:dst + wd] for name, _, dst, wd in lay}

    q = lin(rmsnorm(seg["cq"], sp["mla_q_norm"][None, :], "mla_q_norm"), "mla_w_uq").reshape(M, MLA_HEADS, MLA_NOPE + MLA_ROPE)
    q = jnp.concatenate([q[..., :MLA_NOPE], _axial_rope(q[..., MLA_NOPE:], rope32)], axis=-1)
    kv = lin(rmsnorm(seg["ckv"], sp["mla_kv_norm"][None, :], "mla_kv_norm"), "mla_w_ukv").reshape(M, MLA_HEADS, MLA_NOPE + MLA_V)
    kr = _axial_rope(seg["kr"][:, None, :], rope32)
    k = jnp.concatenate([kv[..., :MLA_NOPE], jnp.broadcast_to(kr, (M, MLA_HEADS, MLA_ROPE))], axis=-1)
    qh, kh, vh = (t.transpose(1, 0, 2) for t in (q, k, kv[..., MLA_NOPE:]))
    scale = (MLA_NOPE + MLA_ROPE) ** -0.5
    o_lat = flash2(qh[:, C:], kh[:, C:], vh[:, C:], kh[:, :C], vh[:, :C], scale, "mla_lat")
    o_ctx = flash1(qh[:, :C], kh[:, :C], vh[:, :C], scale, "mla_ctx")
    o = jnp.concatenate([o_ctx, o_lat], axis=1).transpose(1, 0, 2).reshape(M, MLA_HEADS * MLA_V)
    mla = lin(o, "mla_w_o")

    u = seg["u"]
    y = u * sp["ssm_d"][None, :]
    for direction in range(2):
        a_re, a_im, bb_re, bb_im = _ssm_discretize(sp["ssm_lambda_re"][direction], sp["ssm_lambda_im"][direction],
                                                   sp["ssm_log_dt"][direction], sp["ssm_b_re"][direction],
                                                   sp["ssm_b_im"][direction])
        a, w_drive, w_read = _ssm_layouts(a_re, a_im, bb_re, bb_im, sp["ssm_c_re"][direction], sp["ssm_c_im"][direction])
        drive = bd_linear(u, w_drive, SSM_CHUNKS, "ssm_drive%d" % direction)
        states = diag_scan(a, drive, direction == 1, C, "ssm_scan%d" % direction)
        y = y + bd_linear(states, w_read, SSM_CHUNKS, "ssm_read%d" % direction)
    zz = lin(jax.nn.gelu(y), "ssm_w_glu")
    ssm = zz[:, :D] * jax.nn.sigmoid(zz[:, D:])

    gq = _axial_rope(seg["gq"].reshape(M, GQA_HEADS, GQA_HEAD_DIM), rope64)
    gk = _axial_rope(seg["gk"].reshape(M, GQA_KV_HEADS, GQA_HEAD_DIM), rope64)
    gv = seg["gv"].reshape(M, GQA_KV_HEADS, GQA_HEAD_DIM)
    g_lat = _window_gqa(gq[C:], gk[C:], gv[C:], gk[:C], gv[:C], sp["gqa_sink"])
    g_ctx = _context_gqa(gq[:C], gk[:C], gv[:C], sp["gqa_sink"])
    gqa = lin(jnp.concatenate([g_ctx, g_lat], axis=0), "gqa_w_o")

    gates = jax.nn.sigmoid(seg["gates"])
    mixed = gates[:, :D] * mla + gates[:, D:2 * D] * ssm + gates[:, 2 * D:] * gqa
    hall = gated_residual(hall, lin(mixed, "w_out"), mod[5], 1.0, C, "mix_res")

    return ffn(hall, "ffn2", sp["norm_ffn2"], mod[6], mod[7], mod[8])


PER_LAYER_SMALL = tuple(n for n in SMALL if n not in ("c_ctx", "final_norm"))


def _loss_fn(diff, x, c, ctx, target, whole):
    zeros, small, x = diff
    T, D = x.shape
    C = ctx.shape[0]
    ropes = (_rope_tables(C, T, MLA_ROPE), _rope_tables(C, T, GQA_HEAD_DIM))
    cs = jax.nn.silu(jnp.stack([small["c_ctx"], c]))
    cs8 = jnp.pad(cs, ((0, 6), (0, 0)))
    hall = jnp.concatenate([ctx, x], axis=0)

    def step(h, xs):
        lw, lz, sp = xs
        return _layer(h, lw, lz, sp, cs8, C, ropes), None

    hall, _ = lax.scan(step, hall, (whole, zeros, {n: small[n] for n in PER_LAYER_SMALL}))
    return loss_head(hall[C:], small["final_norm"][None, :], target, "loss_head")


def kernel(*args):
    given = dict(zip(ARG_NAMES + ['loss_target'] + ['m_' + n for n in WEIGHTS] + ['v_' + n for n in WEIGHTS], args))
    x, c, ctx, target = given['x'][0], given['c'][0], given['ctx'][0], given['loss_target'][0]
    D = x.shape[-1]
    shards = {n: given[n] for n in BIG_NAMES}
    small = {n: given[n] for n in SMALL}

    whole = gather_weights(shards)
    whole["w_in"] = _pad_w_in(whole["w_in"], D)
    zeros = {n: jnp.zeros(whole[n].shape, F32) for n in BIG_NAMES}

    loss, (gz, gsmall, gx) = jax.value_and_grad(_loss_fn)((zeros, small, x), x, c, ctx, target, whole)
    loss = lax.psum(loss, ("x", "y", "c"))
    gz["w_in"] = _unpad_w_in(gz["w_in"], D)
    gbig = scatter_gradients(gz, shards)

    res = {}
    for name in BIG_NAMES:
        shp = given[name].shape
        two_d = (shp[0] * shp[1], shp[2])
        outs = _adamw_call(given[name].reshape(two_d), gbig[name].reshape((1,) + two_d), given['m_' + name].reshape(two_d),
                           given['v_' + name].reshape(two_d), "adamw_" + name)
        res[name] = [o.reshape(shp) for o in outs]

    def flat_small(d):
        v = jnp.concatenate([d[n].reshape(-1) for n in SMALL])
        rows = _flat_rows(v.shape[0], 8)
        return jnp.pad(v, (0, rows * FLAT_W - v.shape[0])).reshape(rows, FLAT_W)

    gathered = _all_gather(flat_small(gsmall), "small_gather")
    outs = _adamw_call(flat_small(small), gathered, flat_small({n: given['m_' + n] for n in SMALL}),
                       flat_small({n: given['v_' + n] for n in SMALL}), "adamw_small")
    off = 0
    for name in SMALL:
        shp = given[name].shape
        size = math.prod(shp)
        res[name] = [o.reshape(-1)[off:off + size].reshape(shp) for o in outs]
        off += size

    return (loss, gx[None], *[res[n][0] for n in WEIGHTS], *[res[n][1] for n in WEIGHTS],
            *[res[n][2] for n in WEIGHTS], *[res[n][3] for n in WEIGHTS])
```

```python
import functools
import math

import jax
import jax.numpy as jnp
from jax import lax
from jax.experimental import pallas as pl
from jax.experimental.pallas import tpu as pltpu

F32 = jnp.float32
MXU_DTYPE = jnp.bfloat16
WIRE_DTYPE = jnp.bfloat16

GRID_W = 64
MLA_HEADS, MLA_NOPE, MLA_ROPE, MLA_V = 8, 64, 32, 64
MLA_Q_RANK, MLA_KV_RANK = 384, 256
SSM_WIDTH, SSM_GROUP, SSM_STATE = 512, 16, 64
SSM_GROUPS = SSM_WIDTH // SSM_GROUP
SSM_CHUNK_GROUPS = 8
SSM_CHUNKS = SSM_GROUPS // SSM_CHUNK_GROUPS
SSM_CW = SSM_CHUNK_GROUPS * SSM_STATE
GQA_HEADS, GQA_KV_HEADS, GQA_HEAD_DIM = 8, 2, 64
WINDOW, BLOCK = 128, 128
N_BRANCH, N_MOD = 3, 9
ROPE_BASE = 10000.0
EPS = 1e-6
NEG_INF = -1e30
LANES = 128
FLAT_W = 1024

ADAM_LR, ADAM_B1, ADAM_B2, ADAM_EPS, ADAM_WD, ADAM_STEP = 0.001, 0.9, 0.999, 1e-08, 0.01, 10

VMEM_LIMIT = 48 * 1024 * 1024

ARG_NAMES = ['x', 'c', 'ctx', 'c_ctx', 'ada_w', 'ada_b', 'norm_ffn1', 'norm_mix', 'norm_ffn2', 'ffn1_w13', 'ffn1_w2', 'ffn2_w13', 'ffn2_w2', 'w_in', 'mla_q_norm', 'mla_kv_norm', 'mla_w_uq', 'mla_w_ukv', 'mla_w_o', 'ssm_lambda_re', 'ssm_lambda_im', 'ssm_log_dt', 'ssm_b_re', 'ssm_b_im', 'ssm_c_re', 'ssm_c_im', 'ssm_d', 'ssm_w_glu', 'gqa_sink', 'gqa_w_o', 'w_out', 'final_norm']
WEIGHTS = ARG_NAMES[3:]
BIG = (('ada_w', 2), ('ffn1_w13', 2), ('ffn1_w2', 1), ('ffn2_w13', 2), ('ffn2_w2', 1), ('w_in', 2), ('mla_w_uq', 2),
       ('mla_w_ukv', 2), ('mla_w_o', 2), ('ssm_w_glu', 2), ('gqa_w_o', 2), ('w_out', 1))
BIG_NAMES = tuple(n for n, _ in BIG)
SMALL = tuple(n for n in WEIGHTS if n not in BIG_NAMES)
N_CHIPS = 4


def _pick(n, prefs):
    for p in prefs:
        if n % p == 0:
            return p
    return n


def _params(sem=None):
    return pltpu.CompilerParams(dimension_semantics=sem, vmem_limit_bytes=VMEM_LIMIT)


def _mm_call(a, b, *, grid, a_spec, b_spec, o_spec, o_shape, acc_shape, ta, tb, name, out_dtype=F32):
    nk = grid[2]
    dn = (((0 if ta else 1,), (1 if tb else 0,)), ((), ()))

    def body(a_ref, b_ref, o_ref, acc_ref):
        k = pl.program_id(2)

        @pl.when(k == 0)
        def _():
            acc_ref[...] = jnp.zeros_like(acc_ref)

        acc_ref[...] += lax.dot_general(a_ref[...].astype(MXU_DTYPE), b_ref[...].astype(MXU_DTYPE), dn,
                                        preferred_element_type=F32)

        @pl.when(k == nk - 1)
        def _():
            o_ref[...] = acc_ref[...].astype(o_ref.dtype)

    return pl.pallas_call(
        body, name=name, grid=grid, in_specs=[a_spec, b_spec], out_specs=o_spec,
        out_shape=jax.ShapeDtypeStruct(o_shape, out_dtype), scratch_shapes=[pltpu.VMEM(acc_shape, F32)],
        compiler_params=_params(("parallel", "parallel", "arbitrary")))(a, b)


_ROWS = (768, 512, 256, 128, 64, 32, 16, 8)
_WIDE = (1408, 1024, 512, 256, 128)
MAX_WHOLE = 2816


def _feat(n):
    return n if n <= _WIDE[0] else _pick(n, _WIDE)


def _mm_nn(x, w, name):
    M, K = x.shape
    N = w.shape[1]
    tm, tn = _pick(M, _ROWS), _pick(N, (512, 256, 128))
    tk = K if K <= MAX_WHOLE else _pick(K, (512, 256, 128))
    return _mm_call(x, w, grid=(M // tm, N // tn, K // tk),
                    a_spec=pl.BlockSpec((tm, tk), lambda i, j, k: (i, k)),
                    b_spec=pl.BlockSpec((tk, tn), lambda i, j, k: (k, j)),
                    o_spec=pl.BlockSpec((tm, tn), lambda i, j, k: (i, j)),
                    o_shape=(M, N), acc_shape=(tm, tn), ta=False, tb=False, name=name)


def _mm_nt(dy, w, name):
    M, N = dy.shape
    K = w.shape[0]
    tm, tn, tk = _pick(M, _ROWS), _feat(K), _feat(N)
    return _mm_call(dy, w, grid=(M // tm, K // tn, N // tk),
                    a_spec=pl.BlockSpec((tm, tk), lambda i, j, k: (i, k)),
                    b_spec=pl.BlockSpec((tn, tk), lambda i, j, k: (j, k)),
                    o_spec=pl.BlockSpec((tm, tn), lambda i, j, k: (i, j)),
                    o_shape=(M, K), acc_shape=(tm, tn), ta=False, tb=True, name=name)


def _mm_tn(x, dy, name):
    M, K = x.shape
    N = dy.shape[1]
    tm, tn, tk = _feat(K), _feat(N), _pick(M, (256, 128, 64, 32, 16, 8))
    return _mm_call(x, dy, grid=(K // tm, N // tn, M // tk),
                    a_spec=pl.BlockSpec((tk, tm), lambda i, j, k: (k, i)),
                    b_spec=pl.BlockSpec((tk, tn), lambda i, j, k: (k, j)),
                    o_spec=pl.BlockSpec((tm, tn), lambda i, j, k: (i, j)),
                    o_shape=(K, N), acc_shape=(tm, tn), ta=True, tb=False, name=name)


@functools.partial(jax.custom_vjp, nondiff_argnums=(3,))
def linear(x, w, wz, name):
    return _mm_nn(x, w, name)


def _linear_fwd(x, w, wz, name):
    return _mm_nn(x, w, name), (x, w)


def _linear_bwd(name, res, dy):
    x, w = res
    return _mm_nt(dy, w, name + "_dx"), jnp.zeros_like(w), _mm_tn(x, dy, name + "_dw")


linear.defvjp(_linear_fwd, _linear_bwd)


_BD_ROWS = (256, 128, 64, 32, 16, 8)


def _bd_call(a, b, nblk, kind, name):
    M = a.shape[0]
    tm = _pick(M, _BD_ROWS)
    if kind == "tn":
        aj, bj = a.shape[1] // nblk, b.shape[1] // nblk
        o_shape, o_spec = (nblk * aj, bj), pl.BlockSpec((nblk * aj, bj), lambda i: (0, 0))
        b_spec = pl.BlockSpec((tm, b.shape[1]), lambda i: (i, 0))
    else:
        aj = a.shape[1] // nblk
        wj = b.shape[0] // nblk
        oj = b.shape[1] if kind == "nn" else wj
        o_shape, o_spec = (M, nblk * oj), pl.BlockSpec((tm, nblk * oj), lambda i: (i, 0))
        b_spec = pl.BlockSpec(b.shape, lambda i: (0, 0))

    def body(a_ref, b_ref, o_ref):
        if kind == "tn":
            @pl.when(pl.program_id(0) == 0)
            def _():
                o_ref[...] = jnp.zeros_like(o_ref)

        for j in range(nblk):
            if kind == "nn":
                o_ref[:, j * oj:(j + 1) * oj] = _dot(a_ref[:, j * aj:(j + 1) * aj], b_ref[j * wj:(j + 1) * wj, :], _DN_NN)
            elif kind == "nt":
                o_ref[:, j * oj:(j + 1) * oj] = _dot(a_ref[:, j * aj:(j + 1) * aj], b_ref[j * wj:(j + 1) * wj, :], _DN_NT)
            else:
                o_ref[j * aj:(j + 1) * aj, :] += _dot(a_ref[:, j * aj:(j + 1) * aj], b_ref[:, j * bj:(j + 1) * bj], _DN_TN)

    return pl.pallas_call(body, name=name, grid=(M // tm,), in_specs=[pl.BlockSpec((tm, a.shape[1]), lambda i: (i, 0)), b_spec],
                          out_specs=o_spec, out_shape=jax.ShapeDtypeStruct(o_shape, F32),
                          compiler_params=_params(("arbitrary",) if kind == "tn" else ("parallel",)))(a, b)


def _bd_nn(x, w, nblk, name):
    return _bd_call(x, w, nblk, "nn", name)


def _bd_nt(dy, w, nblk, name):
    return _bd_call(dy, w, nblk, "nt", name)


def _bd_tn(x, dy, nblk, name):
    return _bd_call(x, dy, nblk, "tn", name)


@functools.partial(jax.custom_vjp, nondiff_argnums=(2, 3))
def bd_linear(x, w, nblk, name):
    return _bd_nn(x, w, nblk, name)


def _bd_fwd(x, w, nblk, name):
    return _bd_nn(x, w, nblk, name), (x, w)


def _bd_bwd(nblk, name, res, dy):
    x, w = res
    return _bd_nt(dy, w, nblk, name + "_dx"), _bd_tn(x, dy, nblk, name + "_dw")


bd_linear.defvjp(_bd_fwd, _bd_bwd)


def _row_tile(n_ctx, n_all):
    return _pick(math.gcd(n_ctx, n_all), (256, 128, 64, 32, 16, 8))


def _by_group(ref, is_ctx):
    return jnp.where(is_ctx, ref[0:1, :], ref[1:2, :])


def _acc_by_group(ref, is_ctx, part):
    ref[0:1, :] += jnp.where(is_ctx, part, 0.0)
    ref[1:2, :] += jnp.where(is_ctx, 0.0, part)


def _norm_fwd_call(x, g, shift, scale, n_ctx, name):
    M, D = x.shape
    has_mod = shift is not None
    tm = _row_tile(n_ctx, M) if has_mod else _pick(M, (256, 128, 64, 32, 16, 8))
    nct = n_ctx // tm

    def body(*refs):
        if has_mod:
            x_ref, g_ref, sh_ref, sc_ref, o_ref = refs
        else:
            x_ref, g_ref, o_ref = refs
        xv = x_ref[...]
        r = lax.rsqrt(jnp.mean(xv * xv, axis=-1, keepdims=True) + EPS)
        y = xv * r * g_ref[...]
        if has_mod:
            is_ctx = pl.program_id(0) < nct
            y = y * (1.0 + _by_group(sc_ref, is_ctx)) + _by_group(sh_ref, is_ctx)
        o_ref[...] = y

    row = pl.BlockSpec((tm, D), lambda i: (i, 0))
    vec = pl.BlockSpec((1, D), lambda i: (0, 0))
    two = pl.BlockSpec((2, D), lambda i: (0, 0))
    args = (x, g) + ((shift, scale) if has_mod else ())
    return pl.pallas_call(body, name=name, grid=(M // tm,), in_specs=[row, vec] + ([two, two] if has_mod else []),
                          out_specs=row, out_shape=jax.ShapeDtypeStruct((M, D), F32),
                          compiler_params=_params(("parallel",)))(*args)


def _norm_bwd_call(x, g, shift, scale, dy, n_ctx, name):
    M, D = x.shape
    has_mod = shift is not None
    tm = _row_tile(n_ctx, M) if has_mod else _pick(M, (256, 128, 64, 32, 16, 8))
    nct = n_ctx // tm

    def body(*refs):
        if has_mod:
            x_ref, g_ref, sc_ref, dy_ref, dx_ref, dg_ref, dsh_ref, dsc_ref = refs
        else:
            x_ref, g_ref, dy_ref, dx_ref, dg_ref = refs
        i = pl.program_id(0)

        @pl.when(i == 0)
        def _():
            dg_ref[...] = jnp.zeros_like(dg_ref)
            if has_mod:
                dsh_ref[...] = jnp.zeros_like(dsh_ref)
                dsc_ref[...] = jnp.zeros_like(dsc_ref)

        xv, gv, dyv = x_ref[...], g_ref[...], dy_ref[...]
        r = lax.rsqrt(jnp.mean(xv * xv, axis=-1, keepdims=True) + EPS)
        xhat = xv * r
        if has_mod:
            is_ctx = i < nct
            dy0 = dyv * (1.0 + _by_group(sc_ref, is_ctx))
            _acc_by_group(dsc_ref, is_ctx, jnp.sum(dyv * xhat * gv, axis=0, keepdims=True))
            _acc_by_group(dsh_ref, is_ctx, jnp.sum(dyv, axis=0, keepdims=True))
        else:
            dy0 = dyv
        dg_ref[...] += jnp.sum(dy0 * xhat, axis=0, keepdims=True)
        dxhat = dy0 * gv
        dx_ref[...] = r * (dxhat - xhat * jnp.mean(dxhat * xhat, axis=-1, keepdims=True))

    row = pl.BlockSpec((tm, D), lambda i: (i, 0))
    vec = pl.BlockSpec((1, D), lambda i: (0, 0))
    two = pl.BlockSpec((2, D), lambda i: (0, 0))
    if has_mod:
        args, in_specs = (x, g, scale, dy), [row, vec, two, row]
        out_specs = [row, vec, two, two]
        out_shape = [jax.ShapeDtypeStruct((M, D), F32), jax.ShapeDtypeStruct((1, D), F32),
                     jax.ShapeDtypeStruct((2, D), F32), jax.ShapeDtypeStruct((2, D), F32)]
    else:
        args, in_specs = (x, g, dy), [row, vec, row]
        out_specs = [row, vec]
        out_shape = [jax.ShapeDtypeStruct((M, D), F32), jax.ShapeDtypeStruct((1, D), F32)]
    return pl.pallas_call(body, name=name, grid=(M // tm,), in_specs=in_specs, out_specs=out_specs, out_shape=out_shape,
                          compiler_params=_params(("arbitrary",)))(*args)


@functools.partial(jax.custom_vjp, nondiff_argnums=(4, 5))
def norm_mod(x, g, shift, scale, n_ctx, name):
    return _norm_fwd_call(x, g, shift, scale, n_ctx, name)


def _norm_mod_fwd(x, g, shift, scale, n_ctx, name):
    return _norm_fwd_call(x, g, shift, scale, n_ctx, name), (x, g, shift, scale)


def _norm_mod_bwd(n_ctx, name, res, dy):
    x, g, shift, scale = res
    dx, dg, dsh, dsc = _norm_bwd_call(x, g, shift, scale, dy, n_ctx, name + "_bwd")
    return dx, dg, dsh, dsc


norm_mod.defvjp(_norm_mod_fwd, _norm_mod_bwd)


@functools.partial(jax.custom_vjp, nondiff_argnums=(2,))
def rmsnorm(x, g, name):
    return _norm_fwd_call(x, g, None, None, 0, name)


def _rmsnorm_fwd(x, g, name):
    return _norm_fwd_call(x, g, None, None, 0, name), (x, g)


def _rmsnorm_bwd(name, res, dy):
    x, g = res
    dx, dg = _norm_bwd_call(x, g, None, None, dy, 0, name + "_bwd")
    return dx, dg


rmsnorm.defvjp(_rmsnorm_fwd, _rmsnorm_bwd)


def _gres_fwd_call(h, o, gate, coef, n_ctx, name):
    M, D = h.shape
    tm = _row_tile(n_ctx, M)
    nct = n_ctx // tm

    def body(h_ref, o_ref, g_ref, out_ref):
        is_ctx = pl.program_id(0) < nct
        out_ref[...] = h_ref[...] + coef * _by_group(g_ref, is_ctx) * o_ref[...]

    row = pl.BlockSpec((tm, D), lambda i: (i, 0))
    two = pl.BlockSpec((2, D), lambda i: (0, 0))
    return pl.pallas_call(body, name=name, grid=(M // tm,), in_specs=[row, row, two], out_specs=row,
                          out_shape=jax.ShapeDtypeStruct((M, D), F32), compiler_params=_params(("parallel",)))(h, o, gate)


def _gres_bwd_call(o, gate, d, coef, n_ctx, name):
    M, D = o.shape
    tm = _row_tile(n_ctx, M)
    nct = n_ctx // tm

    def body(o_ref, g_ref, d_ref, do_ref, dg_ref):
        i = pl.program_id(0)
        is_ctx = i < nct

        @pl.when(i == 0)
        def _():
            dg_ref[...] = jnp.zeros_like(dg_ref)

        dv = d_ref[...]
        do_ref[...] = coef * _by_group(g_ref, is_ctx) * dv
        _acc_by_group(dg_ref, is_ctx, coef * jnp.sum(dv * o_ref[...], axis=0, keepdims=True))

    row = pl.BlockSpec((tm, D), lambda i: (i, 0))
    two = pl.BlockSpec((2, D), lambda i: (0, 0))
    return pl.pallas_call(body, name=name, grid=(M // tm,), in_specs=[row, two, row], out_specs=[row, two],
                          out_shape=[jax.ShapeDtypeStruct((M, D), F32), jax.ShapeDtypeStruct((2, D), F32)],
                          compiler_params=_params(("arbitrary",)))(o, gate, d)


@functools.partial(jax.custom_vjp, nondiff_argnums=(3, 4, 5))
def gated_residual(h, o, gate, coef, n_ctx, name):
    return _gres_fwd_call(h, o, gate, coef, n_ctx, name)


def _gres_fwd(h, o, gate, coef, n_ctx, name):
    return _gres_fwd_call(h, o, gate, coef, n_ctx, name), (o, gate)


def _gres_bwd(coef, n_ctx, name, res, d):
    o, gate = res
    do, dg = _gres_bwd_call(o, gate, d, coef, n_ctx, name + "_bwd")
    return d, do, dg


gated_residual.defvjp(_gres_fwd, _gres_bwd)


def _swiglu_fwd_call(ab, name):
    M, F2 = ab.shape
    Fh = F2 // 2
    tm, tn = _pick(M, (128, 64, 32, 16, 8)), Fh
    nf = Fh // tn

    def body(a_ref, b_ref, o_ref):
        a = a_ref[...]
        o_ref[...] = a * jax.nn.sigmoid(a) * b_ref[...]

    return pl.pallas_call(body, name=name, grid=(M // tm, nf),
                          in_specs=[pl.BlockSpec((tm, tn), lambda i, j: (i, j)), pl.BlockSpec((tm, tn), lambda i, j: (i, j + nf))],
                          out_specs=pl.BlockSpec((tm, tn), lambda i, j: (i, j)),
                          out_shape=jax.ShapeDtypeStruct((M, Fh), F32), compiler_params=_params(("parallel", "parallel")))(ab, ab)


def _swiglu_bwd_call(ab, dact, name):
    M, F2 = ab.shape
    Fh = F2 // 2
    tm, tn = _pick(M, (128, 64, 32, 16, 8)), Fh
    nf = Fh // tn

    def body(a_ref, b_ref, d_ref, o_ref):
        a, b, d = a_ref[...], b_ref[...], d_ref[...]
        sig = jax.nn.sigmoid(a)
        da = d * b * sig * (1.0 + a * (1.0 - sig))
        db = d * a * sig
        o_ref[...] = jnp.where(pl.program_id(1) < nf, da, db)

    return pl.pallas_call(body, name=name, grid=(M // tm, 2 * nf),
                          in_specs=[pl.BlockSpec((tm, tn), lambda i, j: (i, j % nf)),
                                    pl.BlockSpec((tm, tn), lambda i, j: (i, j % nf + nf)),
                                    pl.BlockSpec((tm, tn), lambda i, j: (i, j % nf))],
                          out_specs=pl.BlockSpec((tm, tn), lambda i, j: (i, j)),
                          out_shape=jax.ShapeDtypeStruct((M, F2), F32), compiler_params=_params(("parallel", "parallel")))(ab, ab, dact)


@functools.partial(jax.custom_vjp, nondiff_argnums=(1,))
def swiglu_act(ab, name):
    return _swiglu_fwd_call(ab, name)


def _swiglu_fwd(ab, name):
    return _swiglu_fwd_call(ab, name), (ab,)


def _swiglu_bwd(name, res, d):
    return (_swiglu_bwd_call(res[0], d, name + "_bwd"),)


swiglu_act.defvjp(_swiglu_fwd, _swiglu_bwd)


_DN_NT = (((1,), (1,)), ((), ()))
_DN_TN = (((0,), (0,)), ((), ()))
_DN_NN = (((1,), (0,)), ((), ()))


def _dot(a, b, dn):
    return lax.dot_general(a.astype(MXU_DTYPE), b.astype(MXU_DTYPE), dn, preferred_element_type=F32)


_TQ = (1024, 512, 256, 128, 64, 32, 16, 8)


def _flash_fwd_call(q, k1, v1, k2, v2, scale, name):
    H, Tq, dk = q.shape
    T1, dv = k1.shape[1], v1.shape[2]
    has2 = k2 is not None
    tq, tk = _pick(Tq, _TQ), _pick(T1, _TQ)
    off = 1 if has2 else 0
    nkv = T1 // tk + off

    def body(*refs):
        if has2:
            q_ref, k1_ref, v1_ref, k2_ref, v2_ref, o_ref, lse_ref, m_s, l_s, acc_s = refs
        else:
            q_ref, k1_ref, v1_ref, o_ref, lse_ref, m_s, l_s, acc_s = refs
        j = pl.program_id(2)

        @pl.when(j == 0)
        def _():
            m_s[...] = jnp.full_like(m_s, NEG_INF)
            l_s[...] = jnp.zeros_like(l_s)
            acc_s[...] = jnp.zeros_like(acc_s)

        def step(k, v):
            s = _dot(q_ref[0], k, _DN_NT) * scale
            m_prev = m_s[...]
            m_new = jnp.maximum(m_prev, jnp.max(s, axis=-1, keepdims=True))
            alpha = jnp.exp(m_prev - m_new)
            p = jnp.exp(s - m_new)
            l_s[...] = alpha * l_s[...] + jnp.sum(p, axis=-1, keepdims=True)
            acc_s[...] = alpha * acc_s[...] + _dot(p, v, _DN_NN)
            m_s[...] = m_new

        if has2:
            @pl.when(j == 0)
            def _():
                step(k2_ref[0], v2_ref[0])

            @pl.when(j > 0)
            def _():
                step(k1_ref[0], v1_ref[0])
        else:
            step(k1_ref[0], v1_ref[0])

        @pl.when(j == nkv - 1)
        def _():
            o_ref[0] = acc_s[...] / l_s[...]
            lse_ref[0] = m_s[...] + jnp.log(l_s[...])

    qs = pl.BlockSpec((1, tq, dk), lambda h, i, j: (h, i, 0))
    k1s = pl.BlockSpec((1, tk, dk), lambda h, i, j: (h, jnp.maximum(j - off, 0), 0))
    v1s = pl.BlockSpec((1, tk, dv), lambda h, i, j: (h, jnp.maximum(j - off, 0), 0))
    in_specs, args = [qs, k1s, v1s], [q, k1, v1]
    if has2:
        C = k2.shape[1]
        in_specs += [pl.BlockSpec((1, C, dk), lambda h, i, j: (h, 0, 0)), pl.BlockSpec((1, C, dv), lambda h, i, j: (h, 0, 0))]
        args += [k2, v2]
    return pl.pallas_call(
        body, name=name, grid=(H, Tq // tq, nkv), in_specs=in_specs,
        out_specs=[pl.BlockSpec((1, tq, dv), lambda h, i, j: (h, i, 0)), pl.BlockSpec((1, tq, 1), lambda h, i, j: (h, i, 0))],
        out_shape=[jax.ShapeDtypeStruct((H, Tq, dv), F32), jax.ShapeDtypeStruct((H, Tq, 1), F32)],
        scratch_shapes=[pltpu.VMEM((tq, 1), F32), pltpu.VMEM((tq, 1), F32), pltpu.VMEM((tq, dv), F32)],
        compiler_params=_params(("parallel", "parallel", "arbitrary")))(*args)


def _flash_bwd_call(q, k1, v1, k2, v2, o, lse, do, scale, name):
    H, Tq, dk = q.shape
    T1, dv = k1.shape[1], v1.shape[2]
    has2 = k2 is not None
    tq, tk = _pick(Tq, _TQ), _pick(T1, _TQ)
    off = 1 if has2 else 0
    nkv, nq = T1 // tk + off, Tq // tq
    C = k2.shape[1] if has2 else 0
    rows = max(tk, C)

    def body(*refs):
        if has2:
            (q_ref, k1_ref, v1_ref, k2_ref, v2_ref, o_ref, lse_ref, do_ref,
             dq_ref, dk1_ref, dv1_ref, dk2_ref, dv2_ref, dk_s, dv_s) = refs
        else:
            q_ref, k1_ref, v1_ref, o_ref, lse_ref, do_ref, dq_ref, dk1_ref, dv1_ref, dk_s, dv_s = refs
        j, i = pl.program_id(1), pl.program_id(2)

        @pl.when((j == 0) & (i == 0))
        def _():
            dq_ref[...] = jnp.zeros_like(dq_ref)

        @pl.when(i == 0)
        def _():
            dk_s[...] = jnp.zeros_like(dk_s)
            dv_s[...] = jnp.zeros_like(dv_s)

        def step(k, v, n):
            qb, dob = q_ref[0], do_ref[0]
            p = jnp.exp(_dot(qb, k, _DN_NT) * scale - lse_ref[0])
            delta = jnp.sum(dob * o_ref[0], axis=-1, keepdims=True)
            dv_s[0:n, :] += _dot(p, dob, _DN_TN)
            ds = p * (_dot(dob, v, _DN_NT) - delta) * scale
            dk_s[0:n, :] += _dot(ds, qb, _DN_TN)
            r0 = pl.multiple_of(i * tq, tq)
            dq_ref[0, pl.ds(r0, tq), :] += _dot(ds, k, _DN_NN)

        if has2:
            @pl.when(j == 0)
            def _():
                step(k2_ref[0], v2_ref[0], C)

            @pl.when(j > 0)
            def _():
                step(k1_ref[0], v1_ref[0], tk)

            @pl.when((i == nq - 1) & (j == 0))
            def _():
                dk2_ref[0] = dk_s[0:C, :]
                dv2_ref[0] = dv_s[0:C, :]

            @pl.when((i == nq - 1) & (j > 0))
            def _():
                dk1_ref[0] = dk_s[0:tk, :]
                dv1_ref[0] = dv_s[0:tk, :]
        else:
            step(k1_ref[0], v1_ref[0], tk)

            @pl.when(i == nq - 1)
            def _():
                dk1_ref[0] = dk_s[...]
                dv1_ref[0] = dv_s[...]

    qs = pl.BlockSpec((1, tq, dk), lambda h, j, i: (h, i, 0))
    os_ = pl.BlockSpec((1, tq, dv), lambda h, j, i: (h, i, 0))
    ls = pl.BlockSpec((1, tq, 1), lambda h, j, i: (h, i, 0))
    k1s = pl.BlockSpec((1, tk, dk), lambda h, j, i: (h, jnp.maximum(j - off, 0), 0))
    v1s = pl.BlockSpec((1, tk, dv), lambda h, j, i: (h, jnp.maximum(j - off, 0), 0))
    in_specs, args = [qs, k1s, v1s], [q, k1, v1]
    out_specs = [pl.BlockSpec((1, Tq, dk), lambda h, j, i: (h, 0, 0)), k1s, v1s]
    out_shape = [jax.ShapeDtypeStruct((H, Tq, dk), F32), jax.ShapeDtypeStruct((H, T1, dk), F32),
                 jax.ShapeDtypeStruct((H, T1, dv), F32)]
    if has2:
        k2s = pl.BlockSpec((1, C, dk), lambda h, j, i: (h, 0, 0))
        v2s = pl.BlockSpec((1, C, dv), lambda h, j, i: (h, 0, 0))
        in_specs += [k2s, v2s]
        args += [k2, v2]
        out_specs += [k2s, v2s]
        out_shape += [jax.ShapeDtypeStruct((H, C, dk), F32), jax.ShapeDtypeStruct((H, C, dv), F32)]
    in_specs += [os_, ls, os_]
    args += [o, lse, do]
    return pl.pallas_call(
        body, name=name, grid=(H, nkv, nq), in_specs=in_specs, out_specs=out_specs, out_shape=out_shape,
        scratch_shapes=[pltpu.VMEM((rows, dk), F32), pltpu.VMEM((rows, dv), F32)],
        compiler_params=_params(("parallel", "arbitrary", "arbitrary")))(*args)


@functools.partial(jax.custom_vjp, nondiff_argnums=(5, 6))
def flash2(q, k1, v1, k2, v2, scale, name):
    return _flash_fwd_call(q, k1, v1, k2, v2, scale, name)[0]


def _flash2_fwd(q, k1, v1, k2, v2, scale, name):
    o, lse = _flash_fwd_call(q, k1, v1, k2, v2, scale, name)
    return o, (q, k1, v1, k2, v2, o, lse)


def _flash2_bwd(scale, name, res, do):
    q, k1, v1, k2, v2, o, lse = res
    return tuple(_flash_bwd_call(q, k1, v1, k2, v2, o, lse, do, scale, name + "_bwd"))


flash2.defvjp(_flash2_fwd, _flash2_bwd)


@functools.partial(jax.custom_vjp, nondiff_argnums=(3, 4))
def flash1(q, k, v, scale, name):
    return _flash_fwd_call(q, k, v, None, None, scale, name)[0]


def _flash1_fwd(q, k, v, scale, name):
    o, lse = _flash_fwd_call(q, k, v, None, None, scale, name)
    return o, (q, k, v, o, lse)


def _flash1_bwd(scale, name, res, do):
    q, k, v, o, lse = res
    return tuple(_flash_bwd_call(q, k, v, None, None, o, lse, do, scale, name + "_bwd"))


flash1.defvjp(_flash1_fwd, _flash1_bwd)


def _scan_call(a, x, s, *, rev, adj, n_ctx, name):
    M, W = x.shape
    cw = SSM_CW
    J = W // (2 * cw)
    L = _row_tile(n_ctx, M)
    nt, nc = M // L, n_ctx // L
    asc = rev == adj
    n_steps = int(math.log2(L))
    assert 1 << n_steps == L

    def tile(t):
        if not rev:
            return nt - 1 - t if adj else t
        if not adj:
            return jnp.where(t < nc, nc - 1 - t, nt - 1 - (t - nc))
        return jnp.where(t < nt - nc, nc + t, t - (nt - nc))

    def body(*refs):
        if adj:
            a_ref, x_ref, s_ref, o_ref, da_ref, car_ref = refs
        else:
            a_ref, x_ref, o_ref, car_ref = refs
        t = pl.program_id(1)

        @pl.when(t == 0)
        def _():
            car_ref[...] = jnp.zeros_like(car_ref)
            if adj:
                da_ref[...] = jnp.zeros_like(da_ref)

        ar, ai = a_ref[:, 0:cw], a_ref[:, cw:2 * cw]
        cr, ci = car_ref[:, 0:cw], car_ref[:, cw:2 * cw]
        xr, xi = x_ref[:, 0:cw], x_ref[:, cw:2 * cw]
        row = lax.broadcasted_iota(jnp.int32, (L, cw), 0)
        first = 0 if asc else L - 1
        xr = xr + jnp.where(row == first, ar * cr - ai * ci, 0.0)
        xi = xi + jnp.where(row == first, ar * ci + ai * cr, 0.0)
        pr, pi = ar, ai
        k = 1
        for _ in range(n_steps):
            if asc:
                sr, si, keep = pltpu.roll(xr, k, 0), pltpu.roll(xi, k, 0), row >= k
            else:
                sr, si, keep = pltpu.roll(xr, L - k, 0), pltpu.roll(xi, L - k, 0), row < L - k
            sr, si = jnp.where(keep, sr, 0.0), jnp.where(keep, si, 0.0)
            xr, xi = xr + pr * sr - pi * si, xi + pr * si + pi * sr
            pr, pi = pr * pr - pi * pi, 2.0 * pr * pi
            k *= 2
        o_ref[:, 0:cw] = xr
        o_ref[:, cw:2 * cw] = xi
        if adj:
            if asc:
                gr, gi = pltpu.roll(xr, 1, 0), pltpu.roll(xi, 1, 0)
            else:
                gr, gi = pltpu.roll(xr, L - 1, 0), pltpu.roll(xi, L - 1, 0)
            gr, gi = jnp.where(row == first, cr, gr), jnp.where(row == first, ci, gi)
            sr, si = s_ref[:, 0:cw], s_ref[:, cw:2 * cw]
            da_ref[:, 0:cw] += jnp.sum(sr * gr + si * gi, axis=0, keepdims=True)
            da_ref[:, cw:2 * cw] += jnp.sum(sr * gi - si * gr, axis=0, keepdims=True)
        last = L - 1 if asc else 0
        car_ref[...] = o_ref[pl.ds(last, 1), :]

    blk = pl.BlockSpec((L, 2 * cw), lambda j, t: (tile(t), j))
    vec = pl.BlockSpec((1, 2 * cw), lambda j, t: (0, j))
    if adj:
        in_specs, args = [vec, blk, blk], (a, x, s)
        out_specs = [blk, vec]
        out_shape = [jax.ShapeDtypeStruct((M, W), F32), jax.ShapeDtypeStruct((1, W), F32)]
    else:
        in_specs, args = [vec, blk], (a, x)
        out_specs = blk
        out_shape = jax.ShapeDtypeStruct((M, W), F32)
    return pl.pallas_call(body, name=name, grid=(J, nt), in_specs=in_specs, out_specs=out_specs, out_shape=out_shape,
                          scratch_shapes=[pltpu.VMEM((1, 2 * cw), F32)],
                          compiler_params=_params(("parallel", "arbitrary")))(*args)


def _conj_layout(a):
    cw = SSM_CW
    J = a.shape[1] // (2 * cw)
    a4 = a.reshape(1, J, 2, cw)
    return jnp.concatenate([a4[:, :, 0:1], -a4[:, :, 1:2]], axis=2).reshape(a.shape)


@functools.partial(jax.custom_vjp, nondiff_argnums=(2, 3, 4))
def diag_scan(a, x, rev, n_ctx, name):
    return _scan_call(a, x, None, rev=rev, adj=False, n_ctx=n_ctx, name=name)


def _diag_scan_fwd(a, x, rev, n_ctx, name):
    s = _scan_call(a, x, None, rev=rev, adj=False, n_ctx=n_ctx, name=name)
    return s, (a, s)


def _diag_scan_bwd(rev, n_ctx, name, res, ds):
    a, s = res
    g, da = _scan_call(_conj_layout(a), ds, s, rev=rev, adj=True, n_ctx=n_ctx, name=name + "_adj")
    return da, g


diag_scan.defvjp(_diag_scan_fwd, _diag_scan_bwd)


def _loss_call(h, g, target, name):
    M, D = h.shape
    tm = _pick(M, (256, 128, 64, 32, 16, 8))

    def body(h_ref, g_ref, t_ref, loss_ref, dh_ref, dg_ref):
        i = pl.program_id(0)

        @pl.when(i == 0)
        def _():
            loss_ref[...] = jnp.zeros_like(loss_ref)
            dg_ref[...] = jnp.zeros_like(dg_ref)

        xv, gv = h_ref[...], g_ref[...]
        r = lax.rsqrt(jnp.mean(xv * xv, axis=-1, keepdims=True) + EPS)
        xhat = xv * r
        err = xhat * gv - t_ref[...]
        loss_ref[...] += 0.5 * jnp.sum(jnp.mean(err * err, axis=-1, keepdims=True), axis=0, keepdims=True)
        dy = err * (1.0 / D)
        dg_ref[...] += jnp.sum(dy * xhat, axis=0, keepdims=True)
        dxhat = dy * gv
        dh_ref[...] = r * (dxhat - xhat * jnp.mean(dxhat * xhat, axis=-1, keepdims=True))

    row = pl.BlockSpec((tm, D), lambda i: (i, 0))
    vec = pl.BlockSpec((1, D), lambda i: (0, 0))
    one = pl.BlockSpec((1, 1), lambda i: (0, 0))
    return pl.pallas_call(body, name=name, grid=(M // tm,), in_specs=[row, vec, row], out_specs=[one, row, vec],
                          out_shape=[jax.ShapeDtypeStruct((1, 1), F32), jax.ShapeDtypeStruct((M, D), F32),
                                     jax.ShapeDtypeStruct((1, D), F32)],
                          compiler_params=_params(("arbitrary",)))(h, g, target)


@functools.partial(jax.custom_vjp, nondiff_argnums=(3,))
def loss_head(h, g, target, name):
    return _loss_call(h, g, target, name)[0][0, 0]


def _loss_head_fwd(h, g, target, name):
    loss, dh, dg = _loss_call(h, g, target, name)
    return loss[0, 0], (dh, dg, target)


def _loss_head_bwd(name, res, ct):
    dh, dg, target = res
    return ct * dh, ct * dg, jnp.zeros_like(target)


loss_head.defvjp(_loss_head_fwd, _loss_head_bwd)


def _adamw_call(w, gstack, m, v, name):
    R, Cn = w.shape
    n = gstack.shape[0]
    tr = _pick(R, (64, 32, 16, 8))

    def body(w_ref, g_ref, m_ref, v_ref, go_ref, d_ref, mo_ref, vo_ref):
        g = g_ref[0]
        for s in range(1, n):
            g = g + g_ref[s]
        mn = ADAM_B1 * m_ref[...] + (1.0 - ADAM_B1) * g
        vn = ADAM_B2 * v_ref[...] + (1.0 - ADAM_B2) * (g * g)
        m_hat = mn / (1.0 - ADAM_B1 ** ADAM_STEP)
        v_hat = vn / (1.0 - ADAM_B2 ** ADAM_STEP)
        go_ref[...] = g
        d_ref[...] = -ADAM_LR * (m_hat / (jnp.sqrt(v_hat) + ADAM_EPS) + ADAM_WD * w_ref[...])
        mo_ref[...] = mn
        vo_ref[...] = vn

    blk = pl.BlockSpec((tr, Cn), lambda i: (i, 0))
    gblk = pl.BlockSpec((n, tr, Cn), lambda i: (0, i, 0))
    sds = jax.ShapeDtypeStruct((R, Cn), F32)
    return pl.pallas_call(body, name=name, grid=(R // tr,), in_specs=[blk, gblk, blk, blk], out_specs=[blk] * 4,
                          out_shape=[sds] * 4, compiler_params=_params(("parallel",)))(w, gstack, m, v)


MESH = pl.DeviceIdType.MESH
ANY = pl.BlockSpec(memory_space=pl.ANY)


def _place():
    return lax.axis_index("x"), lax.axis_index("y"), lax.axis_index("c")


def _sibling_exchange(v, name):
    n = v.shape[0]

    def body(v_ref, o_ref, send_sems, recv_sems):
        x, y, c = _place()
        copies = [pltpu.make_async_remote_copy(src_ref=v_ref.at[k], dst_ref=o_ref.at[k], send_sem=send_sems.at[k],
                                               recv_sem=recv_sems.at[k], device_id=(x, y, 1 - c), device_id_type=MESH)
                  for k in range(n)]
        for cp in copies:
            cp.start()
        for cp in copies:
            cp.wait()

    return pl.pallas_call(body, name=name, in_specs=[ANY], out_specs=ANY, out_shape=jax.ShapeDtypeStruct(v.shape, v.dtype),
                          scratch_shapes=[pltpu.SemaphoreType.DMA((n,)), pltpu.SemaphoreType.DMA((n,))])(v)


def _sibling_exchange_half(v, name):
    n = v.shape[0]

    def body(v_ref, o_ref, send_sems, recv_sems):
        x, y, c = _place()
        copies = [pltpu.make_async_remote_copy(src_ref=v_ref.at[k, pl.ds(1 - c, 1)], dst_ref=o_ref.at[k],
                                               send_sem=send_sems.at[k], recv_sem=recv_sems.at[k],
                                               device_id=(x, y, 1 - c), device_id_type=MESH)
                  for k in range(n)]
        for cp in copies:
            cp.start()
        for cp in copies:
            cp.wait()

    return pl.pallas_call(body, name=name, in_specs=[ANY], out_specs=ANY,
                          out_shape=jax.ShapeDtypeStruct((n, 1) + v.shape[2:], v.dtype),
                          scratch_shapes=[pltpu.SemaphoreType.DMA((n,)), pltpu.SemaphoreType.DMA((n,))])(v)


def _chip_exchange(v, same, name):
    out_shape = (N_CHIPS,) + (v.shape if same else v.shape[1:])

    def body(v_ref, o_ref, send_sems, recv_sems, local_sem):
        x, y, c = _place()
        me = 2 * x + y
        own = pltpu.make_async_copy(v_ref if same else v_ref.at[me], o_ref.at[me], local_sem)
        own.start()
        copies = []
        for k, (fx, fy) in enumerate(((1, 0), (0, 1), (1, 1))):
            px, py = jnp.where(fx == 1, 1 - x, x), jnp.where(fy == 1, 1 - y, y)
            src = v_ref if same else v_ref.at[2 * px + py]
            copies.append(pltpu.make_async_remote_copy(src_ref=src, dst_ref=o_ref.at[me], send_sem=send_sems.at[k],
                                                       recv_sem=recv_sems.at[k], device_id=(px, py, c), device_id_type=MESH))
        for cp in copies:
            cp.start()
        for cp in copies:
            cp.wait()
        own.wait()

    return pl.pallas_call(body, name=name, in_specs=[ANY], out_specs=ANY, out_shape=jax.ShapeDtypeStruct(out_shape, v.dtype),
                          scratch_shapes=[pltpu.SemaphoreType.DMA((3,)), pltpu.SemaphoreType.DMA((3,)), pltpu.SemaphoreType.DMA])(v)


def _all_gather(v, name):
    def body(v_ref, o_ref, send_sems, recv_sems, local_sem):
        x, y, c = _place()
        me = 4 * x + 2 * y + c
        own = pltpu.make_async_copy(v_ref, o_ref.at[me], local_sem)
        own.start()
        copies = []
        for k in range(1, 8):
            fx, fy, fc = (k >> 2) & 1, (k >> 1) & 1, k & 1
            peer = (jnp.where(fx == 1, 1 - x, x), jnp.where(fy == 1, 1 - y, y), jnp.where(fc == 1, 1 - c, c))
            copies.append(pltpu.make_async_remote_copy(src_ref=v_ref, dst_ref=o_ref.at[me], send_sem=send_sems.at[k - 1],
                                                       recv_sem=recv_sems.at[k - 1], device_id=peer, device_id_type=MESH))
        for cp in copies:
            cp.start()
        for cp in copies:
            cp.wait()
        own.wait()

    return pl.pallas_call(body, name=name, in_specs=[ANY], out_specs=ANY,
                          out_shape=jax.ShapeDtypeStruct((8,) + v.shape, v.dtype),
                          scratch_shapes=[pltpu.SemaphoreType.DMA((7,)), pltpu.SemaphoreType.DMA((7,)), pltpu.SemaphoreType.DMA])(v)


def _add_own_half(g, r, c, name):
    n, _, R, W = g.shape
    tr = _pick(R, (256, 128, 64, 32, 16, 8))

    def body(c_ref, g_ref, r_ref, o_ref):
        o_ref[...] = g_ref[0] + r_ref[0]

    grid_spec = pltpu.PrefetchScalarGridSpec(
        num_scalar_prefetch=1, grid=(n, R // tr),
        in_specs=[pl.BlockSpec((1, 1, tr, W), lambda p, i, c_ref: (p, c_ref[0], i, 0)),
                  pl.BlockSpec((1, 1, tr, W), lambda p, i, c_ref: (p, 0, i, 0))],
        out_specs=pl.BlockSpec((1, tr, W), lambda p, i, c_ref: (p, i, 0)))
    return pl.pallas_call(body, name=name, grid_spec=grid_spec, out_shape=jax.ShapeDtypeStruct((n, R, W), F32),
                          compiler_params=_params(("parallel", "parallel")))(c.reshape(1).astype(jnp.int32), g, r)


def _sum_stack(v, name):
    n, R, W = v.shape
    tr = _pick(R, (256, 128, 64, 32, 16, 8))

    def body(v_ref, o_ref):
        acc = v_ref[0]
        for s in range(1, n):
            acc = acc + v_ref[s]
        o_ref[...] = acc

    return pl.pallas_call(body, name=name, grid=(R // tr,), in_specs=[pl.BlockSpec((n, tr, W), lambda i: (0, i, 0))],
                          out_specs=pl.BlockSpec((tr, W), lambda i: (i, 0)), out_shape=jax.ShapeDtypeStruct((R, W), F32),
                          compiler_params=_params(("parallel",)))(v)


def _flat_rows(n, mult):
    rows = -(-n // FLAT_W)
    return -(-rows // mult) * mult


def _by_core(a, b, c):
    return lax.dynamic_index_in_dim(jnp.stack([a, b]), c, axis=0, keepdims=False)


def gather_weights(shards):
    _, _, c = _place()
    flat = jnp.concatenate([shards[n].astype(WIRE_DTYPE).reshape(-1) for n in BIG_NAMES])
    n_flat = flat.shape[0]
    rh = _flat_rows(n_flat, 32) // 2
    flat = jnp.pad(flat, (0, 2 * rh * FLAT_W - n_flat)).reshape(2, rh, FLAT_W)
    mine = lax.dynamic_index_in_dim(flat, c, axis=0, keepdims=False)
    got = _chip_exchange(mine, True, "gather_chips")
    other = _sibling_exchange(got, "gather_sibling")
    halves = jnp.stack([_by_core(got, other, c), _by_core(other, got, c)], axis=1)
    allflat = halves.reshape(N_CHIPS, 2 * rh * FLAT_W)
    out, off = {}, 0
    for name, axis in BIG:
        shp = shards[name].shape
        size = math.prod(shp)
        parts = allflat[:, off:off + size].reshape((N_CHIPS,) + shp)
        out[name] = [jnp.concatenate([parts[p, l] for p in range(N_CHIPS)], axis=axis - 1) for l in range(shp[0])]
        off += size
    return out


def scatter_gradients(grads, shards):
    _, _, c = _place()
    cols = []
    for name, axis in BIG:
        parts = [jnp.split(g, N_CHIPS, axis=axis - 1) for g in grads[name]]
        cols.append(jnp.stack([jnp.stack([per_layer[p] for per_layer in parts]).reshape(-1) for p in range(N_CHIPS)]))
    flat = jnp.concatenate(cols, axis=1)
    n_flat = flat.shape[1]
    rh = _flat_rows(n_flat, 16) // 2
    flat = jnp.pad(flat, ((0, 0), (0, 2 * rh * FLAT_W - n_flat))).reshape(N_CHIPS, 2, rh, FLAT_W)
    theirs = _sibling_exchange_half(flat, "scatter_sibling")
    pair = _add_own_half(flat, theirs, c, "scatter_pair_sum")
    got = _chip_exchange(pair, False, "scatter_chips")
    mine = _sum_stack(got, "scatter_chip_sum")
    other = _sibling_exchange(mine.reshape(1, rh, FLAT_W), "scatter_halves").reshape(rh, FLAT_W)
    full = jnp.stack([_by_core(mine, other, c), _by_core(other, mine, c)]).reshape(-1)
    out, off = {}, 0
    for name, _ in BIG:
        shp = shards[name].shape
        size = math.prod(shp)
        out[name] = full[off:off + size].reshape(shp)
        off += size
    return out


def _rope_tables(n_ctx, n_lat, n):
    t = jnp.arange(n_lat, dtype=jnp.int32)
    zero = jnp.zeros((n_ctx,), jnp.int32)
    row = jnp.concatenate([zero, t // GRID_W]).astype(F32)
    col = jnp.concatenate([zero, t % GRID_W]).astype(F32)
    half = n // 2
    inv = ROPE_BASE ** (-jnp.arange(0, half, 2, dtype=F32) / half)
    ang_r, ang_c = row[:, None, None] * inv, col[:, None, None] * inv
    return (jnp.cos(ang_r), jnp.sin(ang_r)), (jnp.cos(ang_c), jnp.sin(ang_c))


def _rot(x, cs):
    cos, sin = cs
    h = x.shape[-1] // 2
    x1, x2 = x[..., :h], x[..., h:]
    return jnp.concatenate([x1 * cos - x2 * sin, x1 * sin + x2 * cos], axis=-1)


def _axial_rope(x, tables):
    h = x.shape[-1] // 2
    return jnp.concatenate([_rot(x[..., :h], tables[0]), _rot(x[..., h:], tables[1])], axis=-1)


def _cmul(ar, ai, br, bi):
    return ar * br - ai * bi, ar * bi + ai * br


def _ssm_discretize(lam_re, lam_im, log_dt, b_re, b_im):
    dt = jnp.exp(log_dt)[:, None]
    mag = jnp.exp(lam_re * dt)
    a_re, a_im = mag * jnp.cos(lam_im * dt), mag * jnp.sin(lam_im * dt)
    den = lam_re * lam_re + lam_im * lam_im
    w_re = ((a_re - 1) * lam_re + a_im * lam_im) / den
    w_im = (a_im * lam_re - (a_re - 1) * lam_im) / den
    bb_re, bb_im = _cmul(w_re[..., None], w_im[..., None], b_re, b_im)
    return a_re, a_im, bb_re, bb_im


def _ssm_layouts(a_re, a_im, bb_re, bb_im, c_re, c_im):
    J, g8, P, Mg = SSM_CHUNKS, SSM_CHUNK_GROUPS, SSM_STATE, SSM_GROUP
    eye = jnp.eye(g8, dtype=F32)
    a = jnp.stack([a_re.reshape(J, g8 * P), a_im.reshape(J, g8 * P)], axis=1).reshape(1, J * 2 * g8 * P)
    bb = jnp.stack([bb_re, bb_im]).reshape(2, J, g8, P, Mg)
    w_drive = jnp.einsum('rjgpm,gh->jgmrhp', bb, eye).reshape(J * g8 * Mg, 2 * g8 * P)
    cc = jnp.stack([c_re, -c_im]).reshape(2, J, g8, Mg, P)
    w_read = jnp.einsum('rjgmp,gh->jrhpgm', cc, eye).reshape(J * 2 * g8 * P, g8 * Mg)
    return a, w_drive, w_read


def _sink_softmax(score_list, sink_logit):
    m = sink_logit
    for s in score_list:
        m = jnp.maximum(m, s.max(axis=-1, keepdims=True))
    e = [jnp.exp(s - m) for s in score_list]
    denom = jnp.exp(sink_logit - m)
    for t in e:
        denom = denom + t.sum(axis=-1, keepdims=True)
    return [t / denom for t in e]


def _window_gqa(q, k, v, kc, vc, sink):
    T, H, d = q.shape
    G = H // GQA_KV_HEADS
    nb = T // BLOCK
    scale = d ** -0.5
    qb = q.reshape(nb, BLOCK, GQA_KV_HEADS, G, d)

    def band(t):
        tb = t.reshape(nb, BLOCK, GQA_KV_HEADS, d)
        tp = jnp.pad(tb, ((1, 1), (0, 0), (0, 0), (0, 0)))
        return jnp.concatenate([tp[:-2], tp[1:-1], tp[2:]], axis=1)

    kb, vb = band(k), band(v)
    s_band = jnp.einsum('nqhgd,nkhd->hgnqk', qb, kb, preferred_element_type=F32) * scale
    blk = jnp.arange(nb)[:, None, None]
    qpos = blk * BLOCK + jnp.arange(BLOCK)[None, :, None]
    kpos = (blk - 1) * BLOCK + jnp.arange(3 * BLOCK)[None, None, :]
    valid = (jnp.abs(qpos - kpos) <= WINDOW) & (kpos >= 0) & (kpos < T)
    s_band = jnp.where(valid, s_band, NEG_INF)
    s_ctx = jnp.einsum('nqhgd,chd->hgnqc', qb, kc, preferred_element_type=F32) * scale
    sk = sink.reshape(GQA_KV_HEADS, G)[:, :, None, None, None]
    p_band, p_ctx = _sink_softmax([s_band, s_ctx], sk)
    o = jnp.einsum('hgnqk,nkhd->nqhgd', p_band, vb) + jnp.einsum('hgnqc,chd->nqhgd', p_ctx, vc)
    return o.reshape(T, H * d)


def _context_gqa(qc, kc, vc, sink):
    C, H, d = qc.shape
    G = H // GQA_KV_HEADS
    qg = qc.reshape(C, GQA_KV_HEADS, G, d)
    s = jnp.einsum('qhgd,khd->hgqk', qg, kc, preferred_element_type=F32) * d ** -0.5
    sk = sink.reshape(GQA_KV_HEADS, G)[:, :, None, None]
    (p,) = _sink_softmax([s], sk)
    return jnp.einsum('hgqk,khd->qhgd', p, vc).reshape(C, H * d)


def _w_in_layout(d_model):
    widths = (("cq", MLA_Q_RANK), ("ckv", MLA_KV_RANK), ("kr", MLA_ROPE), ("u", SSM_WIDTH), ("gq", GQA_HEADS * GQA_HEAD_DIM),
              ("gk", GQA_KV_HEADS * GQA_HEAD_DIM), ("gv", GQA_KV_HEADS * GQA_HEAD_DIM), ("gates", N_BRANCH * d_model))
    out, src, dst = [], 0, 0
    for name, w in widths:
        out.append((name, src, dst, w))
        src += w
        dst += -(-w // LANES) * LANES
    return out, src, dst


def _pad_w_in(w, d_model):
    lay, _, _ = _w_in_layout(d_model)
    parts = []
    for _, src, _, wd in lay:
        seg = w[..., src:src + wd]
        pad = -(-wd // LANES) * LANES - wd
        parts.append(jnp.pad(seg, [(0, 0)] * (w.ndim - 1) + [(0, pad)]) if pad else seg)
    return jnp.concatenate(parts, axis=-1)


def _unpad_w_in(w, d_model):
    lay, _, _ = _w_in_layout(d_model)
    return jnp.concatenate([w[..., dst:dst + wd] for _, _, dst, wd in lay], axis=-1)


def _layer(hall, lw, lz, sp, cs8, n_ctx, ropes):
    M, D = hall.shape
    C = n_ctx
    rope32, rope64 = ropes

    def lin(x, name):
        return linear(x, lw[name], lz[name], name)

    mod_all = lin(cs8, "ada_w")[0:2] + sp["ada_b"][None, :]
    mod = [mod_all[:, i * D:(i + 1) * D] for i in range(N_MOD)]

    def ffn(h, tag, norm_g, sh, sc, gate):
        hn = norm_mod(h, norm_g[None, :], sh, sc, C, tag + "_norm")
        act = swiglu_act(lin(hn, tag + "_w13"), tag + "_act")
        return gated_residual(h, lin(act, tag + "_w2"), gate, 0.5, C, tag + "_res")

    hall = ffn(hall, "ffn1", sp["norm_ffn1"], mod[0], mod[1], mod[2])

    xm = norm_mod(hall, sp["norm_mix"][None, :], mod[3], mod[4], C, "mix_norm")
    proj = lin(xm, "w_in")
    lay, _, _ = _w_in_layout(D)
    seg = {name: proj[:, dst:dst + wd] for name, _, dst, wd in lay}

    q = lin(rmsnorm(seg["cq"], sp["mla_q_norm"][None, :], "mla_q_norm"), "mla_w_uq").reshape(M, MLA_HEADS, MLA_NOPE + MLA_ROPE)
    q = jnp.concatenate([q[..., :MLA_NOPE], _axial_rope(q[..., MLA_NOPE:], rope32)], axis=-1)
    kv = lin(rmsnorm(seg["ckv"], sp["mla_kv_norm"][None, :], "mla_kv_norm"), "mla_w_ukv").reshape(M, MLA_HEADS, MLA_NOPE + MLA_V)
    kr = _axial_rope(seg["kr"][:, None, :], rope32)
    k = jnp.concatenate([kv[..., :MLA_NOPE], jnp.broadcast_to(kr, (M, MLA_HEADS, MLA_ROPE))], axis=-1)
    qh, kh, vh = (t.transpose(1, 0, 2) for t in (q, k, kv[..., MLA_NOPE:]))
    scale = (MLA_NOPE + MLA_ROPE) ** -0.5
    o_lat = flash2(qh[:, C:], kh[:, C:], vh[:, C:], kh[:, :C], vh[:, :C], scale, "mla_lat")
    o_ctx = flash1(qh[:, :C], kh[:, :C], vh[:, :C], scale, "mla_ctx")
    o = jnp.concatenate([o_ctx, o_lat], axis=1).transpose(1, 0, 2).reshape(M, MLA_HEADS * MLA_V)
    mla = lin(o, "mla_w_o")

    u = seg["u"]
    y = u * sp["ssm_d"][None, :]
    for direction in range(2):
        a_re, a_im, bb_re, bb_im = _ssm_discretize(sp["ssm_lambda_re"][direction], sp["ssm_lambda_im"][direction],
                                                   sp["ssm_log_dt"][direction], sp["ssm_b_re"][direction],
                                                   sp["ssm_b_im"][direction])
        a, w_drive, w_read = _ssm_layouts(a_re, a_im, bb_re, bb_im, sp["ssm_c_re"][direction], sp["ssm_c_im"][direction])
        drive = bd_linear(u, w_drive, SSM_CHUNKS, "ssm_drive%d" % direction)
        states = diag_scan(a, drive, direction == 1, C, "ssm_scan%d" % direction)
        y = y + bd_linear(states, w_read, SSM_CHUNKS, "ssm_read%d" % direction)
    zz = lin(jax.nn.gelu(y), "ssm_w_glu")
    ssm = zz[:, :D] * jax.nn.sigmoid(zz[:, D:])

    gq = _axial_rope(seg["gq"].reshape(M, GQA_HEADS, GQA_HEAD_DIM), rope64)
    gk = _axial_rope(seg["gk"].reshape(M, GQA_KV_HEADS, GQA_HEAD_DIM), rope64)
    gv = seg["gv"].reshape(M, GQA_KV_HEADS, GQA_HEAD_DIM)
    g_lat = _window_gqa(gq[C:], gk[C:], gv[C:], gk[:C], gv[:C], sp["gqa_sink"])
    g_ctx = _context_gqa(gq[:C], gk[:C], gv[:C], sp["gqa_sink"])
    gqa = lin(jnp.concatenate([g_ctx, g_lat], axis=0), "gqa_w_o")

    gates = jax.nn.sigmoid(seg["gates"])
    mixed = gates[:, :D] * mla + gates[:, D:2 * D] * ssm + gates[:, 2 * D:] * gqa
    hall = gated_residual(hall, lin(mixed, "w_out"), mod[5], 1.0, C, "mix_res")

    return ffn(hall, "ffn2", sp["norm_ffn2"], mod[6], mod[7], mod[8])


PER_LAYER_SMALL = tuple(n for n in SMALL if n not in ("c_ctx", "final_norm"))


def _loss_fn(diff, x, c, ctx, target, whole):
    zeros, small, x = diff
    T, D = x.shape
    C = ctx.shape[0]
    ropes = (_rope_tables(C, T, MLA_ROPE), _rope_tables(C, T, GQA_HEAD_DIM))
    cs = jax.nn.silu(jnp.stack([small["c_ctx"], c]))
    cs8 = jnp.pad(cs, ((0, 6), (0, 0)))
    hall = jnp.concatenate([ctx, x], axis=0)

    for layer in range(len(zeros[BIG_NAMES[0]])):
        hall = _layer(hall, {n: whole[n][layer] for n in BIG_NAMES}, {n: zeros[n][layer] for n in BIG_NAMES},
                      {n: small[n][layer] for n in PER_LAYER_SMALL}, cs8, C, ropes)
    return loss_head(hall[C:], small["final_norm"][None, :], target, "loss_head")


def kernel(*args):
    given = dict(zip(ARG_NAMES + ['loss_target'] + ['m_' + n for n in WEIGHTS] + ['v_' + n for n in WEIGHTS], args))
    x, c, ctx, target = given['x'][0], given['c'][0], given['ctx'][0], given['loss_target'][0]
    D = x.shape[-1]
    shards = {n: given[n] for n in BIG_NAMES}
    small = {n: given[n] for n in SMALL}

    whole = gather_weights(shards)
    whole["w_in"] = [_pad_w_in(w, D) for w in whole["w_in"]]
    zeros = {n: [jnp.zeros(w.shape, F32) for w in whole[n]] for n in BIG_NAMES}

    loss, (gz, gsmall, gx) = jax.value_and_grad(_loss_fn)((zeros, small, x), x, c, ctx, target, whole)
    loss = lax.psum(loss, ("x", "y", "c"))
    gz["w_in"] = [_unpad_w_in(g, D) for g in gz["w_in"]]
    gbig = scatter_gradients(gz, shards)

    res = {}
    for name in BIG_NAMES:
        shp = given[name].shape
        two_d = (shp[0] * shp[1], shp[2])
        outs = _adamw_call(given[name].reshape(two_d), gbig[name].reshape((1,) + two_d), given['m_' + name].reshape(two_d),
                           given['v_' + name].reshape(two_d), "adamw_" + name)
        res[name] = [o.reshape(shp) for o in outs]

    def flat_small(d):
        v = jnp.concatenate([d[n].reshape(-1) for n in SMALL])
        rows = _flat_rows(v.shape[0], 8)
        return jnp.pad(v, (0, rows * FLAT_W - v.shape[0])).reshape(rows, FLAT_W)

    gathered = _all_gather(flat_small(gsmall), "small_gather")
    outs = _adamw_call(flat_small(small), gathered, flat_small({n: given['m_' + n] for n in SMALL}),
                       flat_small({n: given['v_' + n] for n in SMALL}), "adamw_small")
    off = 0
    for name in SMALL:
        shp = given[name].shape
        size = math.prod(shp)
        res[name] = [o.reshape(-1)[off:off + size].reshape(shp) for o in outs]
        off += size

    return (loss, gx[None], *[res[n][0] for n in WEIGHTS], *[res[n][1] for n in WEIGHTS],
            *[res[n][2] for n in WEIGHTS], *[res[n][3] for n in WEIGHTS])
```

```python
import functools
import math

import jax
import jax.numpy as jnp
from jax import lax
from jax.experimental import pallas as pl
from jax.experimental.pallas import tpu as pltpu

F32 = jnp.float32
MXU_DTYPE = jnp.bfloat16
WIRE_DTYPE = jnp.bfloat16

GRID_W = 64
MLA_HEADS, MLA_NOPE, MLA_ROPE, MLA_V = 8, 64, 32, 64
MLA_Q_RANK, MLA_KV_RANK = 384, 256
SSM_WIDTH, SSM_GROUP, SSM_STATE = 512, 16, 64
SSM_GROUPS = SSM_WIDTH // SSM_GROUP
SSM_CHUNK_GROUPS = 8
SSM_CHUNKS = SSM_GROUPS // SSM_CHUNK_GROUPS
SSM_CW = SSM_CHUNK_GROUPS * SSM_STATE
SCAN_SUB = 32
GQA_HEADS, GQA_KV_HEADS, GQA_HEAD_DIM = 8, 2, 64
WINDOW, BLOCK = 128, 128
N_BRANCH, N_MOD = 3, 9
ROPE_BASE = 10000.0
EPS = 1e-6
NEG_INF = -1e30
LOG2E, LN2 = math.log2(math.e), math.log(2.0)
LANES = 128
FLAT_W = 1024

ADAM_LR, ADAM_B1, ADAM_B2, ADAM_EPS, ADAM_WD, ADAM_STEP = 0.001, 0.9, 0.999, 1e-08, 0.01, 10

VMEM_LIMIT = 48 * 1024 * 1024

ARG_NAMES = ['x', 'c', 'ctx', 'c_ctx', 'ada_w', 'ada_b', 'norm_ffn1', 'norm_mix', 'norm_ffn2', 'ffn1_w13', 'ffn1_w2', 'ffn2_w13', 'ffn2_w2', 'w_in', 'mla_q_norm', 'mla_kv_norm', 'mla_w_uq', 'mla_w_ukv', 'mla_w_o', 'ssm_lambda_re', 'ssm_lambda_im', 'ssm_log_dt', 'ssm_b_re', 'ssm_b_im', 'ssm_c_re', 'ssm_c_im', 'ssm_d', 'ssm_w_glu', 'gqa_sink', 'gqa_w_o', 'w_out', 'final_norm']
WEIGHTS = ARG_NAMES[3:]
BIG = (('ada_w', 2), ('ffn1_w13', 2), ('ffn1_w2', 1), ('ffn2_w13', 2), ('ffn2_w2', 1), ('w_in', 2), ('mla_w_uq', 2),
       ('mla_w_ukv', 2), ('mla_w_o', 2), ('ssm_w_glu', 2), ('gqa_w_o', 2), ('w_out', 1))
BIG_NAMES = tuple(n for n, _ in BIG)
SMALL = tuple(n for n in WEIGHTS if n not in BIG_NAMES)
N_CHIPS = 4


def _pick(n, prefs):
    for p in prefs:
        if n % p == 0:
            return p
    return n


def _params(sem=None):
    return pltpu.CompilerParams(dimension_semantics=sem, vmem_limit_bytes=VMEM_LIMIT)


def _mm_call(a, b, *, grid, a_spec, b_spec, o_spec, o_shape, acc_shape, ta, tb, name, out_dtype=F32):
    nk = grid[2]
    dn = (((0 if ta else 1,), (1 if tb else 0,)), ((), ()))

    def body(a_ref, b_ref, o_ref, acc_ref):
        k = pl.program_id(2)

        @pl.when(k == 0)
        def _():
            acc_ref[...] = jnp.zeros_like(acc_ref)

        acc_ref[...] += lax.dot_general(a_ref[...].astype(MXU_DTYPE), b_ref[...].astype(MXU_DTYPE), dn,
                                        preferred_element_type=F32)

        @pl.when(k == nk - 1)
        def _():
            o_ref[...] = acc_ref[...].astype(o_ref.dtype)

    return pl.pallas_call(
        body, name=name, grid=grid, in_specs=[a_spec, b_spec], out_specs=o_spec,
        out_shape=jax.ShapeDtypeStruct(o_shape, out_dtype), scratch_shapes=[pltpu.VMEM(acc_shape, F32)],
        compiler_params=_params(("parallel", "parallel", "arbitrary")))(a, b)


_ROWS = (768, 512, 256, 128, 64, 32, 16, 8)
_WIDE = (1408, 1024, 512, 256, 128)
MAX_WHOLE = 2816


def _feat(n):
    return n if n <= _WIDE[0] else _pick(n, _WIDE)


def _mm_nn(x, w, name):
    M, K = x.shape
    N = w.shape[1]
    tm, tn = _pick(M, _ROWS), _pick(N, (512, 256, 128))
    tk = K if K <= MAX_WHOLE else _pick(K, (512, 256, 128))
    return _mm_call(x, w, grid=(M // tm, N // tn, K // tk),
                    a_spec=pl.BlockSpec((tm, tk), lambda i, j, k: (i, k)),
                    b_spec=pl.BlockSpec((tk, tn), lambda i, j, k: (k, j)),
                    o_spec=pl.BlockSpec((tm, tn), lambda i, j, k: (i, j)),
                    o_shape=(M, N), acc_shape=(tm, tn), ta=False, tb=False, name=name)


def _mm_nt(dy, w, name):
    M, N = dy.shape
    K = w.shape[0]
    tm, tn, tk = _pick(M, _ROWS), _feat(K), _feat(N)
    return _mm_call(dy, w, grid=(M // tm, K // tn, N // tk),
                    a_spec=pl.BlockSpec((tm, tk), lambda i, j, k: (i, k)),
                    b_spec=pl.BlockSpec((tn, tk), lambda i, j, k: (j, k)),
                    o_spec=pl.BlockSpec((tm, tn), lambda i, j, k: (i, j)),
                    o_shape=(M, K), acc_shape=(tm, tn), ta=False, tb=True, name=name)


def _mm_tn(x, dy, name):
    M, K = x.shape
    N = dy.shape[1]
    tm, tn, tk = _feat(K), _feat(N), _pick(M, (256, 128, 64, 32, 16, 8))
    return _mm_call(x, dy, grid=(K // tm, N // tn, M // tk),
                    a_spec=pl.BlockSpec((tk, tm), lambda i, j, k: (k, i)),
                    b_spec=pl.BlockSpec((tk, tn), lambda i, j, k: (k, j)),
                    o_spec=pl.BlockSpec((tm, tn), lambda i, j, k: (i, j)),
                    o_shape=(K, N), acc_shape=(tm, tn), ta=True, tb=False, name=name)


@functools.partial(jax.custom_vjp, nondiff_argnums=(3,))
def linear(x, w, wz, name):
    return _mm_nn(x, w, name)


def _linear_fwd(x, w, wz, name):
    return _mm_nn(x, w, name), (x, w)


def _linear_bwd(name, res, dy):
    x, w = res
    return _mm_nt(dy, w, name + "_dx"), jnp.zeros_like(w), _mm_tn(x, dy, name + "_dw")


linear.defvjp(_linear_fwd, _linear_bwd)


_BD_ROWS = (256, 128, 64, 32, 16, 8)


def _bd_call(a, b, nblk, kind, name):
    M = a.shape[0]
    tm = _pick(M, _BD_ROWS)
    if kind == "tn":
        aj, bj = a.shape[1] // nblk, b.shape[1] // nblk
        o_shape, o_spec = (nblk * aj, bj), pl.BlockSpec((nblk * aj, bj), lambda i: (0, 0))
        b_spec = pl.BlockSpec((tm, b.shape[1]), lambda i: (i, 0))
    else:
        aj = a.shape[1] // nblk
        wj = b.shape[0] // nblk
        oj = b.shape[1] if kind == "nn" else wj
        o_shape, o_spec = (M, nblk * oj), pl.BlockSpec((tm, nblk * oj), lambda i: (i, 0))
        b_spec = pl.BlockSpec(b.shape, lambda i: (0, 0))

    def body(a_ref, b_ref, o_ref):
        if kind == "tn":
            @pl.when(pl.program_id(0) == 0)
            def _():
                o_ref[...] = jnp.zeros_like(o_ref)

        for j in range(nblk):
            if kind == "nn":
                o_ref[:, j * oj:(j + 1) * oj] = _dot(a_ref[:, j * aj:(j + 1) * aj], b_ref[j * wj:(j + 1) * wj, :], _DN_NN)
            elif kind == "nt":
                o_ref[:, j * oj:(j + 1) * oj] = _dot(a_ref[:, j * aj:(j + 1) * aj], b_ref[j * wj:(j + 1) * wj, :], _DN_NT)
            else:
                o_ref[j * aj:(j + 1) * aj, :] += _dot(a_ref[:, j * aj:(j + 1) * aj], b_ref[:, j * bj:(j + 1) * bj], _DN_TN)

    return pl.pallas_call(body, name=name, grid=(M // tm,), in_specs=[pl.BlockSpec((tm, a.shape[1]), lambda i: (i, 0)), b_spec],
                          out_specs=o_spec, out_shape=jax.ShapeDtypeStruct(o_shape, F32),
                          compiler_params=_params(("arbitrary",) if kind == "tn" else ("parallel",)))(a, b)


def _bd_nn(x, w, nblk, name):
    return _bd_call(x, w, nblk, "nn", name)


def _bd_nt(dy, w, nblk, name):
    return _bd_call(dy, w, nblk, "nt", name)


def _bd_tn(x, dy, nblk, name):
    return _bd_call(x, dy, nblk, "tn", name)


@functools.partial(jax.custom_vjp, nondiff_argnums=(2, 3))
def bd_linear(x, w, nblk, name):
    return _bd_nn(x, w, nblk, name)


def _bd_fwd(x, w, nblk, name):
    return _bd_nn(x, w, nblk, name), (x, w)


def _bd_bwd(nblk, name, res, dy):
    x, w = res
    return _bd_nt(dy, w, nblk, name + "_dx"), _bd_tn(x, dy, nblk, name + "_dw")


bd_linear.defvjp(_bd_fwd, _bd_bwd)


def _row_tile(n_ctx, n_all):
    return _pick(math.gcd(n_ctx, n_all), (256, 128, 64, 32, 16, 8))


def _by_group(ref, is_ctx):
    return jnp.where(is_ctx, ref[0:1, :], ref[1:2, :])


def _acc_by_group(ref, is_ctx, part):
    ref[0:1, :] += jnp.where(is_ctx, part, 0.0)
    ref[1:2, :] += jnp.where(is_ctx, 0.0, part)


def _norm_fwd_call(x, g, shift, scale, n_ctx, name):
    M, D = x.shape
    has_mod = shift is not None
    tm = _row_tile(n_ctx, M) if has_mod else _pick(M, (256, 128, 64, 32, 16, 8))
    nct = n_ctx // tm

    def body(*refs):
        if has_mod:
            x_ref, g_ref, sh_ref, sc_ref, o_ref = refs
        else:
            x_ref, g_ref, o_ref = refs
        xv = x_ref[...]
        r = lax.rsqrt(jnp.mean(xv * xv, axis=-1, keepdims=True) + EPS)
        y = xv * r * g_ref[...]
        if has_mod:
            is_ctx = pl.program_id(0) < nct
            y = y * (1.0 + _by_group(sc_ref, is_ctx)) + _by_group(sh_ref, is_ctx)
        o_ref[...] = y

    row = pl.BlockSpec((tm, D), lambda i: (i, 0))
    vec = pl.BlockSpec((1, D), lambda i: (0, 0))
    two = pl.BlockSpec((2, D), lambda i: (0, 0))
    args = (x, g) + ((shift, scale) if has_mod else ())
    return pl.pallas_call(body, name=name, grid=(M // tm,), in_specs=[row, vec] + ([two, two] if has_mod else []),
                          out_specs=row, out_shape=jax.ShapeDtypeStruct((M, D), F32),
                          compiler_params=_params(("parallel",)))(*args)


def _norm_bwd_call(x, g, shift, scale, dy, n_ctx, name):
    M, D = x.shape
    has_mod = shift is not None
    tm = _row_tile(n_ctx, M) if has_mod else _pick(M, (256, 128, 64, 32, 16, 8))
    nct = n_ctx // tm

    def body(*refs):
        if has_mod:
            x_ref, g_ref, sc_ref, dy_ref, dx_ref, dg_ref, dsh_ref, dsc_ref = refs
        else:
            x_ref, g_ref, dy_ref, dx_ref, dg_ref = refs
        i = pl.program_id(0)

        @pl.when(i == 0)
        def _():
            dg_ref[...] = jnp.zeros_like(dg_ref)
            if has_mod:
                dsh_ref[...] = jnp.zeros_like(dsh_ref)
                dsc_ref[...] = jnp.zeros_like(dsc_ref)

        xv, gv, dyv = x_ref[...], g_ref[...], dy_ref[...]
        r = lax.rsqrt(jnp.mean(xv * xv, axis=-1, keepdims=True) + EPS)
        xhat = xv * r
        if has_mod:
            is_ctx = i < nct
            dy0 = dyv * (1.0 + _by_group(sc_ref, is_ctx))
            _acc_by_group(dsc_ref, is_ctx, jnp.sum(dyv * xhat * gv, axis=0, keepdims=True))
            _acc_by_group(dsh_ref, is_ctx, jnp.sum(dyv, axis=0, keepdims=True))
        else:
            dy0 = dyv
        dg_ref[...] += jnp.sum(dy0 * xhat, axis=0, keepdims=True)
        dxhat = dy0 * gv
        dx_ref[...] = r * (dxhat - xhat * jnp.mean(dxhat * xhat, axis=-1, keepdims=True))

    row = pl.BlockSpec((tm, D), lambda i: (i, 0))
    vec = pl.BlockSpec((1, D), lambda i: (0, 0))
    two = pl.BlockSpec((2, D), lambda i: (0, 0))
    if has_mod:
        args, in_specs = (x, g, scale, dy), [row, vec, two, row]
        out_specs = [row, vec, two, two]
        out_shape = [jax.ShapeDtypeStruct((M, D), F32), jax.ShapeDtypeStruct((1, D), F32),
                     jax.ShapeDtypeStruct((2, D), F32), jax.ShapeDtypeStruct((2, D), F32)]
    else:
        args, in_specs = (x, g, dy), [row, vec, row]
        out_specs = [row, vec]
        out_shape = [jax.ShapeDtypeStruct((M, D), F32), jax.ShapeDtypeStruct((1, D), F32)]
    return pl.pallas_call(body, name=name, grid=(M // tm,), in_specs=in_specs, out_specs=out_specs, out_shape=out_shape,
                          compiler_params=_params(("arbitrary",)))(*args)


@functools.partial(jax.custom_vjp, nondiff_argnums=(4, 5))
def norm_mod(x, g, shift, scale, n_ctx, name):
    return _norm_fwd_call(x, g, shift, scale, n_ctx, name)


def _norm_mod_fwd(x, g, shift, scale, n_ctx, name):
    return _norm_fwd_call(x, g, shift, scale, n_ctx, name), (x, g, shift, scale)


def _norm_mod_bwd(n_ctx, name, res, dy):
    x, g, shift, scale = res
    dx, dg, dsh, dsc = _norm_bwd_call(x, g, shift, scale, dy, n_ctx, name + "_bwd")
    return dx, dg, dsh, dsc


norm_mod.defvjp(_norm_mod_fwd, _norm_mod_bwd)


@functools.partial(jax.custom_vjp, nondiff_argnums=(2,))
def rmsnorm(x, g, name):
    return _norm_fwd_call(x, g, None, None, 0, name)


def _rmsnorm_fwd(x, g, name):
    return _norm_fwd_call(x, g, None, None, 0, name), (x, g)


def _rmsnorm_bwd(name, res, dy):
    x, g = res
    dx, dg = _norm_bwd_call(x, g, None, None, dy, 0, name + "_bwd")
    return dx, dg


rmsnorm.defvjp(_rmsnorm_fwd, _rmsnorm_bwd)


def _gres_fwd_call(h, o, gate, coef, n_ctx, name):
    M, D = h.shape
    tm = _row_tile(n_ctx, M)
    nct = n_ctx // tm

    def body(h_ref, o_ref, g_ref, out_ref):
        is_ctx = pl.program_id(0) < nct
        out_ref[...] = h_ref[...] + coef * _by_group(g_ref, is_ctx) * o_ref[...]

    row = pl.BlockSpec((tm, D), lambda i: (i, 0))
    two = pl.BlockSpec((2, D), lambda i: (0, 0))
    return pl.pallas_call(body, name=name, grid=(M // tm,), in_specs=[row, row, two], out_specs=row,
                          out_shape=jax.ShapeDtypeStruct((M, D), F32), compiler_params=_params(("parallel",)))(h, o, gate)


def _gres_bwd_call(o, gate, d, coef, n_ctx, name):
    M, D = o.shape
    tm = _row_tile(n_ctx, M)
    nct = n_ctx // tm

    def body(o_ref, g_ref, d_ref, do_ref, dg_ref):
        i = pl.program_id(0)
        is_ctx = i < nct

        @pl.when(i == 0)
        def _():
            dg_ref[...] = jnp.zeros_like(dg_ref)

        dv = d_ref[...]
        do_ref[...] = coef * _by_group(g_ref, is_ctx) * dv
        _acc_by_group(dg_ref, is_ctx, coef * jnp.sum(dv * o_ref[...], axis=0, keepdims=True))

    row = pl.BlockSpec((tm, D), lambda i: (i, 0))
    two = pl.BlockSpec((2, D), lambda i: (0, 0))
    return pl.pallas_call(body, name=name, grid=(M // tm,), in_specs=[row, two, row], out_specs=[row, two],
                          out_shape=[jax.ShapeDtypeStruct((M, D), F32), jax.ShapeDtypeStruct((2, D), F32)],
                          compiler_params=_params(("arbitrary",)))(o, gate, d)


@functools.partial(jax.custom_vjp, nondiff_argnums=(3, 4, 5))
def gated_residual(h, o, gate, coef, n_ctx, name):
    return _gres_fwd_call(h, o, gate, coef, n_ctx, name)


def _gres_fwd(h, o, gate, coef, n_ctx, name):
    return _gres_fwd_call(h, o, gate, coef, n_ctx, name), (o, gate)


def _gres_bwd(coef, n_ctx, name, res, d):
    o, gate = res
    do, dg = _gres_bwd_call(o, gate, d, coef, n_ctx, name + "_bwd")
    return d, do, dg


gated_residual.defvjp(_gres_fwd, _gres_bwd)


def _swiglu_fwd_call(ab, name):
    M, F2 = ab.shape
    Fh = F2 // 2
    tm, tn = _pick(M, (128, 64, 32, 16, 8)), Fh
    nf = Fh // tn

    def body(a_ref, b_ref, o_ref):
        a = a_ref[...]
        o_ref[...] = a * jax.nn.sigmoid(a) * b_ref[...]

    return pl.pallas_call(body, name=name, grid=(M // tm, nf),
                          in_specs=[pl.BlockSpec((tm, tn), lambda i, j: (i, j)), pl.BlockSpec((tm, tn), lambda i, j: (i, j + nf))],
                          out_specs=pl.BlockSpec((tm, tn), lambda i, j: (i, j)),
                          out_shape=jax.ShapeDtypeStruct((M, Fh), F32), compiler_params=_params(("parallel", "parallel")))(ab, ab)


def _swiglu_bwd_call(ab, dact, name):
    M, F2 = ab.shape
    Fh = F2 // 2
    tm, tn = _pick(M, (128, 64, 32, 16, 8)), Fh
    nf = Fh // tn

    def body(a_ref, b_ref, d_ref, o_ref):
        a, b, d = a_ref[...], b_ref[...], d_ref[...]
        sig = jax.nn.sigmoid(a)
        da = d * b * sig * (1.0 + a * (1.0 - sig))
        db = d * a * sig
        o_ref[...] = jnp.where(pl.program_id(1) < nf, da, db)

    return pl.pallas_call(body, name=name, grid=(M // tm, 2 * nf),
                          in_specs=[pl.BlockSpec((tm, tn), lambda i, j: (i, j % nf)),
                                    pl.BlockSpec((tm, tn), lambda i, j: (i, j % nf + nf)),
                                    pl.BlockSpec((tm, tn), lambda i, j: (i, j % nf))],
                          out_specs=pl.BlockSpec((tm, tn), lambda i, j: (i, j)),
                          out_shape=jax.ShapeDtypeStruct((M, F2), F32), compiler_params=_params(("parallel", "parallel")))(ab, ab, dact)


@functools.partial(jax.custom_vjp, nondiff_argnums=(1,))
def swiglu_act(ab, name):
    return _swiglu_fwd_call(ab, name)


def _swiglu_fwd(ab, name):
    return _swiglu_fwd_call(ab, name), (ab,)


def _swiglu_bwd(name, res, d):
    return (_swiglu_bwd_call(res[0], d, name + "_bwd"),)


swiglu_act.defvjp(_swiglu_fwd, _swiglu_bwd)


_DN_NT = (((1,), (1,)), ((), ()))
_DN_TN = (((0,), (0,)), ((), ()))
_DN_NN = (((1,), (0,)), ((), ()))


def _dot(a, b, dn):
    return lax.dot_general(a.astype(MXU_DTYPE), b.astype(MXU_DTYPE), dn, preferred_element_type=F32)


_TQ = (1024, 512, 256, 128, 64, 32, 16, 8)


def _flash_fwd_call(q, k1, v1, k2, v2, name):
    H, Tq, dk = q.shape
    T1, dv = k1.shape[1], v1.shape[2]
    has2 = k2 is not None
    tq, tk = _pick(Tq, _TQ), _pick(T1, _TQ)
    off = 1 if has2 else 0
    nkv = T1 // tk + off
    C = k2.shape[1] if has2 else 0
    rows = max(tk, C)

    def body(*refs):
        if has2:
            q_ref, k1_ref, v1_ref, k2_ref, v2_ref, o_ref, lse_ref, m_s, acc_s, va_s = refs
        else:
            q_ref, k1_ref, v1_ref, o_ref, lse_ref, m_s, acc_s, va_s = refs
        j = pl.program_id(2)

        @pl.when(j == 0)
        def _():
            m_s[...] = jnp.full_like(m_s, NEG_INF)
            acc_s[...] = jnp.zeros_like(acc_s)
            va_s[:, dv:2 * dv] = jnp.ones((rows, dv), MXU_DTYPE)

        def step(k, v, n):
            va_s[0:n, 0:dv] = v.astype(MXU_DTYPE)
            s = _dot(q_ref[0], k, _DN_NT)
            m_prev = m_s[...]
            m_new = jnp.maximum(m_prev, jnp.max(s, axis=-1, keepdims=True))
            p = jnp.exp2(s - m_new)
            acc_s[...] = jnp.exp2(m_prev - m_new) * acc_s[...] + _dot(p, va_s[0:n, :], _DN_NN)
            m_s[...] = m_new

        if has2:
            @pl.when(j == 0)
            def _():
                step(k2_ref[0], v2_ref[0], C)

            @pl.when(j > 0)
            def _():
                step(k1_ref[0], v1_ref[0], tk)
        else:
            step(k1_ref[0], v1_ref[0], tk)

        @pl.when(j == nkv - 1)
        def _():
            l = acc_s[:, dv:dv + 1]
            o_ref[0] = acc_s[:, 0:dv] / l
            lse_ref[0] = m_s[...] + jnp.log2(l)

    qs = pl.BlockSpec((1, tq, dk), lambda h, i, j: (h, i, 0))
    k1s = pl.BlockSpec((1, tk, dk), lambda h, i, j: (h, jnp.maximum(j - off, 0), 0))
    v1s = pl.BlockSpec((1, tk, dv), lambda h, i, j: (h, jnp.maximum(j - off, 0), 0))
    in_specs, args = [qs, k1s, v1s], [q, k1, v1]
    if has2:
        in_specs += [pl.BlockSpec((1, C, dk), lambda h, i, j: (h, 0, 0)), pl.BlockSpec((1, C, dv), lambda h, i, j: (h, 0, 0))]
        args += [k2, v2]
    return pl.pallas_call(
        body, name=name, grid=(H, Tq // tq, nkv), in_specs=in_specs,
        out_specs=[pl.BlockSpec((1, tq, dv), lambda h, i, j: (h, i, 0)), pl.BlockSpec((1, tq, 1), lambda h, i, j: (h, i, 0))],
        out_shape=[jax.ShapeDtypeStruct((H, Tq, dv), F32), jax.ShapeDtypeStruct((H, Tq, 1), F32)],
        scratch_shapes=[pltpu.VMEM((tq, 1), F32), pltpu.VMEM((tq, 2 * dv), F32), pltpu.VMEM((rows, 2 * dv), MXU_DTYPE)],
        compiler_params=_params(("parallel", "parallel", "arbitrary")))(*args)


def _flash_bwd_call(q, k1, v1, k2, v2, o, lse, do, name):
    H, Tq, dk = q.shape
    T1, dv = k1.shape[1], v1.shape[2]
    has2 = k2 is not None
    tq, tk = _pick(Tq, _TQ), _pick(T1, _TQ)
    off = 1 if has2 else 0
    nkv, nq = T1 // tk + off, Tq // tq
    C = k2.shape[1] if has2 else 0
    rows = max(tk, C)

    def body(*refs):
        if has2:
            (q_ref, k1_ref, v1_ref, k2_ref, v2_ref, o_ref, lse_ref, do_ref,
             dq_ref, dk1_ref, dv1_ref, dk2_ref, dv2_ref, dk_s, dv_s) = refs
        else:
            q_ref, k1_ref, v1_ref, o_ref, lse_ref, do_ref, dq_ref, dk1_ref, dv1_ref, dk_s, dv_s = refs
        j, i = pl.program_id(1), pl.program_id(2)

        @pl.when((j == 0) & (i == 0))
        def _():
            dq_ref[...] = jnp.zeros_like(dq_ref)

        @pl.when(i == 0)
        def _():
            dk_s[...] = jnp.zeros_like(dk_s)
            dv_s[...] = jnp.zeros_like(dv_s)

        def step(k, v, n):
            qb, dob = q_ref[0], do_ref[0]
            p = jnp.exp2(_dot(qb, k, _DN_NT) - lse_ref[0])
            dv_s[0:n, :] += _dot(p, dob, _DN_TN)
            dol = dob * LN2
            ds = p * (_dot(dol, v, _DN_NT) - jnp.sum(dol * o_ref[0], axis=-1, keepdims=True))
            dk_s[0:n, :] += _dot(ds, qb, _DN_TN)
            r0 = pl.multiple_of(i * tq, tq)
            dq_ref[0, pl.ds(r0, tq), :] += _dot(ds, k, _DN_NN)

        if has2:
            @pl.when(j == 0)
            def _():
                step(k2_ref[0], v2_ref[0], C)

            @pl.when(j > 0)
            def _():
                step(k1_ref[0], v1_ref[0], tk)

            @pl.when((i == nq - 1) & (j == 0))
            def _():
                dk2_ref[0] = dk_s[0:C, :]
                dv2_ref[0] = dv_s[0:C, :]

            @pl.when((i == nq - 1) & (j > 0))
            def _():
                dk1_ref[0] = dk_s[0:tk, :]
                dv1_ref[0] = dv_s[0:tk, :]
        else:
            step(k1_ref[0], v1_ref[0], tk)

            @pl.when(i == nq - 1)
            def _():
                dk1_ref[0] = dk_s[...]
                dv1_ref[0] = dv_s[...]

    qs = pl.BlockSpec((1, tq, dk), lambda h, j, i: (h, i, 0))
    os_ = pl.BlockSpec((1, tq, dv), lambda h, j, i: (h, i, 0))
    ls = pl.BlockSpec((1, tq, 1), lambda h, j, i: (h, i, 0))
    k1s = pl.BlockSpec((1, tk, dk), lambda h, j, i: (h, jnp.maximum(j - off, 0), 0))
    v1s = pl.BlockSpec((1, tk, dv), lambda h, j, i: (h, jnp.maximum(j - off, 0), 0))
    in_specs, args = [qs, k1s, v1s], [q, k1, v1]
    out_specs = [pl.BlockSpec((1, Tq, dk), lambda h, j, i: (h, 0, 0)), k1s, v1s]
    out_shape = [jax.ShapeDtypeStruct((H, Tq, dk), F32), jax.ShapeDtypeStruct((H, T1, dk), F32),
                 jax.ShapeDtypeStruct((H, T1, dv), F32)]
    if has2:
        k2s = pl.BlockSpec((1, C, dk), lambda h, j, i: (h, 0, 0))
        v2s = pl.BlockSpec((1, C, dv), lambda h, j, i: (h, 0, 0))
        in_specs += [k2s, v2s]
        args += [k2, v2]
        out_specs += [k2s, v2s]
        out_shape += [jax.ShapeDtypeStruct((H, C, dk), F32), jax.ShapeDtypeStruct((H, C, dv), F32)]
    in_specs += [os_, ls, os_]
    args += [o, lse, do]
    return pl.pallas_call(
        body, name=name, grid=(H, nkv, nq), in_specs=in_specs, out_specs=out_specs, out_shape=out_shape,
        scratch_shapes=[pltpu.VMEM((rows, dk), F32), pltpu.VMEM((rows, dv), F32)],
        compiler_params=_params(("parallel", "arbitrary", "arbitrary")))(*args)


@functools.partial(jax.custom_vjp, nondiff_argnums=(5,))
def flash2(q, k1, v1, k2, v2, name):
    return _flash_fwd_call(q, k1, v1, k2, v2, name)[0]


def _flash2_fwd(q, k1, v1, k2, v2, name):
    o, lse = _flash_fwd_call(q, k1, v1, k2, v2, name)
    return o, (q, k1, v1, k2, v2, o, lse)


def _flash2_bwd(name, res, do):
    q, k1, v1, k2, v2, o, lse = res
    return tuple(_flash_bwd_call(q, k1, v1, k2, v2, o, lse, do, name + "_bwd"))


flash2.defvjp(_flash2_fwd, _flash2_bwd)


@functools.partial(jax.custom_vjp, nondiff_argnums=(3,))
def flash1(q, k, v, name):
    return _flash_fwd_call(q, k, v, None, None, name)[0]


def _flash1_fwd(q, k, v, name):
    o, lse = _flash_fwd_call(q, k, v, None, None, name)
    return o, (q, k, v, o, lse)


def _flash1_bwd(name, res, do):
    q, k, v, o, lse = res
    return tuple(_flash_bwd_call(q, k, v, None, None, o, lse, do, name + "_bwd"))


flash1.defvjp(_flash1_fwd, _flash1_bwd)


def _scan_call(a, x, s, *, rev, adj, n_ctx, name):
    M, W = x.shape
    cw = SSM_CW
    J = W // (2 * cw)
    L = _row_tile(n_ctx, M)
    nt, nc = M // L, n_ctx // L
    asc = rev == adj
    sub = min(SCAN_SUB, L)
    nsub = L // sub
    n_steps = int(math.log2(sub))
    assert 1 << n_steps == sub

    def tile(t):
        if not rev:
            return nt - 1 - t if adj else t
        if not adj:
            return jnp.where(t < nc, nc - 1 - t, nt - 1 - (t - nc))
        return jnp.where(t < nt - nc, nc + t, t - (nt - nc))

    def body(*refs):
        if adj:
            a_ref, x_ref, s_ref, o_ref, da_ref, car_ref = refs
        else:
            a_ref, x_ref, o_ref, car_ref = refs
        t = pl.program_id(1)

        @pl.when(t == 0)
        def _():
            car_ref[...] = jnp.zeros_like(car_ref)
            if adj:
                da_ref[...] = jnp.zeros_like(da_ref)

        ar, ai = a_ref[:, 0:cw], a_ref[:, cw:2 * cw]
        powers, pr, pi = [], ar, ai
        for _ in range(n_steps):
            powers.append((pr, pi))
            pr, pi = pr * pr - pi * pi, 2.0 * pr * pi
        row = lax.broadcasted_iota(jnp.int32, (sub, cw), 0)
        first, last = (0, sub - 1) if asc else (sub - 1, 0)

        def scan_rows(i, carry):
            cr, ci = carry[0], carry[1]
            r0 = pl.multiple_of((i if asc else nsub - 1 - i) * sub, sub)
            xr, xi = x_ref[pl.ds(r0, sub), 0:cw], x_ref[pl.ds(r0, sub), cw:2 * cw]
            xr = xr + jnp.where(row == first, ar * cr - ai * ci, 0.0)
            xi = xi + jnp.where(row == first, ar * ci + ai * cr, 0.0)
            k = 1
            for pr, pi in powers:
                if asc:
                    sr, si, keep = pltpu.roll(xr, k, 0), pltpu.roll(xi, k, 0), row >= k
                else:
                    sr, si, keep = pltpu.roll(xr, sub - k, 0), pltpu.roll(xi, sub - k, 0), row < sub - k
                sr, si = jnp.where(keep, sr, 0.0), jnp.where(keep, si, 0.0)
                xr, xi = xr + pr * sr - pi * si, xi + pr * si + pi * sr
                k *= 2
            o_ref[pl.ds(r0, sub), 0:cw] = xr
            o_ref[pl.ds(r0, sub), cw:2 * cw] = xi
            out = (jnp.sum(jnp.where(row == last, xr, 0.0), axis=0, keepdims=True),
                   jnp.sum(jnp.where(row == last, xi, 0.0), axis=0, keepdims=True))
            if adj:
                if asc:
                    gr, gi = pltpu.roll(xr, 1, 0), pltpu.roll(xi, 1, 0)
                else:
                    gr, gi = pltpu.roll(xr, sub - 1, 0), pltpu.roll(xi, sub - 1, 0)
                gr, gi = jnp.where(row == first, cr, gr), jnp.where(row == first, ci, gi)
                sr, si = s_ref[pl.ds(r0, sub), 0:cw], s_ref[pl.ds(r0, sub), cw:2 * cw]
                out += (carry[2] + jnp.sum(sr * gr + si * gi, axis=0, keepdims=True),
                        carry[3] + jnp.sum(sr * gi - si * gr, axis=0, keepdims=True))
            return out

        init = (car_ref[:, 0:cw], car_ref[:, cw:2 * cw])
        if adj:
            init += (jnp.zeros((1, cw), F32), jnp.zeros((1, cw), F32))
        done = lax.fori_loop(0, nsub, scan_rows, init)
        car_ref[:, 0:cw] = done[0]
        car_ref[:, cw:2 * cw] = done[1]
        if adj:
            da_ref[:, 0:cw] += done[2]
            da_ref[:, cw:2 * cw] += done[3]

    blk = pl.BlockSpec((L, 2 * cw), lambda j, t: (tile(t), j))
    vec = pl.BlockSpec((1, 2 * cw), lambda j, t: (0, j))
    if adj:
        in_specs, args = [vec, blk, blk], (a, x, s)
        out_specs = [blk, vec]
        out_shape = [jax.ShapeDtypeStruct((M, W), F32), jax.ShapeDtypeStruct((1, W), F32)]
    else:
        in_specs, args = [vec, blk], (a, x)
        out_specs = blk
        out_shape = jax.ShapeDtypeStruct((M, W), F32)
    return pl.pallas_call(body, name=name, grid=(J, nt), in_specs=in_specs, out_specs=out_specs, out_shape=out_shape,
                          scratch_shapes=[pltpu.VMEM((1, 2 * cw), F32)],
                          compiler_params=_params(("parallel", "arbitrary")))(*args)


def _conj_layout(a):
    cw = SSM_CW
    J = a.shape[1] // (2 * cw)
    a4 = a.reshape(1, J, 2, cw)
    return jnp.concatenate([a4[:, :, 0:1], -a4[:, :, 1:2]], axis=2).reshape(a.shape)


@functools.partial(jax.custom_vjp, nondiff_argnums=(2, 3, 4))
def diag_scan(a, x, rev, n_ctx, name):
    return _scan_call(a, x, None, rev=rev, adj=False, n_ctx=n_ctx, name=name)


def _diag_scan_fwd(a, x, rev, n_ctx, name):
    s = _scan_call(a, x, None, rev=rev, adj=False, n_ctx=n_ctx, name=name)
    return s, (a, s)


def _diag_scan_bwd(rev, n_ctx, name, res, ds):
    a, s = res
    g, da = _scan_call(_conj_layout(a), ds, s, rev=rev, adj=True, n_ctx=n_ctx, name=name + "_adj")
    return da, g


diag_scan.defvjp(_diag_scan_fwd, _diag_scan_bwd)


def _loss_call(h, g, target, name):
    M, D = h.shape
    tm = _pick(M, (256, 128, 64, 32, 16, 8))

    def body(h_ref, g_ref, t_ref, loss_ref, dh_ref, dg_ref):
        i = pl.program_id(0)

        @pl.when(i == 0)
        def _():
            loss_ref[...] = jnp.zeros_like(loss_ref)
            dg_ref[...] = jnp.zeros_like(dg_ref)

        xv, gv = h_ref[...], g_ref[...]
        r = lax.rsqrt(jnp.mean(xv * xv, axis=-1, keepdims=True) + EPS)
        xhat = xv * r
        err = xhat * gv - t_ref[...]
        loss_ref[...] += 0.5 * jnp.sum(jnp.mean(err * err, axis=-1, keepdims=True), axis=0, keepdims=True)
        dy = err * (1.0 / D)
        dg_ref[...] += jnp.sum(dy * xhat, axis=0, keepdims=True)
        dxhat = dy * gv
        dh_ref[...] = r * (dxhat - xhat * jnp.mean(dxhat * xhat, axis=-1, keepdims=True))

    row = pl.BlockSpec((tm, D), lambda i: (i, 0))
    vec = pl.BlockSpec((1, D), lambda i: (0, 0))
    one = pl.BlockSpec((1, 1), lambda i: (0, 0))
    return pl.pallas_call(body, name=name, grid=(M // tm,), in_specs=[row, vec, row], out_specs=[one, row, vec],
                          out_shape=[jax.ShapeDtypeStruct((1, 1), F32), jax.ShapeDtypeStruct((M, D), F32),
                                     jax.ShapeDtypeStruct((1, D), F32)],
                          compiler_params=_params(("arbitrary",)))(h, g, target)


@functools.partial(jax.custom_vjp, nondiff_argnums=(3,))
def loss_head(h, g, target, name):
    return _loss_call(h, g, target, name)[0][0, 0]


def _loss_head_fwd(h, g, target, name):
    loss, dh, dg = _loss_call(h, g, target, name)
    return loss[0, 0], (dh, dg, target)


def _loss_head_bwd(name, res, ct):
    dh, dg, target = res
    return ct * dh, ct * dg, jnp.zeros_like(target)


loss_head.defvjp(_loss_head_fwd, _loss_head_bwd)


def _adamw_call(w, gstack, m, v, name):
    R, Cn = w.shape
    n = gstack.shape[0]
    tr = _pick(R, (64, 32, 16, 8))

    def body(w_ref, g_ref, m_ref, v_ref, go_ref, d_ref, mo_ref, vo_ref):
        g = g_ref[0]
        for s in range(1, n):
            g = g + g_ref[s]
        mn = ADAM_B1 * m_ref[...] + (1.0 - ADAM_B1) * g
        vn = ADAM_B2 * v_ref[...] + (1.0 - ADAM_B2) * (g * g)
        m_hat = mn / (1.0 - ADAM_B1 ** ADAM_STEP)
        v_hat = vn / (1.0 - ADAM_B2 ** ADAM_STEP)
        go_ref[...] = g
        d_ref[...] = -ADAM_LR * (m_hat / (jnp.sqrt(v_hat) + ADAM_EPS) + ADAM_WD * w_ref[...])
        mo_ref[...] = mn
        vo_ref[...] = vn

    blk = pl.BlockSpec((tr, Cn), lambda i: (i, 0))
    gblk = pl.BlockSpec((n, tr, Cn), lambda i: (0, i, 0))
    sds = jax.ShapeDtypeStruct((R, Cn), F32)
    return pl.pallas_call(body, name=name, grid=(R // tr,), in_specs=[blk, gblk, blk, blk], out_specs=[blk] * 4,
                          out_shape=[sds] * 4, compiler_params=_params(("parallel",)))(w, gstack, m, v)


MESH = pl.DeviceIdType.MESH
ANY = pl.BlockSpec(memory_space=pl.ANY)


def _place():
    return lax.axis_index("x"), lax.axis_index("y"), lax.axis_index("c")


def _sibling_exchange(v, name):
    n = v.shape[0]

    def body(v_ref, o_ref, send_sems, recv_sems):
        x, y, c = _place()
        copies = [pltpu.make_async_remote_copy(src_ref=v_ref.at[k], dst_ref=o_ref.at[k], send_sem=send_sems.at[k],
                                               recv_sem=recv_sems.at[k], device_id=(x, y, 1 - c), device_id_type=MESH)
                  for k in range(n)]
        for cp in copies:
            cp.start()
        for cp in copies:
            cp.wait()

    return pl.pallas_call(body, name=name, in_specs=[ANY], out_specs=ANY, out_shape=jax.ShapeDtypeStruct(v.shape, v.dtype),
                          scratch_shapes=[pltpu.SemaphoreType.DMA((n,)), pltpu.SemaphoreType.DMA((n,))])(v)


def _sibling_exchange_half(v, name):
    n = v.shape[0]

    def body(v_ref, o_ref, send_sems, recv_sems):
        x, y, c = _place()
        copies = [pltpu.make_async_remote_copy(src_ref=v_ref.at[k, pl.ds(1 - c, 1)], dst_ref=o_ref.at[k],
                                               send_sem=send_sems.at[k], recv_sem=recv_sems.at[k],
                                               device_id=(x, y, 1 - c), device_id_type=MESH)
                  for k in range(n)]
        for cp in copies:
            cp.start()
        for cp in copies:
            cp.wait()

    return pl.pallas_call(body, name=name, in_specs=[ANY], out_specs=ANY,
                          out_shape=jax.ShapeDtypeStruct((n, 1) + v.shape[2:], v.dtype),
                          scratch_shapes=[pltpu.SemaphoreType.DMA((n,)), pltpu.SemaphoreType.DMA((n,))])(v)


def _chip_exchange(v, same, name):
    out_shape = (N_CHIPS,) + (v.shape if same else v.shape[1:])

    def body(v_ref, o_ref, send_sems, recv_sems, local_sem):
        x, y, c = _place()
        me = 2 * x + y
        own = pltpu.make_async_copy(v_ref if same else v_ref.at[me], o_ref.at[me], local_sem)
        own.start()
        copies = []
        for k, (fx, fy) in enumerate(((1, 0), (0, 1), (1, 1))):
            px, py = jnp.where(fx == 1, 1 - x, x), jnp.where(fy == 1, 1 - y, y)
            src = v_ref if same else v_ref.at[2 * px + py]
            copies.append(pltpu.make_async_remote_copy(src_ref=src, dst_ref=o_ref.at[me], send_sem=send_sems.at[k],
                                                       recv_sem=recv_sems.at[k], device_id=(px, py, c), device_id_type=MESH))
        for cp in copies:
            cp.start()
        for cp in copies:
            cp.wait()
        own.wait()

    return pl.pallas_call(body, name=name, in_specs=[ANY], out_specs=ANY, out_shape=jax.ShapeDtypeStruct(out_shape, v.dtype),
                          scratch_shapes=[pltpu.SemaphoreType.DMA((3,)), pltpu.SemaphoreType.DMA((3,)), pltpu.SemaphoreType.DMA])(v)


def _all_gather(v, name):
    def body(v_ref, o_ref, send_sems, recv_sems, local_sem):
        x, y, c = _place()
        me = 4 * x + 2 * y + c
        own = pltpu.make_async_copy(v_ref, o_ref.at[me], local_sem)
        own.start()
        copies = []
        for k in range(1, 8):
            fx, fy, fc = (k >> 2) & 1, (k >> 1) & 1, k & 1
            peer = (jnp.where(fx == 1, 1 - x, x), jnp.where(fy == 1, 1 - y, y), jnp.where(fc == 1, 1 - c, c))
            copies.append(pltpu.make_async_remote_copy(src_ref=v_ref, dst_ref=o_ref.at[me], send_sem=send_sems.at[k - 1],
                                                       recv_sem=recv_sems.at[k - 1], device_id=peer, device_id_type=MESH))
        for cp in copies:
            cp.start()
        for cp in copies:
            cp.wait()
        own.wait()

    return pl.pallas_call(body, name=name, in_specs=[ANY], out_specs=ANY,
                          out_shape=jax.ShapeDtypeStruct((8,) + v.shape, v.dtype),
                          scratch_shapes=[pltpu.SemaphoreType.DMA((7,)), pltpu.SemaphoreType.DMA((7,)), pltpu.SemaphoreType.DMA])(v)


def _add_own_half(g, r, c, name):
    n, _, R, W = g.shape
    tr = _pick(R, (256, 128, 64, 32, 16, 8))

    def body(c_ref, g_ref, r_ref, o_ref):
        o_ref[...] = g_ref[0] + r_ref[0]

    grid_spec = pltpu.PrefetchScalarGridSpec(
        num_scalar_prefetch=1, grid=(n, R // tr),
        in_specs=[pl.BlockSpec((1, 1, tr, W), lambda p, i, c_ref: (p, c_ref[0], i, 0)),
                  pl.BlockSpec((1, 1, tr, W), lambda p, i, c_ref: (p, 0, i, 0))],
        out_specs=pl.BlockSpec((1, tr, W), lambda p, i, c_ref: (p, i, 0)))
    return pl.pallas_call(body, name=name, grid_spec=grid_spec, out_shape=jax.ShapeDtypeStruct((n, R, W), F32),
                          compiler_params=_params(("parallel", "parallel")))(c.reshape(1).astype(jnp.int32), g, r)


def _sum_stack(v, name):
    n, R, W = v.shape
    tr = _pick(R, (256, 128, 64, 32, 16, 8))

    def body(v_ref, o_ref):
        acc = v_ref[0]
        for s in range(1, n):
            acc = acc + v_ref[s]
        o_ref[...] = acc

    return pl.pallas_call(body, name=name, grid=(R // tr,), in_specs=[pl.BlockSpec((n, tr, W), lambda i: (0, i, 0))],
                          out_specs=pl.BlockSpec((tr, W), lambda i: (i, 0)), out_shape=jax.ShapeDtypeStruct((R, W), F32),
                          compiler_params=_params(("parallel",)))(v)


def _flat_rows(n, mult):
    rows = -(-n // FLAT_W)
    return -(-rows // mult) * mult


def _by_core(a, b, c):
    return lax.dynamic_index_in_dim(jnp.stack([a, b]), c, axis=0, keepdims=False)


def _lane_padded(n):
    return -(-n // LANES) * LANES


def _lane_pad(a):
    pad = _lane_padded(a.shape[-1]) - a.shape[-1]
    return jnp.pad(a, [(0, 0)] * (a.ndim - 1) + [(0, pad)]) if pad else a


def gather_weights(shards):
    _, _, c = _place()
    flat = jnp.concatenate([_lane_pad(shards[n].astype(WIRE_DTYPE)).reshape(-1) for n in BIG_NAMES])
    n_flat = flat.shape[0]
    rh = _flat_rows(n_flat, 32) // 2
    flat = jnp.pad(flat, (0, 2 * rh * FLAT_W - n_flat)).reshape(2, rh, FLAT_W)
    mine = lax.dynamic_index_in_dim(flat, c, axis=0, keepdims=False)
    got = _chip_exchange(mine, True, "gather_chips")
    other = _sibling_exchange(got, "gather_sibling")
    halves = jnp.stack([_by_core(got, other, c), _by_core(other, got, c)], axis=1)
    allflat = halves.reshape(N_CHIPS, 2 * rh * FLAT_W)
    out, off = {}, 0
    for name, axis in BIG:
        shp = shards[name].shape
        padded = shp[:-1] + (_lane_padded(shp[-1]),)
        size = math.prod(padded)
        parts = allflat[:, off:off + size].reshape((N_CHIPS,) + padded)[..., :shp[-1]]
        out[name] = [jnp.concatenate([parts[p, l] for p in range(N_CHIPS)], axis=axis - 1) for l in range(shp[0])]
        off += size
    return out


def scatter_gradients(grads, shards):
    _, _, c = _place()
    cols = []
    for name, axis in BIG:
        parts = [jnp.split(g, N_CHIPS, axis=axis - 1) for g in grads[name]]
        cols.append(jnp.stack([_lane_pad(jnp.stack([per_layer[p] for per_layer in parts])).reshape(-1) for p in range(N_CHIPS)]))
    flat = jnp.concatenate(cols, axis=1)
    n_flat = flat.shape[1]
    rh = _flat_rows(n_flat, 16) // 2
    flat = jnp.pad(flat, ((0, 0), (0, 2 * rh * FLAT_W - n_flat))).reshape(N_CHIPS, 2, rh, FLAT_W)
    theirs = _sibling_exchange_half(flat, "scatter_sibling")
    pair = _add_own_half(flat, theirs, c, "scatter_pair_sum")
    got = _chip_exchange(pair, False, "scatter_chips")
    mine = _sum_stack(got, "scatter_chip_sum")
    other = _sibling_exchange(mine.reshape(1, rh, FLAT_W), "scatter_halves").reshape(rh, FLAT_W)
    full = jnp.stack([_by_core(mine, other, c), _by_core(other, mine, c)]).reshape(-1)
    out, off = {}, 0
    for name, _ in BIG:
        shp = shards[name].shape
        padded = shp[:-1] + (_lane_padded(shp[-1]),)
        size = math.prod(padded)
        out[name] = full[off:off + size].reshape(padded)[..., :shp[-1]]
        off += size
    return out


def _rope_tables(n_ctx, n_lat, n):
    t = jnp.arange(n_lat, dtype=jnp.int32)
    zero = jnp.zeros((n_ctx,), jnp.int32)
    row = jnp.concatenate([zero, t // GRID_W]).astype(F32)
    col = jnp.concatenate([zero, t % GRID_W]).astype(F32)
    half = n // 2
    inv = ROPE_BASE ** (-jnp.arange(0, half, 2, dtype=F32) / half)
    ang_r, ang_c = row[:, None, None] * inv, col[:, None, None] * inv
    return (jnp.cos(ang_r), jnp.sin(ang_r)), (jnp.cos(ang_c), jnp.sin(ang_c))


def _rot(x, cs):
    cos, sin = cs
    h = x.shape[-1] // 2
    x1, x2 = x[..., :h], x[..., h:]
    return jnp.concatenate([x1 * cos - x2 * sin, x1 * sin + x2 * cos], axis=-1)


def _axial_rope(x, tables):
    h = x.shape[-1] // 2
    return jnp.concatenate([_rot(x[..., :h], tables[0]), _rot(x[..., h:], tables[1])], axis=-1)


def _cmul(ar, ai, br, bi):
    return ar * br - ai * bi, ar * bi + ai * br


def _ssm_discretize(lam_re, lam_im, log_dt, b_re, b_im):
    dt = jnp.exp(log_dt)[:, None]
    mag = jnp.exp(lam_re * dt)
    a_re, a_im = mag * jnp.cos(lam_im * dt), mag * jnp.sin(lam_im * dt)
    den = lam_re * lam_re + lam_im * lam_im
    w_re = ((a_re - 1) * lam_re + a_im * lam_im) / den
    w_im = (a_im * lam_re - (a_re - 1) * lam_im) / den
    bb_re, bb_im = _cmul(w_re[..., None], w_im[..., None], b_re, b_im)
    return a_re, a_im, bb_re, bb_im


def _ssm_layouts(a_re, a_im, bb_re, bb_im, c_re, c_im):
    J, g8, P, Mg = SSM_CHUNKS, SSM_CHUNK_GROUPS, SSM_STATE, SSM_GROUP
    eye = jnp.eye(g8, dtype=F32)
    a = jnp.stack([a_re.reshape(J, g8 * P), a_im.reshape(J, g8 * P)], axis=1).reshape(1, J * 2 * g8 * P)
    bb = jnp.stack([bb_re, bb_im]).reshape(2, J, g8, P, Mg)
    w_drive = jnp.einsum('rjgpm,gh->jgmrhp', bb, eye).reshape(J * g8 * Mg, 2 * g8 * P)
    cc = jnp.stack([c_re, -c_im]).reshape(2, J, g8, Mg, P)
    w_read = jnp.einsum('rjgmp,gh->jrhpgm', cc, eye).reshape(J * 2 * g8 * P, g8 * Mg)
    return a, w_drive, w_read


def _sink_softmax(score_list, sink_logit):
    m = sink_logit
    for s in score_list:
        m = jnp.maximum(m, s.max(axis=-1, keepdims=True))
    e = [jnp.exp(s - m) for s in score_list]
    denom = jnp.exp(sink_logit - m)
    for t in e:
        denom = denom + t.sum(axis=-1, keepdims=True)
    return [t / denom for t in e]


def _window_gqa(q, k, v, kc, vc, sink):
    T, H, d = q.shape
    G = H // GQA_KV_HEADS
    nb = T // BLOCK
    scale = d ** -0.5
    qb = q.reshape(nb, BLOCK, GQA_KV_HEADS, G, d)

    def band(t):
        tb = t.reshape(nb, BLOCK, GQA_KV_HEADS, d)
        tp = jnp.pad(tb, ((1, 1), (0, 0), (0, 0), (0, 0)))
        return jnp.concatenate([tp[:-2], tp[1:-1], tp[2:]], axis=1)

    kb, vb = band(k), band(v)
    s_band = jnp.einsum('nqhgd,nkhd->hgnqk', qb, kb, preferred_element_type=F32) * scale
    blk = jnp.arange(nb)[:, None, None]
    qpos = blk * BLOCK + jnp.arange(BLOCK)[None, :, None]
    kpos = (blk - 1) * BLOCK + jnp.arange(3 * BLOCK)[None, None, :]
    valid = (jnp.abs(qpos - kpos) <= WINDOW) & (kpos >= 0) & (kpos < T)
    s_band = jnp.where(valid, s_band, NEG_INF)
    s_ctx = jnp.einsum('nqhgd,chd->hgnqc', qb, kc, preferred_element_type=F32) * scale
    sk = sink.reshape(GQA_KV_HEADS, G)[:, :, None, None, None]
    p_band, p_ctx = _sink_softmax([s_band, s_ctx], sk)
    o = jnp.einsum('hgnqk,nkhd->nqhgd', p_band, vb) + jnp.einsum('hgnqc,chd->nqhgd', p_ctx, vc)
    return o.reshape(T, H * d)


def _context_gqa(qc, kc, vc, sink):
    C, H, d = qc.shape
    G = H // GQA_KV_HEADS
    qg = qc.reshape(C, GQA_KV_HEADS, G, d)
    s = jnp.einsum('qhgd,khd->hgqk', qg, kc, preferred_element_type=F32) * d ** -0.5
    sk = sink.reshape(GQA_KV_HEADS, G)[:, :, None, None]
    (p,) = _sink_softmax([s], sk)
    return jnp.einsum('hgqk,khd->qhgd', p, vc).reshape(C, H * d)


def _w_in_layout(d_model):
    widths = (("cq", MLA_Q_RANK), ("ckv", MLA_KV_RANK), ("kr", MLA_ROPE), ("u", SSM_WIDTH), ("gq", GQA_HEADS * GQA_HEAD_DIM),
              ("gk", GQA_KV_HEADS * GQA_HEAD_DIM), ("gv", GQA_KV_HEADS * GQA_HEAD_DIM), ("gates", N_BRANCH * d_model))
    out, src, dst = [], 0, 0
    for name, w in widths:
        out.append((name, src, dst, w))
        src += w
        dst += -(-w // LANES) * LANES
    return out, src, dst


def _pad_w_in(w, d_model):
    lay, _, _ = _w_in_layout(d_model)
    parts = []
    for _, src, _, wd in lay:
        seg = w[..., src:src + wd]
        pad = -(-wd // LANES) * LANES - wd
        parts.append(jnp.pad(seg, [(0, 0)] * (w.ndim - 1) + [(0, pad)]) if pad else seg)
    return jnp.concatenate(parts, axis=-1)


def _unpad_w_in(w, d_model):
    lay, _, _ = _w_in_layout(d_model)
    return jnp.concatenate([w[..., dst:dst + wd] for _, _, dst, wd in lay], axis=-1)


@functools.partial(jax.custom_vjp, nondiff_argnums=(1,))
def split_cols(proj, bounds):
    return tuple(proj[:, s:s + w] for s, w in bounds[0])


def _split_cols_fwd(proj, bounds):
    return split_cols(proj, bounds), None


def _split_cols_bwd(bounds, _, cts):
    segments, total = bounds
    rows, pieces, pos = cts[0].shape[0], [], 0
    for (s, w), ct in zip(segments, cts):
        if s > pos:
            pieces.append(jnp.zeros((rows, s - pos), ct.dtype))
        pieces.append(ct)
        pos = s + w
    if pos < total:
        pieces.append(jnp.zeros((rows, total - pos), cts[0].dtype))
    return (jnp.concatenate(pieces, axis=1),)


split_cols.defvjp(_split_cols_fwd, _split_cols_bwd)


def _layer(hall, lw, lz, sp, cs8, n_ctx, ropes):
    M, D = hall.shape
    C = n_ctx
    rope32, rope64 = ropes

    def lin(x, name):
        return linear(x, lw[name], lz[name], name)

    mod_all = lin(cs8, "ada_w")[0:2] + sp["ada_b"][None, :]
    mod = [mod_all[:, i * D:(i + 1) * D] for i in range(N_MOD)]

    def ffn(h, tag, norm_g, sh, sc, gate):
        hn = norm_mod(h, norm_g[None, :], sh, sc, C, tag + "_norm")
        act = swiglu_act(lin(hn, tag + "_w13"), tag + "_act")
        return gated_residual(h, lin(act, tag + "_w2"), gate, 0.5, C, tag + "_res")

    hall = ffn(hall, "ffn1", sp["norm_ffn1"], mod[0], mod[1], mod[2])

    xm = norm_mod(hall, sp["norm_mix"][None, :], mod[3], mod[4], C, "mix_norm")
    proj = lin(xm, "w_in")
    lay, _, total = _w_in_layout(D)
    seg = dict(zip([name for name, _, _, _ in lay], split_cols(proj, (tuple((dst, wd) for _, _, dst, wd in lay), total))))

    q = lin(rmsnorm(seg["cq"], sp["mla_q_norm"][None, :], "mla_q_norm"), "mla_w_uq").reshape(M, MLA_HEADS, MLA_NOPE + MLA_ROPE)
    q = jnp.concatenate([q[..., :MLA_NOPE], _axial_rope(q[..., MLA_NOPE:], rope32)], axis=-1)
    kv = lin(rmsnorm(seg["ckv"], sp["mla_kv_norm"][None, :], "mla_kv_norm"), "mla_w_ukv").reshape(M, MLA_HEADS, MLA_NOPE + MLA_V)
    kr = _axial_rope(seg["kr"][:, None, :], rope32)
    k = jnp.concatenate([kv[..., :MLA_NOPE], jnp.broadcast_to(kr, (M, MLA_HEADS, MLA_ROPE))], axis=-1)
    qh, kh, vh = (t.transpose(1, 0, 2) for t in (q, k, kv[..., MLA_NOPE:]))
    qh = qh * ((MLA_NOPE + MLA_ROPE) ** -0.5 * LOG2E)
    o_lat = flash2(qh[:, C:], kh[:, C:], vh[:, C:], kh[:, :C], vh[:, :C], "mla_lat")
    o_ctx = flash1(qh[:, :C], kh[:, :C], vh[:, :C], "mla_ctx")
    o = jnp.concatenate([o_ctx, o_lat], axis=1).transpose(1, 0, 2).reshape(M, MLA_HEADS * MLA_V)
    mla = lin(o, "mla_w_o")

    u = seg["u"]
    y = u * sp["ssm_d"][None, :]
    for direction in range(2):
        a_re, a_im, bb_re, bb_im = _ssm_discretize(sp["ssm_lambda_re"][direction], sp["ssm_lambda_im"][direction],
                                                   sp["ssm_log_dt"][direction], sp["ssm_b_re"][direction],
                                                   sp["ssm_b_im"][direction])
        a, w_drive, w_read = _ssm_layouts(a_re, a_im, bb_re, bb_im, sp["ssm_c_re"][direction], sp["ssm_c_im"][direction])
        drive = bd_linear(u, w_drive, SSM_CHUNKS, "ssm_drive%d" % direction)
        states = diag_scan(a, drive, direction == 1, C, "ssm_scan%d" % direction)
        y = y + bd_linear(states, w_read, SSM_CHUNKS, "ssm_read%d" % direction)
    zz = lin(jax.nn.gelu(y), "ssm_w_glu")
    ssm = zz[:, :D] * jax.nn.sigmoid(zz[:, D:])

    gq = _axial_rope(seg["gq"].reshape(M, GQA_HEADS, GQA_HEAD_DIM), rope64)
    gk = _axial_rope(seg["gk"].reshape(M, GQA_KV_HEADS, GQA_HEAD_DIM), rope64)
    gv = seg["gv"].reshape(M, GQA_KV_HEADS, GQA_HEAD_DIM)
    g_lat = _window_gqa(gq[C:], gk[C:], gv[C:], gk[:C], gv[:C], sp["gqa_sink"])
    g_ctx = _context_gqa(gq[:C], gk[:C], gv[:C], sp["gqa_sink"])
    gqa = lin(jnp.concatenate([g_ctx, g_lat], axis=0), "gqa_w_o")

    gates = jax.nn.sigmoid(seg["gates"])
    mixed = gates[:, :D] * mla + gates[:, D:2 * D] * ssm + gates[:, 2 * D:] * gqa
    hall = gated_residual(hall, lin(mixed, "w_out"), mod[5], 1.0, C, "mix_res")

    return ffn(hall, "ffn2", sp["norm_ffn2"], mod[6], mod[7], mod[8])


PER_LAYER_SMALL = tuple(n for n in SMALL if n not in ("c_ctx", "final_norm"))


def _loss_fn(diff, x, c, ctx, target, whole):
    zeros, small, x = diff
    T, D = x.shape
    C = ctx.shape[0]
    ropes = (_rope_tables(C, T, MLA_ROPE), _rope_tables(C, T, GQA_HEAD_DIM))
    cs = jax.nn.silu(jnp.stack([small["c_ctx"], c]))
    cs8 = jnp.pad(cs, ((0, 6), (0, 0)))
    hall = jnp.concatenate([ctx, x], axis=0)

    for layer in range(len(zeros[BIG_NAMES[0]])):
        hall = _layer(hall, {n: whole[n][layer] for n in BIG_NAMES}, {n: zeros[n][layer] for n in BIG_NAMES},
                      {n: small[n][layer] for n in PER_LAYER_SMALL}, cs8, C, ropes)
    return loss_head(hall[C:], small["final_norm"][None, :], target, "loss_head")


def kernel(*args):
    given = dict(zip(ARG_NAMES + ['loss_target'] + ['m_' + n for n in WEIGHTS] + ['v_' + n for n in WEIGHTS], args))
    x, c, ctx, target = given['x'][0], given['c'][0], given['ctx'][0], given['loss_target'][0]
    D = x.shape[-1]
    shards = {n: given[n] for n in BIG_NAMES}
    small = {n: given[n] for n in SMALL}

    whole = gather_weights(shards)
    whole["w_in"] = [_pad_w_in(w, D) for w in whole["w_in"]]
    zeros = {n: [jnp.zeros(w.shape, F32) for w in whole[n]] for n in BIG_NAMES}

    loss, (gz, gsmall, gx) = jax.value_and_grad(_loss_fn)((zeros, small, x), x, c, ctx, target, whole)
    loss = lax.psum(loss, ("x", "y", "c"))
    gz["w_in"] = [_unpad_w_in(g, D) for g in gz["w_in"]]
    gbig = scatter_gradients(gz, shards)

    res = {}
    for name in BIG_NAMES:
        shp = given[name].shape
        two_d = (shp[0] * shp[1], shp[2])
        outs = _adamw_call(given[name].reshape(two_d), gbig[name].reshape((1,) + two_d), given['m_' + name].reshape(two_d),
                           given['v_' + name].reshape(two_d), "adamw_" + name)
        res[name] = [o.reshape(shp) for o in outs]

    def flat_small(d):
        v = jnp.concatenate([d[n].reshape(-1) for n in SMALL])
        rows = _flat_rows(v.shape[0], 8)
        return jnp.pad(v, (0, rows * FLAT_W - v.shape[0])).reshape(rows, FLAT_W)

    gathered = _all_gather(flat_small(gsmall), "small_gather")
    outs = _adamw_call(flat_small(small), gathered, flat_small({n: given['m_' + n] for n in SMALL}),
                       flat_small({n: given['v_' + n] for n in SMALL}), "adamw_small")
    off = 0
    for name in SMALL:
        shp = given[name].shape
        size = math.prod(shp)
        res[name] = [o.reshape(-1)[off:off + size].reshape(shp) for o in outs]
        off += size

    return (loss, gx[None], *[res[n][0] for n in WEIGHTS], *[res[n][1] for n in WEIGHTS],
            *[res[n][2] for n in WEIGHTS], *[res[n][3] for n in WEIGHTS])
```

```python
import functools
import math

import jax
import jax.numpy as jnp
from jax import lax
from jax.experimental import pallas as pl
from jax.experimental.pallas import tpu as pltpu

F32 = jnp.float32
MXU_DTYPE = jnp.bfloat16
WIRE_DTYPE = jnp.bfloat16

GRID_W = 64
MLA_HEADS, MLA_NOPE, MLA_ROPE, MLA_V = 8, 64, 32, 64
MLA_Q_RANK, MLA_KV_RANK = 384, 256
SSM_WIDTH, SSM_GROUP, SSM_STATE = 512, 16, 64
SSM_GROUPS = SSM_WIDTH // SSM_GROUP
SSM_CHUNK_GROUPS = 8
SSM_CHUNKS = SSM_GROUPS // SSM_CHUNK_GROUPS
SSM_CW = SSM_CHUNK_GROUPS * SSM_STATE
SCAN_SUB = 32
GQA_HEADS, GQA_KV_HEADS, GQA_HEAD_DIM = 8, 2, 64
WINDOW, BLOCK = 128, 128
N_BRANCH, N_MOD = 3, 9
ROPE_BASE = 10000.0
EPS = 1e-6
NEG_INF = -1e30
LOG2E, LN2 = math.log2(math.e), math.log(2.0)
LANES = 128
FLAT_W = 1024

ADAM_LR, ADAM_B1, ADAM_B2, ADAM_EPS, ADAM_WD, ADAM_STEP = 0.001, 0.9, 0.999, 1e-08, 0.01, 10

VMEM_LIMIT = 48 * 1024 * 1024

ARG_NAMES = ['x', 'c', 'ctx', 'c_ctx', 'ada_w', 'ada_b', 'norm_ffn1', 'norm_mix', 'norm_ffn2', 'ffn1_w13', 'ffn1_w2', 'ffn2_w13', 'ffn2_w2', 'w_in', 'mla_q_norm', 'mla_kv_norm', 'mla_w_uq', 'mla_w_ukv', 'mla_w_o', 'ssm_lambda_re', 'ssm_lambda_im', 'ssm_log_dt', 'ssm_b_re', 'ssm_b_im', 'ssm_c_re', 'ssm_c_im', 'ssm_d', 'ssm_w_glu', 'gqa_sink', 'gqa_w_o', 'w_out', 'final_norm']
WEIGHTS = ARG_NAMES[3:]
BIG = (('ada_w', 2), ('ffn1_w13', 2), ('ffn1_w2', 1), ('ffn2_w13', 2), ('ffn2_w2', 1), ('w_in', 2), ('mla_w_uq', 2),
       ('mla_w_ukv', 2), ('mla_w_o', 2), ('ssm_w_glu', 2), ('gqa_w_o', 2), ('w_out', 1))
BIG_NAMES = tuple(n for n, _ in BIG)
SCATTERED = tuple((n, a) for n, a in BIG if n != 'ada_w')
SCATTERED_NAMES = tuple(n for n, _ in SCATTERED)
SMALL = tuple(n for n in WEIGHTS if n not in BIG_NAMES)
N_CHIPS = 4


def _pick(n, prefs):
    for p in prefs:
        if n % p == 0:
            return p
    return n


def _params(sem=None):
    return pltpu.CompilerParams(dimension_semantics=sem, vmem_limit_bytes=VMEM_LIMIT)


def _mm_call(a, b, *, grid, a_spec, b_spec, o_spec, o_shape, acc_shape, ta, tb, name, out_dtype=F32):
    nk = grid[2]
    dn = (((0 if ta else 1,), (1 if tb else 0,)), ((), ()))

    def body(a_ref, b_ref, o_ref, acc_ref):
        k = pl.program_id(2)

        @pl.when(k == 0)
        def _():
            acc_ref[...] = jnp.zeros_like(acc_ref)

        acc_ref[...] += lax.dot_general(a_ref[...].astype(MXU_DTYPE), b_ref[...].astype(MXU_DTYPE), dn,
                                        preferred_element_type=F32)

        @pl.when(k == nk - 1)
        def _():
            o_ref[...] = acc_ref[...].astype(o_ref.dtype)

    return pl.pallas_call(
        body, name=name, grid=grid, in_specs=[a_spec, b_spec], out_specs=o_spec,
        out_shape=jax.ShapeDtypeStruct(o_shape, out_dtype), scratch_shapes=[pltpu.VMEM(acc_shape, F32)],
        compiler_params=_params(("parallel", "parallel", "arbitrary")))(a, b)


_ROWS = (768, 512, 256, 128, 64, 32, 16, 8)
_WIDE = (1408, 1024, 512, 256, 128)
MAX_WHOLE = 2816


def _feat(n):
    return n if n <= _WIDE[0] else _pick(n, _WIDE)


def _mm_nn(x, w, name):
    M, K = x.shape
    N = w.shape[1]
    tm, tn = _pick(M, _ROWS), _pick(N, (512, 256, 128))
    tk = K if K <= MAX_WHOLE else _pick(K, (512, 256, 128))
    return _mm_call(x, w, grid=(M // tm, N // tn, K // tk),
                    a_spec=pl.BlockSpec((tm, tk), lambda i, j, k: (i, k)),
                    b_spec=pl.BlockSpec((tk, tn), lambda i, j, k: (k, j)),
                    o_spec=pl.BlockSpec((tm, tn), lambda i, j, k: (i, j)),
                    o_shape=(M, N), acc_shape=(tm, tn), ta=False, tb=False, name=name)


def _mm_nt(dy, w, name):
    M, N = dy.shape
    K = w.shape[0]
    tm, tn, tk = _pick(M, _ROWS), _feat(K), _feat(N)
    return _mm_call(dy, w, grid=(M // tm, K // tn, N // tk),
                    a_spec=pl.BlockSpec((tm, tk), lambda i, j, k: (i, k)),
                    b_spec=pl.BlockSpec((tn, tk), lambda i, j, k: (j, k)),
                    o_spec=pl.BlockSpec((tm, tn), lambda i, j, k: (i, j)),
                    o_shape=(M, K), acc_shape=(tm, tn), ta=False, tb=True, name=name)


def _mm_tn(x, dy, name):
    M, K = x.shape
    N = dy.shape[1]
    tm, tn, tk = _feat(K), _feat(N), _pick(M, (256, 128, 64, 32, 16, 8))
    return _mm_call(x, dy, grid=(K // tm, N // tn, M // tk),
                    a_spec=pl.BlockSpec((tk, tm), lambda i, j, k: (k, i)),
                    b_spec=pl.BlockSpec((tk, tn), lambda i, j, k: (k, j)),
                    o_spec=pl.BlockSpec((tm, tn), lambda i, j, k: (i, j)),
                    o_shape=(K, N), acc_shape=(tm, tn), ta=True, tb=False, name=name)


@functools.partial(jax.custom_vjp, nondiff_argnums=(3,))
def linear(x, w, wz, name):
    return _mm_nn(x, w, name)


def _linear_fwd(x, w, wz, name):
    return _mm_nn(x, w, name), (x, w)


def _linear_bwd(name, res, dy):
    x, w = res
    return _mm_nt(dy, w, name + "_dx"), jnp.zeros_like(w), _mm_tn(x, dy, name + "_dw")


linear.defvjp(_linear_fwd, _linear_bwd)


@functools.partial(jax.custom_vjp, nondiff_argnums=(2,))
def linear_x(x, w, name):
    return _mm_nn(x, w, name)


def _linear_x_fwd(x, w, name):
    return _mm_nn(x, w, name), (w,)


def _linear_x_bwd(name, res, dy):
    return _mm_nt(dy, res[0], name + "_dx"), jnp.zeros_like(res[0])


linear_x.defvjp(_linear_x_fwd, _linear_x_bwd)


_BD_ROWS = (256, 128, 64, 32, 16, 8)


def _bd_call(a, b, nblk, kind, name):
    M = a.shape[0]
    tm = _pick(M, _BD_ROWS)
    if kind == "tn":
        aj, bj = a.shape[1] // nblk, b.shape[1] // nblk
        o_shape, o_spec = (nblk * aj, bj), pl.BlockSpec((nblk * aj, bj), lambda i: (0, 0))
        b_spec = pl.BlockSpec((tm, b.shape[1]), lambda i: (i, 0))
    else:
        aj = a.shape[1] // nblk
        wj = b.shape[0] // nblk
        oj = b.shape[1] if kind == "nn" else wj
        o_shape, o_spec = (M, nblk * oj), pl.BlockSpec((tm, nblk * oj), lambda i: (i, 0))
        b_spec = pl.BlockSpec(b.shape, lambda i: (0, 0))

    def body(a_ref, b_ref, o_ref):
        if kind == "tn":
            @pl.when(pl.program_id(0) == 0)
            def _():
                o_ref[...] = jnp.zeros_like(o_ref)

        for j in range(nblk):
            if kind == "nn":
                o_ref[:, j * oj:(j + 1) * oj] = _dot(a_ref[:, j * aj:(j + 1) * aj], b_ref[j * wj:(j + 1) * wj, :], _DN_NN)
            elif kind == "nt":
                o_ref[:, j * oj:(j + 1) * oj] = _dot(a_ref[:, j * aj:(j + 1) * aj], b_ref[j * wj:(j + 1) * wj, :], _DN_NT)
            else:
                o_ref[j * aj:(j + 1) * aj, :] += _dot(a_ref[:, j * aj:(j + 1) * aj], b_ref[:, j * bj:(j + 1) * bj], _DN_TN)

    return pl.pallas_call(body, name=name, grid=(M // tm,), in_specs=[pl.BlockSpec((tm, a.shape[1]), lambda i: (i, 0)), b_spec],
                          out_specs=o_spec, out_shape=jax.ShapeDtypeStruct(o_shape, F32),
                          compiler_params=_params(("arbitrary",) if kind == "tn" else ("parallel",)))(a, b)


def _bd_nn(x, w, nblk, name):
    return _bd_call(x, w, nblk, "nn", name)


def _bd_nt(dy, w, nblk, name):
    return _bd_call(dy, w, nblk, "nt", name)


def _bd_tn(x, dy, nblk, name):
    return _bd_call(x, dy, nblk, "tn", name)


@functools.partial(jax.custom_vjp, nondiff_argnums=(2, 3))
def bd_linear(x, w, nblk, name):
    return _bd_nn(x, w, nblk, name)


def _bd_fwd(x, w, nblk, name):
    return _bd_nn(x, w, nblk, name), (x, w)


def _bd_bwd(nblk, name, res, dy):
    x, w = res
    return _bd_nt(dy, w, nblk, name + "_dx"), _bd_tn(x, dy, nblk, name + "_dw")


bd_linear.defvjp(_bd_fwd, _bd_bwd)


def _row_tile(n_ctx, n_all):
    return _pick(math.gcd(n_ctx, n_all), (256, 128, 64, 32, 16, 8))


def _by_group(ref, is_ctx):
    return jnp.where(is_ctx, ref[0:1, :], ref[1:2, :])


def _acc_by_group(ref, is_ctx, part):
    ref[0:1, :] += jnp.where(is_ctx, part, 0.0)
    ref[1:2, :] += jnp.where(is_ctx, 0.0, part)


def _norm_fwd_call(x, g, shift, scale, n_ctx, name):
    M, D = x.shape
    has_mod = shift is not None
    tm = _row_tile(n_ctx, M) if has_mod else _pick(M, (256, 128, 64, 32, 16, 8))
    nct = n_ctx // tm

    def body(*refs):
        if has_mod:
            x_ref, g_ref, sh_ref, sc_ref, o_ref = refs
        else:
            x_ref, g_ref, o_ref = refs
        xv = x_ref[...]
        r = lax.rsqrt(jnp.mean(xv * xv, axis=-1, keepdims=True) + EPS)
        y = xv * r * g_ref[...]
        if has_mod:
            is_ctx = pl.program_id(0) < nct
            y = y * (1.0 + _by_group(sc_ref, is_ctx)) + _by_group(sh_ref, is_ctx)
        o_ref[...] = y

    row = pl.BlockSpec((tm, D), lambda i: (i, 0))
    vec = pl.BlockSpec((1, D), lambda i: (0, 0))
    two = pl.BlockSpec((2, D), lambda i: (0, 0))
    args = (x, g) + ((shift, scale) if has_mod else ())
    return pl.pallas_call(body, name=name, grid=(M // tm,), in_specs=[row, vec] + ([two, two] if has_mod else []),
                          out_specs=row, out_shape=jax.ShapeDtypeStruct((M, D), F32),
                          compiler_params=_params(("parallel",)))(*args)


def _norm_bwd_call(x, g, shift, scale, dy, n_ctx, name):
    M, D = x.shape
    has_mod = shift is not None
    tm = _row_tile(n_ctx, M) if has_mod else _pick(M, (256, 128, 64, 32, 16, 8))
    nct = n_ctx // tm

    def body(*refs):
        if has_mod:
            x_ref, g_ref, sc_ref, dy_ref, dx_ref, dg_ref, dsh_ref, dsc_ref = refs
        else:
            x_ref, g_ref, dy_ref, dx_ref, dg_ref = refs
        i = pl.program_id(0)

        @pl.when(i == 0)
        def _():
            dg_ref[...] = jnp.zeros_like(dg_ref)
            if has_mod:
                dsh_ref[...] = jnp.zeros_like(dsh_ref)
                dsc_ref[...] = jnp.zeros_like(dsc_ref)

        xv, gv, dyv = x_ref[...], g_ref[...], dy_ref[...]
        r = lax.rsqrt(jnp.mean(xv * xv, axis=-1, keepdims=True) + EPS)
        xhat = xv * r
        if has_mod:
            is_ctx = i < nct
            dy0 = dyv * (1.0 + _by_group(sc_ref, is_ctx))
            _acc_by_group(dsc_ref, is_ctx, jnp.sum(dyv * xhat * gv, axis=0, keepdims=True))
            _acc_by_group(dsh_ref, is_ctx, jnp.sum(dyv, axis=0, keepdims=True))
        else:
            dy0 = dyv
        dg_ref[...] += jnp.sum(dy0 * xhat, axis=0, keepdims=True)
        dxhat = dy0 * gv
        dx_ref[...] = r * (dxhat - xhat * jnp.mean(dxhat * xhat, axis=-1, keepdims=True))

    row = pl.BlockSpec((tm, D), lambda i: (i, 0))
    vec = pl.BlockSpec((1, D), lambda i: (0, 0))
    two = pl.BlockSpec((2, D), lambda i: (0, 0))
    if has_mod:
        args, in_specs = (x, g, scale, dy), [row, vec, two, row]
        out_specs = [row, vec, two, two]
        out_shape = [jax.ShapeDtypeStruct((M, D), F32), jax.ShapeDtypeStruct((1, D), F32),
                     jax.ShapeDtypeStruct((2, D), F32), jax.ShapeDtypeStruct((2, D), F32)]
    else:
        args, in_specs = (x, g, dy), [row, vec, row]
        out_specs = [row, vec]
        out_shape = [jax.ShapeDtypeStruct((M, D), F32), jax.ShapeDtypeStruct((1, D), F32)]
    return pl.pallas_call(body, name=name, grid=(M // tm,), in_specs=in_specs, out_specs=out_specs, out_shape=out_shape,
                          compiler_params=_params(("arbitrary",)))(*args)


@functools.partial(jax.custom_vjp, nondiff_argnums=(4, 5))
def norm_mod(x, g, shift, scale, n_ctx, name):
    return _norm_fwd_call(x, g, shift, scale, n_ctx, name)


def _norm_mod_fwd(x, g, shift, scale, n_ctx, name):
    return _norm_fwd_call(x, g, shift, scale, n_ctx, name), (x, g, shift, scale)


def _norm_mod_bwd(n_ctx, name, res, dy):
    x, g, shift, scale = res
    dx, dg, dsh, dsc = _norm_bwd_call(x, g, shift, scale, dy, n_ctx, name + "_bwd")
    return dx, dg, dsh, dsc


norm_mod.defvjp(_norm_mod_fwd, _norm_mod_bwd)


@functools.partial(jax.custom_vjp, nondiff_argnums=(2,))
def rmsnorm(x, g, name):
    return _norm_fwd_call(x, g, None, None, 0, name)


def _rmsnorm_fwd(x, g, name):
    return _norm_fwd_call(x, g, None, None, 0, name), (x, g)


def _rmsnorm_bwd(name, res, dy):
    x, g = res
    dx, dg = _norm_bwd_call(x, g, None, None, dy, 0, name + "_bwd")
    return dx, dg


rmsnorm.defvjp(_rmsnorm_fwd, _rmsnorm_bwd)


def _gres_fwd_call(h, o, gate, coef, n_ctx, name):
    M, D = h.shape
    tm = _row_tile(n_ctx, M)
    nct = n_ctx // tm

    def body(h_ref, o_ref, g_ref, out_ref):
        is_ctx = pl.program_id(0) < nct
        out_ref[...] = h_ref[...] + coef * _by_group(g_ref, is_ctx) * o_ref[...]

    row = pl.BlockSpec((tm, D), lambda i: (i, 0))
    two = pl.BlockSpec((2, D), lambda i: (0, 0))
    return pl.pallas_call(body, name=name, grid=(M // tm,), in_specs=[row, row, two], out_specs=row,
                          out_shape=jax.ShapeDtypeStruct((M, D), F32), compiler_params=_params(("parallel",)))(h, o, gate)


def _gres_bwd_call(o, gate, d, coef, n_ctx, name):
    M, D = o.shape
    tm = _row_tile(n_ctx, M)
    nct = n_ctx // tm

    def body(o_ref, g_ref, d_ref, do_ref, dg_ref):
        i = pl.program_id(0)
        is_ctx = i < nct

        @pl.when(i == 0)
        def _():
            dg_ref[...] = jnp.zeros_like(dg_ref)

        dv = d_ref[...]
        do_ref[...] = coef * _by_group(g_ref, is_ctx) * dv
        _acc_by_group(dg_ref, is_ctx, coef * jnp.sum(dv * o_ref[...], axis=0, keepdims=True))

    row = pl.BlockSpec((tm, D), lambda i: (i, 0))
    two = pl.BlockSpec((2, D), lambda i: (0, 0))
    return pl.pallas_call(body, name=name, grid=(M // tm,), in_specs=[row, two, row], out_specs=[row, two],
                          out_shape=[jax.ShapeDtypeStruct((M, D), F32), jax.ShapeDtypeStruct((2, D), F32)],
                          compiler_params=_params(("arbitrary",)))(o, gate, d)


@functools.partial(jax.custom_vjp, nondiff_argnums=(3, 4, 5))
def gated_residual(h, o, gate, coef, n_ctx, name):
    return _gres_fwd_call(h, o, gate, coef, n_ctx, name)


def _gres_fwd(h, o, gate, coef, n_ctx, name):
    return _gres_fwd_call(h, o, gate, coef, n_ctx, name), (o, gate)


def _gres_bwd(coef, n_ctx, name, res, d):
    o, gate = res
    do, dg = _gres_bwd_call(o, gate, d, coef, n_ctx, name + "_bwd")
    return d, do, dg


gated_residual.defvjp(_gres_fwd, _gres_bwd)


def _swiglu_fwd_call(ab, name):
    M, F2 = ab.shape
    Fh = F2 // 2
    tm, tn = _pick(M, (128, 64, 32, 16, 8)), Fh
    nf = Fh // tn

    def body(a_ref, b_ref, o_ref):
        a = a_ref[...]
        o_ref[...] = a * jax.nn.sigmoid(a) * b_ref[...]

    return pl.pallas_call(body, name=name, grid=(M // tm, nf),
                          in_specs=[pl.BlockSpec((tm, tn), lambda i, j: (i, j)), pl.BlockSpec((tm, tn), lambda i, j: (i, j + nf))],
                          out_specs=pl.BlockSpec((tm, tn), lambda i, j: (i, j)),
                          out_shape=jax.ShapeDtypeStruct((M, Fh), F32), compiler_params=_params(("parallel", "parallel")))(ab, ab)


def _swiglu_bwd_call(ab, dact, name):
    M, F2 = ab.shape
    Fh = F2 // 2
    tm, tn = _pick(M, (128, 64, 32, 16, 8)), Fh
    nf = Fh // tn

    def body(a_ref, b_ref, d_ref, o_ref):
        a, b, d = a_ref[...], b_ref[...], d_ref[...]
        sig = jax.nn.sigmoid(a)
        da = d * b * sig * (1.0 + a * (1.0 - sig))
        db = d * a * sig
        o_ref[...] = jnp.where(pl.program_id(1) < nf, da, db)

    return pl.pallas_call(body, name=name, grid=(M // tm, 2 * nf),
                          in_specs=[pl.BlockSpec((tm, tn), lambda i, j: (i, j % nf)),
                                    pl.BlockSpec((tm, tn), lambda i, j: (i, j % nf + nf)),
                                    pl.BlockSpec((tm, tn), lambda i, j: (i, j % nf))],
                          out_specs=pl.BlockSpec((tm, tn), lambda i, j: (i, j)),
                          out_shape=jax.ShapeDtypeStruct((M, F2), F32), compiler_params=_params(("parallel", "parallel")))(ab, ab, dact)


@functools.partial(jax.custom_vjp, nondiff_argnums=(1,))
def swiglu_act(ab, name):
    return _swiglu_fwd_call(ab, name)


def _swiglu_fwd(ab, name):
    return _swiglu_fwd_call(ab, name), (ab,)


def _swiglu_bwd(name, res, d):
    return (_swiglu_bwd_call(res[0], d, name + "_bwd"),)


swiglu_act.defvjp(_swiglu_fwd, _swiglu_bwd)


def _mix_call(gl, mla, zz, gqa, d, name):
    M, D = mla.shape
    tm = _pick(M, (128, 64, 32, 16, 8))
    bwd = d is not None

    def body(*refs):
        gl_ref, mla_ref, zz_ref, gqa_ref = refs[:4]
        g0, g1, g2 = (jax.nn.sigmoid(gl_ref[:, i * D:(i + 1) * D]) for i in range(3))
        za, sb = zz_ref[:, 0:D], jax.nn.sigmoid(zz_ref[:, D:2 * D])
        ssm = za * sb
        if not bwd:
            refs[4][...] = g0 * mla_ref[...] + g1 * ssm + g2 * gqa_ref[...]
            return
        d_ref, dgl_ref, dmla_ref, dzz_ref, dgqa_ref = refs[4:]
        dv = d_ref[...]
        dmla_ref[...] = g0 * dv
        dgqa_ref[...] = g2 * dv
        dssm = g1 * dv
        dzz_ref[:, 0:D] = dssm * sb
        dzz_ref[:, D:2 * D] = dssm * ssm * (1.0 - sb)
        dgl_ref[:, 0:D] = dv * mla_ref[...] * g0 * (1.0 - g0)
        dgl_ref[:, D:2 * D] = dv * ssm * g1 * (1.0 - g1)
        dgl_ref[:, 2 * D:3 * D] = dv * gqa_ref[...] * g2 * (1.0 - g2)

    def rows(w):
        return pl.BlockSpec((tm, w), lambda i: (i, 0))

    def sds(w):
        return jax.ShapeDtypeStruct((M, w), F32)

    in_specs, args = [rows(3 * D), rows(D), rows(2 * D), rows(D)], (gl, mla, zz, gqa)
    if bwd:
        return pl.pallas_call(body, name=name, grid=(M // tm,), in_specs=in_specs + [rows(D)],
                              out_specs=[rows(3 * D), rows(D), rows(2 * D), rows(D)],
                              out_shape=[sds(3 * D), sds(D), sds(2 * D), sds(D)],
                              compiler_params=_params(("parallel",)))(*args, d)
    return pl.pallas_call(body, name=name, grid=(M // tm,), in_specs=in_specs, out_specs=rows(D), out_shape=sds(D),
                          compiler_params=_params(("parallel",)))(*args)


@functools.partial(jax.custom_vjp, nondiff_argnums=(4,))
def gated_mix(gl, mla, zz, gqa, name):
    return _mix_call(gl, mla, zz, gqa, None, name)


def _gated_mix_fwd(gl, mla, zz, gqa, name):
    return _mix_call(gl, mla, zz, gqa, None, name), (gl, mla, zz, gqa)


def _gated_mix_bwd(name, res, d):
    return tuple(_mix_call(*res, d, name + "_bwd"))


gated_mix.defvjp(_gated_mix_fwd, _gated_mix_bwd)


_DN_NT = (((1,), (1,)), ((), ()))
_DN_TN = (((0,), (0,)), ((), ()))
_DN_NN = (((1,), (0,)), ((), ()))


def _dot(a, b, dn):
    return lax.dot_general(a.astype(MXU_DTYPE), b.astype(MXU_DTYPE), dn, preferred_element_type=F32)


_TQ = (1024, 512, 256, 128, 64, 32, 16, 8)


def _flash_fwd_call(q, k1, v1, k2, v2, name):
    H, Tq, dk = q.shape
    T1, dv = k1.shape[1], v1.shape[2]
    has2 = k2 is not None
    tq, tk = _pick(Tq, _TQ), _pick(T1, _TQ)
    off = 1 if has2 else 0
    nkv = T1 // tk + off
    C = k2.shape[1] if has2 else 0
    rows = max(tk, C)

    def body(*refs):
        if has2:
            q_ref, k1_ref, v1_ref, k2_ref, v2_ref, o_ref, lse_ref, m_s, acc_s, va_s = refs
        else:
            q_ref, k1_ref, v1_ref, o_ref, lse_ref, m_s, acc_s, va_s = refs
        j = pl.program_id(2)

        @pl.when(j == 0)
        def _():
            m_s[...] = jnp.full_like(m_s, NEG_INF)
            acc_s[...] = jnp.zeros_like(acc_s)
            va_s[:, dv:2 * dv] = jnp.ones((rows, dv), MXU_DTYPE)

        def step(k, v, n):
            va_s[0:n, 0:dv] = v.astype(MXU_DTYPE)
            s = _dot(q_ref[0], k, _DN_NT)
            m_prev = m_s[...]
            m_new = jnp.maximum(m_prev, jnp.max(s, axis=-1, keepdims=True))
            p = jnp.exp2(s - m_new)
            acc_s[...] = jnp.exp2(m_prev - m_new) * acc_s[...] + _dot(p, va_s[0:n, :], _DN_NN)
            m_s[...] = m_new

        if has2:
            @pl.when(j == 0)
            def _():
                step(k2_ref[0], v2_ref[0], C)

            @pl.when(j > 0)
            def _():
                step(k1_ref[0], v1_ref[0], tk)
        else:
            step(k1_ref[0], v1_ref[0], tk)

        @pl.when(j == nkv - 1)
        def _():
            l = acc_s[:, dv:dv + 1]
            o_ref[0] = acc_s[:, 0:dv] / l
            lse_ref[0] = m_s[...] + jnp.log2(l)

    qs = pl.BlockSpec((1, tq, dk), lambda h, i, j: (h, i, 0))
    k1s = pl.BlockSpec((1, tk, dk), lambda h, i, j: (h, jnp.maximum(j - off, 0), 0))
    v1s = pl.BlockSpec((1, tk, dv), lambda h, i, j: (h, jnp.maximum(j - off, 0), 0))
    in_specs, args = [qs, k1s, v1s], [q, k1, v1]
    if has2:
        in_specs += [pl.BlockSpec((1, C, dk), lambda h, i, j: (h, 0, 0)), pl.BlockSpec((1, C, dv), lambda h, i, j: (h, 0, 0))]
        args += [k2, v2]
    return pl.pallas_call(
        body, name=name, grid=(H, Tq // tq, nkv), in_specs=in_specs,
        out_specs=[pl.BlockSpec((1, tq, dv), lambda h, i, j: (h, i, 0)), pl.BlockSpec((1, tq, 1), lambda h, i, j: (h, i, 0))],
        out_shape=[jax.ShapeDtypeStruct((H, Tq, dv), F32), jax.ShapeDtypeStruct((H, Tq, 1), F32)],
        scratch_shapes=[pltpu.VMEM((tq, 1), F32), pltpu.VMEM((tq, 2 * dv), F32), pltpu.VMEM((rows, 2 * dv), MXU_DTYPE)],
        compiler_params=_params(("parallel", "parallel", "arbitrary")))(*args)


def _flash_bwd_call(q, k1, v1, k2, v2, o, lse, do, name):
    H, Tq, dk = q.shape
    T1, dv = k1.shape[1], v1.shape[2]
    has2 = k2 is not None
    tq, tk = _pick(Tq, _TQ), _pick(T1, _TQ)
    off = 1 if has2 else 0
    nkv, nq = T1 // tk + off, Tq // tq
    C = k2.shape[1] if has2 else 0
    rows = max(tk, C)

    def body(*refs):
        if has2:
            (q_ref, k1_ref, v1_ref, k2_ref, v2_ref, o_ref, lse_ref, do_ref,
             dq_ref, dk1_ref, dv1_ref, dk2_ref, dv2_ref, dk_s, dv_s) = refs
        else:
            q_ref, k1_ref, v1_ref, o_ref, lse_ref, do_ref, dq_ref, dk1_ref, dv1_ref, dk_s, dv_s = refs
        j, i = pl.program_id(1), pl.program_id(2)

        @pl.when((j == 0) & (i == 0))
        def _():
            dq_ref[...] = jnp.zeros_like(dq_ref)

        @pl.when(i == 0)
        def _():
            dk_s[...] = jnp.zeros_like(dk_s)
            dv_s[...] = jnp.zeros_like(dv_s)

        def step(k, v, n):
            qb, dob = q_ref[0], do_ref[0]
            p = jnp.exp2(_dot(qb, k, _DN_NT) - lse_ref[0])
            dv_s[0:n, :] += _dot(p, dob, _DN_TN)
            dol = dob * LN2
            ds = p * (_dot(dol, v, _DN_NT) - jnp.sum(dol * o_ref[0], axis=-1, keepdims=True))
            dk_s[0:n, :] += _dot(ds, qb, _DN_TN)
            r0 = pl.multiple_of(i * tq, tq)
            dq_ref[0, pl.ds(r0, tq), :] += _dot(ds, k, _DN_NN)

        if has2:
            @pl.when(j == 0)
            def _():
                step(k2_ref[0], v2_ref[0], C)

            @pl.when(j > 0)
            def _():
                step(k1_ref[0], v1_ref[0], tk)

            @pl.when((i == nq - 1) & (j == 0))
            def _():
                dk2_ref[0] = dk_s[0:C, :]
                dv2_ref[0] = dv_s[0:C, :]

            @pl.when((i == nq - 1) & (j > 0))
            def _():
                dk1_ref[0] = dk_s[0:tk, :]
                dv1_ref[0] = dv_s[0:tk, :]
        else:
            step(k1_ref[0], v1_ref[0], tk)

            @pl.when(i == nq - 1)
            def _():
                dk1_ref[0] = dk_s[...]
                dv1_ref[0] = dv_s[...]

    qs = pl.BlockSpec((1, tq, dk), lambda h, j, i: (h, i, 0))
    os_ = pl.BlockSpec((1, tq, dv), lambda h, j, i: (h, i, 0))
    ls = pl.BlockSpec((1, tq, 1), lambda h, j, i: (h, i, 0))
    k1s = pl.BlockSpec((1, tk, dk), lambda h, j, i: (h, jnp.maximum(j - off, 0), 0))
    v1s = pl.BlockSpec((1, tk, dv), lambda h, j, i: (h, jnp.maximum(j - off, 0), 0))
    in_specs, args = [qs, k1s, v1s], [q, k1, v1]
    out_specs = [pl.BlockSpec((1, Tq, dk), lambda h, j, i: (h, 0, 0)), k1s, v1s]
    out_shape = [jax.ShapeDtypeStruct((H, Tq, dk), F32), jax.ShapeDtypeStruct((H, T1, dk), F32),
                 jax.ShapeDtypeStruct((H, T1, dv), F32)]
    if has2:
        k2s = pl.BlockSpec((1, C, dk), lambda h, j, i: (h, 0, 0))
        v2s = pl.BlockSpec((1, C, dv), lambda h, j, i: (h, 0, 0))
        in_specs += [k2s, v2s]
        args += [k2, v2]
        out_specs += [k2s, v2s]
        out_shape += [jax.ShapeDtypeStruct((H, C, dk), F32), jax.ShapeDtypeStruct((H, C, dv), F32)]
    in_specs += [os_, ls, os_]
    args += [o, lse, do]
    return pl.pallas_call(
        body, name=name, grid=(H, nkv, nq), in_specs=in_specs, out_specs=out_specs, out_shape=out_shape,
        scratch_shapes=[pltpu.VMEM((rows, dk), F32), pltpu.VMEM((rows, dv), F32)],
        compiler_params=_params(("parallel", "arbitrary", "arbitrary")))(*args)


@functools.partial(jax.custom_vjp, nondiff_argnums=(5,))
def flash2(q, k1, v1, k2, v2, name):
    return _flash_fwd_call(q, k1, v1, k2, v2, name)[0]


def _flash2_fwd(q, k1, v1, k2, v2, name):
    o, lse = _flash_fwd_call(q, k1, v1, k2, v2, name)
    return o, (q, k1, v1, k2, v2, o, lse)


def _flash2_bwd(name, res, do):
    q, k1, v1, k2, v2, o, lse = res
    return tuple(_flash_bwd_call(q, k1, v1, k2, v2, o, lse, do, name + "_bwd"))


flash2.defvjp(_flash2_fwd, _flash2_bwd)


@functools.partial(jax.custom_vjp, nondiff_argnums=(3,))
def flash1(q, k, v, name):
    return _flash_fwd_call(q, k, v, None, None, name)[0]


def _flash1_fwd(q, k, v, name):
    o, lse = _flash_fwd_call(q, k, v, None, None, name)
    return o, (q, k, v, o, lse)


def _flash1_bwd(name, res, do):
    q, k, v, o, lse = res
    return tuple(_flash_bwd_call(q, k, v, None, None, o, lse, do, name + "_bwd"))


flash1.defvjp(_flash1_fwd, _flash1_bwd)


def _scan_call(a, x, s, *, rev, adj, n_ctx, name):
    M, W = x.shape
    cw = SSM_CW
    J = W // (2 * cw)
    L = _row_tile(n_ctx, M)
    nt, nc = M // L, n_ctx // L
    asc = rev == adj
    sub = min(SCAN_SUB, L)
    nsub = L // sub
    n_steps = int(math.log2(sub))
    assert 1 << n_steps == sub

    def tile(t):
        if not rev:
            return nt - 1 - t if adj else t
        if not adj:
            return jnp.where(t < nc, nc - 1 - t, nt - 1 - (t - nc))
        return jnp.where(t < nt - nc, nc + t, t - (nt - nc))

    def body(*refs):
        if adj:
            a_ref, x_ref, s_ref, o_ref, da_ref, car_ref = refs
        else:
            a_ref, x_ref, o_ref, car_ref = refs
        t = pl.program_id(1)

        @pl.when(t == 0)
        def _():
            car_ref[...] = jnp.zeros_like(car_ref)
            if adj:
                da_ref[...] = jnp.zeros_like(da_ref)

        ar, ai = a_ref[:, 0:cw], a_ref[:, cw:2 * cw]
        powers, pr, pi = [], ar, ai
        for _ in range(n_steps):
            powers.append((pr, pi))
            pr, pi = pr * pr - pi * pi, 2.0 * pr * pi
        row = lax.broadcasted_iota(jnp.int32, (sub, cw), 0)
        first, last = (0, sub - 1) if asc else (sub - 1, 0)

        def scan_rows(i, carry):
            cr, ci = carry[0], carry[1]
            r0 = pl.multiple_of((i if asc else nsub - 1 - i) * sub, sub)
            xr, xi = x_ref[pl.ds(r0, sub), 0:cw], x_ref[pl.ds(r0, sub), cw:2 * cw]
            xr = xr + jnp.where(row == first, ar * cr - ai * ci, 0.0)
            xi = xi + jnp.where(row == first, ar * ci + ai * cr, 0.0)
            k = 1
            for pr, pi in powers:
                if asc:
                    sr, si, keep = pltpu.roll(xr, k, 0), pltpu.roll(xi, k, 0), row >= k
                else:
                    sr, si, keep = pltpu.roll(xr, sub - k, 0), pltpu.roll(xi, sub - k, 0), row < sub - k
                sr, si = jnp.where(keep, sr, 0.0), jnp.where(keep, si, 0.0)
                xr, xi = xr + pr * sr - pi * si, xi + pr * si + pi * sr
                k *= 2
            o_ref[pl.ds(r0, sub), 0:cw] = xr
            o_ref[pl.ds(r0, sub), cw:2 * cw] = xi
            out = (jnp.sum(jnp.where(row == last, xr, 0.0), axis=0, keepdims=True),
                   jnp.sum(jnp.where(row == last, xi, 0.0), axis=0, keepdims=True))
            if adj:
                if asc:
                    gr, gi = pltpu.roll(xr, 1, 0), pltpu.roll(xi, 1, 0)
                else:
                    gr, gi = pltpu.roll(xr, sub - 1, 0), pltpu.roll(xi, sub - 1, 0)
                gr, gi = jnp.where(row == first, cr, gr), jnp.where(row == first, ci, gi)
                sr, si = s_ref[pl.ds(r0, sub), 0:cw], s_ref[pl.ds(r0, sub), cw:2 * cw]
                out += (carry[2] + jnp.sum(sr * gr + si * gi, axis=0, keepdims=True),
                        carry[3] + jnp.sum(sr * gi - si * gr, axis=0, keepdims=True))
            return out

        init = (car_ref[:, 0:cw], car_ref[:, cw:2 * cw])
        if adj:
            init += (jnp.zeros((1, cw), F32), jnp.zeros((1, cw), F32))
        done = lax.fori_loop(0, nsub, scan_rows, init)
        car_ref[:, 0:cw] = done[0]
        car_ref[:, cw:2 * cw] = done[1]
        if adj:
            da_ref[:, 0:cw] += done[2]
            da_ref[:, cw:2 * cw] += done[3]

    blk = pl.BlockSpec((L, 2 * cw), lambda j, t: (tile(t), j))
    vec = pl.BlockSpec((1, 2 * cw), lambda j, t: (0, j))
    if adj:
        in_specs, args = [vec, blk, blk], (a, x, s)
        out_specs = [blk, vec]
        out_shape = [jax.ShapeDtypeStruct((M, W), F32), jax.ShapeDtypeStruct((1, W), F32)]
    else:
        in_specs, args = [vec, blk], (a, x)
        out_specs = blk
        out_shape = jax.ShapeDtypeStruct((M, W), F32)
    return pl.pallas_call(body, name=name, grid=(J, nt), in_specs=in_specs, out_specs=out_specs, out_shape=out_shape,
                          scratch_shapes=[pltpu.VMEM((1, 2 * cw), F32)],
                          compiler_params=_params(("parallel", "arbitrary")))(*args)


def _conj_layout(a):
    cw = SSM_CW
    J = a.shape[1] // (2 * cw)
    a4 = a.reshape(1, J, 2, cw)
    return jnp.concatenate([a4[:, :, 0:1], -a4[:, :, 1:2]], axis=2).reshape(a.shape)


@functools.partial(jax.custom_vjp, nondiff_argnums=(2, 3, 4))
def diag_scan(a, x, rev, n_ctx, name):
    return _scan_call(a, x, None, rev=rev, adj=False, n_ctx=n_ctx, name=name)


def _diag_scan_fwd(a, x, rev, n_ctx, name):
    s = _scan_call(a, x, None, rev=rev, adj=False, n_ctx=n_ctx, name=name)
    return s, (a, s)


def _diag_scan_bwd(rev, n_ctx, name, res, ds):
    a, s = res
    g, da = _scan_call(_conj_layout(a), ds, s, rev=rev, adj=True, n_ctx=n_ctx, name=name + "_adj")
    return da, g


diag_scan.defvjp(_diag_scan_fwd, _diag_scan_bwd)


def _loss_call(h, g, target, name):
    M, D = h.shape
    tm = _pick(M, (256, 128, 64, 32, 16, 8))

    def body(h_ref, g_ref, t_ref, loss_ref, dh_ref, dg_ref):
        i = pl.program_id(0)

        @pl.when(i == 0)
        def _():
            loss_ref[...] = jnp.zeros_like(loss_ref)
            dg_ref[...] = jnp.zeros_like(dg_ref)

        xv, gv = h_ref[...], g_ref[...]
        r = lax.rsqrt(jnp.mean(xv * xv, axis=-1, keepdims=True) + EPS)
        xhat = xv * r
        err = xhat * gv - t_ref[...]
        loss_ref[...] += 0.5 * jnp.sum(jnp.mean(err * err, axis=-1, keepdims=True), axis=0, keepdims=True)
        dy = err * (1.0 / D)
        dg_ref[...] += jnp.sum(dy * xhat, axis=0, keepdims=True)
        dxhat = dy * gv
        dh_ref[...] = r * (dxhat - xhat * jnp.mean(dxhat * xhat, axis=-1, keepdims=True))

    row = pl.BlockSpec((tm, D), lambda i: (i, 0))
    vec = pl.BlockSpec((1, D), lambda i: (0, 0))
    one = pl.BlockSpec((1, 1), lambda i: (0, 0))
    return pl.pallas_call(body, name=name, grid=(M // tm,), in_specs=[row, vec, row], out_specs=[one, row, vec],
                          out_shape=[jax.ShapeDtypeStruct((1, 1), F32), jax.ShapeDtypeStruct((M, D), F32),
                                     jax.ShapeDtypeStruct((1, D), F32)],
                          compiler_params=_params(("arbitrary",)))(h, g, target)


@functools.partial(jax.custom_vjp, nondiff_argnums=(3,))
def loss_head(h, g, target, name):
    return _loss_call(h, g, target, name)[0][0, 0]


def _loss_head_fwd(h, g, target, name):
    loss, dh, dg = _loss_call(h, g, target, name)
    return loss[0, 0], (dh, dg, target)


def _loss_head_bwd(name, res, ct):
    dh, dg, target = res
    return ct * dh, ct * dg, jnp.zeros_like(target)


loss_head.defvjp(_loss_head_fwd, _loss_head_bwd)


def _adamw_call(w, gstack, m, v, name):
    R, Cn = w.shape
    n = gstack.shape[0]
    tr = _pick(R, (64, 32, 16, 8))

    def body(w_ref, g_ref, m_ref, v_ref, go_ref, d_ref, mo_ref, vo_ref):
        g = g_ref[0]
        for s in range(1, n):
            g = g + g_ref[s]
        mn = ADAM_B1 * m_ref[...] + (1.0 - ADAM_B1) * g
        vn = ADAM_B2 * v_ref[...] + (1.0 - ADAM_B2) * (g * g)
        m_hat = mn / (1.0 - ADAM_B1 ** ADAM_STEP)
        v_hat = vn / (1.0 - ADAM_B2 ** ADAM_STEP)
        go_ref[...] = g
        d_ref[...] = -ADAM_LR * (m_hat / (jnp.sqrt(v_hat) + ADAM_EPS) + ADAM_WD * w_ref[...])
        mo_ref[...] = mn
        vo_ref[...] = vn

    blk = pl.BlockSpec((tr, Cn), lambda i: (i, 0))
    gblk = pl.BlockSpec((n, tr, Cn), lambda i: (0, i, 0))
    sds = jax.ShapeDtypeStruct((R, Cn), F32)
    return pl.pallas_call(body, name=name, grid=(R // tr,), in_specs=[blk, gblk, blk, blk], out_specs=[blk] * 4,
                          out_shape=[sds] * 4, compiler_params=_params(("parallel",)))(w, gstack, m, v)


MESH = pl.DeviceIdType.MESH
ANY = pl.BlockSpec(memory_space=pl.ANY)


def _place():
    return lax.axis_index("x"), lax.axis_index("y"), lax.axis_index("c")


def _sibling_exchange(v, name):
    n = v.shape[0]

    def body(v_ref, o_ref, send_sems, recv_sems):
        x, y, c = _place()
        copies = [pltpu.make_async_remote_copy(src_ref=v_ref.at[k], dst_ref=o_ref.at[k], send_sem=send_sems.at[k],
                                               recv_sem=recv_sems.at[k], device_id=(x, y, 1 - c), device_id_type=MESH)
                  for k in range(n)]
        for cp in copies:
            cp.start()
        for cp in copies:
            cp.wait()

    return pl.pallas_call(body, name=name, in_specs=[ANY], out_specs=ANY, out_shape=jax.ShapeDtypeStruct(v.shape, v.dtype),
                          scratch_shapes=[pltpu.SemaphoreType.DMA((n,)), pltpu.SemaphoreType.DMA((n,))])(v)


def _sibling_exchange_half(v, name):
    n = v.shape[0]

    def body(v_ref, o_ref, send_sems, recv_sems):
        x, y, c = _place()
        copies = [pltpu.make_async_remote_copy(src_ref=v_ref.at[k, pl.ds(1 - c, 1)], dst_ref=o_ref.at[k],
                                               send_sem=send_sems.at[k], recv_sem=recv_sems.at[k],
                                               device_id=(x, y, 1 - c), device_id_type=MESH)
                  for k in range(n)]
        for cp in copies:
            cp.start()
        for cp in copies:
            cp.wait()

    return pl.pallas_call(body, name=name, in_specs=[ANY], out_specs=ANY,
                          out_shape=jax.ShapeDtypeStruct((n, 1) + v.shape[2:], v.dtype),
                          scratch_shapes=[pltpu.SemaphoreType.DMA((n,)), pltpu.SemaphoreType.DMA((n,))])(v)


def _chip_exchange(v, same, name):
    out_shape = (N_CHIPS,) + (v.shape if same else v.shape[1:])

    def body(v_ref, o_ref, send_sems, recv_sems, local_sem):
        x, y, c = _place()
        me = 2 * x + y
        own = pltpu.make_async_copy(v_ref if same else v_ref.at[me], o_ref.at[me], local_sem)
        own.start()
        copies = []
        for k, (fx, fy) in enumerate(((1, 0), (0, 1), (1, 1))):
            px, py = jnp.where(fx == 1, 1 - x, x), jnp.where(fy == 1, 1 - y, y)
            src = v_ref if same else v_ref.at[2 * px + py]
            copies.append(pltpu.make_async_remote_copy(src_ref=src, dst_ref=o_ref.at[me], send_sem=send_sems.at[k],
                                                       recv_sem=recv_sems.at[k], device_id=(px, py, c), device_id_type=MESH))
        for cp in copies:
            cp.start()
        for cp in copies:
            cp.wait()
        own.wait()

    return pl.pallas_call(body, name=name, in_specs=[ANY], out_specs=ANY, out_shape=jax.ShapeDtypeStruct(out_shape, v.dtype),
                          scratch_shapes=[pltpu.SemaphoreType.DMA((3,)), pltpu.SemaphoreType.DMA((3,)), pltpu.SemaphoreType.DMA])(v)


def _all_gather(v, name):
    def body(v_ref, o_ref, send_sems, recv_sems, local_sem):
        x, y, c = _place()
        me = 4 * x + 2 * y + c
        own = pltpu.make_async_copy(v_ref, o_ref.at[me], local_sem)
        own.start()
        copies = []
        for k in range(1, 8):
            fx, fy, fc = (k >> 2) & 1, (k >> 1) & 1, k & 1
            peer = (jnp.where(fx == 1, 1 - x, x), jnp.where(fy == 1, 1 - y, y), jnp.where(fc == 1, 1 - c, c))
            copies.append(pltpu.make_async_remote_copy(src_ref=v_ref, dst_ref=o_ref.at[me], send_sem=send_sems.at[k - 1],
                                                       recv_sem=recv_sems.at[k - 1], device_id=peer, device_id_type=MESH))
        for cp in copies:
            cp.start()
        for cp in copies:
            cp.wait()
        own.wait()

    return pl.pallas_call(body, name=name, in_specs=[ANY], out_specs=ANY,
                          out_shape=jax.ShapeDtypeStruct((8,) + v.shape, v.dtype),
                          scratch_shapes=[pltpu.SemaphoreType.DMA((7,)), pltpu.SemaphoreType.DMA((7,)), pltpu.SemaphoreType.DMA])(v)


def _add_own_half(g, r, c, name):
    n, _, R, W = g.shape
    tr = _pick(R, (256, 128, 64, 32, 16, 8))

    def body(c_ref, g_ref, r_ref, o_ref):
        o_ref[...] = g_ref[0] + r_ref[0]

    grid_spec = pltpu.PrefetchScalarGridSpec(
        num_scalar_prefetch=1, grid=(n, R // tr),
        in_specs=[pl.BlockSpec((1, 1, tr, W), lambda p, i, c_ref: (p, c_ref[0], i, 0)),
                  pl.BlockSpec((1, 1, tr, W), lambda p, i, c_ref: (p, 0, i, 0))],
        out_specs=pl.BlockSpec((1, tr, W), lambda p, i, c_ref: (p, i, 0)))
    return pl.pallas_call(body, name=name, grid_spec=grid_spec, out_shape=jax.ShapeDtypeStruct((n, R, W), F32),
                          compiler_params=_params(("parallel", "parallel")))(c.reshape(1).astype(jnp.int32), g, r)


def _sum_stack(v, name):
    n, R, W = v.shape
    tr = _pick(R, (256, 128, 64, 32, 16, 8))

    def body(v_ref, o_ref):
        acc = v_ref[0]
        for s in range(1, n):
            acc = acc + v_ref[s]
        o_ref[...] = acc

    return pl.pallas_call(body, name=name, grid=(R // tr,), in_specs=[pl.BlockSpec((n, tr, W), lambda i: (0, i, 0))],
                          out_specs=pl.BlockSpec((tr, W), lambda i: (i, 0)), out_shape=jax.ShapeDtypeStruct((R, W), F32),
                          compiler_params=_params(("parallel",)))(v)


def _flat_rows(n, mult):
    rows = -(-n // FLAT_W)
    return -(-rows // mult) * mult


def _by_core(a, b, c):
    return lax.dynamic_index_in_dim(jnp.stack([a, b]), c, axis=0, keepdims=False)


def _lane_padded(n):
    return -(-n // LANES) * LANES


def _lane_pad(a):
    pad = _lane_padded(a.shape[-1]) - a.shape[-1]
    return jnp.pad(a, [(0, 0)] * (a.ndim - 1) + [(0, pad)]) if pad else a


def gather_weights(shards):
    _, _, c = _place()
    flat = jnp.concatenate([_lane_pad(shards[n].astype(WIRE_DTYPE)).reshape(-1) for n in BIG_NAMES])
    n_flat = flat.shape[0]
    rh = _flat_rows(n_flat, 32) // 2
    flat = jnp.pad(flat, (0, 2 * rh * FLAT_W - n_flat)).reshape(2, rh, FLAT_W)
    mine = lax.dynamic_index_in_dim(flat, c, axis=0, keepdims=False)
    got = _chip_exchange(mine, True, "gather_chips")
    other = _sibling_exchange(got, "gather_sibling")
    halves = jnp.stack([_by_core(got, other, c), _by_core(other, got, c)], axis=1)
    allflat = halves.reshape(N_CHIPS, 2 * rh * FLAT_W)
    out, off = {}, 0
    for name, axis in BIG:
        shp = shards[name].shape
        padded = shp[:-1] + (_lane_padded(shp[-1]),)
        size = math.prod(padded)
        parts = allflat[:, off:off + size].reshape((N_CHIPS,) + padded)[..., :shp[-1]]
        out[name] = [jnp.concatenate([parts[p, l] for p in range(N_CHIPS)], axis=axis - 1) for l in range(shp[0])]
        off += size
    return out


def scatter_gradients(grads, shards):
    _, _, c = _place()
    cols = []
    for name, axis in SCATTERED:
        parts = [jnp.split(g, N_CHIPS, axis=axis - 1) for g in grads[name]]
        cols.append(jnp.stack([_lane_pad(jnp.stack([per_layer[p] for per_layer in parts])).reshape(-1) for p in range(N_CHIPS)]))
    flat = jnp.concatenate(cols, axis=1)
    n_flat = flat.shape[1]
    rh = _flat_rows(n_flat, 16) // 2
    flat = jnp.pad(flat, ((0, 0), (0, 2 * rh * FLAT_W - n_flat))).reshape(N_CHIPS, 2, rh, FLAT_W)
    theirs = _sibling_exchange_half(flat, "scatter_sibling")
    pair = _add_own_half(flat, theirs, c, "scatter_pair_sum")
    got = _chip_exchange(pair, False, "scatter_chips")
    mine = _sum_stack(got, "scatter_chip_sum")
    other = _sibling_exchange(mine.reshape(1, rh, FLAT_W), "scatter_halves").reshape(rh, FLAT_W)
    full = jnp.stack([_by_core(mine, other, c), _by_core(other, mine, c)]).reshape(-1)
    out, off = {}, 0
    for name, _ in SCATTERED:
        shp = shards[name].shape
        padded = shp[:-1] + (_lane_padded(shp[-1]),)
        size = math.prod(padded)
        out[name] = full[off:off + size].reshape(padded)[..., :shp[-1]]
        off += size
    return out


def _rope_tables(n_ctx, n_lat, n):
    t = jnp.arange(n_lat, dtype=jnp.int32)
    zero = jnp.zeros((n_ctx,), jnp.int32)
    row = jnp.concatenate([zero, t // GRID_W]).astype(F32)
    col = jnp.concatenate([zero, t % GRID_W]).astype(F32)
    half = n // 2
    inv = ROPE_BASE ** (-jnp.arange(0, half, 2, dtype=F32) / half)
    ang_r, ang_c = row[:, None, None] * inv, col[:, None, None] * inv
    return (jnp.cos(ang_r), jnp.sin(ang_r)), (jnp.cos(ang_c), jnp.sin(ang_c))


def _rot(x, cs):
    cos, sin = cs
    h = x.shape[-1] // 2
    x1, x2 = x[..., :h], x[..., h:]
    return jnp.concatenate([x1 * cos - x2 * sin, x1 * sin + x2 * cos], axis=-1)


def _axial_rope(x, tables):
    h = x.shape[-1] // 2
    return jnp.concatenate([_rot(x[..., :h], tables[0]), _rot(x[..., h:], tables[1])], axis=-1)


def _cmul(ar, ai, br, bi):
    return ar * br - ai * bi, ar * bi + ai * br


def _ssm_discretize(lam_re, lam_im, log_dt, b_re, b_im):
    dt = jnp.exp(log_dt)[:, None]
    mag = jnp.exp(lam_re * dt)
    a_re, a_im = mag * jnp.cos(lam_im * dt), mag * jnp.sin(lam_im * dt)
    den = lam_re * lam_re + lam_im * lam_im
    w_re = ((a_re - 1) * lam_re + a_im * lam_im) / den
    w_im = (a_im * lam_re - (a_re - 1) * lam_im) / den
    bb_re, bb_im = _cmul(w_re[..., None], w_im[..., None], b_re, b_im)
    return a_re, a_im, bb_re, bb_im


def _ssm_layouts(a_re, a_im, bb_re, bb_im, c_re, c_im):
    J, g8, P, Mg = SSM_CHUNKS, SSM_CHUNK_GROUPS, SSM_STATE, SSM_GROUP
    eye = jnp.eye(g8, dtype=F32)
    a = jnp.stack([a_re.reshape(J, g8 * P), a_im.reshape(J, g8 * P)], axis=1).reshape(1, J * 2 * g8 * P)
    bb = jnp.stack([bb_re, bb_im]).reshape(2, J, g8, P, Mg)
    w_drive = jnp.einsum('rjgpm,gh->jgmrhp', bb, eye).reshape(J * g8 * Mg, 2 * g8 * P)
    cc = jnp.stack([c_re, -c_im]).reshape(2, J, g8, Mg, P)
    w_read = jnp.einsum('rjgmp,gh->jrhpgm', cc, eye).reshape(J * 2 * g8 * P, g8 * Mg)
    return a, w_drive, w_read


def _sink_softmax(score_list, sink_logit):
    m = sink_logit
    for s in score_list:
        m = jnp.maximum(m, s.max(axis=-1, keepdims=True))
    e = [jnp.exp(s - m) for s in score_list]
    denom = jnp.exp(sink_logit - m)
    for t in e:
        denom = denom + t.sum(axis=-1, keepdims=True)
    return [t / denom for t in e]


def _window_gqa(q, k, v, kc, vc, sink):
    T, H, d = q.shape
    G = H // GQA_KV_HEADS
    nb = T // BLOCK
    scale = d ** -0.5
    qb = q.reshape(nb, BLOCK, GQA_KV_HEADS, G, d)

    def band(t):
        tb = t.reshape(nb, BLOCK, GQA_KV_HEADS, d)
        tp = jnp.pad(tb, ((1, 1), (0, 0), (0, 0), (0, 0)))
        return jnp.concatenate([tp[:-2], tp[1:-1], tp[2:]], axis=1)

    kb, vb = band(k), band(v)
    s_band = jnp.einsum('nqhgd,nkhd->hgnqk', qb, kb, preferred_element_type=F32) * scale
    blk = jnp.arange(nb)[:, None, None]
    qpos = blk * BLOCK + jnp.arange(BLOCK)[None, :, None]
    kpos = (blk - 1) * BLOCK + jnp.arange(3 * BLOCK)[None, None, :]
    valid = (jnp.abs(qpos - kpos) <= WINDOW) & (kpos >= 0) & (kpos < T)
    s_band = jnp.where(valid, s_band, NEG_INF)
    s_ctx = jnp.einsum('nqhgd,chd->hgnqc', qb, kc, preferred_element_type=F32) * scale
    sk = sink.reshape(GQA_KV_HEADS, G)[:, :, None, None, None]
    p_band, p_ctx = _sink_softmax([s_band, s_ctx], sk)
    o = jnp.einsum('hgnqk,nkhd->nqhgd', p_band, vb) + jnp.einsum('hgnqc,chd->nqhgd', p_ctx, vc)
    return o.reshape(T, H * d)


def _context_gqa(qc, kc, vc, sink):
    C, H, d = qc.shape
    G = H // GQA_KV_HEADS
    qg = qc.reshape(C, GQA_KV_HEADS, G, d)
    s = jnp.einsum('qhgd,khd->hgqk', qg, kc, preferred_element_type=F32) * d ** -0.5
    sk = sink.reshape(GQA_KV_HEADS, G)[:, :, None, None]
    (p,) = _sink_softmax([s], sk)
    return jnp.einsum('hgqk,khd->qhgd', p, vc).reshape(C, H * d)


def _w_in_layout(d_model):
    widths = (("cq", MLA_Q_RANK), ("ckv", MLA_KV_RANK), ("kr", MLA_ROPE), ("u", SSM_WIDTH), ("gq", GQA_HEADS * GQA_HEAD_DIM),
              ("gk", GQA_KV_HEADS * GQA_HEAD_DIM), ("gv", GQA_KV_HEADS * GQA_HEAD_DIM), ("gates", N_BRANCH * d_model))
    out, src, dst = [], 0, 0
    for name, w in widths:
        out.append((name, src, dst, w))
        src += w
        dst += -(-w // LANES) * LANES
    return out, src, dst


def _pad_w_in(w, d_model):
    lay, _, _ = _w_in_layout(d_model)
    parts = []
    for _, src, _, wd in lay:
        seg = w[..., src:src + wd]
        pad = -(-wd // LANES) * LANES - wd
        parts.append(jnp.pad(seg, [(0, 0)] * (w.ndim - 1) + [(0, pad)]) if pad else seg)
    return jnp.concatenate(parts, axis=-1)


def _unpad_w_in(w, d_model):
    lay, _, _ = _w_in_layout(d_model)
    return jnp.concatenate([w[..., dst:dst + wd] for _, _, dst, wd in lay], axis=-1)


@functools.partial(jax.custom_vjp, nondiff_argnums=(1,))
def split_cols(proj, bounds):
    return tuple(proj[:, s:s + w] for s, w in bounds[0])


def _split_cols_fwd(proj, bounds):
    return split_cols(proj, bounds), None


def _split_cols_bwd(bounds, _, cts):
    segments, total = bounds
    rows, pieces, pos = cts[0].shape[0], [], 0
    for (s, w), ct in zip(segments, cts):
        if s > pos:
            pieces.append(jnp.zeros((rows, s - pos), ct.dtype))
        pieces.append(ct)
        pos = s + w
    if pos < total:
        pieces.append(jnp.zeros((rows, total - pos), cts[0].dtype))
    return (jnp.concatenate(pieces, axis=1),)


split_cols.defvjp(_split_cols_fwd, _split_cols_bwd)


def _layer(hall, lw, lz, sp, cs8, n_ctx, ropes):
    M, D = hall.shape
    C = n_ctx
    rope32, rope64 = ropes

    def lin(x, name):
        return linear(x, lw[name], lz[name], name)

    mod_all = linear_x(cs8, lw["ada_w"], "ada_w")[0:2] + sp["ada_b"][None, :] + lz["ada_tap"]
    mod = [mod_all[:, i * D:(i + 1) * D] for i in range(N_MOD)]

    def ffn(h, tag, norm_g, sh, sc, gate):
        hn = norm_mod(h, norm_g[None, :], sh, sc, C, tag + "_norm")
        act = swiglu_act(lin(hn, tag + "_w13"), tag + "_act")
        return gated_residual(h, lin(act, tag + "_w2"), gate, 0.5, C, tag + "_res")

    hall = ffn(hall, "ffn1", sp["norm_ffn1"], mod[0], mod[1], mod[2])

    xm = norm_mod(hall, sp["norm_mix"][None, :], mod[3], mod[4], C, "mix_norm")
    proj = lin(xm, "w_in")
    lay, _, total = _w_in_layout(D)
    seg = dict(zip([name for name, _, _, _ in lay], split_cols(proj, (tuple((dst, wd) for _, _, dst, wd in lay), total))))

    q = lin(rmsnorm(seg["cq"], sp["mla_q_norm"][None, :], "mla_q_norm"), "mla_w_uq").reshape(M, MLA_HEADS, MLA_NOPE + MLA_ROPE)
    q = jnp.concatenate([q[..., :MLA_NOPE], _axial_rope(q[..., MLA_NOPE:], rope32)], axis=-1)
    kv = lin(rmsnorm(seg["ckv"], sp["mla_kv_norm"][None, :], "mla_kv_norm"), "mla_w_ukv").reshape(M, MLA_HEADS, MLA_NOPE + MLA_V)
    kr = _axial_rope(seg["kr"][:, None, :], rope32)
    k = jnp.concatenate([kv[..., :MLA_NOPE], jnp.broadcast_to(kr, (M, MLA_HEADS, MLA_ROPE))], axis=-1)
    qh, kh, vh = (t.transpose(1, 0, 2) for t in (q, k, kv[..., MLA_NOPE:]))
    qh = qh * ((MLA_NOPE + MLA_ROPE) ** -0.5 * LOG2E)
    o_lat = flash2(qh[:, C:], kh[:, C:], vh[:, C:], kh[:, :C], vh[:, :C], "mla_lat")
    o_ctx = flash1(qh[:, :C], kh[:, :C], vh[:, :C], "mla_ctx")
    o = jnp.concatenate([o_ctx, o_lat], axis=1).transpose(1, 0, 2).reshape(M, MLA_HEADS * MLA_V)
    mla = lin(o, "mla_w_o")

    u = seg["u"]
    y = u * sp["ssm_d"][None, :]
    for direction in range(2):
        a_re, a_im, bb_re, bb_im = _ssm_discretize(sp["ssm_lambda_re"][direction], sp["ssm_lambda_im"][direction],
                                                   sp["ssm_log_dt"][direction], sp["ssm_b_re"][direction],
                                                   sp["ssm_b_im"][direction])
        a, w_drive, w_read = _ssm_layouts(a_re, a_im, bb_re, bb_im, sp["ssm_c_re"][direction], sp["ssm_c_im"][direction])
        drive = bd_linear(u, w_drive, SSM_CHUNKS, "ssm_drive%d" % direction)
        states = diag_scan(a, drive, direction == 1, C, "ssm_scan%d" % direction)
        y = y + bd_linear(states, w_read, SSM_CHUNKS, "ssm_read%d" % direction)
    zz = lin(jax.nn.gelu(y), "ssm_w_glu")

    gq = _axial_rope(seg["gq"].reshape(M, GQA_HEADS, GQA_HEAD_DIM), rope64)
    gk = _axial_rope(seg["gk"].reshape(M, GQA_KV_HEADS, GQA_HEAD_DIM), rope64)
    gv = seg["gv"].reshape(M, GQA_KV_HEADS, GQA_HEAD_DIM)
    g_lat = _window_gqa(gq[C:], gk[C:], gv[C:], gk[:C], gv[:C], sp["gqa_sink"])
    g_ctx = _context_gqa(gq[:C], gk[:C], gv[:C], sp["gqa_sink"])
    gqa = lin(jnp.concatenate([g_ctx, g_lat], axis=0), "gqa_w_o")

    mixed = gated_mix(seg["gates"], mla, zz, gqa, "mix_gate")
    hall = gated_residual(hall, lin(mixed, "w_out"), mod[5], 1.0, C, "mix_res")

    return ffn(hall, "ffn2", sp["norm_ffn2"], mod[6], mod[7], mod[8])


PER_LAYER_SMALL = tuple(n for n in SMALL if n not in ("c_ctx", "final_norm"))


def _loss_fn(diff, x, c, ctx, target, whole):
    zeros, small, x = diff
    T, D = x.shape
    C = ctx.shape[0]
    ropes = (_rope_tables(C, T, MLA_ROPE), _rope_tables(C, T, GQA_HEAD_DIM))
    cs8 = jnp.pad(_cond_rows(small["c_ctx"], c), ((0, 6), (0, 0)))
    hall = jnp.concatenate([ctx, x], axis=0)

    for layer in range(len(zeros["ada_tap"])):
        hall = _layer(hall, {n: whole[n][layer] for n in BIG_NAMES}, {n: zeros[n][layer] for n in zeros},
                      {n: small[n][layer] for n in PER_LAYER_SMALL}, cs8, C, ropes)
    return loss_head(hall[C:], small["final_norm"][None, :], target, "loss_head")


def _cond_rows(c_ctx, c):
    return jax.nn.silu(jnp.stack([c_ctx, c]))


def kernel(*args):
    given = dict(zip(ARG_NAMES + ['loss_target'] + ['m_' + n for n in WEIGHTS] + ['v_' + n for n in WEIGHTS], args))
    x, c, ctx, target = given['x'][0], given['c'][0], given['ctx'][0], given['loss_target'][0]
    px, py, _ = _place()
    D = x.shape[-1]
    depth = given['ada_w'].shape[0]
    shards = {n: given[n] for n in BIG_NAMES}
    small = {n: given[n] for n in SMALL}

    whole = gather_weights(shards)
    whole["w_in"] = [_pad_w_in(w, D) for w in whole["w_in"]]
    zeros = {n: [jnp.zeros(w.shape, F32) for w in whole[n]] for n in SCATTERED_NAMES}
    zeros["ada_tap"] = [jnp.zeros((2, N_MOD * D), F32) for _ in range(depth)]

    loss, (gz, gsmall, gx) = jax.value_and_grad(_loss_fn)((zeros, small, x), x, c, ctx, target, whole)
    loss = lax.psum(loss, ("x", "y", "c"))
    taps = jnp.stack(gz.pop("ada_tap"))
    gz["w_in"] = [_unpad_w_in(g, D) for g in gz["w_in"]]
    gbig = scatter_gradients(gz, shards)

    extra = [taps.reshape(-1), _cond_rows(small["c_ctx"], c).reshape(-1)]
    n_small = sum(math.prod(given[n].shape) for n in SMALL)
    n_extra = sum(e.shape[0] for e in extra)

    def flat_small(d, tail=None):
        v = jnp.concatenate([d[n].reshape(-1) for n in SMALL] + (tail or [jnp.zeros((n_extra,), F32)]))
        rows = _flat_rows(v.shape[0], 8)
        return jnp.pad(v, (0, rows * FLAT_W - v.shape[0])).reshape(rows, FLAT_W)

    gathered = _all_gather(flat_small(gsmall, extra), "small_gather")
    outs = _adamw_call(flat_small(small), gathered, flat_small({n: given['m_' + n] for n in SMALL}),
                       flat_small({n: given['v_' + n] for n in SMALL}), "adamw_small")
    res, off = {}, 0
    for name in SMALL:
        shp = given[name].shape
        size = math.prod(shp)
        res[name] = [o.reshape(-1)[off:off + size].reshape(shp) for o in outs]
        off += size

    tails = gathered.reshape(8, -1)[:, n_small:n_small + n_extra]
    all_taps = tails[:, :taps.size].reshape(8, depth, 2, N_MOD * D)
    all_cs = tails[:, taps.size:].reshape(8 * 2, D)
    n_cols = given['ada_w'].shape[2]
    mine = lax.dynamic_slice_in_dim(all_taps, (2 * px + py) * n_cols, n_cols, axis=3)
    gbig["ada_w"] = jnp.stack([_mm_tn(all_cs, mine[:, layer].reshape(8 * 2, n_cols), "ada_w_dw") for layer in range(depth)])

    for name in BIG_NAMES:
        shp = given[name].shape
        two_d = (shp[0] * shp[1], shp[2])
        outs = _adamw_call(given[name].reshape(two_d), gbig[name].reshape((1,) + two_d), given['m_' + name].reshape(two_d),
                           given['v_' + name].reshape(two_d), "adamw_" + name)
        res[name] = [o.reshape(shp) for o in outs]

    return (loss, gx[None], *[res[n][0] for n in WEIGHTS], *[res[n][1] for n in WEIGHTS],
            *[res[n][2] for n in WEIGHTS], *[res[n][3] for n in WEIGHTS])
```

```python
import functools
import math

import jax
import jax.numpy as jnp
from jax import lax
from jax.experimental import pallas as pl
from jax.experimental.pallas import tpu as pltpu

F32 = jnp.float32
MXU_DTYPE = jnp.bfloat16
WIRE_DTYPE = jnp.bfloat16

GRID_W = 64
MLA_HEADS, MLA_NOPE, MLA_ROPE, MLA_V = 8, 64, 32, 64
MLA_Q_RANK, MLA_KV_RANK = 384, 256
SSM_WIDTH, SSM_GROUP, SSM_STATE = 512, 16, 64
SSM_GROUPS = SSM_WIDTH // SSM_GROUP
SSM_CHUNK_GROUPS = 8
SSM_CHUNKS = SSM_GROUPS // SSM_CHUNK_GROUPS
SSM_CW = SSM_CHUNK_GROUPS * SSM_STATE
SCAN_SUB = 32
GQA_HEADS, GQA_KV_HEADS, GQA_HEAD_DIM = 8, 2, 64
WINDOW, BLOCK = 128, 128
N_BRANCH, N_MOD = 3, 9
ROPE_BASE = 10000.0
EPS = 1e-6
NEG_INF = -1e30
LOG2E, LN2 = math.log2(math.e), math.log(2.0)
LANES = 128
FLAT_W = 1024

ADAM_LR, ADAM_B1, ADAM_B2, ADAM_EPS, ADAM_WD, ADAM_STEP = 0.001, 0.9, 0.999, 1e-08, 0.01, 10

VMEM_LIMIT = 48 * 1024 * 1024

ARG_NAMES = ['x', 'c', 'ctx', 'c_ctx', 'ada_w', 'ada_b', 'norm_ffn1', 'norm_mix', 'norm_ffn2', 'ffn1_w13', 'ffn1_w2', 'ffn2_w13', 'ffn2_w2', 'w_in', 'mla_q_norm', 'mla_kv_norm', 'mla_w_uq', 'mla_w_ukv', 'mla_w_o', 'ssm_lambda_re', 'ssm_lambda_im', 'ssm_log_dt', 'ssm_b_re', 'ssm_b_im', 'ssm_c_re', 'ssm_c_im', 'ssm_d', 'ssm_w_glu', 'gqa_sink', 'gqa_w_o', 'w_out', 'final_norm']
WEIGHTS = ARG_NAMES[3:]
BIG = (('ada_w', 2), ('ffn1_w13', 2), ('ffn1_w2', 1), ('ffn2_w13', 2), ('ffn2_w2', 1), ('w_in', 2), ('mla_w_uq', 2),
       ('mla_w_ukv', 2), ('mla_w_o', 2), ('ssm_w_glu', 2), ('gqa_w_o', 2), ('w_out', 1))
BIG_NAMES = tuple(n for n, _ in BIG)
SCATTERED = tuple((n, a) for n, a in BIG if n != 'ada_w')
SCATTERED_NAMES = tuple(n for n, _ in SCATTERED)
SMALL = tuple(n for n in WEIGHTS if n not in BIG_NAMES)
N_CHIPS = 4


def _pick(n, prefs):
    for p in prefs:
        if n % p == 0:
            return p
    return n


def _params(sem=None):
    return pltpu.CompilerParams(dimension_semantics=sem, vmem_limit_bytes=VMEM_LIMIT)


def _mm_call(a, b, *, grid, a_spec, b_spec, o_spec, o_shape, acc_shape, ta, tb, name, out_dtype=F32):
    nk = grid[2]
    dn = (((0 if ta else 1,), (1 if tb else 0,)), ((), ()))

    def body(a_ref, b_ref, o_ref, acc_ref):
        k = pl.program_id(2)

        @pl.when(k == 0)
        def _():
            acc_ref[...] = jnp.zeros_like(acc_ref)

        acc_ref[...] += lax.dot_general(a_ref[...].astype(MXU_DTYPE), b_ref[...].astype(MXU_DTYPE), dn,
                                        preferred_element_type=F32)

        @pl.when(k == nk - 1)
        def _():
            o_ref[...] = acc_ref[...].astype(o_ref.dtype)

    return pl.pallas_call(
        body, name=name, grid=grid, in_specs=[a_spec, b_spec], out_specs=o_spec,
        out_shape=jax.ShapeDtypeStruct(o_shape, out_dtype), scratch_shapes=[pltpu.VMEM(acc_shape, F32)],
        compiler_params=_params(("parallel", "parallel", "arbitrary")))(a, b)


_ROWS = (768, 512, 256, 128, 64, 32, 16, 8)
_WIDE = (1408, 1024, 512, 256, 128)
MAX_WHOLE = 2816


def _feat(n):
    return n if n <= _WIDE[0] else _pick(n, _WIDE)


def _mm_nn(x, w, name):
    M, K = x.shape
    N = w.shape[1]
    tm, tn = _pick(M, _ROWS), _pick(N, (512, 256, 128))
    tk = K if K <= MAX_WHOLE else _pick(K, (512, 256, 128))
    return _mm_call(x, w, grid=(M // tm, N // tn, K // tk),
                    a_spec=pl.BlockSpec((tm, tk), lambda i, j, k: (i, k)),
                    b_spec=pl.BlockSpec((tk, tn), lambda i, j, k: (k, j)),
                    o_spec=pl.BlockSpec((tm, tn), lambda i, j, k: (i, j)),
                    o_shape=(M, N), acc_shape=(tm, tn), ta=False, tb=False, name=name)


def _mm_nt(dy, w, name):
    M, N = dy.shape
    K = w.shape[0]
    tm, tn, tk = _pick(M, _ROWS), _feat(K), _feat(N)
    return _mm_call(dy, w, grid=(M // tm, K // tn, N // tk),
                    a_spec=pl.BlockSpec((tm, tk), lambda i, j, k: (i, k)),
                    b_spec=pl.BlockSpec((tn, tk), lambda i, j, k: (j, k)),
                    o_spec=pl.BlockSpec((tm, tn), lambda i, j, k: (i, j)),
                    o_shape=(M, K), acc_shape=(tm, tn), ta=False, tb=True, name=name)


def _mm_tn(x, dy, name):
    M, K = x.shape
    N = dy.shape[1]
    tm, tn, tk = _feat(K), _feat(N), _pick(M, (256, 128, 64, 32, 16, 8))
    return _mm_call(x, dy, grid=(K // tm, N // tn, M // tk),
                    a_spec=pl.BlockSpec((tk, tm), lambda i, j, k: (k, i)),
                    b_spec=pl.BlockSpec((tk, tn), lambda i, j, k: (k, j)),
                    o_spec=pl.BlockSpec((tm, tn), lambda i, j, k: (i, j)),
                    o_shape=(K, N), acc_shape=(tm, tn), ta=True, tb=False, name=name)


@functools.partial(jax.custom_vjp, nondiff_argnums=(3,))
def linear(x, w, wz, name):
    return _mm_nn(x, w, name)


def _linear_fwd(x, w, wz, name):
    return _mm_nn(x, w, name), (x, w)


def _linear_bwd(name, res, dy):
    x, w = res
    return _mm_nt(dy, w, name + "_dx"), jnp.zeros_like(w), _mm_tn(x, dy, name + "_dw")


linear.defvjp(_linear_fwd, _linear_bwd)


@functools.partial(jax.custom_vjp, nondiff_argnums=(2,))
def linear_x(x, w, name):
    return _mm_nn(x, w, name)


def _linear_x_fwd(x, w, name):
    return _mm_nn(x, w, name), (w,)


def _linear_x_bwd(name, res, dy):
    return _mm_nt(dy, res[0], name + "_dx"), jnp.zeros_like(res[0])


linear_x.defvjp(_linear_x_fwd, _linear_x_bwd)


_BD_ROWS = (256, 128, 64, 32, 16, 8)


def _bd_call(a, b, nblk, kind, name):
    M = a.shape[0]
    tm = _pick(M, _BD_ROWS)
    if kind == "tn":
        aj, bj = a.shape[1] // nblk, b.shape[1] // nblk
        o_shape, o_spec = (nblk * aj, bj), pl.BlockSpec((nblk * aj, bj), lambda i: (0, 0))
        b_spec = pl.BlockSpec((tm, b.shape[1]), lambda i: (i, 0))
    else:
        aj = a.shape[1] // nblk
        wj = b.shape[0] // nblk
        oj = b.shape[1] if kind == "nn" else wj
        o_shape, o_spec = (M, nblk * oj), pl.BlockSpec((tm, nblk * oj), lambda i: (i, 0))
        b_spec = pl.BlockSpec(b.shape, lambda i: (0, 0))

    def body(a_ref, b_ref, o_ref):
        if kind == "tn":
            @pl.when(pl.program_id(0) == 0)
            def _():
                o_ref[...] = jnp.zeros_like(o_ref)

        for j in range(nblk):
            if kind == "nn":
                o_ref[:, j * oj:(j + 1) * oj] = _dot(a_ref[:, j * aj:(j + 1) * aj], b_ref[j * wj:(j + 1) * wj, :], _DN_NN)
            elif kind == "nt":
                o_ref[:, j * oj:(j + 1) * oj] = _dot(a_ref[:, j * aj:(j + 1) * aj], b_ref[j * wj:(j + 1) * wj, :], _DN_NT)
            else:
                o_ref[j * aj:(j + 1) * aj, :] += _dot(a_ref[:, j * aj:(j + 1) * aj], b_ref[:, j * bj:(j + 1) * bj], _DN_TN)

    return pl.pallas_call(body, name=name, grid=(M // tm,), in_specs=[pl.BlockSpec((tm, a.shape[1]), lambda i: (i, 0)), b_spec],
                          out_specs=o_spec, out_shape=jax.ShapeDtypeStruct(o_shape, F32),
                          compiler_params=_params(("arbitrary",) if kind == "tn" else ("parallel",)))(a, b)


def _bd_nn(x, w, nblk, name):
    return _bd_call(x, w, nblk, "nn", name)


def _bd_nt(dy, w, nblk, name):
    return _bd_call(dy, w, nblk, "nt", name)


def _bd_tn(x, dy, nblk, name):
    return _bd_call(x, dy, nblk, "tn", name)


@functools.partial(jax.custom_vjp, nondiff_argnums=(2, 3))
def bd_linear(x, w, nblk, name):
    return _bd_nn(x, w, nblk, name)


def _bd_fwd(x, w, nblk, name):
    return _bd_nn(x, w, nblk, name), (x, w)


def _bd_bwd(nblk, name, res, dy):
    x, w = res
    return _bd_nt(dy, w, nblk, name + "_dx"), _bd_tn(x, dy, nblk, name + "_dw")


bd_linear.defvjp(_bd_fwd, _bd_bwd)


def _row_tile(n_ctx, n_all):
    return _pick(math.gcd(n_ctx, n_all), (256, 128, 64, 32, 16, 8))


def _by_group(ref, is_ctx):
    return jnp.where(is_ctx, ref[0:1, :], ref[1:2, :])


def _acc_by_group(ref, is_ctx, part):
    ref[0:1, :] += jnp.where(is_ctx, part, 0.0)
    ref[1:2, :] += jnp.where(is_ctx, 0.0, part)


def _norm_fwd_call(x, g, shift, scale, n_ctx, name):
    M, D = x.shape
    has_mod = shift is not None
    tm = _row_tile(n_ctx, M) if has_mod else _pick(M, (256, 128, 64, 32, 16, 8))
    nct = n_ctx // tm

    def body(*refs):
        if has_mod:
            x_ref, g_ref, sh_ref, sc_ref, o_ref = refs
        else:
            x_ref, g_ref, o_ref = refs
        xv = x_ref[...]
        r = lax.rsqrt(jnp.mean(xv * xv, axis=-1, keepdims=True) + EPS)
        y = xv * r * g_ref[...]
        if has_mod:
            is_ctx = pl.program_id(0) < nct
            y = y * (1.0 + _by_group(sc_ref, is_ctx)) + _by_group(sh_ref, is_ctx)
        o_ref[...] = y

    row = pl.BlockSpec((tm, D), lambda i: (i, 0))
    vec = pl.BlockSpec((1, D), lambda i: (0, 0))
    two = pl.BlockSpec((2, D), lambda i: (0, 0))
    args = (x, g) + ((shift, scale) if has_mod else ())
    return pl.pallas_call(body, name=name, grid=(M // tm,), in_specs=[row, vec] + ([two, two] if has_mod else []),
                          out_specs=row, out_shape=jax.ShapeDtypeStruct((M, D), F32),
                          compiler_params=_params(("parallel",)))(*args)


def _norm_bwd_call(x, g, shift, scale, dy, n_ctx, name):
    M, D = x.shape
    has_mod = shift is not None
    tm = _row_tile(n_ctx, M) if has_mod else _pick(M, (256, 128, 64, 32, 16, 8))
    nct = n_ctx // tm

    def body(*refs):
        if has_mod:
            x_ref, g_ref, sc_ref, dy_ref, dx_ref, dg_ref, dsh_ref, dsc_ref = refs
        else:
            x_ref, g_ref, dy_ref, dx_ref, dg_ref = refs
        i = pl.program_id(0)

        @pl.when(i == 0)
        def _():
            dg_ref[...] = jnp.zeros_like(dg_ref)
            if has_mod:
                dsh_ref[...] = jnp.zeros_like(dsh_ref)
                dsc_ref[...] = jnp.zeros_like(dsc_ref)

        xv, gv, dyv = x_ref[...], g_ref[...], dy_ref[...]
        r = lax.rsqrt(jnp.mean(xv * xv, axis=-1, keepdims=True) + EPS)
        xhat = xv * r
        if has_mod:
            is_ctx = i < nct
            dy0 = dyv * (1.0 + _by_group(sc_ref, is_ctx))
            _acc_by_group(dsc_ref, is_ctx, jnp.sum(dyv * xhat * gv, axis=0, keepdims=True))
            _acc_by_group(dsh_ref, is_ctx, jnp.sum(dyv, axis=0, keepdims=True))
        else:
            dy0 = dyv
        dg_ref[...] += jnp.sum(dy0 * xhat, axis=0, keepdims=True)
        dxhat = dy0 * gv
        dx_ref[...] = r * (dxhat - xhat * jnp.mean(dxhat * xhat, axis=-1, keepdims=True))

    row = pl.BlockSpec((tm, D), lambda i: (i, 0))
    vec = pl.BlockSpec((1, D), lambda i: (0, 0))
    two = pl.BlockSpec((2, D), lambda i: (0, 0))
    if has_mod:
        args, in_specs = (x, g, scale, dy), [row, vec, two, row]
        out_specs = [row, vec, two, two]
        out_shape = [jax.ShapeDtypeStruct((M, D), F32), jax.ShapeDtypeStruct((1, D), F32),
                     jax.ShapeDtypeStruct((2, D), F32), jax.ShapeDtypeStruct((2, D), F32)]
    else:
        args, in_specs = (x, g, dy), [row, vec, row]
        out_specs = [row, vec]
        out_shape = [jax.ShapeDtypeStruct((M, D), F32), jax.ShapeDtypeStruct((1, D), F32)]
    return pl.pallas_call(body, name=name, grid=(M // tm,), in_specs=in_specs, out_specs=out_specs, out_shape=out_shape,
                          compiler_params=_params(("arbitrary",)))(*args)


@functools.partial(jax.custom_vjp, nondiff_argnums=(4, 5))
def norm_mod(x, g, shift, scale, n_ctx, name):
    return _norm_fwd_call(x, g, shift, scale, n_ctx, name)


def _norm_mod_fwd(x, g, shift, scale, n_ctx, name):
    return _norm_fwd_call(x, g, shift, scale, n_ctx, name), (x, g, shift, scale)


def _norm_mod_bwd(n_ctx, name, res, dy):
    x, g, shift, scale = res
    dx, dg, dsh, dsc = _norm_bwd_call(x, g, shift, scale, dy, n_ctx, name + "_bwd")
    return dx, dg, dsh, dsc


norm_mod.defvjp(_norm_mod_fwd, _norm_mod_bwd)


@functools.partial(jax.custom_vjp, nondiff_argnums=(2,))
def rmsnorm(x, g, name):
    return _norm_fwd_call(x, g, None, None, 0, name)


def _rmsnorm_fwd(x, g, name):
    return _norm_fwd_call(x, g, None, None, 0, name), (x, g)


def _rmsnorm_bwd(name, res, dy):
    x, g = res
    dx, dg = _norm_bwd_call(x, g, None, None, dy, 0, name + "_bwd")
    return dx, dg


rmsnorm.defvjp(_rmsnorm_fwd, _rmsnorm_bwd)


def _gres_fwd_call(h, o, gate, coef, n_ctx, name):
    M, D = h.shape
    tm = _row_tile(n_ctx, M)
    nct = n_ctx // tm

    def body(h_ref, o_ref, g_ref, out_ref):
        is_ctx = pl.program_id(0) < nct
        out_ref[...] = h_ref[...] + coef * _by_group(g_ref, is_ctx) * o_ref[...]

    row = pl.BlockSpec((tm, D), lambda i: (i, 0))
    two = pl.BlockSpec((2, D), lambda i: (0, 0))
    return pl.pallas_call(body, name=name, grid=(M // tm,), in_specs=[row, row, two], out_specs=row,
                          out_shape=jax.ShapeDtypeStruct((M, D), F32), compiler_params=_params(("parallel",)))(h, o, gate)


def _gres_bwd_call(o, gate, d, coef, n_ctx, name):
    M, D = o.shape
    tm = _row_tile(n_ctx, M)
    nct = n_ctx // tm

    def body(o_ref, g_ref, d_ref, do_ref, dg_ref):
        i = pl.program_id(0)
        is_ctx = i < nct

        @pl.when(i == 0)
        def _():
            dg_ref[...] = jnp.zeros_like(dg_ref)

        dv = d_ref[...]
        do_ref[...] = coef * _by_group(g_ref, is_ctx) * dv
        _acc_by_group(dg_ref, is_ctx, coef * jnp.sum(dv * o_ref[...], axis=0, keepdims=True))

    row = pl.BlockSpec((tm, D), lambda i: (i, 0))
    two = pl.BlockSpec((2, D), lambda i: (0, 0))
    return pl.pallas_call(body, name=name, grid=(M // tm,), in_specs=[row, two, row], out_specs=[row, two],
                          out_shape=[jax.ShapeDtypeStruct((M, D), F32), jax.ShapeDtypeStruct((2, D), F32)],
                          compiler_params=_params(("arbitrary",)))(o, gate, d)


@functools.partial(jax.custom_vjp, nondiff_argnums=(3, 4, 5))
def gated_residual(h, o, gate, coef, n_ctx, name):
    return _gres_fwd_call(h, o, gate, coef, n_ctx, name)


def _gres_fwd(h, o, gate, coef, n_ctx, name):
    return _gres_fwd_call(h, o, gate, coef, n_ctx, name), (o, gate)


def _gres_bwd(coef, n_ctx, name, res, d):
    o, gate = res
    do, dg = _gres_bwd_call(o, gate, d, coef, n_ctx, name + "_bwd")
    return d, do, dg


gated_residual.defvjp(_gres_fwd, _gres_bwd)


def _swiglu_fwd_call(ab, name):
    M, F2 = ab.shape
    Fh = F2 // 2
    tm, tn = _pick(M, (128, 64, 32, 16, 8)), Fh
    nf = Fh // tn

    def body(a_ref, b_ref, o_ref):
        a = a_ref[...]
        o_ref[...] = a * jax.nn.sigmoid(a) * b_ref[...]

    return pl.pallas_call(body, name=name, grid=(M // tm, nf),
                          in_specs=[pl.BlockSpec((tm, tn), lambda i, j: (i, j)), pl.BlockSpec((tm, tn), lambda i, j: (i, j + nf))],
                          out_specs=pl.BlockSpec((tm, tn), lambda i, j: (i, j)),
                          out_shape=jax.ShapeDtypeStruct((M, Fh), F32), compiler_params=_params(("parallel", "parallel")))(ab, ab)


def _swiglu_bwd_call(ab, dact, name):
    M, F2 = ab.shape
    Fh = F2 // 2
    tm = _pick(M, (128, 64, 32, 16, 8))

    def body(ab_ref, d_ref, o_ref):
        a, b, d = ab_ref[:, 0:Fh], ab_ref[:, Fh:F2], d_ref[...]
        sig = jax.nn.sigmoid(a)
        o_ref[:, 0:Fh] = d * b * sig * (1.0 + a * (1.0 - sig))
        o_ref[:, Fh:F2] = d * a * sig

    return pl.pallas_call(body, name=name, grid=(M // tm,),
                          in_specs=[pl.BlockSpec((tm, F2), lambda i: (i, 0)), pl.BlockSpec((tm, Fh), lambda i: (i, 0))],
                          out_specs=pl.BlockSpec((tm, F2), lambda i: (i, 0)),
                          out_shape=jax.ShapeDtypeStruct((M, F2), F32), compiler_params=_params(("parallel",)))(ab, dact)


@functools.partial(jax.custom_vjp, nondiff_argnums=(1,))
def swiglu_act(ab, name):
    return _swiglu_fwd_call(ab, name)


def _swiglu_fwd(ab, name):
    return _swiglu_fwd_call(ab, name), (ab,)


def _swiglu_bwd(name, res, d):
    return (_swiglu_bwd_call(res[0], d, name + "_bwd"),)


swiglu_act.defvjp(_swiglu_fwd, _swiglu_bwd)


def _mix_call(gl, mla, zz, gqa, d, name):
    M, D = mla.shape
    tm = _pick(M, (128, 64, 32, 16, 8))
    bwd = d is not None

    def body(*refs):
        gl_ref, mla_ref, zz_ref, gqa_ref = refs[:4]
        g0, g1, g2 = (jax.nn.sigmoid(gl_ref[:, i * D:(i + 1) * D]) for i in range(3))
        za, sb = zz_ref[:, 0:D], jax.nn.sigmoid(zz_ref[:, D:2 * D])
        ssm = za * sb
        if not bwd:
            refs[4][...] = g0 * mla_ref[...] + g1 * ssm + g2 * gqa_ref[...]
            return
        d_ref, dgl_ref, dmla_ref, dzz_ref, dgqa_ref = refs[4:]
        dv = d_ref[...]
        dmla_ref[...] = g0 * dv
        dgqa_ref[...] = g2 * dv
        dssm = g1 * dv
        dzz_ref[:, 0:D] = dssm * sb
        dzz_ref[:, D:2 * D] = dssm * ssm * (1.0 - sb)
        dgl_ref[:, 0:D] = dv * mla_ref[...] * g0 * (1.0 - g0)
        dgl_ref[:, D:2 * D] = dv * ssm * g1 * (1.0 - g1)
        dgl_ref[:, 2 * D:3 * D] = dv * gqa_ref[...] * g2 * (1.0 - g2)

    def rows(w):
        return pl.BlockSpec((tm, w), lambda i: (i, 0))

    def sds(w):
        return jax.ShapeDtypeStruct((M, w), F32)

    in_specs, args = [rows(3 * D), rows(D), rows(2 * D), rows(D)], (gl, mla, zz, gqa)
    if bwd:
        return pl.pallas_call(body, name=name, grid=(M // tm,), in_specs=in_specs + [rows(D)],
                              out_specs=[rows(3 * D), rows(D), rows(2 * D), rows(D)],
                              out_shape=[sds(3 * D), sds(D), sds(2 * D), sds(D)],
                              compiler_params=_params(("parallel",)))(*args, d)
    return pl.pallas_call(body, name=name, grid=(M // tm,), in_specs=in_specs, out_specs=rows(D), out_shape=sds(D),
                          compiler_params=_params(("parallel",)))(*args)


@functools.partial(jax.custom_vjp, nondiff_argnums=(4,))
def gated_mix(gl, mla, zz, gqa, name):
    return _mix_call(gl, mla, zz, gqa, None, name)


def _gated_mix_fwd(gl, mla, zz, gqa, name):
    return _mix_call(gl, mla, zz, gqa, None, name), (gl, mla, zz, gqa)


def _gated_mix_bwd(name, res, d):
    return tuple(_mix_call(*res, d, name + "_bwd"))


gated_mix.defvjp(_gated_mix_fwd, _gated_mix_bwd)


_DN_NT = (((1,), (1,)), ((), ()))
_DN_TN = (((0,), (0,)), ((), ()))
_DN_NN = (((1,), (0,)), ((), ()))


def _dot(a, b, dn):
    return lax.dot_general(a.astype(MXU_DTYPE), b.astype(MXU_DTYPE), dn, preferred_element_type=F32)


_TQ = (1024, 512, 256, 128, 64, 32, 16, 8)


def _flash_fwd_call(q, k1, v1, k2, v2, name):
    H, Tq, dk = q.shape
    T1, dv = k1.shape[1], v1.shape[2]
    has2 = k2 is not None
    tq, tk = _pick(Tq, _TQ), _pick(T1, _TQ)
    off = 1 if has2 else 0
    nkv = T1 // tk + off
    C = k2.shape[1] if has2 else 0
    rows = max(tk, C)

    def body(*refs):
        if has2:
            q_ref, k1_ref, v1_ref, k2_ref, v2_ref, o_ref, lse_ref, m_s, acc_s, va_s = refs
        else:
            q_ref, k1_ref, v1_ref, o_ref, lse_ref, m_s, acc_s, va_s = refs
        j = pl.program_id(2)

        @pl.when(j == 0)
        def _():
            m_s[...] = jnp.full_like(m_s, NEG_INF)
            acc_s[...] = jnp.zeros_like(acc_s)
            va_s[:, dv:2 * dv] = jnp.ones((rows, dv), MXU_DTYPE)

        def step(k, v, n):
            va_s[0:n, 0:dv] = v.astype(MXU_DTYPE)
            s = _dot(q_ref[0], k, _DN_NT)
            m_prev = m_s[...]
            m_new = jnp.maximum(m_prev, jnp.max(s, axis=-1, keepdims=True))
            p = jnp.exp2(s - m_new)
            acc_s[...] = jnp.exp2(m_prev - m_new) * acc_s[...] + _dot(p, va_s[0:n, :], _DN_NN)
            m_s[...] = m_new

        if has2:
            @pl.when(j == 0)
            def _():
                step(k2_ref[0], v2_ref[0], C)

            @pl.when(j > 0)
            def _():
                step(k1_ref[0], v1_ref[0], tk)
        else:
            step(k1_ref[0], v1_ref[0], tk)

        @pl.when(j == nkv - 1)
        def _():
            l = acc_s[:, dv:dv + 1]
            o_ref[0] = acc_s[:, 0:dv] / l
            lse_ref[0] = m_s[...] + jnp.log2(l)

    qs = pl.BlockSpec((1, tq, dk), lambda h, i, j: (h, i, 0))
    k1s = pl.BlockSpec((1, tk, dk), lambda h, i, j: (h, jnp.maximum(j - off, 0), 0))
    v1s = pl.BlockSpec((1, tk, dv), lambda h, i, j: (h, jnp.maximum(j - off, 0), 0))
    in_specs, args = [qs, k1s, v1s], [q, k1, v1]
    if has2:
        in_specs += [pl.BlockSpec((1, C, dk), lambda h, i, j: (h, 0, 0)), pl.BlockSpec((1, C, dv), lambda h, i, j: (h, 0, 0))]
        args += [k2, v2]
    return pl.pallas_call(
        body, name=name, grid=(H, Tq // tq, nkv), in_specs=in_specs,
        out_specs=[pl.BlockSpec((1, tq, dv), lambda h, i, j: (h, i, 0)), pl.BlockSpec((1, tq, 1), lambda h, i, j: (h, i, 0))],
        out_shape=[jax.ShapeDtypeStruct((H, Tq, dv), F32), jax.ShapeDtypeStruct((H, Tq, 1), F32)],
        scratch_shapes=[pltpu.VMEM((tq, 1), F32), pltpu.VMEM((tq, 2 * dv), F32), pltpu.VMEM((rows, 2 * dv), MXU_DTYPE)],
        compiler_params=_params(("parallel", "parallel", "arbitrary")))(*args)


def _flash_bwd_call(q, k1, v1, k2, v2, o, lse, do, name):
    H, Tq, dk = q.shape
    T1, dv = k1.shape[1], v1.shape[2]
    has2 = k2 is not None
    tq, tk = _pick(Tq, _TQ), _pick(T1, _TQ)
    off = 1 if has2 else 0
    nkv, nq = T1 // tk + off, Tq // tq
    C = k2.shape[1] if has2 else 0
    rows = max(tk, C)

    def body(*refs):
        if has2:
            (q_ref, k1_ref, v1_ref, k2_ref, v2_ref, o_ref, lse_ref, do_ref,
             dq_ref, dk1_ref, dv1_ref, dk2_ref, dv2_ref, dk_s, dv_s) = refs
        else:
            q_ref, k1_ref, v1_ref, o_ref, lse_ref, do_ref, dq_ref, dk1_ref, dv1_ref, dk_s, dv_s = refs
        j, i = pl.program_id(1), pl.program_id(2)

        @pl.when((j == 0) & (i == 0))
        def _():
            dq_ref[...] = jnp.zeros_like(dq_ref)

        @pl.when(i == 0)
        def _():
            dk_s[...] = jnp.zeros_like(dk_s)
            dv_s[...] = jnp.zeros_like(dv_s)

        def step(k, v, n):
            qb, dob = q_ref[0], do_ref[0]
            p = jnp.exp2(_dot(qb, k, _DN_NT) - lse_ref[0])
            dv_s[0:n, :] += _dot(p, dob, _DN_TN)
            dol = dob * LN2
            ds = p * (_dot(dol, v, _DN_NT) - jnp.sum(dol * o_ref[0], axis=-1, keepdims=True))
            dk_s[0:n, :] += _dot(ds, qb, _DN_TN)
            r0 = pl.multiple_of(i * tq, tq)
            dq_ref[0, pl.ds(r0, tq), :] += _dot(ds, k, _DN_NN)

        if has2:
            @pl.when(j == 0)
            def _():
                step(k2_ref[0], v2_ref[0], C)

            @pl.when(j > 0)
            def _():
                step(k1_ref[0], v1_ref[0], tk)

            @pl.when((i == nq - 1) & (j == 0))
            def _():
                dk2_ref[0] = dk_s[0:C, :]
                dv2_ref[0] = dv_s[0:C, :]

            @pl.when((i == nq - 1) & (j > 0))
            def _():
                dk1_ref[0] = dk_s[0:tk, :]
                dv1_ref[0] = dv_s[0:tk, :]
        else:
            step(k1_ref[0], v1_ref[0], tk)

            @pl.when(i == nq - 1)
            def _():
                dk1_ref[0] = dk_s[...]
                dv1_ref[0] = dv_s[...]

    qs = pl.BlockSpec((1, tq, dk), lambda h, j, i: (h, i, 0))
    os_ = pl.BlockSpec((1, tq, dv), lambda h, j, i: (h, i, 0))
    ls = pl.BlockSpec((1, tq, 1), lambda h, j, i: (h, i, 0))
    k1s = pl.BlockSpec((1, tk, dk), lambda h, j, i: (h, jnp.maximum(j - off, 0), 0))
    v1s = pl.BlockSpec((1, tk, dv), lambda h, j, i: (h, jnp.maximum(j - off, 0), 0))
    in_specs, args = [qs, k1s, v1s], [q, k1, v1]
    out_specs = [pl.BlockSpec((1, Tq, dk), lambda h, j, i: (h, 0, 0)), k1s, v1s]
    out_shape = [jax.ShapeDtypeStruct((H, Tq, dk), F32), jax.ShapeDtypeStruct((H, T1, dk), F32),
                 jax.ShapeDtypeStruct((H, T1, dv), F32)]
    if has2:
        k2s = pl.BlockSpec((1, C, dk), lambda h, j, i: (h, 0, 0))
        v2s = pl.BlockSpec((1, C, dv), lambda h, j, i: (h, 0, 0))
        in_specs += [k2s, v2s]
        args += [k2, v2]
        out_specs += [k2s, v2s]
        out_shape += [jax.ShapeDtypeStruct((H, C, dk), F32), jax.ShapeDtypeStruct((H, C, dv), F32)]
    in_specs += [os_, ls, os_]
    args += [o, lse, do]
    return pl.pallas_call(
        body, name=name, grid=(H, nkv, nq), in_specs=in_specs, out_specs=out_specs, out_shape=out_shape,
        scratch_shapes=[pltpu.VMEM((rows, dk), F32), pltpu.VMEM((rows, dv), F32)],
        compiler_params=_params(("parallel", "arbitrary", "arbitrary")))(*args)


@functools.partial(jax.custom_vjp, nondiff_argnums=(5,))
def flash2(q, k1, v1, k2, v2, name):
    return _flash_fwd_call(q, k1, v1, k2, v2, name)[0]


def _flash2_fwd(q, k1, v1, k2, v2, name):
    o, lse = _flash_fwd_call(q, k1, v1, k2, v2, name)
    return o, (q, k1, v1, k2, v2, o, lse)


def _flash2_bwd(name, res, do):
    q, k1, v1, k2, v2, o, lse = res
    return tuple(_flash_bwd_call(q, k1, v1, k2, v2, o, lse, do, name + "_bwd"))


flash2.defvjp(_flash2_fwd, _flash2_bwd)


@functools.partial(jax.custom_vjp, nondiff_argnums=(3,))
def flash1(q, k, v, name):
    return _flash_fwd_call(q, k, v, None, None, name)[0]


def _flash1_fwd(q, k, v, name):
    o, lse = _flash_fwd_call(q, k, v, None, None, name)
    return o, (q, k, v, o, lse)


def _flash1_bwd(name, res, do):
    q, k, v, o, lse = res
    return tuple(_flash_bwd_call(q, k, v, None, None, o, lse, do, name + "_bwd"))


flash1.defvjp(_flash1_fwd, _flash1_bwd)


def _gqa_call(q, k, v, sink_rows, n_ctx, res, name):
    KV, G, M, d = q.shape
    B, C = BLOCK, n_ctx
    assert WINDOW == BLOCK
    nt, nct, R = M // B, n_ctx // B, G * B
    scale = d ** -0.5
    bwd = res is not None

    def body(*refs):
        q_ref, kc_ref, vc_ref, k0, k1, k2, v0, v1, v2, sink_ref, o_ref, lse_ref = refs[:12]
        i = pl.program_id(1)
        qb = q_ref[0].reshape(R, d)
        a = lax.broadcasted_iota(jnp.int32, (R, B), 0) % B
        b = lax.broadcasted_iota(jnp.int32, (R, B), 1)
        kb, vb, masks = (k0[0], k1[0], k2[0]), (v0[0], v1[0], v2[0]), []
        for r in (-1, 0, 1):
            in_range = (i >= nct) & (i + r >= nct) & (i + r <= nt - 1)
            masks.append(in_range & (a <= b) if r == -1 else (in_range & (a >= b) if r == 1 else in_range & (a >= 0)))
        kc, vc, sink = kc_ref[0], vc_ref[0], sink_ref[0]
        s_c = _dot(qb, kc, _DN_NT) * scale
        s_b = [jnp.where(masks[t], _dot(qb, kb[t], _DN_NT) * scale, NEG_INF) for t in range(3)]
        if not bwd:
            m = jnp.maximum(sink, jnp.max(s_c, axis=-1, keepdims=True))
            for s in s_b:
                m = jnp.maximum(m, jnp.max(s, axis=-1, keepdims=True))
            e_c, e_b = jnp.exp(s_c - m), [jnp.exp(s - m) for s in s_b]
            den = jnp.exp(sink - m) + jnp.sum(e_c, axis=-1, keepdims=True)
            for e in e_b:
                den = den + jnp.sum(e, axis=-1, keepdims=True)
            inv = 1.0 / den
            o = _dot(e_c * inv, vc, _DN_NN)
            for t in range(3):
                o = o + _dot(e_b[t] * inv, vb[t], _DN_NN)
            o_ref[0] = o.reshape(G, B, d)
            lse_ref[0] = (m + jnp.log(den)).reshape(G, B, 1)
            return
        do_ref, dq_ref, dkc_ref, dvc_ref, dkb_ref, dvb_ref, dsink_ref = refs[12:]

        @pl.when(i == 0)
        def _():
            dkc_ref[...] = jnp.zeros_like(dkc_ref)
            dvc_ref[...] = jnp.zeros_like(dvc_ref)
            dsink_ref[...] = jnp.zeros_like(dsink_ref)

        lse, dob = lse_ref[0].reshape(R, 1), do_ref[0].reshape(R, d)
        delta = jnp.sum(dob * o_ref[0].reshape(R, d), axis=-1, keepdims=True)
        p_c = jnp.exp(s_c - lse)
        ds_c = p_c * (_dot(dob, vc, _DN_NT) - delta) * scale
        dq = _dot(ds_c, kc, _DN_NN)
        dkc_ref[0] += _dot(ds_c, qb, _DN_TN)
        dvc_ref[0] += _dot(p_c, dob, _DN_TN)
        for t in range(3):
            p = jnp.exp(s_b[t] - lse)
            ds = p * (_dot(dob, vb[t], _DN_NT) - delta) * scale
            dq = dq + _dot(ds, kb[t], _DN_NN)
            dkb_ref[0, 0, t] = _dot(ds, qb, _DN_TN)
            dvb_ref[0, 0, t] = _dot(p, dob, _DN_TN)
        dq_ref[0] = dq.reshape(G, B, d)
        dsink_ref[0] -= jnp.exp(sink - lse) * delta

    def band(r):
        return lambda h, i: (h, jnp.clip(i + r, nct, nt - 1), 0)

    qs = pl.BlockSpec((1, G, B, d), lambda h, i: (h, 0, i, 0))
    ls = pl.BlockSpec((1, G, B, 1), lambda h, i: (h, 0, i, 0))
    cs = pl.BlockSpec((1, C, d), lambda h, i: (h, 0, 0))
    ss = pl.BlockSpec((1, R, 1), lambda h, i: (h, 0, 0))
    bs = [pl.BlockSpec((1, B, d), band(r)) for r in (-1, 0, 1)]
    in_specs = [qs, cs, cs] + bs + bs + [ss]
    args = [q, k, v, k, k, k, v, v, v, sink_rows]
    if not bwd:
        return pl.pallas_call(body, name=name, grid=(KV, nt), in_specs=in_specs, out_specs=[qs, ls],
                              out_shape=[jax.ShapeDtypeStruct((KV, G, M, d), F32), jax.ShapeDtypeStruct((KV, G, M, 1), F32)],
                              compiler_params=_params(("parallel", "parallel")))(*args)
    o, lse, do = res
    part = pl.BlockSpec((1, 1, 3, B, d), lambda h, i: (h, i, 0, 0, 0))
    part_shape = jax.ShapeDtypeStruct((KV, nt, 3, B, d), F32)
    return pl.pallas_call(body, name=name, grid=(KV, nt), in_specs=in_specs + [qs, ls, qs],
                          out_specs=[qs, cs, cs, part, part, ss],
                          out_shape=[jax.ShapeDtypeStruct((KV, G, M, d), F32), jax.ShapeDtypeStruct((KV, C, d), F32),
                                     jax.ShapeDtypeStruct((KV, C, d), F32), part_shape, part_shape,
                                     jax.ShapeDtypeStruct((KV, R, 1), F32)],
                          compiler_params=_params(("parallel", "arbitrary")))(*args, o, lse, do)


@functools.partial(jax.custom_vjp, nondiff_argnums=(4, 5))
def gqa_core(q, k, v, sink_rows, n_ctx, name):
    return _gqa_call(q, k, v, sink_rows, n_ctx, None, name)[0]


def _gqa_core_fwd(q, k, v, sink_rows, n_ctx, name):
    o, lse = _gqa_call(q, k, v, sink_rows, n_ctx, None, name)
    return o, (q, k, v, sink_rows, o, lse)


def _gqa_core_bwd(n_ctx, name, res, do):
    q, k, v, sink_rows, o, lse = res
    dq, dkc, dvc, dkb, dvb, dsink = _gqa_call(q, k, v, sink_rows, n_ctx, (o, lse, do), name + "_bwd")

    def keys(ctx_part, band_part):
        zero = jnp.zeros_like(band_part[:, :1, 0])
        blocks = (jnp.concatenate([band_part[:, 1:, 0], zero], axis=1) + band_part[:, :, 1]
                  + jnp.concatenate([zero, band_part[:, :-1, 2]], axis=1))
        rows = blocks.reshape(k.shape)
        return jnp.concatenate([rows[:, :n_ctx] + ctx_part, rows[:, n_ctx:]], axis=1)

    return dq, keys(dkc, dkb), keys(dvc, dvb), dsink


gqa_core.defvjp(_gqa_core_fwd, _gqa_core_bwd)


def gqa_attention(gq, gk, gv, sink, n_ctx, name):
    M, H, d = gq.shape
    G = H // GQA_KV_HEADS
    q4 = gq.reshape(M, GQA_KV_HEADS, G, d).transpose(1, 2, 0, 3)
    sink_rows = jnp.repeat(sink.reshape(GQA_KV_HEADS, G), BLOCK, axis=1)[..., None]
    o = gqa_core(q4, gk.transpose(1, 0, 2), gv.transpose(1, 0, 2), sink_rows, n_ctx, name)
    return o.transpose(2, 0, 1, 3).reshape(M, H * d)


def _scan_call(a, x, s, *, rev, adj, n_ctx, name):
    M, W = x.shape
    cw = SSM_CW
    J = W // (2 * cw)
    L = _row_tile(n_ctx, M)
    nt, nc = M // L, n_ctx // L
    asc = rev == adj
    sub = min(SCAN_SUB, L)
    nsub = L // sub
    n_steps = int(math.log2(sub))
    assert 1 << n_steps == sub

    def tile(t):
        if not rev:
            return nt - 1 - t if adj else t
        if not adj:
            return jnp.where(t < nc, nc - 1 - t, nt - 1 - (t - nc))
        return jnp.where(t < nt - nc, nc + t, t - (nt - nc))

    def body(*refs):
        if adj:
            a_ref, x_ref, s_ref, o_ref, da_ref, car_ref = refs
        else:
            a_ref, x_ref, o_ref, car_ref = refs
        t = pl.program_id(1)

        @pl.when(t == 0)
        def _():
            car_ref[...] = jnp.zeros_like(car_ref)
            if adj:
                da_ref[...] = jnp.zeros_like(da_ref)

        ar, ai = a_ref[:, 0:cw], a_ref[:, cw:2 * cw]
        powers, pr, pi = [], ar, ai
        for _ in range(n_steps):
            powers.append((pr, pi))
            pr, pi = pr * pr - pi * pi, 2.0 * pr * pi
        row = lax.broadcasted_iota(jnp.int32, (sub, cw), 0)
        first, last = (0, sub - 1) if asc else (sub - 1, 0)

        def scan_rows(i, carry):
            cr, ci = carry[0], carry[1]
            r0 = pl.multiple_of((i if asc else nsub - 1 - i) * sub, sub)
            xr, xi = x_ref[pl.ds(r0, sub), 0:cw], x_ref[pl.ds(r0, sub), cw:2 * cw]
            xr = xr + jnp.where(row == first, ar * cr - ai * ci, 0.0)
            xi = xi + jnp.where(row == first, ar * ci + ai * cr, 0.0)
            k = 1
            for pr, pi in powers:
                if asc:
                    sr, si, keep = pltpu.roll(xr, k, 0), pltpu.roll(xi, k, 0), row >= k
                else:
                    sr, si, keep = pltpu.roll(xr, sub - k, 0), pltpu.roll(xi, sub - k, 0), row < sub - k
                sr, si = jnp.where(keep, sr, 0.0), jnp.where(keep, si, 0.0)
                xr, xi = xr + pr * sr - pi * si, xi + pr * si + pi * sr
                k *= 2
            o_ref[pl.ds(r0, sub), 0:cw] = xr
            o_ref[pl.ds(r0, sub), cw:2 * cw] = xi
            out = (jnp.sum(jnp.where(row == last, xr, 0.0), axis=0, keepdims=True),
                   jnp.sum(jnp.where(row == last, xi, 0.0), axis=0, keepdims=True))
            if adj:
                if asc:
                    gr, gi = pltpu.roll(xr, 1, 0), pltpu.roll(xi, 1, 0)
                else:
                    gr, gi = pltpu.roll(xr, sub - 1, 0), pltpu.roll(xi, sub - 1, 0)
                gr, gi = jnp.where(row == first, cr, gr), jnp.where(row == first, ci, gi)
                sr, si = s_ref[pl.ds(r0, sub), 0:cw], s_ref[pl.ds(r0, sub), cw:2 * cw]
                out += (carry[2] + jnp.sum(sr * gr + si * gi, axis=0, keepdims=True),
                        carry[3] + jnp.sum(sr * gi - si * gr, axis=0, keepdims=True))
            return out

        init = (car_ref[:, 0:cw], car_ref[:, cw:2 * cw])
        if adj:
            init += (jnp.zeros((1, cw), F32), jnp.zeros((1, cw), F32))
        done = lax.fori_loop(0, nsub, scan_rows, init)
        car_ref[:, 0:cw] = done[0]
        car_ref[:, cw:2 * cw] = done[1]
        if adj:
            da_ref[:, 0:cw] += done[2]
            da_ref[:, cw:2 * cw] += done[3]

    blk = pl.BlockSpec((L, 2 * cw), lambda j, t: (tile(t), j))
    vec = pl.BlockSpec((1, 2 * cw), lambda j, t: (0, j))
    if adj:
        in_specs, args = [vec, blk, blk], (a, x, s)
        out_specs = [blk, vec]
        out_shape = [jax.ShapeDtypeStruct((M, W), F32), jax.ShapeDtypeStruct((1, W), F32)]
    else:
        in_specs, args = [vec, blk], (a, x)
        out_specs = blk
        out_shape = jax.ShapeDtypeStruct((M, W), F32)
    return pl.pallas_call(body, name=name, grid=(J, nt), in_specs=in_specs, out_specs=out_specs, out_shape=out_shape,
                          scratch_shapes=[pltpu.VMEM((1, 2 * cw), F32)],
                          compiler_params=_params(("parallel", "arbitrary")))(*args)


def _conj_layout(a):
    cw = SSM_CW
    J = a.shape[1] // (2 * cw)
    a4 = a.reshape(1, J, 2, cw)
    return jnp.concatenate([a4[:, :, 0:1], -a4[:, :, 1:2]], axis=2).reshape(a.shape)


@functools.partial(jax.custom_vjp, nondiff_argnums=(2, 3, 4))
def diag_scan(a, x, rev, n_ctx, name):
    return _scan_call(a, x, None, rev=rev, adj=False, n_ctx=n_ctx, name=name)


def _diag_scan_fwd(a, x, rev, n_ctx, name):
    s = _scan_call(a, x, None, rev=rev, adj=False, n_ctx=n_ctx, name=name)
    return s, (a, s)


def _diag_scan_bwd(rev, n_ctx, name, res, ds):
    a, s = res
    g, da = _scan_call(_conj_layout(a), ds, s, rev=rev, adj=True, n_ctx=n_ctx, name=name + "_adj")
    return da, g


diag_scan.defvjp(_diag_scan_fwd, _diag_scan_bwd)


def _loss_call(h, g, target, name):
    M, D = h.shape
    tm = _pick(M, (256, 128, 64, 32, 16, 8))

    def body(h_ref, g_ref, t_ref, loss_ref, dh_ref, dg_ref):
        i = pl.program_id(0)

        @pl.when(i == 0)
        def _():
            loss_ref[...] = jnp.zeros_like(loss_ref)
            dg_ref[...] = jnp.zeros_like(dg_ref)

        xv, gv = h_ref[...], g_ref[...]
        r = lax.rsqrt(jnp.mean(xv * xv, axis=-1, keepdims=True) + EPS)
        xhat = xv * r
        err = xhat * gv - t_ref[...]
        loss_ref[...] += 0.5 * jnp.sum(jnp.mean(err * err, axis=-1, keepdims=True), axis=0, keepdims=True)
        dy = err * (1.0 / D)
        dg_ref[...] += jnp.sum(dy * xhat, axis=0, keepdims=True)
        dxhat = dy * gv
        dh_ref[...] = r * (dxhat - xhat * jnp.mean(dxhat * xhat, axis=-1, keepdims=True))

    row = pl.BlockSpec((tm, D), lambda i: (i, 0))
    vec = pl.BlockSpec((1, D), lambda i: (0, 0))
    one = pl.BlockSpec((1, 1), lambda i: (0, 0))
    return pl.pallas_call(body, name=name, grid=(M // tm,), in_specs=[row, vec, row], out_specs=[one, row, vec],
                          out_shape=[jax.ShapeDtypeStruct((1, 1), F32), jax.ShapeDtypeStruct((M, D), F32),
                                     jax.ShapeDtypeStruct((1, D), F32)],
                          compiler_params=_params(("arbitrary",)))(h, g, target)


@functools.partial(jax.custom_vjp, nondiff_argnums=(3,))
def loss_head(h, g, target, name):
    return _loss_call(h, g, target, name)[0][0, 0]


def _loss_head_fwd(h, g, target, name):
    loss, dh, dg = _loss_call(h, g, target, name)
    return loss[0, 0], (dh, dg, target)


def _loss_head_bwd(name, res, ct):
    dh, dg, target = res
    return ct * dh, ct * dg, jnp.zeros_like(target)


loss_head.defvjp(_loss_head_fwd, _loss_head_bwd)


def _adamw_call(w, gstack, m, v, name):
    R, Cn = w.shape
    n = gstack.shape[0]
    tr = _pick(R, (64, 32, 16, 8))

    def body(w_ref, g_ref, m_ref, v_ref, go_ref, d_ref, mo_ref, vo_ref):
        g = g_ref[0]
        for s in range(1, n):
            g = g + g_ref[s]
        mn = ADAM_B1 * m_ref[...] + (1.0 - ADAM_B1) * g
        vn = ADAM_B2 * v_ref[...] + (1.0 - ADAM_B2) * (g * g)
        m_hat = mn / (1.0 - ADAM_B1 ** ADAM_STEP)
        v_hat = vn / (1.0 - ADAM_B2 ** ADAM_STEP)
        go_ref[...] = g
        d_ref[...] = -ADAM_LR * (m_hat / (jnp.sqrt(v_hat) + ADAM_EPS) + ADAM_WD * w_ref[...])
        mo_ref[...] = mn
        vo_ref[...] = vn

    blk = pl.BlockSpec((tr, Cn), lambda i: (i, 0))
    gblk = pl.BlockSpec((n, tr, Cn), lambda i: (0, i, 0))
    sds = jax.ShapeDtypeStruct((R, Cn), F32)
    return pl.pallas_call(body, name=name, grid=(R // tr,), in_specs=[blk, gblk, blk, blk], out_specs=[blk] * 4,
                          out_shape=[sds] * 4, compiler_params=_params(("parallel",)))(w, gstack, m, v)


MESH = pl.DeviceIdType.MESH
ANY = pl.BlockSpec(memory_space=pl.ANY)


def _place():
    return lax.axis_index("x"), lax.axis_index("y"), lax.axis_index("c")


def _sibling_exchange(v, name):
    n = v.shape[0]

    def body(v_ref, o_ref, send_sems, recv_sems):
        x, y, c = _place()
        copies = [pltpu.make_async_remote_copy(src_ref=v_ref.at[k], dst_ref=o_ref.at[k], send_sem=send_sems.at[k],
                                               recv_sem=recv_sems.at[k], device_id=(x, y, 1 - c), device_id_type=MESH)
                  for k in range(n)]
        for cp in copies:
            cp.start()
        for cp in copies:
            cp.wait()

    return pl.pallas_call(body, name=name, in_specs=[ANY], out_specs=ANY, out_shape=jax.ShapeDtypeStruct(v.shape, v.dtype),
                          scratch_shapes=[pltpu.SemaphoreType.DMA((n,)), pltpu.SemaphoreType.DMA((n,))])(v)


def _sibling_exchange_half(v, name):
    n = v.shape[0]

    def body(v_ref, o_ref, send_sems, recv_sems):
        x, y, c = _place()
        copies = [pltpu.make_async_remote_copy(src_ref=v_ref.at[k, pl.ds(1 - c, 1)], dst_ref=o_ref.at[k],
                                               send_sem=send_sems.at[k], recv_sem=recv_sems.at[k],
                                               device_id=(x, y, 1 - c), device_id_type=MESH)
                  for k in range(n)]
        for cp in copies:
            cp.start()
        for cp in copies:
            cp.wait()

    return pl.pallas_call(body, name=name, in_specs=[ANY], out_specs=ANY,
                          out_shape=jax.ShapeDtypeStruct((n, 1) + v.shape[2:], v.dtype),
                          scratch_shapes=[pltpu.SemaphoreType.DMA((n,)), pltpu.SemaphoreType.DMA((n,))])(v)


def _chip_exchange(v, same, name):
    out_shape = (N_CHIPS,) + (v.shape if same else v.shape[1:])

    def body(v_ref, o_ref, send_sems, recv_sems, local_sem):
        x, y, c = _place()
        me = 2 * x + y
        own = pltpu.make_async_copy(v_ref if same else v_ref.at[me], o_ref.at[me], local_sem)
        own.start()
        copies = []
        for k, (fx, fy) in enumerate(((1, 0), (0, 1), (1, 1))):
            px, py = jnp.where(fx == 1, 1 - x, x), jnp.where(fy == 1, 1 - y, y)
            src = v_ref if same else v_ref.at[2 * px + py]
            copies.append(pltpu.make_async_remote_copy(src_ref=src, dst_ref=o_ref.at[me], send_sem=send_sems.at[k],
                                                       recv_sem=recv_sems.at[k], device_id=(px, py, c), device_id_type=MESH))
        for cp in copies:
            cp.start()
        for cp in copies:
            cp.wait()
        own.wait()

    return pl.pallas_call(body, name=name, in_specs=[ANY], out_specs=ANY, out_shape=jax.ShapeDtypeStruct(out_shape, v.dtype),
                          scratch_shapes=[pltpu.SemaphoreType.DMA((3,)), pltpu.SemaphoreType.DMA((3,)), pltpu.SemaphoreType.DMA])(v)


def _all_gather(v, name):
    def body(v_ref, o_ref, send_sems, recv_sems, local_sem):
        x, y, c = _place()
        me = 4 * x + 2 * y + c
        own = pltpu.make_async_copy(v_ref, o_ref.at[me], local_sem)
        own.start()
        copies = []
        for k in range(1, 8):
            fx, fy, fc = (k >> 2) & 1, (k >> 1) & 1, k & 1
            peer = (jnp.where(fx == 1, 1 - x, x), jnp.where(fy == 1, 1 - y, y), jnp.where(fc == 1, 1 - c, c))
            copies.append(pltpu.make_async_remote_copy(src_ref=v_ref, dst_ref=o_ref.at[me], send_sem=send_sems.at[k - 1],
                                                       recv_sem=recv_sems.at[k - 1], device_id=peer, device_id_type=MESH))
        for cp in copies:
            cp.start()
        for cp in copies:
            cp.wait()
        own.wait()

    return pl.pallas_call(body, name=name, in_specs=[ANY], out_specs=ANY,
                          out_shape=jax.ShapeDtypeStruct((8,) + v.shape, v.dtype),
                          scratch_shapes=[pltpu.SemaphoreType.DMA((7,)), pltpu.SemaphoreType.DMA((7,)), pltpu.SemaphoreType.DMA])(v)


def _add_own_half(g, r, c, name):
    n, _, R, W = g.shape
    tr = _pick(R, (256, 128, 64, 32, 16, 8))

    def body(c_ref, g_ref, r_ref, o_ref):
        o_ref[...] = g_ref[0] + r_ref[0]

    grid_spec = pltpu.PrefetchScalarGridSpec(
        num_scalar_prefetch=1, grid=(n, R // tr),
        in_specs=[pl.BlockSpec((1, 1, tr, W), lambda p, i, c_ref: (p, c_ref[0], i, 0)),
                  pl.BlockSpec((1, 1, tr, W), lambda p, i, c_ref: (p, 0, i, 0))],
        out_specs=pl.BlockSpec((1, tr, W), lambda p, i, c_ref: (p, i, 0)))
    return pl.pallas_call(body, name=name, grid_spec=grid_spec, out_shape=jax.ShapeDtypeStruct((n, R, W), F32),
                          compiler_params=_params(("parallel", "parallel")))(c.reshape(1).astype(jnp.int32), g, r)


def _sum_stack(v, name):
    n, R, W = v.shape
    tr = _pick(R, (256, 128, 64, 32, 16, 8))

    def body(v_ref, o_ref):
        acc = v_ref[0]
        for s in range(1, n):
            acc = acc + v_ref[s]
        o_ref[...] = acc

    return pl.pallas_call(body, name=name, grid=(R // tr,), in_specs=[pl.BlockSpec((n, tr, W), lambda i: (0, i, 0))],
                          out_specs=pl.BlockSpec((tr, W), lambda i: (i, 0)), out_shape=jax.ShapeDtypeStruct((R, W), F32),
                          compiler_params=_params(("parallel",)))(v)


def _flat_rows(n, mult):
    rows = -(-n // FLAT_W)
    return -(-rows // mult) * mult


def _by_core(a, b, c):
    return lax.dynamic_index_in_dim(jnp.stack([a, b]), c, axis=0, keepdims=False)


def _lane_padded(n):
    return -(-n // LANES) * LANES


def _lane_pad(a):
    pad = _lane_padded(a.shape[-1]) - a.shape[-1]
    return jnp.pad(a, [(0, 0)] * (a.ndim - 1) + [(0, pad)]) if pad else a


def gather_weights(shards):
    _, _, c = _place()
    flat = jnp.concatenate([_lane_pad(shards[n].astype(WIRE_DTYPE)).reshape(-1) for n in BIG_NAMES])
    n_flat = flat.shape[0]
    rh = _flat_rows(n_flat, 32) // 2
    flat = jnp.pad(flat, (0, 2 * rh * FLAT_W - n_flat)).reshape(2, rh, FLAT_W)
    mine = lax.dynamic_index_in_dim(flat, c, axis=0, keepdims=False)
    got = _chip_exchange(mine, True, "gather_chips")
    other = _sibling_exchange(got, "gather_sibling")
    halves = jnp.stack([_by_core(got, other, c), _by_core(other, got, c)], axis=1)
    allflat = halves.reshape(N_CHIPS, 2 * rh * FLAT_W)
    out, off = {}, 0
    for name, axis in BIG:
        shp = shards[name].shape
        padded = shp[:-1] + (_lane_padded(shp[-1]),)
        size = math.prod(padded)
        parts = allflat[:, off:off + size].reshape((N_CHIPS,) + padded)[..., :shp[-1]]
        out[name] = [jnp.concatenate([parts[p, l] for p in range(N_CHIPS)], axis=axis - 1) for l in range(shp[0])]
        off += size
    return out


def scatter_gradients(grads, shards):
    _, _, c = _place()
    cols = []
    for name, axis in SCATTERED:
        parts = [jnp.split(g, N_CHIPS, axis=axis - 1) for g in grads[name]]
        cols.append(jnp.stack([_lane_pad(jnp.stack([per_layer[p] for per_layer in parts])).reshape(-1) for p in range(N_CHIPS)]))
    flat = jnp.concatenate(cols, axis=1)
    n_flat = flat.shape[1]
    rh = _flat_rows(n_flat, 16) // 2
    flat = jnp.pad(flat, ((0, 0), (0, 2 * rh * FLAT_W - n_flat))).reshape(N_CHIPS, 2, rh, FLAT_W)
    theirs = _sibling_exchange_half(flat, "scatter_sibling")
    pair = _add_own_half(flat, theirs, c, "scatter_pair_sum")
    got = _chip_exchange(pair, False, "scatter_chips")
    mine = _sum_stack(got, "scatter_chip_sum")
    other = _sibling_exchange(mine.reshape(1, rh, FLAT_W), "scatter_halves").reshape(rh, FLAT_W)
    full = jnp.stack([_by_core(mine, other, c), _by_core(other, mine, c)]).reshape(-1)
    out, off = {}, 0
    for name, _ in SCATTERED:
        shp = shards[name].shape
        padded = shp[:-1] + (_lane_padded(shp[-1]),)
        size = math.prod(padded)
        out[name] = full[off:off + size].reshape(padded)[..., :shp[-1]]
        off += size
    return out


def _rope_tables(n_ctx, n_lat, n):
    t = jnp.arange(n_lat, dtype=jnp.int32)
    zero = jnp.zeros((n_ctx,), jnp.int32)
    row = jnp.concatenate([zero, t // GRID_W]).astype(F32)
    col = jnp.concatenate([zero, t % GRID_W]).astype(F32)
    half = n // 2
    inv = ROPE_BASE ** (-jnp.arange(0, half, 2, dtype=F32) / half)
    ang_r, ang_c = row[:, None, None] * inv, col[:, None, None] * inv
    return (jnp.cos(ang_r), jnp.sin(ang_r)), (jnp.cos(ang_c), jnp.sin(ang_c))


def _rot(x, cs):
    cos, sin = cs
    h = x.shape[-1] // 2
    x1, x2 = x[..., :h], x[..., h:]
    return jnp.concatenate([x1 * cos - x2 * sin, x1 * sin + x2 * cos], axis=-1)


def _axial_rope(x, tables):
    h = x.shape[-1] // 2
    return jnp.concatenate([_rot(x[..., :h], tables[0]), _rot(x[..., h:], tables[1])], axis=-1)


def _cmul(ar, ai, br, bi):
    return ar * br - ai * bi, ar * bi + ai * br


def _ssm_discretize(lam_re, lam_im, log_dt, b_re, b_im):
    dt = jnp.exp(log_dt)[:, None]
    mag = jnp.exp(lam_re * dt)
    a_re, a_im = mag * jnp.cos(lam_im * dt), mag * jnp.sin(lam_im * dt)
    den = lam_re * lam_re + lam_im * lam_im
    w_re = ((a_re - 1) * lam_re + a_im * lam_im) / den
    w_im = (a_im * lam_re - (a_re - 1) * lam_im) / den
    bb_re, bb_im = _cmul(w_re[..., None], w_im[..., None], b_re, b_im)
    return a_re, a_im, bb_re, bb_im


def _ssm_layouts(a_re, a_im, bb_re, bb_im, c_re, c_im):
    J, g8, P, Mg = SSM_CHUNKS, SSM_CHUNK_GROUPS, SSM_STATE, SSM_GROUP
    eye = jnp.eye(g8, dtype=F32)
    a = jnp.stack([a_re.reshape(J, g8 * P), a_im.reshape(J, g8 * P)], axis=1).reshape(1, J * 2 * g8 * P)
    bb = jnp.stack([bb_re, bb_im]).reshape(2, J, g8, P, Mg)
    w_drive = jnp.einsum('rjgpm,gh->jgmrhp', bb, eye).reshape(J * g8 * Mg, 2 * g8 * P)
    cc = jnp.stack([c_re, -c_im]).reshape(2, J, g8, Mg, P)
    w_read = jnp.einsum('rjgmp,gh->jrhpgm', cc, eye).reshape(J * 2 * g8 * P, g8 * Mg)
    return a, w_drive, w_read


def _w_in_layout(d_model):
    widths = (("cq", MLA_Q_RANK), ("ckv", MLA_KV_RANK), ("kr", MLA_ROPE), ("u", SSM_WIDTH), ("gq", GQA_HEADS * GQA_HEAD_DIM),
              ("gk", GQA_KV_HEADS * GQA_HEAD_DIM), ("gv", GQA_KV_HEADS * GQA_HEAD_DIM), ("gates", N_BRANCH * d_model))
    out, src, dst = [], 0, 0
    for name, w in widths:
        out.append((name, src, dst, w))
        src += w
        dst += -(-w // LANES) * LANES
    return out, src, dst


def _pad_w_in(w, d_model):
    lay, _, _ = _w_in_layout(d_model)
    parts = []
    for _, src, _, wd in lay:
        seg = w[..., src:src + wd]
        pad = -(-wd // LANES) * LANES - wd
        parts.append(jnp.pad(seg, [(0, 0)] * (w.ndim - 1) + [(0, pad)]) if pad else seg)
    return jnp.concatenate(parts, axis=-1)


def _unpad_w_in(w, d_model):
    lay, _, _ = _w_in_layout(d_model)
    return jnp.concatenate([w[..., dst:dst + wd] for _, _, dst, wd in lay], axis=-1)


@functools.partial(jax.custom_vjp, nondiff_argnums=(1,))
def split_cols(proj, bounds):
    return tuple(proj[:, s:s + w] for s, w in bounds[0])


def _split_cols_fwd(proj, bounds):
    return split_cols(proj, bounds), None


def _split_cols_bwd(bounds, _, cts):
    segments, total = bounds
    rows, pieces, pos = cts[0].shape[0], [], 0
    for (s, w), ct in zip(segments, cts):
        if s > pos:
            pieces.append(jnp.zeros((rows, s - pos), ct.dtype))
        pieces.append(ct)
        pos = s + w
    if pos < total:
        pieces.append(jnp.zeros((rows, total - pos), cts[0].dtype))
    return (jnp.concatenate(pieces, axis=1),)


split_cols.defvjp(_split_cols_fwd, _split_cols_bwd)


def _layer(hall, lw, lz, sp, cs8, n_ctx, ropes):
    M, D = hall.shape
    C = n_ctx
    rope32, rope64 = ropes

    def lin(x, name):
        return linear(x, lw[name], lz[name], name)

    mod_all = linear_x(cs8, lw["ada_w"], "ada_w")[0:2] + sp["ada_b"][None, :] + lz["ada_tap"]
    mod = [mod_all[:, i * D:(i + 1) * D] for i in range(N_MOD)]

    def ffn(h, tag, norm_g, sh, sc, gate):
        hn = norm_mod(h, norm_g[None, :], sh, sc, C, tag + "_norm")
        act = swiglu_act(lin(hn, tag + "_w13"), tag + "_act")
        return gated_residual(h, lin(act, tag + "_w2"), gate, 0.5, C, tag + "_res")

    hall = ffn(hall, "ffn1", sp["norm_ffn1"], mod[0], mod[1], mod[2])

    xm = norm_mod(hall, sp["norm_mix"][None, :], mod[3], mod[4], C, "mix_norm")
    proj = lin(xm, "w_in")
    lay, _, total = _w_in_layout(D)
    seg = dict(zip([name for name, _, _, _ in lay], split_cols(proj, (tuple((dst, wd) for _, _, dst, wd in lay), total))))

    q = lin(rmsnorm(seg["cq"], sp["mla_q_norm"][None, :], "mla_q_norm"), "mla_w_uq").reshape(M, MLA_HEADS, MLA_NOPE + MLA_ROPE)
    q = jnp.concatenate([q[..., :MLA_NOPE], _axial_rope(q[..., MLA_NOPE:], rope32)], axis=-1)
    kv = lin(rmsnorm(seg["ckv"], sp["mla_kv_norm"][None, :], "mla_kv_norm"), "mla_w_ukv").reshape(M, MLA_HEADS, MLA_NOPE + MLA_V)
    kr = _axial_rope(seg["kr"][:, None, :], rope32)
    k = jnp.concatenate([kv[..., :MLA_NOPE], jnp.broadcast_to(kr, (M, MLA_HEADS, MLA_ROPE))], axis=-1)
    qh, kh, vh = (t.transpose(1, 0, 2) for t in (q, k, kv[..., MLA_NOPE:]))
    qh = qh * ((MLA_NOPE + MLA_ROPE) ** -0.5 * LOG2E)
    o_lat = flash2(qh[:, C:], kh[:, C:], vh[:, C:], kh[:, :C], vh[:, :C], "mla_lat")
    o_ctx = flash1(qh[:, :C], kh[:, :C], vh[:, :C], "mla_ctx")
    o = jnp.concatenate([o_ctx, o_lat], axis=1).transpose(1, 0, 2).reshape(M, MLA_HEADS * MLA_V)
    mla = lin(o, "mla_w_o")

    u = seg["u"]
    y = u * sp["ssm_d"][None, :]
    for direction in range(2):
        a_re, a_im, bb_re, bb_im = _ssm_discretize(sp["ssm_lambda_re"][direction], sp["ssm_lambda_im"][direction],
                                                   sp["ssm_log_dt"][direction], sp["ssm_b_re"][direction],
                                                   sp["ssm_b_im"][direction])
        a, w_drive, w_read = _ssm_layouts(a_re, a_im, bb_re, bb_im, sp["ssm_c_re"][direction], sp["ssm_c_im"][direction])
        drive = bd_linear(u, w_drive, SSM_CHUNKS, "ssm_drive%d" % direction)
        states = diag_scan(a, drive, direction == 1, C, "ssm_scan%d" % direction)
        y = y + bd_linear(states, w_read, SSM_CHUNKS, "ssm_read%d" % direction)
    zz = lin(jax.nn.gelu(y), "ssm_w_glu")

    gq = _axial_rope(seg["gq"].reshape(M, GQA_HEADS, GQA_HEAD_DIM), rope64)
    gk = _axial_rope(seg["gk"].reshape(M, GQA_KV_HEADS, GQA_HEAD_DIM), rope64)
    gv = seg["gv"].reshape(M, GQA_KV_HEADS, GQA_HEAD_DIM)
    gqa = lin(gqa_attention(gq, gk, gv, sp["gqa_sink"], C, "gqa"), "gqa_w_o")

    mixed = gated_mix(seg["gates"], mla, zz, gqa, "mix_gate")
    hall = gated_residual(hall, lin(mixed, "w_out"), mod[5], 1.0, C, "mix_res")

    return ffn(hall, "ffn2", sp["norm_ffn2"], mod[6], mod[7], mod[8])


PER_LAYER_SMALL = tuple(n for n in SMALL if n not in ("c_ctx", "final_norm"))


def _loss_fn(diff, x, c, ctx, target, whole):
    zeros, small, x = diff
    T, D = x.shape
    C = ctx.shape[0]
    ropes = (_rope_tables(C, T, MLA_ROPE), _rope_tables(C, T, GQA_HEAD_DIM))
    cs8 = jnp.pad(_cond_rows(small["c_ctx"], c), ((0, 6), (0, 0)))
    hall = jnp.concatenate([ctx, x], axis=0)

    for layer in range(len(zeros["ada_tap"])):
        hall = _layer(hall, {n: whole[n][layer] for n in BIG_NAMES}, {n: zeros[n][layer] for n in zeros},
                      {n: small[n][layer] for n in PER_LAYER_SMALL}, cs8, C, ropes)
    return loss_head(hall[C:], small["final_norm"][None, :], target, "loss_head")


def _cond_rows(c_ctx, c):
    return jax.nn.silu(jnp.stack([c_ctx, c]))


def kernel(*args):
    given = dict(zip(ARG_NAMES + ['loss_target'] + ['m_' + n for n in WEIGHTS] + ['v_' + n for n in WEIGHTS], args))
    x, c, ctx, target = given['x'][0], given['c'][0], given['ctx'][0], given['loss_target'][0]
    px, py, _ = _place()
    D = x.shape[-1]
    depth = given['ada_w'].shape[0]
    shards = {n: given[n] for n in BIG_NAMES}
    small = {n: given[n] for n in SMALL}

    whole = gather_weights(shards)
    whole["w_in"] = [_pad_w_in(w, D) for w in whole["w_in"]]
    zeros = {n: [jnp.zeros(w.shape, F32) for w in whole[n]] for n in SCATTERED_NAMES}
    zeros["ada_tap"] = [jnp.zeros((2, N_MOD * D), F32) for _ in range(depth)]

    loss, (gz, gsmall, gx) = jax.value_and_grad(_loss_fn)((zeros, small, x), x, c, ctx, target, whole)
    loss = lax.psum(loss, ("x", "y", "c"))
    taps = jnp.stack(gz.pop("ada_tap"))
    gz["w_in"] = [_unpad_w_in(g, D) for g in gz["w_in"]]
    gbig = scatter_gradients(gz, shards)

    extra = [taps.reshape(-1), _cond_rows(small["c_ctx"], c).reshape(-1)]
    n_small = sum(math.prod(given[n].shape) for n in SMALL)
    n_extra = sum(e.shape[0] for e in extra)

    def flat_small(d, tail=None):
        v = jnp.concatenate([d[n].reshape(-1) for n in SMALL] + (tail or [jnp.zeros((n_extra,), F32)]))
        rows = _flat_rows(v.shape[0], 8)
        return jnp.pad(v, (0, rows * FLAT_W - v.shape[0])).reshape(rows, FLAT_W)

    gathered = _all_gather(flat_small(gsmall, extra), "small_gather")
    outs = _adamw_call(flat_small(small), gathered, flat_small({n: given['m_' + n] for n in SMALL}),
                       flat_small({n: given['v_' + n] for n in SMALL}), "adamw_small")
    res, off = {}, 0
    for name in SMALL:
        shp = given[name].shape
        size = math.prod(shp)
        res[name] = [o.reshape(-1)[off:off + size].reshape(shp) for o in outs]
        off += size

    tails = gathered.reshape(8, -1)[:, n_small:n_small + n_extra]
    all_taps = tails[:, :taps.size].reshape(8, depth, 2, N_MOD * D)
    all_cs = tails[:, taps.size:].reshape(8 * 2, D)
    n_cols = given['ada_w'].shape[2]
    mine = lax.dynamic_slice_in_dim(all_taps, (2 * px + py) * n_cols, n_cols, axis=3)
    gbig["ada_w"] = jnp.stack([_mm_tn(all_cs, mine[:, layer].reshape(8 * 2, n_cols), "ada_w_dw") for layer in range(depth)])

    for name in BIG_NAMES:
        shp = given[name].shape
        two_d = (shp[0] * shp[1], shp[2])
        outs = _adamw_call(given[name].reshape(two_d), gbig[name].reshape((1,) + two_d), given['m_' + name].reshape(two_d),
                           given['v_' + name].reshape(two_d), "adamw_" + name)
        res[name] = [o.reshape(shp) for o in outs]

    return (loss, gx[None], *[res[n][0] for n in WEIGHTS], *[res[n][1] for n in WEIGHTS],
            *[res[n][2] for n in WEIGHTS], *[res[n][3] for n in WEIGHTS])
```

```python
import functools
import math

import jax
import jax.numpy as jnp
from jax import lax
from jax.experimental import pallas as pl
from jax.experimental.pallas import tpu as pltpu

F32 = jnp.float32
MXU_DTYPE = jnp.bfloat16
WIRE_DTYPE = jnp.bfloat16

GRID_W = 64
MLA_HEADS, MLA_NOPE, MLA_ROPE, MLA_V = 8, 64, 32, 64
MLA_Q_RANK, MLA_KV_RANK = 384, 256
SSM_WIDTH, SSM_GROUP, SSM_STATE = 512, 16, 64
SSM_GROUPS = SSM_WIDTH // SSM_GROUP
SSM_CHUNK_GROUPS = 8
SSM_CHUNKS = SSM_GROUPS // SSM_CHUNK_GROUPS
SSM_CW = SSM_CHUNK_GROUPS * SSM_STATE
SCAN_SUB = 32
GQA_HEADS, GQA_KV_HEADS, GQA_HEAD_DIM = 8, 2, 64
WINDOW, BLOCK = 128, 128
N_BRANCH, N_MOD = 3, 9
ROPE_BASE = 10000.0
EPS = 1e-6
NEG_INF = -1e30
LOG2E, LN2 = math.log2(math.e), math.log(2.0)
LANES = 128
FLAT_W = 1024

ADAM_LR, ADAM_B1, ADAM_B2, ADAM_EPS, ADAM_WD, ADAM_STEP = 0.001, 0.9, 0.999, 1e-08, 0.01, 10

VMEM_LIMIT = 48 * 1024 * 1024

ARG_NAMES = ['x', 'c', 'ctx', 'c_ctx', 'ada_w', 'ada_b', 'norm_ffn1', 'norm_mix', 'norm_ffn2', 'ffn1_w13', 'ffn1_w2', 'ffn2_w13', 'ffn2_w2', 'w_in', 'mla_q_norm', 'mla_kv_norm', 'mla_w_uq', 'mla_w_ukv', 'mla_w_o', 'ssm_lambda_re', 'ssm_lambda_im', 'ssm_log_dt', 'ssm_b_re', 'ssm_b_im', 'ssm_c_re', 'ssm_c_im', 'ssm_d', 'ssm_w_glu', 'gqa_sink', 'gqa_w_o', 'w_out', 'final_norm']
WEIGHTS = ARG_NAMES[3:]
BIG = (('ada_w', 2), ('ffn1_w13', 2), ('ffn1_w2', 1), ('ffn2_w13', 2), ('ffn2_w2', 1), ('w_in', 2), ('mla_w_uq', 2),
       ('mla_w_ukv', 2), ('mla_w_o', 2), ('ssm_w_glu', 2), ('gqa_w_o', 2), ('w_out', 1))
BIG_NAMES = tuple(n for n, _ in BIG)
SCATTERED = tuple((n, a) for n, a in BIG if n != 'ada_w')
SCATTERED_NAMES = tuple(n for n, _ in SCATTERED)
SMALL = tuple(n for n in WEIGHTS if n not in BIG_NAMES)
N_CHIPS = 4


def _pick(n, prefs):
    for p in prefs:
        if n % p == 0:
            return p
    return n


def _params(sem=None):
    return pltpu.CompilerParams(dimension_semantics=sem, vmem_limit_bytes=VMEM_LIMIT)


def _mm_call(a, b, *, grid, a_spec, b_spec, o_spec, o_shape, acc_shape, ta, tb, name, out_dtype=F32):
    nk = grid[2]
    dn = (((0 if ta else 1,), (1 if tb else 0,)), ((), ()))

    def body(a_ref, b_ref, o_ref, acc_ref):
        k = pl.program_id(2)

        @pl.when(k == 0)
        def _():
            acc_ref[...] = jnp.zeros_like(acc_ref)

        acc_ref[...] += lax.dot_general(a_ref[...].astype(MXU_DTYPE), b_ref[...].astype(MXU_DTYPE), dn,
                                        preferred_element_type=F32)

        @pl.when(k == nk - 1)
        def _():
            o_ref[...] = acc_ref[...].astype(o_ref.dtype)

    return pl.pallas_call(
        body, name=name, grid=grid, in_specs=[a_spec, b_spec], out_specs=o_spec,
        out_shape=jax.ShapeDtypeStruct(o_shape, out_dtype), scratch_shapes=[pltpu.VMEM(acc_shape, F32)],
        compiler_params=_params(("parallel", "parallel", "arbitrary")))(a, b)


_ROWS = (768, 512, 256, 128, 64, 32, 16, 8)
_WIDE = (1408, 1024, 512, 256, 128)
MAX_WHOLE = 2816


def _feat(n):
    return n if n <= _WIDE[0] else _pick(n, _WIDE)


def _mm_nn(x, w, name):
    M, K = x.shape
    N = w.shape[1]
    tm, tn = _pick(M, _ROWS), _pick(N, (512, 256, 128))
    tk = K if K <= MAX_WHOLE else _pick(K, (512, 256, 128))
    return _mm_call(x, w, grid=(M // tm, N // tn, K // tk),
                    a_spec=pl.BlockSpec((tm, tk), lambda i, j, k: (i, k)),
                    b_spec=pl.BlockSpec((tk, tn), lambda i, j, k: (k, j)),
                    o_spec=pl.BlockSpec((tm, tn), lambda i, j, k: (i, j)),
                    o_shape=(M, N), acc_shape=(tm, tn), ta=False, tb=False, name=name)


def _mm_nt(dy, w, name):
    M, N = dy.shape
    K = w.shape[0]
    tm, tn, tk = _pick(M, _ROWS), _feat(K), _feat(N)
    return _mm_call(dy, w, grid=(M // tm, K // tn, N // tk),
                    a_spec=pl.BlockSpec((tm, tk), lambda i, j, k: (i, k)),
                    b_spec=pl.BlockSpec((tn, tk), lambda i, j, k: (j, k)),
                    o_spec=pl.BlockSpec((tm, tn), lambda i, j, k: (i, j)),
                    o_shape=(M, K), acc_shape=(tm, tn), ta=False, tb=True, name=name)


def _mm_tn(x, dy, name):
    M, K = x.shape
    N = dy.shape[1]
    tm, tn, tk = _feat(K), _feat(N), _pick(M, (256, 128, 64, 32, 16, 8))
    return _mm_call(x, dy, grid=(K // tm, N // tn, M // tk),
                    a_spec=pl.BlockSpec((tk, tm), lambda i, j, k: (k, i)),
                    b_spec=pl.BlockSpec((tk, tn), lambda i, j, k: (k, j)),
                    o_spec=pl.BlockSpec((tm, tn), lambda i, j, k: (i, j)),
                    o_shape=(K, N), acc_shape=(tm, tn), ta=True, tb=False, name=name)


@functools.partial(jax.custom_vjp, nondiff_argnums=(3,))
def linear(x, w, wz, name):
    return _mm_nn(x, w, name)


def _linear_fwd(x, w, wz, name):
    return _mm_nn(x, w, name), (x, w)


def _linear_bwd(name, res, dy):
    x, w = res
    return _mm_nt(dy, w, name + "_dx"), jnp.zeros_like(w), _mm_tn(x, dy, name + "_dw")


linear.defvjp(_linear_fwd, _linear_bwd)


@functools.partial(jax.custom_vjp, nondiff_argnums=(2,))
def linear_x(x, w, name):
    return _mm_nn(x, w, name)


def _linear_x_fwd(x, w, name):
    return _mm_nn(x, w, name), (w,)


def _linear_x_bwd(name, res, dy):
    return _mm_nt(dy, res[0], name + "_dx"), jnp.zeros_like(res[0])


linear_x.defvjp(_linear_x_fwd, _linear_x_bwd)


_BD_ROWS = (256, 128, 64, 32, 16, 8)


def _bd_call(a, b, nblk, kind, name):
    M = a.shape[0]
    tm = _pick(M, _BD_ROWS)
    if kind == "tn":
        aj, bj = a.shape[1] // nblk, b.shape[1] // nblk
        o_shape, o_spec = (nblk * aj, bj), pl.BlockSpec((nblk * aj, bj), lambda i: (0, 0))
        b_spec = pl.BlockSpec((tm, b.shape[1]), lambda i: (i, 0))
    else:
        aj = a.shape[1] // nblk
        wj = b.shape[0] // nblk
        oj = b.shape[1] if kind == "nn" else wj
        o_shape, o_spec = (M, nblk * oj), pl.BlockSpec((tm, nblk * oj), lambda i: (i, 0))
        b_spec = pl.BlockSpec(b.shape, lambda i: (0, 0))

    def body(a_ref, b_ref, o_ref):
        if kind == "tn":
            @pl.when(pl.program_id(0) == 0)
            def _():
                o_ref[...] = jnp.zeros_like(o_ref)

        for j in range(nblk):
            if kind == "nn":
                o_ref[:, j * oj:(j + 1) * oj] = _dot(a_ref[:, j * aj:(j + 1) * aj], b_ref[j * wj:(j + 1) * wj, :], _DN_NN)
            elif kind == "nt":
                o_ref[:, j * oj:(j + 1) * oj] = _dot(a_ref[:, j * aj:(j + 1) * aj], b_ref[j * wj:(j + 1) * wj, :], _DN_NT)
            else:
                o_ref[j * aj:(j + 1) * aj, :] += _dot(a_ref[:, j * aj:(j + 1) * aj], b_ref[:, j * bj:(j + 1) * bj], _DN_TN)

    return pl.pallas_call(body, name=name, grid=(M // tm,), in_specs=[pl.BlockSpec((tm, a.shape[1]), lambda i: (i, 0)), b_spec],
                          out_specs=o_spec, out_shape=jax.ShapeDtypeStruct(o_shape, F32),
                          compiler_params=_params(("arbitrary",) if kind == "tn" else ("parallel",)))(a, b)


def _bd_nn(x, w, nblk, name):
    return _bd_call(x, w, nblk, "nn", name)


def _bd_nt(dy, w, nblk, name):
    return _bd_call(dy, w, nblk, "nt", name)


def _bd_tn(x, dy, nblk, name):
    return _bd_call(x, dy, nblk, "tn", name)


@functools.partial(jax.custom_vjp, nondiff_argnums=(2, 3))
def bd_linear(x, w, nblk, name):
    return _bd_nn(x, w, nblk, name)


def _bd_fwd(x, w, nblk, name):
    return _bd_nn(x, w, nblk, name), (x, w)


def _bd_bwd(nblk, name, res, dy):
    x, w = res
    return _bd_nt(dy, w, nblk, name + "_dx"), _bd_tn(x, dy, nblk, name + "_dw")


bd_linear.defvjp(_bd_fwd, _bd_bwd)


def _row_tile(n_ctx, n_all):
    return _pick(math.gcd(n_ctx, n_all), (256, 128, 64, 32, 16, 8))


def _by_group(ref, is_ctx):
    return jnp.where(is_ctx, ref[0:1, :], ref[1:2, :])


def _acc_by_group(ref, is_ctx, part):
    ref[0:1, :] += jnp.where(is_ctx, part, 0.0)
    ref[1:2, :] += jnp.where(is_ctx, 0.0, part)


def _norm_fwd_call(x, g, shift, scale, n_ctx, name):
    M, D = x.shape
    has_mod = shift is not None
    tm = _row_tile(n_ctx, M) if has_mod else _pick(M, (256, 128, 64, 32, 16, 8))
    nct = n_ctx // tm

    def body(*refs):
        if has_mod:
            x_ref, g_ref, sh_ref, sc_ref, o_ref = refs
        else:
            x_ref, g_ref, o_ref = refs
        xv = x_ref[...]
        r = lax.rsqrt(jnp.mean(xv * xv, axis=-1, keepdims=True) + EPS)
        y = xv * r * g_ref[...]
        if has_mod:
            is_ctx = pl.program_id(0) < nct
            y = y * (1.0 + _by_group(sc_ref, is_ctx)) + _by_group(sh_ref, is_ctx)
        o_ref[...] = y

    row = pl.BlockSpec((tm, D), lambda i: (i, 0))
    vec = pl.BlockSpec((1, D), lambda i: (0, 0))
    two = pl.BlockSpec((2, D), lambda i: (0, 0))
    args = (x, g) + ((shift, scale) if has_mod else ())
    return pl.pallas_call(body, name=name, grid=(M // tm,), in_specs=[row, vec] + ([two, two] if has_mod else []),
                          out_specs=row, out_shape=jax.ShapeDtypeStruct((M, D), F32),
                          compiler_params=_params(("parallel",)))(*args)


def _norm_bwd_call(x, g, shift, scale, dy, n_ctx, name):
    M, D = x.shape
    has_mod = shift is not None
    tm = _row_tile(n_ctx, M) if has_mod else _pick(M, (256, 128, 64, 32, 16, 8))
    nct = n_ctx // tm

    def body(*refs):
        if has_mod:
            x_ref, g_ref, sc_ref, dy_ref, dx_ref, dg_ref, dsh_ref, dsc_ref = refs
        else:
            x_ref, g_ref, dy_ref, dx_ref, dg_ref = refs
        i = pl.program_id(0)

        @pl.when(i == 0)
        def _():
            dg_ref[...] = jnp.zeros_like(dg_ref)
            if has_mod:
                dsh_ref[...] = jnp.zeros_like(dsh_ref)
                dsc_ref[...] = jnp.zeros_like(dsc_ref)

        xv, gv, dyv = x_ref[...], g_ref[...], dy_ref[...]
        r = lax.rsqrt(jnp.mean(xv * xv, axis=-1, keepdims=True) + EPS)
        xhat = xv * r
        if has_mod:
            is_ctx = i < nct
            dy0 = dyv * (1.0 + _by_group(sc_ref, is_ctx))
            _acc_by_group(dsc_ref, is_ctx, jnp.sum(dyv * xhat * gv, axis=0, keepdims=True))
            _acc_by_group(dsh_ref, is_ctx, jnp.sum(dyv, axis=0, keepdims=True))
        else:
            dy0 = dyv
        dg_ref[...] += jnp.sum(dy0 * xhat, axis=0, keepdims=True)
        dxhat = dy0 * gv
        dx_ref[...] = r * (dxhat - xhat * jnp.mean(dxhat * xhat, axis=-1, keepdims=True))

    row = pl.BlockSpec((tm, D), lambda i: (i, 0))
    vec = pl.BlockSpec((1, D), lambda i: (0, 0))
    two = pl.BlockSpec((2, D), lambda i: (0, 0))
    if has_mod:
        args, in_specs = (x, g, scale, dy), [row, vec, two, row]
        out_specs = [row, vec, two, two]
        out_shape = [jax.ShapeDtypeStruct((M, D), F32), jax.ShapeDtypeStruct((1, D), F32),
                     jax.ShapeDtypeStruct((2, D), F32), jax.ShapeDtypeStruct((2, D), F32)]
    else:
        args, in_specs = (x, g, dy), [row, vec, row]
        out_specs = [row, vec]
        out_shape = [jax.ShapeDtypeStruct((M, D), F32), jax.ShapeDtypeStruct((1, D), F32)]
    return pl.pallas_call(body, name=name, grid=(M // tm,), in_specs=in_specs, out_specs=out_specs, out_shape=out_shape,
                          compiler_params=_params(("arbitrary",)))(*args)


@functools.partial(jax.custom_vjp, nondiff_argnums=(4, 5))
def norm_mod(x, g, shift, scale, n_ctx, name):
    return _norm_fwd_call(x, g, shift, scale, n_ctx, name)


def _norm_mod_fwd(x, g, shift, scale, n_ctx, name):
    return _norm_fwd_call(x, g, shift, scale, n_ctx, name), (x, g, shift, scale)


def _norm_mod_bwd(n_ctx, name, res, dy):
    x, g, shift, scale = res
    dx, dg, dsh, dsc = _norm_bwd_call(x, g, shift, scale, dy, n_ctx, name + "_bwd")
    return dx, dg, dsh, dsc


norm_mod.defvjp(_norm_mod_fwd, _norm_mod_bwd)


@functools.partial(jax.custom_vjp, nondiff_argnums=(2,))
def rmsnorm(x, g, name):
    return _norm_fwd_call(x, g, None, None, 0, name)


def _rmsnorm_fwd(x, g, name):
    return _norm_fwd_call(x, g, None, None, 0, name), (x, g)


def _rmsnorm_bwd(name, res, dy):
    x, g = res
    dx, dg = _norm_bwd_call(x, g, None, None, dy, 0, name + "_bwd")
    return dx, dg


rmsnorm.defvjp(_rmsnorm_fwd, _rmsnorm_bwd)


def _gres_fwd_call(h, o, gate, coef, n_ctx, name):
    M, D = h.shape
    tm = _row_tile(n_ctx, M)
    nct = n_ctx // tm

    def body(h_ref, o_ref, g_ref, out_ref):
        is_ctx = pl.program_id(0) < nct
        out_ref[...] = h_ref[...] + coef * _by_group(g_ref, is_ctx) * o_ref[...]

    row = pl.BlockSpec((tm, D), lambda i: (i, 0))
    two = pl.BlockSpec((2, D), lambda i: (0, 0))
    return pl.pallas_call(body, name=name, grid=(M // tm,), in_specs=[row, row, two], out_specs=row,
                          out_shape=jax.ShapeDtypeStruct((M, D), F32), compiler_params=_params(("parallel",)))(h, o, gate)


def _gres_bwd_call(o, gate, d, coef, n_ctx, name):
    M, D = o.shape
    tm = _row_tile(n_ctx, M)
    nct = n_ctx // tm

    def body(o_ref, g_ref, d_ref, do_ref, dg_ref):
        i = pl.program_id(0)
        is_ctx = i < nct

        @pl.when(i == 0)
        def _():
            dg_ref[...] = jnp.zeros_like(dg_ref)

        dv = d_ref[...]
        do_ref[...] = coef * _by_group(g_ref, is_ctx) * dv
        _acc_by_group(dg_ref, is_ctx, coef * jnp.sum(dv * o_ref[...], axis=0, keepdims=True))

    row = pl.BlockSpec((tm, D), lambda i: (i, 0))
    two = pl.BlockSpec((2, D), lambda i: (0, 0))
    return pl.pallas_call(body, name=name, grid=(M // tm,), in_specs=[row, two, row], out_specs=[row, two],
                          out_shape=[jax.ShapeDtypeStruct((M, D), F32), jax.ShapeDtypeStruct((2, D), F32)],
                          compiler_params=_params(("arbitrary",)))(o, gate, d)


@functools.partial(jax.custom_vjp, nondiff_argnums=(3, 4, 5))
def gated_residual(h, o, gate, coef, n_ctx, name):
    return _gres_fwd_call(h, o, gate, coef, n_ctx, name)


def _gres_fwd(h, o, gate, coef, n_ctx, name):
    return _gres_fwd_call(h, o, gate, coef, n_ctx, name), (o, gate)


def _gres_bwd(coef, n_ctx, name, res, d):
    o, gate = res
    do, dg = _gres_bwd_call(o, gate, d, coef, n_ctx, name + "_bwd")
    return d, do, dg


gated_residual.defvjp(_gres_fwd, _gres_bwd)


def _swiglu_fwd_call(ab, name):
    M, F2 = ab.shape
    Fh = F2 // 2
    tm, tn = _pick(M, (128, 64, 32, 16, 8)), Fh
    nf = Fh // tn

    def body(a_ref, b_ref, o_ref):
        a = a_ref[...]
        o_ref[...] = a * jax.nn.sigmoid(a) * b_ref[...]

    return pl.pallas_call(body, name=name, grid=(M // tm, nf),
                          in_specs=[pl.BlockSpec((tm, tn), lambda i, j: (i, j)), pl.BlockSpec((tm, tn), lambda i, j: (i, j + nf))],
                          out_specs=pl.BlockSpec((tm, tn), lambda i, j: (i, j)),
                          out_shape=jax.ShapeDtypeStruct((M, Fh), F32), compiler_params=_params(("parallel", "parallel")))(ab, ab)


def _swiglu_bwd_call(ab, dact, name):
    M, F2 = ab.shape
    Fh = F2 // 2
    tm = _pick(M, (128, 64, 32, 16, 8))

    def body(ab_ref, d_ref, o_ref):
        a, b, d = ab_ref[:, 0:Fh], ab_ref[:, Fh:F2], d_ref[...]
        sig = jax.nn.sigmoid(a)
        o_ref[:, 0:Fh] = d * b * sig * (1.0 + a * (1.0 - sig))
        o_ref[:, Fh:F2] = d * a * sig

    return pl.pallas_call(body, name=name, grid=(M // tm,),
                          in_specs=[pl.BlockSpec((tm, F2), lambda i: (i, 0)), pl.BlockSpec((tm, Fh), lambda i: (i, 0))],
                          out_specs=pl.BlockSpec((tm, F2), lambda i: (i, 0)),
                          out_shape=jax.ShapeDtypeStruct((M, F2), F32), compiler_params=_params(("parallel",)))(ab, dact)


@functools.partial(jax.custom_vjp, nondiff_argnums=(1,))
def swiglu_act(ab, name):
    return _swiglu_fwd_call(ab, name)


def _swiglu_fwd(ab, name):
    return _swiglu_fwd_call(ab, name), (ab,)


def _swiglu_bwd(name, res, d):
    return (_swiglu_bwd_call(res[0], d, name + "_bwd"),)


swiglu_act.defvjp(_swiglu_fwd, _swiglu_bwd)


def _mix_call(gl, mla, zz, gqa, d, name):
    M, D = mla.shape
    tm = _pick(M, (128, 64, 32, 16, 8))
    bwd = d is not None

    def body(*refs):
        gl_ref, mla_ref, zz_ref, gqa_ref = refs[:4]
        g0, g1, g2 = (jax.nn.sigmoid(gl_ref[:, i * D:(i + 1) * D]) for i in range(3))
        za, sb = zz_ref[:, 0:D], jax.nn.sigmoid(zz_ref[:, D:2 * D])
        ssm = za * sb
        if not bwd:
            refs[4][...] = g0 * mla_ref[...] + g1 * ssm + g2 * gqa_ref[...]
            return
        d_ref, dgl_ref, dmla_ref, dzz_ref, dgqa_ref = refs[4:]
        dv = d_ref[...]
        dmla_ref[...] = g0 * dv
        dgqa_ref[...] = g2 * dv
        dssm = g1 * dv
        dzz_ref[:, 0:D] = dssm * sb
        dzz_ref[:, D:2 * D] = dssm * ssm * (1.0 - sb)
        dgl_ref[:, 0:D] = dv * mla_ref[...] * g0 * (1.0 - g0)
        dgl_ref[:, D:2 * D] = dv * ssm * g1 * (1.0 - g1)
        dgl_ref[:, 2 * D:3 * D] = dv * gqa_ref[...] * g2 * (1.0 - g2)

    def rows(w):
        return pl.BlockSpec((tm, w), lambda i: (i, 0))

    def sds(w):
        return jax.ShapeDtypeStruct((M, w), F32)

    in_specs, args = [rows(3 * D), rows(D), rows(2 * D), rows(D)], (gl, mla, zz, gqa)
    if bwd:
        return pl.pallas_call(body, name=name, grid=(M // tm,), in_specs=in_specs + [rows(D)],
                              out_specs=[rows(3 * D), rows(D), rows(2 * D), rows(D)],
                              out_shape=[sds(3 * D), sds(D), sds(2 * D), sds(D)],
                              compiler_params=_params(("parallel",)))(*args, d)
    return pl.pallas_call(body, name=name, grid=(M // tm,), in_specs=in_specs, out_specs=rows(D), out_shape=sds(D),
                          compiler_params=_params(("parallel",)))(*args)


@functools.partial(jax.custom_vjp, nondiff_argnums=(4,))
def gated_mix(gl, mla, zz, gqa, name):
    return _mix_call(gl, mla, zz, gqa, None, name)


def _gated_mix_fwd(gl, mla, zz, gqa, name):
    return _mix_call(gl, mla, zz, gqa, None, name), (gl, mla, zz, gqa)


def _gated_mix_bwd(name, res, d):
    return tuple(_mix_call(*res, d, name + "_bwd"))


gated_mix.defvjp(_gated_mix_fwd, _gated_mix_bwd)


_DN_NT = (((1,), (1,)), ((), ()))
_DN_TN = (((0,), (0,)), ((), ()))
_DN_NN = (((1,), (0,)), ((), ()))


def _dot(a, b, dn):
    return lax.dot_general(a.astype(MXU_DTYPE), b.astype(MXU_DTYPE), dn, preferred_element_type=F32)


_TQ = (1024, 512, 256, 128, 64, 32, 16, 8)


def _flash_fwd_call(q, k1, v1, k2, v2, name):
    H, Tq, dk = q.shape
    T1, dv = k1.shape[1], v1.shape[2]
    has2 = k2 is not None
    tq, tk = _pick(Tq, _TQ), _pick(T1, _TQ)
    off = 1 if has2 else 0
    nkv = T1 // tk + off
    C = k2.shape[1] if has2 else 0
    rows = max(tk, C)

    def body(*refs):
        if has2:
            q_ref, k1_ref, v1_ref, k2_ref, v2_ref, o_ref, lse_ref, m_s, acc_s, va_s = refs
        else:
            q_ref, k1_ref, v1_ref, o_ref, lse_ref, m_s, acc_s, va_s = refs
        j = pl.program_id(2)

        @pl.when(j == 0)
        def _():
            m_s[...] = jnp.full_like(m_s, NEG_INF)
            acc_s[...] = jnp.zeros_like(acc_s)
            va_s[:, dv:2 * dv] = jnp.ones((rows, dv), MXU_DTYPE)

        def step(k, v, n):
            va_s[0:n, 0:dv] = v.astype(MXU_DTYPE)
            s = _dot(q_ref[0], k, _DN_NT)
            m_prev = m_s[...]
            m_new = jnp.maximum(m_prev, jnp.max(s, axis=-1, keepdims=True))
            p = jnp.exp2(s - m_new)
            acc_s[...] = jnp.exp2(m_prev - m_new) * acc_s[...] + _dot(p, va_s[0:n, :], _DN_NN)
            m_s[...] = m_new

        if has2:
            @pl.when(j == 0)
            def _():
                step(k2_ref[0], v2_ref[0], C)

            @pl.when(j > 0)
            def _():
                step(k1_ref[0], v1_ref[0], tk)
        else:
            step(k1_ref[0], v1_ref[0], tk)

        @pl.when(j == nkv - 1)
        def _():
            l = acc_s[:, dv:dv + 1]
            o_ref[0] = acc_s[:, 0:dv] / l
            lse_ref[0] = m_s[...] + jnp.log2(l)

    qs = pl.BlockSpec((1, tq, dk), lambda h, i, j: (h, i, 0))
    k1s = pl.BlockSpec((1, tk, dk), lambda h, i, j: (h, jnp.maximum(j - off, 0), 0))
    v1s = pl.BlockSpec((1, tk, dv), lambda h, i, j: (h, jnp.maximum(j - off, 0), 0))
    in_specs, args = [qs, k1s, v1s], [q, k1, v1]
    if has2:
        in_specs += [pl.BlockSpec((1, C, dk), lambda h, i, j: (h, 0, 0)), pl.BlockSpec((1, C, dv), lambda h, i, j: (h, 0, 0))]
        args += [k2, v2]
    return pl.pallas_call(
        body, name=name, grid=(H, Tq // tq, nkv), in_specs=in_specs,
        out_specs=[pl.BlockSpec((1, tq, dv), lambda h, i, j: (h, i, 0)), pl.BlockSpec((1, tq, 1), lambda h, i, j: (h, i, 0))],
        out_shape=[jax.ShapeDtypeStruct((H, Tq, dv), F32), jax.ShapeDtypeStruct((H, Tq, 1), F32)],
        scratch_shapes=[pltpu.VMEM((tq, 1), F32), pltpu.VMEM((tq, 2 * dv), F32), pltpu.VMEM((rows, 2 * dv), MXU_DTYPE)],
        compiler_params=_params(("parallel", "parallel", "arbitrary")))(*args)


def _flash_bwd_call(q, k1, v1, k2, v2, o, lse, do, name):
    H, Tq, dk = q.shape
    T1, dv = k1.shape[1], v1.shape[2]
    has2 = k2 is not None
    tq, tk = _pick(Tq, _TQ), _pick(T1, _TQ)
    off = 1 if has2 else 0
    nkv, nq = T1 // tk + off, Tq // tq
    C = k2.shape[1] if has2 else 0
    rows = max(tk, C)

    def body(*refs):
        if has2:
            (q_ref, k1_ref, v1_ref, k2_ref, v2_ref, o_ref, lse_ref, do_ref,
             dq_ref, dk1_ref, dv1_ref, dk2_ref, dv2_ref, dk_s, dv_s) = refs
        else:
            q_ref, k1_ref, v1_ref, o_ref, lse_ref, do_ref, dq_ref, dk1_ref, dv1_ref, dk_s, dv_s = refs
        j, i = pl.program_id(1), pl.program_id(2)

        @pl.when((j == 0) & (i == 0))
        def _():
            dq_ref[...] = jnp.zeros_like(dq_ref)

        @pl.when(i == 0)
        def _():
            dk_s[...] = jnp.zeros_like(dk_s)
            dv_s[...] = jnp.zeros_like(dv_s)

        def step(k, v, n):
            qb, dob = q_ref[0], do_ref[0]
            p = jnp.exp2(_dot(qb, k, _DN_NT) - lse_ref[0])
            dv_s[0:n, :] += _dot(p, dob, _DN_TN)
            dol = dob * LN2
            ds = p * (_dot(dol, v, _DN_NT) - jnp.sum(dol * o_ref[0], axis=-1, keepdims=True))
            dk_s[0:n, :] += _dot(ds, qb, _DN_TN)
            r0 = pl.multiple_of(i * tq, tq)
            dq_ref[0, pl.ds(r0, tq), :] += _dot(ds, k, _DN_NN)

        if has2:
            @pl.when(j == 0)
            def _():
                step(k2_ref[0], v2_ref[0], C)

            @pl.when(j > 0)
            def _():
                step(k1_ref[0], v1_ref[0], tk)

            @pl.when((i == nq - 1) & (j == 0))
            def _():
                dk2_ref[0] = dk_s[0:C, :]
                dv2_ref[0] = dv_s[0:C, :]

            @pl.when((i == nq - 1) & (j > 0))
            def _():
                dk1_ref[0] = dk_s[0:tk, :]
                dv1_ref[0] = dv_s[0:tk, :]
        else:
            step(k1_ref[0], v1_ref[0], tk)

            @pl.when(i == nq - 1)
            def _():
                dk1_ref[0] = dk_s[...]
                dv1_ref[0] = dv_s[...]

    qs = pl.BlockSpec((1, tq, dk), lambda h, j, i: (h, i, 0))
    os_ = pl.BlockSpec((1, tq, dv), lambda h, j, i: (h, i, 0))
    ls = pl.BlockSpec((1, tq, 1), lambda h, j, i: (h, i, 0))
    k1s = pl.BlockSpec((1, tk, dk), lambda h, j, i: (h, jnp.maximum(j - off, 0), 0))
    v1s = pl.BlockSpec((1, tk, dv), lambda h, j, i: (h, jnp.maximum(j - off, 0), 0))
    in_specs, args = [qs, k1s, v1s], [q, k1, v1]
    out_specs = [pl.BlockSpec((1, Tq, dk), lambda h, j, i: (h, 0, 0)), k1s, v1s]
    out_shape = [jax.ShapeDtypeStruct((H, Tq, dk), F32), jax.ShapeDtypeStruct((H, T1, dk), F32),
                 jax.ShapeDtypeStruct((H, T1, dv), F32)]
    if has2:
        k2s = pl.BlockSpec((1, C, dk), lambda h, j, i: (h, 0, 0))
        v2s = pl.BlockSpec((1, C, dv), lambda h, j, i: (h, 0, 0))
        in_specs += [k2s, v2s]
        args += [k2, v2]
        out_specs += [k2s, v2s]
        out_shape += [jax.ShapeDtypeStruct((H, C, dk), F32), jax.ShapeDtypeStruct((H, C, dv), F32)]
    in_specs += [os_, ls, os_]
    args += [o, lse, do]
    return pl.pallas_call(
        body, name=name, grid=(H, nkv, nq), in_specs=in_specs, out_specs=out_specs, out_shape=out_shape,
        scratch_shapes=[pltpu.VMEM((rows, dk), F32), pltpu.VMEM((rows, dv), F32)],
        compiler_params=_params(("parallel", "arbitrary", "arbitrary")))(*args)


@functools.partial(jax.custom_vjp, nondiff_argnums=(5,))
def flash2(q, k1, v1, k2, v2, name):
    return _flash_fwd_call(q, k1, v1, k2, v2, name)[0]


def _flash2_fwd(q, k1, v1, k2, v2, name):
    o, lse = _flash_fwd_call(q, k1, v1, k2, v2, name)
    return o, (q, k1, v1, k2, v2, o, lse)


def _flash2_bwd(name, res, do):
    q, k1, v1, k2, v2, o, lse = res
    return tuple(_flash_bwd_call(q, k1, v1, k2, v2, o, lse, do, name + "_bwd"))


flash2.defvjp(_flash2_fwd, _flash2_bwd)


@functools.partial(jax.custom_vjp, nondiff_argnums=(3,))
def flash1(q, k, v, name):
    return _flash_fwd_call(q, k, v, None, None, name)[0]


def _flash1_fwd(q, k, v, name):
    o, lse = _flash_fwd_call(q, k, v, None, None, name)
    return o, (q, k, v, o, lse)


def _flash1_bwd(name, res, do):
    q, k, v, o, lse = res
    return tuple(_flash_bwd_call(q, k, v, None, None, o, lse, do, name + "_bwd"))


flash1.defvjp(_flash1_fwd, _flash1_bwd)


def _gqa_call(q, k, v, sink_rows, n_ctx, res, name):
    KV, G, M, d = q.shape
    B, C = BLOCK, n_ctx
    assert WINDOW == BLOCK
    nt, nct, R = M // B, n_ctx // B, G * B
    scale = d ** -0.5
    bwd = res is not None

    def body(*refs):
        q_ref, kc_ref, vc_ref, k0, k1, k2, v0, v1, v2, sink_ref, o_ref, lse_ref = refs[:12]
        i = pl.program_id(1)
        qb = q_ref[0].reshape(R, d)
        a = lax.broadcasted_iota(jnp.int32, (R, B), 0) % B
        b = lax.broadcasted_iota(jnp.int32, (R, B), 1)
        kb, vb, masks = (k0[0], k1[0], k2[0]), (v0[0], v1[0], v2[0]), []
        for r in (-1, 0, 1):
            in_range = (i >= nct) & (i + r >= nct) & (i + r <= nt - 1)
            masks.append(in_range & (a <= b) if r == -1 else (in_range & (a >= b) if r == 1 else in_range & (a >= 0)))
        kc, vc, sink = kc_ref[0], vc_ref[0], sink_ref[0]
        s_c = _dot(qb, kc, _DN_NT) * scale
        s_b = [jnp.where(masks[t], _dot(qb, kb[t], _DN_NT) * scale, NEG_INF) for t in range(3)]
        if not bwd:
            m = jnp.maximum(sink, jnp.max(s_c, axis=-1, keepdims=True))
            for s in s_b:
                m = jnp.maximum(m, jnp.max(s, axis=-1, keepdims=True))
            e_c, e_b = jnp.exp(s_c - m), [jnp.exp(s - m) for s in s_b]
            den = jnp.exp(sink - m) + jnp.sum(e_c, axis=-1, keepdims=True)
            for e in e_b:
                den = den + jnp.sum(e, axis=-1, keepdims=True)
            inv = 1.0 / den
            o = _dot(e_c * inv, vc, _DN_NN)
            for t in range(3):
                o = o + _dot(e_b[t] * inv, vb[t], _DN_NN)
            o_ref[0] = o.reshape(G, B, d)
            lse_ref[0] = (m + jnp.log(den)).reshape(G, B, 1)
            return
        do_ref, dq_ref, dkc_ref, dvc_ref, dkb_ref, dvb_ref, dsink_ref = refs[12:]

        @pl.when(i == 0)
        def _():
            dkc_ref[...] = jnp.zeros_like(dkc_ref)
            dvc_ref[...] = jnp.zeros_like(dvc_ref)
            dsink_ref[...] = jnp.zeros_like(dsink_ref)

        lse, dob = lse_ref[0].reshape(R, 1), do_ref[0].reshape(R, d)
        delta = jnp.sum(dob * o_ref[0].reshape(R, d), axis=-1, keepdims=True)
        p_c = jnp.exp(s_c - lse)
        ds_c = p_c * (_dot(dob, vc, _DN_NT) - delta) * scale
        dq = _dot(ds_c, kc, _DN_NN)
        dkc_ref[0] += _dot(ds_c, qb, _DN_TN)
        dvc_ref[0] += _dot(p_c, dob, _DN_TN)
        for t in range(3):
            p = jnp.exp(s_b[t] - lse)
            ds = p * (_dot(dob, vb[t], _DN_NT) - delta) * scale
            dq = dq + _dot(ds, kb[t], _DN_NN)
            dkb_ref[0, 0, t] = _dot(ds, qb, _DN_TN)
            dvb_ref[0, 0, t] = _dot(p, dob, _DN_TN)
        dq_ref[0] = dq.reshape(G, B, d)
        dsink_ref[0] -= jnp.exp(sink - lse) * delta

    def band(r):
        return lambda h, i: (h, jnp.clip(i + r, nct, nt - 1), 0)

    qs = pl.BlockSpec((1, G, B, d), lambda h, i: (h, 0, i, 0))
    ls = pl.BlockSpec((1, G, B, 1), lambda h, i: (h, 0, i, 0))
    cs = pl.BlockSpec((1, C, d), lambda h, i: (h, 0, 0))
    ss = pl.BlockSpec((1, R, 1), lambda h, i: (h, 0, 0))
    bs = [pl.BlockSpec((1, B, d), band(r)) for r in (-1, 0, 1)]
    in_specs = [qs, cs, cs] + bs + bs + [ss]
    args = [q, k, v, k, k, k, v, v, v, sink_rows]
    if not bwd:
        return pl.pallas_call(body, name=name, grid=(KV, nt), in_specs=in_specs, out_specs=[qs, ls],
                              out_shape=[jax.ShapeDtypeStruct((KV, G, M, d), F32), jax.ShapeDtypeStruct((KV, G, M, 1), F32)],
                              compiler_params=_params(("parallel", "parallel")))(*args)
    o, lse, do = res
    part = pl.BlockSpec((1, 1, 3, B, d), lambda h, i: (h, i, 0, 0, 0))
    part_shape = jax.ShapeDtypeStruct((KV, nt, 3, B, d), F32)
    return pl.pallas_call(body, name=name, grid=(KV, nt), in_specs=in_specs + [qs, ls, qs],
                          out_specs=[qs, cs, cs, part, part, ss],
                          out_shape=[jax.ShapeDtypeStruct((KV, G, M, d), F32), jax.ShapeDtypeStruct((KV, C, d), F32),
                                     jax.ShapeDtypeStruct((KV, C, d), F32), part_shape, part_shape,
                                     jax.ShapeDtypeStruct((KV, R, 1), F32)],
                          compiler_params=_params(("parallel", "arbitrary")))(*args, o, lse, do)


@functools.partial(jax.custom_vjp, nondiff_argnums=(4, 5))
def gqa_core(q, k, v, sink_rows, n_ctx, name):
    return _gqa_call(q, k, v, sink_rows, n_ctx, None, name)[0]


def _gqa_core_fwd(q, k, v, sink_rows, n_ctx, name):
    o, lse = _gqa_call(q, k, v, sink_rows, n_ctx, None, name)
    return o, (q, k, v, sink_rows, o, lse)


def _gqa_core_bwd(n_ctx, name, res, do):
    q, k, v, sink_rows, o, lse = res
    dq, dkc, dvc, dkb, dvb, dsink = _gqa_call(q, k, v, sink_rows, n_ctx, (o, lse, do), name + "_bwd")

    def keys(ctx_part, band_part):
        zero = jnp.zeros_like(band_part[:, :1, 0])
        blocks = (jnp.concatenate([band_part[:, 1:, 0], zero], axis=1) + band_part[:, :, 1]
                  + jnp.concatenate([zero, band_part[:, :-1, 2]], axis=1))
        rows = blocks.reshape(k.shape)
        return jnp.concatenate([rows[:, :n_ctx] + ctx_part, rows[:, n_ctx:]], axis=1)

    return dq, keys(dkc, dkb), keys(dvc, dvb), dsink


gqa_core.defvjp(_gqa_core_fwd, _gqa_core_bwd)


def gqa_attention(gq, gk, gv, sink, n_ctx, name):
    M, H, d = gq.shape
    G = H // GQA_KV_HEADS
    q4 = gq.reshape(M, GQA_KV_HEADS, G, d).transpose(1, 2, 0, 3)
    sink_rows = jnp.repeat(sink.reshape(GQA_KV_HEADS, G), BLOCK, axis=1)[..., None]
    o = gqa_core(q4, gk.transpose(1, 0, 2), gv.transpose(1, 0, 2), sink_rows, n_ctx, name)
    return o.transpose(2, 0, 1, 3).reshape(M, H * d)


def _scan_call(a, x, s, *, rev, adj, n_ctx, name):
    M, W = x.shape
    cw = SSM_CW
    J = W // (2 * cw)
    L = _row_tile(n_ctx, M)
    nt, nc = M // L, n_ctx // L
    asc = rev == adj
    sub = min(SCAN_SUB, L)
    nsub = L // sub
    n_steps = int(math.log2(sub))
    assert 1 << n_steps == sub

    def tile(t):
        if not rev:
            return nt - 1 - t if adj else t
        if not adj:
            return jnp.where(t < nc, nc - 1 - t, nt - 1 - (t - nc))
        return jnp.where(t < nt - nc, nc + t, t - (nt - nc))

    def body(*refs):
        if adj:
            a_ref, x_ref, s_ref, o_ref, da_ref, car_ref = refs
        else:
            a_ref, x_ref, o_ref, car_ref = refs
        t = pl.program_id(1)

        @pl.when(t == 0)
        def _():
            car_ref[...] = jnp.zeros_like(car_ref)
            if adj:
                da_ref[...] = jnp.zeros_like(da_ref)

        ar, ai = a_ref[:, 0:cw], a_ref[:, cw:2 * cw]
        powers, pr, pi = [], ar, ai
        for _ in range(n_steps):
            powers.append((pr, pi))
            pr, pi = pr * pr - pi * pi, 2.0 * pr * pi
        row = lax.broadcasted_iota(jnp.int32, (sub, cw), 0)
        first, last = (0, sub - 1) if asc else (sub - 1, 0)

        def scan_rows(i, carry):
            cr, ci = carry[0], carry[1]
            r0 = pl.multiple_of((i if asc else nsub - 1 - i) * sub, sub)
            xr, xi = x_ref[pl.ds(r0, sub), 0:cw], x_ref[pl.ds(r0, sub), cw:2 * cw]
            xr = xr + jnp.where(row == first, ar * cr - ai * ci, 0.0)
            xi = xi + jnp.where(row == first, ar * ci + ai * cr, 0.0)
            k = 1
            for pr, pi in powers:
                if asc:
                    sr, si, keep = pltpu.roll(xr, k, 0), pltpu.roll(xi, k, 0), row >= k
                else:
                    sr, si, keep = pltpu.roll(xr, sub - k, 0), pltpu.roll(xi, sub - k, 0), row < sub - k
                sr, si = jnp.where(keep, sr, 0.0), jnp.where(keep, si, 0.0)
                xr, xi = xr + pr * sr - pi * si, xi + pr * si + pi * sr
                k *= 2
            o_ref[pl.ds(r0, sub), 0:cw] = xr
            o_ref[pl.ds(r0, sub), cw:2 * cw] = xi
            out = (jnp.sum(jnp.where(row == last, xr, 0.0), axis=0, keepdims=True),
                   jnp.sum(jnp.where(row == last, xi, 0.0), axis=0, keepdims=True))
            if adj:
                if asc:
                    gr, gi = pltpu.roll(xr, 1, 0), pltpu.roll(xi, 1, 0)
                else:
                    gr, gi = pltpu.roll(xr, sub - 1, 0), pltpu.roll(xi, sub - 1, 0)
                gr, gi = jnp.where(row == first, cr, gr), jnp.where(row == first, ci, gi)
                sr, si = s_ref[pl.ds(r0, sub), 0:cw], s_ref[pl.ds(r0, sub), cw:2 * cw]
                out += (carry[2] + jnp.sum(sr * gr + si * gi, axis=0, keepdims=True),
                        carry[3] + jnp.sum(sr * gi - si * gr, axis=0, keepdims=True))
            return out

        init = (car_ref[:, 0:cw], car_ref[:, cw:2 * cw])
        if adj:
            init += (jnp.zeros((1, cw), F32), jnp.zeros((1, cw), F32))
        done = lax.fori_loop(0, nsub, scan_rows, init)
        car_ref[:, 0:cw] = done[0]
        car_ref[:, cw:2 * cw] = done[1]
        if adj:
            da_ref[:, 0:cw] += done[2]
            da_ref[:, cw:2 * cw] += done[3]

    blk = pl.BlockSpec((L, 2 * cw), lambda j, t: (tile(t), j))
    vec = pl.BlockSpec((1, 2 * cw), lambda j, t: (0, j))
    if adj:
        in_specs, args = [vec, blk, blk], (a, x, s)
        out_specs = [blk, vec]
        out_shape = [jax.ShapeDtypeStruct((M, W), F32), jax.ShapeDtypeStruct((1, W), F32)]
    else:
        in_specs, args = [vec, blk], (a, x)
        out_specs = blk
        out_shape = jax.ShapeDtypeStruct((M, W), F32)
    return pl.pallas_call(body, name=name, grid=(J, nt), in_specs=in_specs, out_specs=out_specs, out_shape=out_shape,
                          scratch_shapes=[pltpu.VMEM((1, 2 * cw), F32)],
                          compiler_params=_params(("parallel", "arbitrary")))(*args)


def _conj_layout(a):
    cw = SSM_CW
    J = a.shape[1] // (2 * cw)
    a4 = a.reshape(1, J, 2, cw)
    return jnp.concatenate([a4[:, :, 0:1], -a4[:, :, 1:2]], axis=2).reshape(a.shape)


@functools.partial(jax.custom_vjp, nondiff_argnums=(2, 3, 4))
def diag_scan(a, x, rev, n_ctx, name):
    return _scan_call(a, x, None, rev=rev, adj=False, n_ctx=n_ctx, name=name)


def _diag_scan_fwd(a, x, rev, n_ctx, name):
    s = _scan_call(a, x, None, rev=rev, adj=False, n_ctx=n_ctx, name=name)
    return s, (a, s)


def _diag_scan_bwd(rev, n_ctx, name, res, ds):
    a, s = res
    g, da = _scan_call(_conj_layout(a), ds, s, rev=rev, adj=True, n_ctx=n_ctx, name=name + "_adj")
    return da, g


diag_scan.defvjp(_diag_scan_fwd, _diag_scan_bwd)


def _loss_call(h, g, target, name):
    M, D = h.shape
    tm = _pick(M, (256, 128, 64, 32, 16, 8))

    def body(h_ref, g_ref, t_ref, loss_ref, dh_ref, dg_ref):
        i = pl.program_id(0)

        @pl.when(i == 0)
        def _():
            loss_ref[...] = jnp.zeros_like(loss_ref)
            dg_ref[...] = jnp.zeros_like(dg_ref)

        xv, gv = h_ref[...], g_ref[...]
        r = lax.rsqrt(jnp.mean(xv * xv, axis=-1, keepdims=True) + EPS)
        xhat = xv * r
        err = xhat * gv - t_ref[...]
        loss_ref[...] += 0.5 * jnp.sum(jnp.mean(err * err, axis=-1, keepdims=True), axis=0, keepdims=True)
        dy = err * (1.0 / D)
        dg_ref[...] += jnp.sum(dy * xhat, axis=0, keepdims=True)
        dxhat = dy * gv
        dh_ref[...] = r * (dxhat - xhat * jnp.mean(dxhat * xhat, axis=-1, keepdims=True))

    row = pl.BlockSpec((tm, D), lambda i: (i, 0))
    vec = pl.BlockSpec((1, D), lambda i: (0, 0))
    one = pl.BlockSpec((1, 1), lambda i: (0, 0))
    return pl.pallas_call(body, name=name, grid=(M // tm,), in_specs=[row, vec, row], out_specs=[one, row, vec],
                          out_shape=[jax.ShapeDtypeStruct((1, 1), F32), jax.ShapeDtypeStruct((M, D), F32),
                                     jax.ShapeDtypeStruct((1, D), F32)],
                          compiler_params=_params(("arbitrary",)))(h, g, target)


@functools.partial(jax.custom_vjp, nondiff_argnums=(3,))
def loss_head(h, g, target, name):
    return _loss_call(h, g, target, name)[0][0, 0]


def _loss_head_fwd(h, g, target, name):
    loss, dh, dg = _loss_call(h, g, target, name)
    return loss[0, 0], (dh, dg, target)


def _loss_head_bwd(name, res, ct):
    dh, dg, target = res
    return ct * dh, ct * dg, jnp.zeros_like(target)


loss_head.defvjp(_loss_head_fwd, _loss_head_bwd)


def _adamw_call(w, gstack, m, v, name):
    R, Cn = w.shape
    n = gstack.shape[0]
    tr = _pick(R, (64, 32, 16, 8))

    def body(w_ref, g_ref, m_ref, v_ref, go_ref, d_ref, mo_ref, vo_ref):
        g = g_ref[0]
        for s in range(1, n):
            g = g + g_ref[s]
        mn = ADAM_B1 * m_ref[...] + (1.0 - ADAM_B1) * g
        vn = ADAM_B2 * v_ref[...] + (1.0 - ADAM_B2) * (g * g)
        m_hat = mn / (1.0 - ADAM_B1 ** ADAM_STEP)
        v_hat = vn / (1.0 - ADAM_B2 ** ADAM_STEP)
        go_ref[...] = g
        d_ref[...] = -ADAM_LR * (m_hat / (jnp.sqrt(v_hat) + ADAM_EPS) + ADAM_WD * w_ref[...])
        mo_ref[...] = mn
        vo_ref[...] = vn

    blk = pl.BlockSpec((tr, Cn), lambda i: (i, 0))
    gblk = pl.BlockSpec((n, tr, Cn), lambda i: (0, i, 0))
    sds = jax.ShapeDtypeStruct((R, Cn), F32)
    return pl.pallas_call(body, name=name, grid=(R // tr,), in_specs=[blk, gblk, blk, blk], out_specs=[blk] * 4,
                          out_shape=[sds] * 4, compiler_params=_params(("parallel",)))(w, gstack, m, v)


MESH = pl.DeviceIdType.MESH
ANY = pl.BlockSpec(memory_space=pl.ANY)


def _place():
    return lax.axis_index("x"), lax.axis_index("y"), lax.axis_index("c")


def _sibling_exchange(v, name):
    n = v.shape[0]

    def body(v_ref, o_ref, send_sems, recv_sems):
        x, y, c = _place()
        copies = [pltpu.make_async_remote_copy(src_ref=v_ref.at[k], dst_ref=o_ref.at[k], send_sem=send_sems.at[k],
                                               recv_sem=recv_sems.at[k], device_id=(x, y, 1 - c), device_id_type=MESH)
                  for k in range(n)]
        for cp in copies:
            cp.start()
        for cp in copies:
            cp.wait()

    return pl.pallas_call(body, name=name, in_specs=[ANY], out_specs=ANY, out_shape=jax.ShapeDtypeStruct(v.shape, v.dtype),
                          scratch_shapes=[pltpu.SemaphoreType.DMA((n,)), pltpu.SemaphoreType.DMA((n,))])(v)


def _sibling_exchange_half(v, name):
    n = v.shape[0]

    def body(v_ref, o_ref, send_sems, recv_sems):
        x, y, c = _place()
        copies = [pltpu.make_async_remote_copy(src_ref=v_ref.at[k, pl.ds(1 - c, 1)], dst_ref=o_ref.at[k],
                                               send_sem=send_sems.at[k], recv_sem=recv_sems.at[k],
                                               device_id=(x, y, 1 - c), device_id_type=MESH)
                  for k in range(n)]
        for cp in copies:
            cp.start()
        for cp in copies:
            cp.wait()

    return pl.pallas_call(body, name=name, in_specs=[ANY], out_specs=ANY,
                          out_shape=jax.ShapeDtypeStruct((n, 1) + v.shape[2:], v.dtype),
                          scratch_shapes=[pltpu.SemaphoreType.DMA((n,)), pltpu.SemaphoreType.DMA((n,))])(v)


def _chip_exchange(v, same, name):
    out_shape = (N_CHIPS,) + (v.shape if same else v.shape[1:])

    def body(v_ref, o_ref, send_sems, recv_sems, local_sem):
        x, y, c = _place()
        me = 2 * x + y
        own = pltpu.make_async_copy(v_ref if same else v_ref.at[me], o_ref.at[me], local_sem)
        own.start()
        copies = []
        for k, (fx, fy) in enumerate(((1, 0), (0, 1), (1, 1))):
            px, py = jnp.where(fx == 1, 1 - x, x), jnp.where(fy == 1, 1 - y, y)
            src = v_ref if same else v_ref.at[2 * px + py]
            copies.append(pltpu.make_async_remote_copy(src_ref=src, dst_ref=o_ref.at[me], send_sem=send_sems.at[k],
                                                       recv_sem=recv_sems.at[k], device_id=(px, py, c), device_id_type=MESH))
        for cp in copies:
            cp.start()
        for cp in copies:
            cp.wait()
        own.wait()

    return pl.pallas_call(body, name=name, in_specs=[ANY], out_specs=ANY, out_shape=jax.ShapeDtypeStruct(out_shape, v.dtype),
                          scratch_shapes=[pltpu.SemaphoreType.DMA((3,)), pltpu.SemaphoreType.DMA((3,)), pltpu.SemaphoreType.DMA])(v)


def _all_gather(v, name):
    def body(v_ref, o_ref, send_sems, recv_sems, local_sem):
        x, y, c = _place()
        me = 4 * x + 2 * y + c
        own = pltpu.make_async_copy(v_ref, o_ref.at[me], local_sem)
        own.start()
        copies = []
        for k in range(1, 8):
            fx, fy, fc = (k >> 2) & 1, (k >> 1) & 1, k & 1
            peer = (jnp.where(fx == 1, 1 - x, x), jnp.where(fy == 1, 1 - y, y), jnp.where(fc == 1, 1 - c, c))
            copies.append(pltpu.make_async_remote_copy(src_ref=v_ref, dst_ref=o_ref.at[me], send_sem=send_sems.at[k - 1],
                                                       recv_sem=recv_sems.at[k - 1], device_id=peer, device_id_type=MESH))
        for cp in copies:
            cp.start()
        for cp in copies:
            cp.wait()
        own.wait()

    return pl.pallas_call(body, name=name, in_specs=[ANY], out_specs=ANY,
                          out_shape=jax.ShapeDtypeStruct((8,) + v.shape, v.dtype),
                          scratch_shapes=[pltpu.SemaphoreType.DMA((7,)), pltpu.SemaphoreType.DMA((7,)), pltpu.SemaphoreType.DMA])(v)


def _add_own_half(g, r, c, name):
    n, _, R, W = g.shape
    tr = _pick(R, (256, 128, 64, 32, 16, 8))

    def body(c_ref, g_ref, r_ref, o_ref):
        o_ref[...] = g_ref[0] + r_ref[0]

    grid_spec = pltpu.PrefetchScalarGridSpec(
        num_scalar_prefetch=1, grid=(n, R // tr),
        in_specs=[pl.BlockSpec((1, 1, tr, W), lambda p, i, c_ref: (p, c_ref[0], i, 0)),
                  pl.BlockSpec((1, 1, tr, W), lambda p, i, c_ref: (p, 0, i, 0))],
        out_specs=pl.BlockSpec((1, tr, W), lambda p, i, c_ref: (p, i, 0)))
    return pl.pallas_call(body, name=name, grid_spec=grid_spec, out_shape=jax.ShapeDtypeStruct((n, R, W), F32),
                          compiler_params=_params(("parallel", "parallel")))(c.reshape(1).astype(jnp.int32), g, r)


def _sum_stack(v, name):
    n, R, W = v.shape
    tr = _pick(R, (256, 128, 64, 32, 16, 8))

    def body(v_ref, o_ref):
        acc = v_ref[0]
        for s in range(1, n):
            acc = acc + v_ref[s]
        o_ref[...] = acc

    return pl.pallas_call(body, name=name, grid=(R // tr,), in_specs=[pl.BlockSpec((n, tr, W), lambda i: (0, i, 0))],
                          out_specs=pl.BlockSpec((tr, W), lambda i: (i, 0)), out_shape=jax.ShapeDtypeStruct((R, W), F32),
                          compiler_params=_params(("parallel",)))(v)


def _flat_rows(n, mult):
    rows = -(-n // FLAT_W)
    return -(-rows // mult) * mult


def _by_core(a, b, c):
    return jnp.where(c == 0, a, b)


def _lane_padded(n):
    return -(-n // LANES) * LANES


def _lane_pad(a):
    pad = _lane_padded(a.shape[-1]) - a.shape[-1]
    return jnp.pad(a, [(0, 0)] * (a.ndim - 1) + [(0, pad)]) if pad else a


def gather_weights(shards):
    _, _, c = _place()
    flat = jnp.concatenate([_lane_pad(shards[n].astype(WIRE_DTYPE)).reshape(-1) for n in BIG_NAMES])
    n_flat = flat.shape[0]
    rh = _flat_rows(n_flat, 32) // 2
    flat = jnp.pad(flat, (0, 2 * rh * FLAT_W - n_flat)).reshape(2, rh, FLAT_W)
    mine = lax.dynamic_index_in_dim(flat, c, axis=0, keepdims=False)
    got = _chip_exchange(mine, True, "gather_chips")
    other = _sibling_exchange(got, "gather_sibling")
    halves = jnp.stack([_by_core(got, other, c), _by_core(other, got, c)], axis=1)
    allflat = halves.reshape(N_CHIPS, 2 * rh * FLAT_W)
    out, off = {}, 0
    for name, axis in BIG:
        shp = shards[name].shape
        padded = shp[:-1] + (_lane_padded(shp[-1]),)
        size = math.prod(padded)
        parts = allflat[:, off:off + size].reshape((N_CHIPS,) + padded)[..., :shp[-1]]
        out[name] = [jnp.concatenate([parts[p, l] for p in range(N_CHIPS)], axis=axis - 1) for l in range(shp[0])]
        off += size
    return out


def scatter_gradients(grads, shards):
    _, _, c = _place()
    cols = []
    for name, axis in SCATTERED:
        parts = [jnp.split(g, N_CHIPS, axis=axis - 1) for g in grads[name]]
        cols.append(jnp.stack([_lane_pad(jnp.stack([per_layer[p] for per_layer in parts])).reshape(-1) for p in range(N_CHIPS)]))
    flat = jnp.concatenate(cols, axis=1)
    n_flat = flat.shape[1]
    rh = _flat_rows(n_flat, 16) // 2
    flat = jnp.pad(flat, ((0, 0), (0, 2 * rh * FLAT_W - n_flat))).reshape(N_CHIPS, 2, rh, FLAT_W)
    theirs = _sibling_exchange_half(flat, "scatter_sibling")
    pair = _add_own_half(flat, theirs, c, "scatter_pair_sum")
    got = _chip_exchange(pair, False, "scatter_chips")
    mine = _sum_stack(got, "scatter_chip_sum")
    other = _sibling_exchange(mine.reshape(1, rh, FLAT_W), "scatter_halves").reshape(rh, FLAT_W)
    full = jnp.stack([_by_core(mine, other, c), _by_core(other, mine, c)]).reshape(-1)
    out, off = {}, 0
    for name, _ in SCATTERED:
        shp = shards[name].shape
        padded = shp[:-1] + (_lane_padded(shp[-1]),)
        size = math.prod(padded)
        out[name] = full[off:off + size].reshape(padded)[..., :shp[-1]]
        off += size
    return out


def _rope_tables(n_ctx, n_lat, n):
    t = jnp.arange(n_lat, dtype=jnp.int32)
    zero = jnp.zeros((n_ctx,), jnp.int32)
    row = jnp.concatenate([zero, t // GRID_W]).astype(F32)
    col = jnp.concatenate([zero, t % GRID_W]).astype(F32)
    half = n // 2
    inv = ROPE_BASE ** (-jnp.arange(0, half, 2, dtype=F32) / half)
    ang_r, ang_c = row[:, None, None] * inv, col[:, None, None] * inv
    return (jnp.cos(ang_r), jnp.sin(ang_r)), (jnp.cos(ang_c), jnp.sin(ang_c))


def _rot(x, cs):
    cos, sin = cs
    h = x.shape[-1] // 2
    x1, x2 = x[..., :h], x[..., h:]
    return jnp.concatenate([x1 * cos - x2 * sin, x1 * sin + x2 * cos], axis=-1)


def _axial_rope(x, tables):
    h = x.shape[-1] // 2
    return jnp.concatenate([_rot(x[..., :h], tables[0]), _rot(x[..., h:], tables[1])], axis=-1)


def _cmul(ar, ai, br, bi):
    return ar * br - ai * bi, ar * bi + ai * br


def _ssm_discretize(lam_re, lam_im, log_dt, b_re, b_im):
    dt = jnp.exp(log_dt)[:, None]
    mag = jnp.exp(lam_re * dt)
    a_re, a_im = mag * jnp.cos(lam_im * dt), mag * jnp.sin(lam_im * dt)
    den = lam_re * lam_re + lam_im * lam_im
    w_re = ((a_re - 1) * lam_re + a_im * lam_im) / den
    w_im = (a_im * lam_re - (a_re - 1) * lam_im) / den
    bb_re, bb_im = _cmul(w_re[..., None], w_im[..., None], b_re, b_im)
    return a_re, a_im, bb_re, bb_im


def _ssm_layouts(a_re, a_im, bb_re, bb_im, c_re, c_im):
    J, g8, P, Mg = SSM_CHUNKS, SSM_CHUNK_GROUPS, SSM_STATE, SSM_GROUP
    eye = jnp.eye(g8, dtype=F32)
    a = jnp.stack([a_re.reshape(J, g8 * P), a_im.reshape(J, g8 * P)], axis=1).reshape(1, J * 2 * g8 * P)
    bb = jnp.stack([bb_re, bb_im]).reshape(2, J, g8, P, Mg)
    w_drive = jnp.einsum('rjgpm,gh->jgmrhp', bb, eye).reshape(J * g8 * Mg, 2 * g8 * P)
    cc = jnp.stack([c_re, -c_im]).reshape(2, J, g8, Mg, P)
    w_read = jnp.einsum('rjgmp,gh->jrhpgm', cc, eye).reshape(J * 2 * g8 * P, g8 * Mg)
    return a, w_drive, w_read


def _w_in_layout(d_model):
    widths = (("cq", MLA_Q_RANK), ("ckv", MLA_KV_RANK), ("kr", MLA_ROPE), ("u", SSM_WIDTH), ("gq", GQA_HEADS * GQA_HEAD_DIM),
              ("gk", GQA_KV_HEADS * GQA_HEAD_DIM), ("gv", GQA_KV_HEADS * GQA_HEAD_DIM), ("gates", N_BRANCH * d_model))
    out, src, dst = [], 0, 0
    for name, w in widths:
        out.append((name, src, dst, w))
        src += w
        dst += -(-w // LANES) * LANES
    return out, src, dst


def _pad_w_in(w, d_model):
    lay, _, _ = _w_in_layout(d_model)
    parts = []
    for _, src, _, wd in lay:
        seg = w[..., src:src + wd]
        pad = -(-wd // LANES) * LANES - wd
        parts.append(jnp.pad(seg, [(0, 0)] * (w.ndim - 1) + [(0, pad)]) if pad else seg)
    return jnp.concatenate(parts, axis=-1)


def _unpad_w_in(w, d_model):
    lay, _, _ = _w_in_layout(d_model)
    return jnp.concatenate([w[..., dst:dst + wd] for _, _, dst, wd in lay], axis=-1)


@functools.partial(jax.custom_vjp, nondiff_argnums=(1,))
def split_cols(proj, bounds):
    return tuple(proj[:, s:s + w] for s, w in bounds[0])


def _split_cols_fwd(proj, bounds):
    return split_cols(proj, bounds), None


def _split_cols_bwd(bounds, _, cts):
    segments, total = bounds
    rows, pieces, pos = cts[0].shape[0], [], 0
    for (s, w), ct in zip(segments, cts):
        if s > pos:
            pieces.append(jnp.zeros((rows, s - pos), ct.dtype))
        pieces.append(ct)
        pos = s + w
    if pos < total:
        pieces.append(jnp.zeros((rows, total - pos), cts[0].dtype))
    return (jnp.concatenate(pieces, axis=1),)


split_cols.defvjp(_split_cols_fwd, _split_cols_bwd)


def _layer(hall, lw, lz, sp, cs8, n_ctx, ropes):
    M, D = hall.shape
    C = n_ctx
    rope32, rope64 = ropes

    def lin(x, name):
        return linear(x, lw[name], lz[name], name)

    mod_all = linear_x(cs8, lw["ada_w"], "ada_w")[0:2] + sp["ada_b"][None, :] + lz["ada_tap"]
    mod = [mod_all[:, i * D:(i + 1) * D] for i in range(N_MOD)]

    def ffn(h, tag, norm_g, sh, sc, gate):
        hn = norm_mod(h, norm_g[None, :], sh, sc, C, tag + "_norm")
        act = swiglu_act(lin(hn, tag + "_w13"), tag + "_act")
        return gated_residual(h, lin(act, tag + "_w2"), gate, 0.5, C, tag + "_res")

    hall = ffn(hall, "ffn1", sp["norm_ffn1"], mod[0], mod[1], mod[2])

    xm = norm_mod(hall, sp["norm_mix"][None, :], mod[3], mod[4], C, "mix_norm")
    proj = lin(xm, "w_in")
    lay, _, total = _w_in_layout(D)
    seg = dict(zip([name for name, _, _, _ in lay], split_cols(proj, (tuple((dst, wd) for _, _, dst, wd in lay), total))))

    q = lin(rmsnorm(seg["cq"], sp["mla_q_norm"][None, :], "mla_q_norm"), "mla_w_uq").reshape(M, MLA_HEADS, MLA_NOPE + MLA_ROPE)
    q = jnp.concatenate([q[..., :MLA_NOPE], _axial_rope(q[..., MLA_NOPE:], rope32)], axis=-1)
    kv = lin(rmsnorm(seg["ckv"], sp["mla_kv_norm"][None, :], "mla_kv_norm"), "mla_w_ukv").reshape(M, MLA_HEADS, MLA_NOPE + MLA_V)
    kr = _axial_rope(seg["kr"][:, None, :], rope32)
    k = jnp.concatenate([kv[..., :MLA_NOPE], jnp.broadcast_to(kr, (M, MLA_HEADS, MLA_ROPE))], axis=-1)
    q = q * ((MLA_NOPE + MLA_ROPE) ** -0.5 * LOG2E)

    def heads(t, lo, hi):
        return t[lo:hi].transpose(1, 0, 2)

    v = kv[..., MLA_NOPE:]
    k_ctx, v_ctx = heads(k, 0, C), heads(v, 0, C)
    o_lat = flash2(heads(q, C, M), heads(k, C, M), heads(v, C, M), k_ctx, v_ctx, "mla_lat")
    o_ctx = flash1(heads(q, 0, C), k_ctx, v_ctx, "mla_ctx")
    o = jnp.concatenate([o_ctx, o_lat], axis=1).transpose(1, 0, 2).reshape(M, MLA_HEADS * MLA_V)
    mla = lin(o, "mla_w_o")

    u = seg["u"]
    y = u * sp["ssm_d"][None, :]
    for direction in range(2):
        a_re, a_im, bb_re, bb_im = _ssm_discretize(sp["ssm_lambda_re"][direction], sp["ssm_lambda_im"][direction],
                                                   sp["ssm_log_dt"][direction], sp["ssm_b_re"][direction],
                                                   sp["ssm_b_im"][direction])
        a, w_drive, w_read = _ssm_layouts(a_re, a_im, bb_re, bb_im, sp["ssm_c_re"][direction], sp["ssm_c_im"][direction])
        drive = bd_linear(u, w_drive, SSM_CHUNKS, "ssm_drive%d" % direction)
        states = diag_scan(a, drive, direction == 1, C, "ssm_scan%d" % direction)
        y = y + bd_linear(states, w_read, SSM_CHUNKS, "ssm_read%d" % direction)
    zz = lin(jax.nn.gelu(y), "ssm_w_glu")

    gq = _axial_rope(seg["gq"].reshape(M, GQA_HEADS, GQA_HEAD_DIM), rope64)
    gk = _axial_rope(seg["gk"].reshape(M, GQA_KV_HEADS, GQA_HEAD_DIM), rope64)
    gv = seg["gv"].reshape(M, GQA_KV_HEADS, GQA_HEAD_DIM)
    gqa = lin(gqa_attention(gq, gk, gv, sp["gqa_sink"], C, "gqa"), "gqa_w_o")

    mixed = gated_mix(seg["gates"], mla, zz, gqa, "mix_gate")
    hall = gated_residual(hall, lin(mixed, "w_out"), mod[5], 1.0, C, "mix_res")

    return ffn(hall, "ffn2", sp["norm_ffn2"], mod[6], mod[7], mod[8])


PER_LAYER_SMALL = tuple(n for n in SMALL if n not in ("c_ctx", "final_norm"))


def _loss_fn(diff, x, c, ctx, target, whole):
    zeros, small, x = diff
    T, D = x.shape
    C = ctx.shape[0]
    ropes = (_rope_tables(C, T, MLA_ROPE), _rope_tables(C, T, GQA_HEAD_DIM))
    cs8 = jnp.pad(_cond_rows(small["c_ctx"], c), ((0, 6), (0, 0)))
    hall = jnp.concatenate([ctx, x], axis=0)

    for layer in range(len(zeros["ada_tap"])):
        hall = _layer(hall, {n: whole[n][layer] for n in BIG_NAMES}, {n: zeros[n][layer] for n in zeros},
                      {n: small[n][layer] for n in PER_LAYER_SMALL}, cs8, C, ropes)
    return loss_head(hall[C:], small["final_norm"][None, :], target, "loss_head")


def _cond_rows(c_ctx, c):
    return jax.nn.silu(jnp.stack([c_ctx, c]))


def kernel(*args):
    given = dict(zip(ARG_NAMES + ['loss_target'] + ['m_' + n for n in WEIGHTS] + ['v_' + n for n in WEIGHTS], args))
    x, c, ctx, target = given['x'][0], given['c'][0], given['ctx'][0], given['loss_target'][0]
    px, py, _ = _place()
    D = x.shape[-1]
    depth = given['ada_w'].shape[0]
    shards = {n: given[n] for n in BIG_NAMES}
    small = {n: given[n] for n in SMALL}

    whole = gather_weights(shards)
    whole["w_in"] = [_pad_w_in(w, D) for w in whole["w_in"]]
    zeros = {n: [jnp.zeros(w.shape, F32) for w in whole[n]] for n in SCATTERED_NAMES}
    zeros["ada_tap"] = [jnp.zeros((2, N_MOD * D), F32) for _ in range(depth)]

    loss, (gz, gsmall, gx) = jax.value_and_grad(_loss_fn)((zeros, small, x), x, c, ctx, target, whole)
    loss = lax.psum(loss, ("x", "y", "c"))
    taps = jnp.stack(gz.pop("ada_tap"))
    gz["w_in"] = [_unpad_w_in(g, D) for g in gz["w_in"]]
    gbig = scatter_gradients(gz, shards)

    extra = [taps.reshape(-1), _cond_rows(small["c_ctx"], c).reshape(-1)]
    n_small = sum(math.prod(given[n].shape) for n in SMALL)
    n_extra = sum(e.shape[0] for e in extra)

    def flat_small(d, tail=None):
        v = jnp.concatenate([d[n].reshape(-1) for n in SMALL] + (tail or [jnp.zeros((n_extra,), F32)]))
        rows = _flat_rows(v.shape[0], 8)
        return jnp.pad(v, (0, rows * FLAT_W - v.shape[0])).reshape(rows, FLAT_W)

    gathered = _all_gather(flat_small(gsmall, extra), "small_gather")
    outs = _adamw_call(flat_small(small), gathered, flat_small({n: given['m_' + n] for n in SMALL}),
                       flat_small({n: given['v_' + n] for n in SMALL}), "adamw_small")
    res, off = {}, 0
    for name in SMALL:
        shp = given[name].shape
        size = math.prod(shp)
        res[name] = [o.reshape(-1)[off:off + size].reshape(shp) for o in outs]
        off += size

    tails = gathered.reshape(8, -1)[:, n_small:n_small + n_extra]
    all_taps = tails[:, :taps.size].reshape(8, depth, 2, N_MOD * D)
    all_cs = tails[:, taps.size:].reshape(8 * 2, D)
    n_cols = given['ada_w'].shape[2]
    mine = lax.dynamic_slice_in_dim(all_taps, (2 * px + py) * n_cols, n_cols, axis=3)
    gbig["ada_w"] = jnp.stack([_mm_tn(all_cs, mine[:, layer].reshape(8 * 2, n_cols), "ada_w_dw") for layer in range(depth)])

    for name in BIG_NAMES:
        shp = given[name].shape
        two_d = (shp[0] * shp[1], shp[2])
        outs = _adamw_call(given[name].reshape(two_d), gbig[name].reshape((1,) + two_d), given['m_' + name].reshape(two_d),
                           given['v_' + name].reshape(two_d), "adamw_" + name)
        res[name] = [o.reshape(shp) for o in outs]

    return (loss, gx[None], *[res[n][0] for n in WEIGHTS], *[res[n][1] for n in WEIGHTS],
            *[res[n][2] for n in WEIGHTS], *[res[n][3] for n in WEIGHTS])
```

```python
import functools
import math

import jax
import jax.numpy as jnp
from jax import lax
from jax.experimental import pallas as pl
from jax.experimental.pallas import tpu as pltpu

F32 = jnp.float32
MXU_DTYPE = jnp.bfloat16
WIRE_DTYPE = jnp.bfloat16

GRID_W = 64
MLA_HEADS, MLA_NOPE, MLA_ROPE, MLA_V = 8, 64, 32, 64
MLA_Q_RANK, MLA_KV_RANK = 384, 256
SSM_WIDTH, SSM_GROUP, SSM_STATE = 512, 16, 64
SSM_GROUPS = SSM_WIDTH // SSM_GROUP
SSM_CHUNK_GROUPS = 8
SSM_CHUNKS = SSM_GROUPS // SSM_CHUNK_GROUPS
SSM_CW = SSM_CHUNK_GROUPS * SSM_STATE
SCAN_SUB = 32
GQA_HEADS, GQA_KV_HEADS, GQA_HEAD_DIM = 8, 2, 64
WINDOW, BLOCK = 128, 128
N_BRANCH, N_MOD = 3, 9
ROPE_BASE = 10000.0
EPS = 1e-6
NEG_INF = -1e30
LOG2E, LN2 = math.log2(math.e), math.log(2.0)
LANES = 128
FLAT_W = 1024

ADAM_LR, ADAM_B1, ADAM_B2, ADAM_EPS, ADAM_WD, ADAM_STEP = 0.001, 0.9, 0.999, 1e-08, 0.01, 10

VMEM_LIMIT = 48 * 1024 * 1024

ARG_NAMES = ['x', 'c', 'ctx', 'c_ctx', 'ada_w', 'ada_b', 'norm_ffn1', 'norm_mix', 'norm_ffn2', 'ffn1_w13', 'ffn1_w2', 'ffn2_w13', 'ffn2_w2', 'w_in', 'mla_q_norm', 'mla_kv_norm', 'mla_w_uq', 'mla_w_ukv', 'mla_w_o', 'ssm_lambda_re', 'ssm_lambda_im', 'ssm_log_dt', 'ssm_b_re', 'ssm_b_im', 'ssm_c_re', 'ssm_c_im', 'ssm_d', 'ssm_w_glu', 'gqa_sink', 'gqa_w_o', 'w_out', 'final_norm']
WEIGHTS = ARG_NAMES[3:]
BIG = (('ada_w', 2), ('ffn1_w13', 2), ('ffn1_w2', 1), ('ffn2_w13', 2), ('ffn2_w2', 1), ('w_in', 2), ('mla_w_uq', 2),
       ('mla_w_ukv', 2), ('mla_w_o', 2), ('ssm_w_glu', 2), ('gqa_w_o', 2), ('w_out', 1))
BIG_NAMES = tuple(n for n, _ in BIG)
SCATTERED = tuple((n, a) for n, a in BIG if n != 'ada_w')
SCATTERED_NAMES = tuple(n for n, _ in SCATTERED)
SMALL = tuple(n for n in WEIGHTS if n not in BIG_NAMES)
N_CHIPS = 4


def _pick(n, prefs):
    for p in prefs:
        if n % p == 0:
            return p
    return n


def _params(sem=None):
    return pltpu.CompilerParams(dimension_semantics=sem, vmem_limit_bytes=VMEM_LIMIT)


def _mm_call(a, b, *, grid, a_spec, b_spec, o_spec, o_shape, acc_shape, ta, tb, name, out_dtype=F32):
    nk = grid[2]
    dn = (((0 if ta else 1,), (1 if tb else 0,)), ((), ()))

    def body(a_ref, b_ref, o_ref, acc_ref):
        k = pl.program_id(2)

        @pl.when(k == 0)
        def _():
            acc_ref[...] = jnp.zeros_like(acc_ref)

        acc_ref[...] += lax.dot_general(a_ref[...].astype(MXU_DTYPE), b_ref[...].astype(MXU_DTYPE), dn,
                                        preferred_element_type=F32)

        @pl.when(k == nk - 1)
        def _():
            o_ref[...] = acc_ref[...].astype(o_ref.dtype)

    return pl.pallas_call(
        body, name=name, grid=grid, in_specs=[a_spec, b_spec], out_specs=o_spec,
        out_shape=jax.ShapeDtypeStruct(o_shape, out_dtype), scratch_shapes=[pltpu.VMEM(acc_shape, F32)],
        compiler_params=_params(("parallel", "parallel", "arbitrary")))(a, b)


_ROWS = (768, 512, 256, 128, 64, 32, 16, 8)
_WIDE = (1408, 1024, 512, 256, 128)
MAX_WHOLE = 2816


def _feat(n):
    return n if n <= _WIDE[0] else _pick(n, _WIDE)


def _mm_nn(x, w, name):
    M, K = x.shape
    N = w.shape[1]
    tm, tn = _pick(M, _ROWS), _pick(N, (512, 256, 128))
    tk = K if K <= MAX_WHOLE else _pick(K, (512, 256, 128))
    return _mm_call(x, w, grid=(M // tm, N // tn, K // tk),
                    a_spec=pl.BlockSpec((tm, tk), lambda i, j, k: (i, k)),
                    b_spec=pl.BlockSpec((tk, tn), lambda i, j, k: (k, j)),
                    o_spec=pl.BlockSpec((tm, tn), lambda i, j, k: (i, j)),
                    o_shape=(M, N), acc_shape=(tm, tn), ta=False, tb=False, name=name)


def _mm_nt(dy, w, name):
    M, N = dy.shape
    K = w.shape[0]
    tm, tn, tk = _pick(M, _ROWS), _feat(K), _feat(N)
    return _mm_call(dy, w, grid=(M // tm, K // tn, N // tk),
                    a_spec=pl.BlockSpec((tm, tk), lambda i, j, k: (i, k)),
                    b_spec=pl.BlockSpec((tn, tk), lambda i, j, k: (j, k)),
                    o_spec=pl.BlockSpec((tm, tn), lambda i, j, k: (i, j)),
                    o_shape=(M, K), acc_shape=(tm, tn), ta=False, tb=True, name=name)


def _mm_tn(x, dy, name):
    M, K = x.shape
    N = dy.shape[1]
    tm, tn, tk = _feat(K), _feat(N), _pick(M, (256, 128, 64, 32, 16, 8))
    return _mm_call(x, dy, grid=(K // tm, N // tn, M // tk),
                    a_spec=pl.BlockSpec((tk, tm), lambda i, j, k: (k, i)),
                    b_spec=pl.BlockSpec((tk, tn), lambda i, j, k: (k, j)),
                    o_spec=pl.BlockSpec((tm, tn), lambda i, j, k: (i, j)),
                    o_shape=(K, N), acc_shape=(tm, tn), ta=True, tb=False, name=name)


@functools.partial(jax.custom_vjp, nondiff_argnums=(3,))
def linear(x, w, wz, name):
    return _mm_nn(x, w, name)


def _linear_fwd(x, w, wz, name):
    return _mm_nn(x, w, name), (x, w)


def _linear_bwd(name, res, dy):
    x, w = res
    return _mm_nt(dy, w, name + "_dx"), jnp.zeros_like(w), _mm_tn(x, dy, name + "_dw")


linear.defvjp(_linear_fwd, _linear_bwd)


@functools.partial(jax.custom_vjp, nondiff_argnums=(2,))
def linear_x(x, w, name):
    return _mm_nn(x, w, name)


def _linear_x_fwd(x, w, name):
    return _mm_nn(x, w, name), (w,)


def _linear_x_bwd(name, res, dy):
    return _mm_nt(dy, res[0], name + "_dx"), jnp.zeros_like(res[0])


linear_x.defvjp(_linear_x_fwd, _linear_x_bwd)


_BD_ROWS = (256, 128, 64, 32, 16, 8)


def _bd_call(a, b, nblk, kind, name):
    M = a.shape[0]
    tm = _pick(M, _BD_ROWS)
    if kind == "tn":
        aj, bj = a.shape[1] // nblk, b.shape[1] // nblk
        o_shape, o_spec = (nblk * aj, bj), pl.BlockSpec((nblk * aj, bj), lambda i: (0, 0))
        b_spec = pl.BlockSpec((tm, b.shape[1]), lambda i: (i, 0))
    else:
        aj = a.shape[1] // nblk
        wj = b.shape[0] // nblk
        oj = b.shape[1] if kind == "nn" else wj
        o_shape, o_spec = (M, nblk * oj), pl.BlockSpec((tm, nblk * oj), lambda i: (i, 0))
        b_spec = pl.BlockSpec(b.shape, lambda i: (0, 0))

    def body(a_ref, b_ref, o_ref):
        if kind == "tn":
            @pl.when(pl.program_id(0) == 0)
            def _():
                o_ref[...] = jnp.zeros_like(o_ref)

        for j in range(nblk):
            if kind == "nn":
                o_ref[:, j * oj:(j + 1) * oj] = _dot(a_ref[:, j * aj:(j + 1) * aj], b_ref[j * wj:(j + 1) * wj, :], _DN_NN)
            elif kind == "nt":
                o_ref[:, j * oj:(j + 1) * oj] = _dot(a_ref[:, j * aj:(j + 1) * aj], b_ref[j * wj:(j + 1) * wj, :], _DN_NT)
            else:
                o_ref[j * aj:(j + 1) * aj, :] += _dot(a_ref[:, j * aj:(j + 1) * aj], b_ref[:, j * bj:(j + 1) * bj], _DN_TN)

    return pl.pallas_call(body, name=name, grid=(M // tm,), in_specs=[pl.BlockSpec((tm, a.shape[1]), lambda i: (i, 0)), b_spec],
                          out_specs=o_spec, out_shape=jax.ShapeDtypeStruct(o_shape, F32),
                          compiler_params=_params(("arbitrary",) if kind == "tn" else ("parallel",)))(a, b)


def _bd_nn(x, w, nblk, name):
    return _bd_call(x, w, nblk, "nn", name)


def _bd_nt(dy, w, nblk, name):
    return _bd_call(dy, w, nblk, "nt", name)


def _bd_tn(x, dy, nblk, name):
    return _bd_call(x, dy, nblk, "tn", name)


@functools.partial(jax.custom_vjp, nondiff_argnums=(2, 3))
def bd_linear(x, w, nblk, name):
    return _bd_nn(x, w, nblk, name)


def _bd_fwd(x, w, nblk, name):
    return _bd_nn(x, w, nblk, name), (x, w)


def _bd_bwd(nblk, name, res, dy):
    x, w = res
    return _bd_nt(dy, w, nblk, name + "_dx"), _bd_tn(x, dy, nblk, name + "_dw")


bd_linear.defvjp(_bd_fwd, _bd_bwd)


def _row_tile(n_ctx, n_all):
    return _pick(math.gcd(n_ctx, n_all), (256, 128, 64, 32, 16, 8))


def _by_group(ref, is_ctx):
    return jnp.where(is_ctx, ref[0:1, :], ref[1:2, :])


def _acc_by_group(ref, is_ctx, part):
    ref[0:1, :] += jnp.where(is_ctx, part, 0.0)
    ref[1:2, :] += jnp.where(is_ctx, 0.0, part)


def _norm_fwd_call(x, g, shift, scale, n_ctx, name):
    M, D = x.shape
    has_mod = shift is not None
    tm = _row_tile(n_ctx, M) if has_mod else _pick(M, (256, 128, 64, 32, 16, 8))
    nct = n_ctx // tm

    def body(*refs):
        if has_mod:
            x_ref, g_ref, sh_ref, sc_ref, o_ref = refs
        else:
            x_ref, g_ref, o_ref = refs
        xv = x_ref[...]
        r = lax.rsqrt(jnp.mean(xv * xv, axis=-1, keepdims=True) + EPS)
        y = xv * r * g_ref[...]
        if has_mod:
            is_ctx = pl.program_id(0) < nct
            y = y * (1.0 + _by_group(sc_ref, is_ctx)) + _by_group(sh_ref, is_ctx)
        o_ref[...] = y

    row = pl.BlockSpec((tm, D), lambda i: (i, 0))
    vec = pl.BlockSpec((1, D), lambda i: (0, 0))
    two = pl.BlockSpec((2, D), lambda i: (0, 0))
    args = (x, g) + ((shift, scale) if has_mod else ())
    return pl.pallas_call(body, name=name, grid=(M // tm,), in_specs=[row, vec] + ([two, two] if has_mod else []),
                          out_specs=row, out_shape=jax.ShapeDtypeStruct((M, D), F32),
                          compiler_params=_params(("parallel",)))(*args)


def _norm_bwd_call(x, g, shift, scale, dy, n_ctx, name):
    M, D = x.shape
    has_mod = shift is not None
    tm = _row_tile(n_ctx, M) if has_mod else _pick(M, (256, 128, 64, 32, 16, 8))
    nct = n_ctx // tm

    def body(*refs):
        if has_mod:
            x_ref, g_ref, sc_ref, dy_ref, dx_ref, dg_ref, dsh_ref, dsc_ref = refs
        else:
            x_ref, g_ref, dy_ref, dx_ref, dg_ref = refs
        i = pl.program_id(0)

        @pl.when(i == 0)
        def _():
            dg_ref[...] = jnp.zeros_like(dg_ref)
            if has_mod:
                dsh_ref[...] = jnp.zeros_like(dsh_ref)
                dsc_ref[...] = jnp.zeros_like(dsc_ref)

        xv, gv, dyv = x_ref[...], g_ref[...], dy_ref[...]
        r = lax.rsqrt(jnp.mean(xv * xv, axis=-1, keepdims=True) + EPS)
        xhat = xv * r
        if has_mod:
            is_ctx = i < nct
            dy0 = dyv * (1.0 + _by_group(sc_ref, is_ctx))
            _acc_by_group(dsc_ref, is_ctx, jnp.sum(dyv * xhat * gv, axis=0, keepdims=True))
            _acc_by_group(dsh_ref, is_ctx, jnp.sum(dyv, axis=0, keepdims=True))
        else:
            dy0 = dyv
        dg_ref[...] += jnp.sum(dy0 * xhat, axis=0, keepdims=True)
        dxhat = dy0 * gv
        dx_ref[...] = r * (dxhat - xhat * jnp.mean(dxhat * xhat, axis=-1, keepdims=True))

    row = pl.BlockSpec((tm, D), lambda i: (i, 0))
    vec = pl.BlockSpec((1, D), lambda i: (0, 0))
    two = pl.BlockSpec((2, D), lambda i: (0, 0))
    if has_mod:
        args, in_specs = (x, g, scale, dy), [row, vec, two, row]
        out_specs = [row, vec, two, two]
        out_shape = [jax.ShapeDtypeStruct((M, D), F32), jax.ShapeDtypeStruct((1, D), F32),
                     jax.ShapeDtypeStruct((2, D), F32), jax.ShapeDtypeStruct((2, D), F32)]
    else:
        args, in_specs = (x, g, dy), [row, vec, row]
        out_specs = [row, vec]
        out_shape = [jax.ShapeDtypeStruct((M, D), F32), jax.ShapeDtypeStruct((1, D), F32)]
    return pl.pallas_call(body, name=name, grid=(M // tm,), in_specs=in_specs, out_specs=out_specs, out_shape=out_shape,
                          compiler_params=_params(("arbitrary",)))(*args)


@functools.partial(jax.custom_vjp, nondiff_argnums=(4, 5))
def norm_mod(x, g, shift, scale, n_ctx, name):
    return _norm_fwd_call(x, g, shift, scale, n_ctx, name)


def _norm_mod_fwd(x, g, shift, scale, n_ctx, name):
    return _norm_fwd_call(x, g, shift, scale, n_ctx, name), (x, g, shift, scale)


def _norm_mod_bwd(n_ctx, name, res, dy):
    x, g, shift, scale = res
    dx, dg, dsh, dsc = _norm_bwd_call(x, g, shift, scale, dy, n_ctx, name + "_bwd")
    return dx, dg, dsh, dsc


norm_mod.defvjp(_norm_mod_fwd, _norm_mod_bwd)


@functools.partial(jax.custom_vjp, nondiff_argnums=(2,))
def rmsnorm(x, g, name):
    return _norm_fwd_call(x, g, None, None, 0, name)


def _rmsnorm_fwd(x, g, name):
    return _norm_fwd_call(x, g, None, None, 0, name), (x, g)


def _rmsnorm_bwd(name, res, dy):
    x, g = res
    dx, dg = _norm_bwd_call(x, g, None, None, dy, 0, name + "_bwd")
    return dx, dg


rmsnorm.defvjp(_rmsnorm_fwd, _rmsnorm_bwd)


def _gres_fwd_call(h, o, gate, coef, n_ctx, name):
    M, D = h.shape
    tm = _row_tile(n_ctx, M)
    nct = n_ctx // tm

    def body(h_ref, o_ref, g_ref, out_ref):
        is_ctx = pl.program_id(0) < nct
        out_ref[...] = h_ref[...] + coef * _by_group(g_ref, is_ctx) * o_ref[...]

    row = pl.BlockSpec((tm, D), lambda i: (i, 0))
    two = pl.BlockSpec((2, D), lambda i: (0, 0))
    return pl.pallas_call(body, name=name, grid=(M // tm,), in_specs=[row, row, two], out_specs=row,
                          out_shape=jax.ShapeDtypeStruct((M, D), F32), compiler_params=_params(("parallel",)))(h, o, gate)


def _gres_bwd_call(o, gate, d, coef, n_ctx, name):
    M, D = o.shape
    tm = _row_tile(n_ctx, M)
    nct = n_ctx // tm

    def body(o_ref, g_ref, d_ref, do_ref, dg_ref):
        i = pl.program_id(0)
        is_ctx = i < nct

        @pl.when(i == 0)
        def _():
            dg_ref[...] = jnp.zeros_like(dg_ref)

        dv = d_ref[...]
        do_ref[...] = coef * _by_group(g_ref, is_ctx) * dv
        _acc_by_group(dg_ref, is_ctx, coef * jnp.sum(dv * o_ref[...], axis=0, keepdims=True))

    row = pl.BlockSpec((tm, D), lambda i: (i, 0))
    two = pl.BlockSpec((2, D), lambda i: (0, 0))
    return pl.pallas_call(body, name=name, grid=(M // tm,), in_specs=[row, two, row], out_specs=[row, two],
                          out_shape=[jax.ShapeDtypeStruct((M, D), F32), jax.ShapeDtypeStruct((2, D), F32)],
                          compiler_params=_params(("arbitrary",)))(o, gate, d)


@functools.partial(jax.custom_vjp, nondiff_argnums=(3, 4, 5))
def gated_residual(h, o, gate, coef, n_ctx, name):
    return _gres_fwd_call(h, o, gate, coef, n_ctx, name)


def _gres_fwd(h, o, gate, coef, n_ctx, name):
    return _gres_fwd_call(h, o, gate, coef, n_ctx, name), (o, gate)


def _gres_bwd(coef, n_ctx, name, res, d):
    o, gate = res
    do, dg = _gres_bwd_call(o, gate, d, coef, n_ctx, name + "_bwd")
    return d, do, dg


gated_residual.defvjp(_gres_fwd, _gres_bwd)


def _swiglu_fwd_call(ab, name):
    M, F2 = ab.shape
    Fh = F2 // 2
    tm, tn = _pick(M, (128, 64, 32, 16, 8)), Fh
    nf = Fh // tn

    def body(a_ref, b_ref, o_ref):
        a = a_ref[...]
        o_ref[...] = a * jax.nn.sigmoid(a) * b_ref[...]

    return pl.pallas_call(body, name=name, grid=(M // tm, nf),
                          in_specs=[pl.BlockSpec((tm, tn), lambda i, j: (i, j)), pl.BlockSpec((tm, tn), lambda i, j: (i, j + nf))],
                          out_specs=pl.BlockSpec((tm, tn), lambda i, j: (i, j)),
                          out_shape=jax.ShapeDtypeStruct((M, Fh), F32), compiler_params=_params(("parallel", "parallel")))(ab, ab)


def _swiglu_bwd_call(ab, dact, name):
    M, F2 = ab.shape
    Fh = F2 // 2
    tm = _pick(M, (128, 64, 32, 16, 8))

    def body(ab_ref, d_ref, o_ref):
        a, b, d = ab_ref[:, 0:Fh], ab_ref[:, Fh:F2], d_ref[...]
        sig = jax.nn.sigmoid(a)
        o_ref[:, 0:Fh] = d * b * sig * (1.0 + a * (1.0 - sig))
        o_ref[:, Fh:F2] = d * a * sig

    return pl.pallas_call(body, name=name, grid=(M // tm,),
                          in_specs=[pl.BlockSpec((tm, F2), lambda i: (i, 0)), pl.BlockSpec((tm, Fh), lambda i: (i, 0))],
                          out_specs=pl.BlockSpec((tm, F2), lambda i: (i, 0)),
                          out_shape=jax.ShapeDtypeStruct((M, F2), F32), compiler_params=_params(("parallel",)))(ab, dact)


@functools.partial(jax.custom_vjp, nondiff_argnums=(1,))
def swiglu_act(ab, name):
    return _swiglu_fwd_call(ab, name)


def _swiglu_fwd(ab, name):
    return _swiglu_fwd_call(ab, name), (ab,)


def _swiglu_bwd(name, res, d):
    return (_swiglu_bwd_call(res[0], d, name + "_bwd"),)


swiglu_act.defvjp(_swiglu_fwd, _swiglu_bwd)


def _mix_call(gl, mla, zz, gqa, d, name):
    M, D = mla.shape
    tm = _pick(M, (128, 64, 32, 16, 8))
    bwd = d is not None

    def body(*refs):
        gl_ref, mla_ref, zz_ref, gqa_ref = refs[:4]
        g0, g1, g2 = (jax.nn.sigmoid(gl_ref[:, i * D:(i + 1) * D]) for i in range(3))
        za, sb = zz_ref[:, 0:D], jax.nn.sigmoid(zz_ref[:, D:2 * D])
        ssm = za * sb
        if not bwd:
            refs[4][...] = g0 * mla_ref[...] + g1 * ssm + g2 * gqa_ref[...]
            return
        d_ref, dgl_ref, dmla_ref, dzz_ref, dgqa_ref = refs[4:]
        dv = d_ref[...]
        dmla_ref[...] = g0 * dv
        dgqa_ref[...] = g2 * dv
        dssm = g1 * dv
        dzz_ref[:, 0:D] = dssm * sb
        dzz_ref[:, D:2 * D] = dssm * ssm * (1.0 - sb)
        dgl_ref[:, 0:D] = dv * mla_ref[...] * g0 * (1.0 - g0)
        dgl_ref[:, D:2 * D] = dv * ssm * g1 * (1.0 - g1)
        dgl_ref[:, 2 * D:3 * D] = dv * gqa_ref[...] * g2 * (1.0 - g2)

    def rows(w):
        return pl.BlockSpec((tm, w), lambda i: (i, 0))

    def sds(w):
        return jax.ShapeDtypeStruct((M, w), F32)

    in_specs, args = [rows(3 * D), rows(D), rows(2 * D), rows(D)], (gl, mla, zz, gqa)
    if bwd:
        return pl.pallas_call(body, name=name, grid=(M // tm,), in_specs=in_specs + [rows(D)],
                              out_specs=[rows(3 * D), rows(D), rows(2 * D), rows(D)],
                              out_shape=[sds(3 * D), sds(D), sds(2 * D), sds(D)],
                              compiler_params=_params(("parallel",)))(*args, d)
    return pl.pallas_call(body, name=name, grid=(M // tm,), in_specs=in_specs, out_specs=rows(D), out_shape=sds(D),
                          compiler_params=_params(("parallel",)))(*args)


@functools.partial(jax.custom_vjp, nondiff_argnums=(4,))
def gated_mix(gl, mla, zz, gqa, name):
    return _mix_call(gl, mla, zz, gqa, None, name)


def _gated_mix_fwd(gl, mla, zz, gqa, name):
    return _mix_call(gl, mla, zz, gqa, None, name), (gl, mla, zz, gqa)


def _gated_mix_bwd(name, res, d):
    return tuple(_mix_call(*res, d, name + "_bwd"))


gated_mix.defvjp(_gated_mix_fwd, _gated_mix_bwd)


_DN_NT = (((1,), (1,)), ((), ()))
_DN_TN = (((0,), (0,)), ((), ()))
_DN_NN = (((1,), (0,)), ((), ()))


def _dot(a, b, dn):
    return lax.dot_general(a.astype(MXU_DTYPE), b.astype(MXU_DTYPE), dn, preferred_element_type=F32)


_TQ = (1024, 512, 256, 128, 64, 32, 16, 8)


def _flash_fwd_call(q, k1, v1, k2, v2, name):
    H, Tq, dk = q.shape
    T1, dv = k1.shape[1], v1.shape[2]
    has2 = k2 is not None
    tq, tk = _pick(Tq, _TQ), _pick(T1, _TQ)
    off = 1 if has2 else 0
    nkv = T1 // tk + off
    C = k2.shape[1] if has2 else 0
    rows = max(tk, C)

    def body(*refs):
        if has2:
            q_ref, k1_ref, v1_ref, k2_ref, v2_ref, o_ref, lse_ref, m_s, acc_s, va_s = refs
        else:
            q_ref, k1_ref, v1_ref, o_ref, lse_ref, m_s, acc_s, va_s = refs
        j = pl.program_id(2)

        @pl.when(j == 0)
        def _():
            m_s[...] = jnp.full_like(m_s, NEG_INF)
            acc_s[...] = jnp.zeros_like(acc_s)
            va_s[:, dv:2 * dv] = jnp.ones((rows, dv), MXU_DTYPE)

        def step(k, v, n):
            va_s[0:n, 0:dv] = v.astype(MXU_DTYPE)
            s = _dot(q_ref[0], k, _DN_NT)
            m_prev = m_s[...]
            m_new = jnp.maximum(m_prev, jnp.max(s, axis=-1, keepdims=True))
            p = jnp.exp2(s - m_new)
            acc_s[...] = jnp.exp2(m_prev - m_new) * acc_s[...] + _dot(p, va_s[0:n, :], _DN_NN)
            m_s[...] = m_new

        if has2:
            @pl.when(j == 0)
            def _():
                step(k2_ref[0], v2_ref[0], C)

            @pl.when(j > 0)
            def _():
                step(k1_ref[0], v1_ref[0], tk)
        else:
            step(k1_ref[0], v1_ref[0], tk)

        @pl.when(j == nkv - 1)
        def _():
            l = acc_s[:, dv:dv + 1]
            o_ref[0] = acc_s[:, 0:dv] / l
            lse_ref[0] = m_s[...] + jnp.log2(l)

    qs = pl.BlockSpec((1, tq, dk), lambda h, i, j: (h, i, 0))
    k1s = pl.BlockSpec((1, tk, dk), lambda h, i, j: (h, jnp.maximum(j - off, 0), 0))
    v1s = pl.BlockSpec((1, tk, dv), lambda h, i, j: (h, jnp.maximum(j - off, 0), 0))
    in_specs, args = [qs, k1s, v1s], [q, k1, v1]
    if has2:
        in_specs += [pl.BlockSpec((1, C, dk), lambda h, i, j: (h, 0, 0)), pl.BlockSpec((1, C, dv), lambda h, i, j: (h, 0, 0))]
        args += [k2, v2]
    return pl.pallas_call(
        body, name=name, grid=(H, Tq // tq, nkv), in_specs=in_specs,
        out_specs=[pl.BlockSpec((1, tq, dv), lambda h, i, j: (h, i, 0)), pl.BlockSpec((1, tq, 1), lambda h, i, j: (h, i, 0))],
        out_shape=[jax.ShapeDtypeStruct((H, Tq, dv), F32), jax.ShapeDtypeStruct((H, Tq, 1), F32)],
        scratch_shapes=[pltpu.VMEM((tq, 1), F32), pltpu.VMEM((tq, 2 * dv), F32), pltpu.VMEM((rows, 2 * dv), MXU_DTYPE)],
        compiler_params=_params(("parallel", "parallel", "arbitrary")))(*args)


def _flash_bwd_call(q, k1, v1, k2, v2, o, lse, do, name):
    H, Tq, dk = q.shape
    T1, dv = k1.shape[1], v1.shape[2]
    has2 = k2 is not None
    tq, tk = _pick(Tq, _TQ), _pick(T1, _TQ)
    off = 1 if has2 else 0
    nkv, nq = T1 // tk + off, Tq // tq
    C = k2.shape[1] if has2 else 0
    rows = max(tk, C)

    def body(*refs):
        if has2:
            (q_ref, k1_ref, v1_ref, k2_ref, v2_ref, o_ref, lse_ref, do_ref,
             dq_ref, dk1_ref, dv1_ref, dk2_ref, dv2_ref, dk_s, dv_s) = refs
        else:
            q_ref, k1_ref, v1_ref, o_ref, lse_ref, do_ref, dq_ref, dk1_ref, dv1_ref, dk_s, dv_s = refs
        j, i = pl.program_id(1), pl.program_id(2)

        @pl.when((j == 0) & (i == 0))
        def _():
            dq_ref[...] = jnp.zeros_like(dq_ref)

        @pl.when(i == 0)
        def _():
            dk_s[...] = jnp.zeros_like(dk_s)
            dv_s[...] = jnp.zeros_like(dv_s)

        def step(k, v, n):
            qb, dob = q_ref[0], do_ref[0]
            p = jnp.exp2(_dot(qb, k, _DN_NT) - lse_ref[0])
            dv_s[0:n, :] += _dot(p, dob, _DN_TN)
            dol = dob * LN2
            ds = p * (_dot(dol, v, _DN_NT) - jnp.sum(dol * o_ref[0], axis=-1, keepdims=True))
            dk_s[0:n, :] += _dot(ds, qb, _DN_TN)
            r0 = pl.multiple_of(i * tq, tq)
            dq_ref[0, pl.ds(r0, tq), :] += _dot(ds, k, _DN_NN)

        if has2:
            @pl.when(j == 0)
            def _():
                step(k2_ref[0], v2_ref[0], C)

            @pl.when(j > 0)
            def _():
                step(k1_ref[0], v1_ref[0], tk)

            @pl.when((i == nq - 1) & (j == 0))
            def _():
                dk2_ref[0] = dk_s[0:C, :]
                dv2_ref[0] = dv_s[0:C, :]

            @pl.when((i == nq - 1) & (j > 0))
            def _():
                dk1_ref[0] = dk_s[0:tk, :]
                dv1_ref[0] = dv_s[0:tk, :]
        else:
            step(k1_ref[0], v1_ref[0], tk)

            @pl.when(i == nq - 1)
            def _():
                dk1_ref[0] = dk_s[...]
                dv1_ref[0] = dv_s[...]

    qs = pl.BlockSpec((1, tq, dk), lambda h, j, i: (h, i, 0))
    os_ = pl.BlockSpec((1, tq, dv), lambda h, j, i: (h, i, 0))
    ls = pl.BlockSpec((1, tq, 1), lambda h, j, i: (h, i, 0))
    k1s = pl.BlockSpec((1, tk, dk), lambda h, j, i: (h, jnp.maximum(j - off, 0), 0))
    v1s = pl.BlockSpec((1, tk, dv), lambda h, j, i: (h, jnp.maximum(j - off, 0), 0))
    in_specs, args = [qs, k1s, v1s], [q, k1, v1]
    out_specs = [pl.BlockSpec((1, Tq, dk), lambda h, j, i: (h, 0, 0)), k1s, v1s]
    out_shape = [jax.ShapeDtypeStruct((H, Tq, dk), F32), jax.ShapeDtypeStruct((H, T1, dk), F32),
                 jax.ShapeDtypeStruct((H, T1, dv), F32)]
    if has2:
        k2s = pl.BlockSpec((1, C, dk), lambda h, j, i: (h, 0, 0))
        v2s = pl.BlockSpec((1, C, dv), lambda h, j, i: (h, 0, 0))
        in_specs += [k2s, v2s]
        args += [k2, v2]
        out_specs += [k2s, v2s]
        out_shape += [jax.ShapeDtypeStruct((H, C, dk), F32), jax.ShapeDtypeStruct((H, C, dv), F32)]
    in_specs += [os_, ls, os_]
    args += [o, lse, do]
    return pl.pallas_call(
        body, name=name, grid=(H, nkv, nq), in_specs=in_specs, out_specs=out_specs, out_shape=out_shape,
        scratch_shapes=[pltpu.VMEM((rows, dk), F32), pltpu.VMEM((rows, dv), F32)],
        compiler_params=_params(("parallel", "arbitrary", "arbitrary")))(*args)


@functools.partial(jax.custom_vjp, nondiff_argnums=(5,))
def flash2(q, k1, v1, k2, v2, name):
    return _flash_fwd_call(q, k1, v1, k2, v2, name)[0]


def _flash2_fwd(q, k1, v1, k2, v2, name):
    o, lse = _flash_fwd_call(q, k1, v1, k2, v2, name)
    return o, (q, k1, v1, k2, v2, o, lse)


def _flash2_bwd(name, res, do):
    q, k1, v1, k2, v2, o, lse = res
    return tuple(_flash_bwd_call(q, k1, v1, k2, v2, o, lse, do, name + "_bwd"))


flash2.defvjp(_flash2_fwd, _flash2_bwd)


@functools.partial(jax.custom_vjp, nondiff_argnums=(3,))
def flash1(q, k, v, name):
    return _flash_fwd_call(q, k, v, None, None, name)[0]


def _flash1_fwd(q, k, v, name):
    o, lse = _flash_fwd_call(q, k, v, None, None, name)
    return o, (q, k, v, o, lse)


def _flash1_bwd(name, res, do):
    q, k, v, o, lse = res
    return tuple(_flash_bwd_call(q, k, v, None, None, o, lse, do, name + "_bwd"))


flash1.defvjp(_flash1_fwd, _flash1_bwd)


def _gqa_call(q, k, v, sink_rows, n_ctx, res, name):
    KV, G, M, d = q.shape
    B, C = BLOCK, n_ctx
    assert WINDOW == BLOCK
    nt, nct, R = M // B, n_ctx // B, G * B
    scale = d ** -0.5
    bwd = res is not None

    def body(*refs):
        q_ref, kc_ref, vc_ref, k0, k1, k2, v0, v1, v2, sink_ref, o_ref, lse_ref = refs[:12]
        i = pl.program_id(1)
        qb = q_ref[0].reshape(R, d)
        a = lax.broadcasted_iota(jnp.int32, (R, B), 0) % B
        b = lax.broadcasted_iota(jnp.int32, (R, B), 1)
        kb, vb, masks = (k0[0], k1[0], k2[0]), (v0[0], v1[0], v2[0]), []
        for r in (-1, 0, 1):
            in_range = (i >= nct) & (i + r >= nct) & (i + r <= nt - 1)
            masks.append(in_range & (a <= b) if r == -1 else (in_range & (a >= b) if r == 1 else in_range & (a >= 0)))
        kc, vc, sink = kc_ref[0], vc_ref[0], sink_ref[0]
        s_c = _dot(qb, kc, _DN_NT) * scale
        s_b = [jnp.where(masks[t], _dot(qb, kb[t], _DN_NT) * scale, NEG_INF) for t in range(3)]
        if not bwd:
            m = jnp.maximum(sink, jnp.max(s_c, axis=-1, keepdims=True))
            for s in s_b:
                m = jnp.maximum(m, jnp.max(s, axis=-1, keepdims=True))
            e_c, e_b = jnp.exp(s_c - m), [jnp.exp(s - m) for s in s_b]
            den = jnp.exp(sink - m) + jnp.sum(e_c, axis=-1, keepdims=True)
            for e in e_b:
                den = den + jnp.sum(e, axis=-1, keepdims=True)
            inv = 1.0 / den
            o = _dot(e_c * inv, vc, _DN_NN)
            for t in range(3):
                o = o + _dot(e_b[t] * inv, vb[t], _DN_NN)
            o_ref[0] = o.reshape(G, B, d)
            lse_ref[0] = (m + jnp.log(den)).reshape(G, B, 1)
            return
        do_ref, dq_ref, dkc_ref, dvc_ref, dkb_ref, dvb_ref, dsink_ref = refs[12:]

        @pl.when(i == 0)
        def _():
            dkc_ref[...] = jnp.zeros_like(dkc_ref)
            dvc_ref[...] = jnp.zeros_like(dvc_ref)
            dsink_ref[...] = jnp.zeros_like(dsink_ref)

        lse, dob = lse_ref[0].reshape(R, 1), do_ref[0].reshape(R, d)
        delta = jnp.sum(dob * o_ref[0].reshape(R, d), axis=-1, keepdims=True)
        p_c = jnp.exp(s_c - lse)
        ds_c = p_c * (_dot(dob, vc, _DN_NT) - delta) * scale
        dq = _dot(ds_c, kc, _DN_NN)
        dkc_ref[0] += _dot(ds_c, qb, _DN_TN)
        dvc_ref[0] += _dot(p_c, dob, _DN_TN)
        for t in range(3):
            p = jnp.exp(s_b[t] - lse)
            ds = p * (_dot(dob, vb[t], _DN_NT) - delta) * scale
            dq = dq + _dot(ds, kb[t], _DN_NN)
            dkb_ref[0, 0, t] = _dot(ds, qb, _DN_TN)
            dvb_ref[0, 0, t] = _dot(p, dob, _DN_TN)
        dq_ref[0] = dq.reshape(G, B, d)
        dsink_ref[0] -= jnp.exp(sink - lse) * delta

    def band(r):
        return lambda h, i: (h, jnp.clip(i + r, nct, nt - 1), 0)

    qs = pl.BlockSpec((1, G, B, d), lambda h, i: (h, 0, i, 0))
    ls = pl.BlockSpec((1, G, B, 1), lambda h, i: (h, 0, i, 0))
    cs = pl.BlockSpec((1, C, d), lambda h, i: (h, 0, 0))
    ss = pl.BlockSpec((1, R, 1), lambda h, i: (h, 0, 0))
    bs = [pl.BlockSpec((1, B, d), band(r)) for r in (-1, 0, 1)]
    in_specs = [qs, cs, cs] + bs + bs + [ss]
    args = [q, k, v, k, k, k, v, v, v, sink_rows]
    if not bwd:
        return pl.pallas_call(body, name=name, grid=(KV, nt), in_specs=in_specs, out_specs=[qs, ls],
                              out_shape=[jax.ShapeDtypeStruct((KV, G, M, d), F32), jax.ShapeDtypeStruct((KV, G, M, 1), F32)],
                              compiler_params=_params(("parallel", "parallel")))(*args)
    o, lse, do = res
    part = pl.BlockSpec((1, 1, 3, B, d), lambda h, i: (h, i, 0, 0, 0))
    part_shape = jax.ShapeDtypeStruct((KV, nt, 3, B, d), F32)
    return pl.pallas_call(body, name=name, grid=(KV, nt), in_specs=in_specs + [qs, ls, qs],
                          out_specs=[qs, cs, cs, part, part, ss],
                          out_shape=[jax.ShapeDtypeStruct((KV, G, M, d), F32), jax.ShapeDtypeStruct((KV, C, d), F32),
                                     jax.ShapeDtypeStruct((KV, C, d), F32), part_shape, part_shape,
                                     jax.ShapeDtypeStruct((KV, R, 1), F32)],
                          compiler_params=_params(("parallel", "arbitrary")))(*args, o, lse, do)


@functools.partial(jax.custom_vjp, nondiff_argnums=(4, 5))
def gqa_core(q, k, v, sink_rows, n_ctx, name):
    return _gqa_call(q, k, v, sink_rows, n_ctx, None, name)[0]


def _gqa_core_fwd(q, k, v, sink_rows, n_ctx, name):
    o, lse = _gqa_call(q, k, v, sink_rows, n_ctx, None, name)
    return o, (q, k, v, sink_rows, o, lse)


def _gqa_core_bwd(n_ctx, name, res, do):
    q, k, v, sink_rows, o, lse = res
    dq, dkc, dvc, dkb, dvb, dsink = _gqa_call(q, k, v, sink_rows, n_ctx, (o, lse, do), name + "_bwd")

    def keys(ctx_part, band_part):
        zero = jnp.zeros_like(band_part[:, :1, 0])
        blocks = (jnp.concatenate([band_part[:, 1:, 0], zero], axis=1) + band_part[:, :, 1]
                  + jnp.concatenate([zero, band_part[:, :-1, 2]], axis=1))
        rows = blocks.reshape(k.shape)
        return jnp.concatenate([rows[:, :n_ctx] + ctx_part, rows[:, n_ctx:]], axis=1)

    return dq, keys(dkc, dkb), keys(dvc, dvb), dsink


gqa_core.defvjp(_gqa_core_fwd, _gqa_core_bwd)


def gqa_attention(gq, gk, gv, sink, n_ctx, name):
    M, H, d = gq.shape
    G = H // GQA_KV_HEADS
    q4 = gq.reshape(M, GQA_KV_HEADS, G, d).transpose(1, 2, 0, 3)
    sink_rows = jnp.repeat(sink.reshape(GQA_KV_HEADS, G), BLOCK, axis=1)[..., None]
    o = gqa_core(q4, gk.transpose(1, 0, 2), gv.transpose(1, 0, 2), sink_rows, n_ctx, name)
    return o.transpose(2, 0, 1, 3).reshape(M, H * d)


def _scan_call(a, x, s, *, rev, adj, n_ctx, name):
    M, W = x.shape
    cw = SSM_CW
    J = W // (2 * cw)
    L = _row_tile(n_ctx, M)
    nt, nc = M // L, n_ctx // L
    asc = rev == adj
    sub = min(SCAN_SUB, L)
    nsub = L // sub
    n_steps = int(math.log2(sub))
    assert 1 << n_steps == sub

    def tile(t):
        if not rev:
            return nt - 1 - t if adj else t
        if not adj:
            return jnp.where(t < nc, nc - 1 - t, nt - 1 - (t - nc))
        return jnp.where(t < nt - nc, nc + t, t - (nt - nc))

    def body(*refs):
        if adj:
            a_ref, x_ref, s_ref, o_ref, da_ref, car_ref = refs
        else:
            a_ref, x_ref, o_ref, car_ref = refs
        t = pl.program_id(1)

        @pl.when(t == 0)
        def _():
            car_ref[...] = jnp.zeros_like(car_ref)
            if adj:
                da_ref[...] = jnp.zeros_like(da_ref)

        ar, ai = a_ref[:, 0:cw], a_ref[:, cw:2 * cw]
        powers, pr, pi = [], ar, ai
        for _ in range(n_steps):
            powers.append((pr, pi))
            pr, pi = pr * pr - pi * pi, 2.0 * pr * pi
        row = lax.broadcasted_iota(jnp.int32, (sub, cw), 0)
        first, last = (0, sub - 1) if asc else (sub - 1, 0)

        def scan_rows(i, carry):
            cr, ci = carry[0], carry[1]
            r0 = pl.multiple_of((i if asc else nsub - 1 - i) * sub, sub)
            xr, xi = x_ref[pl.ds(r0, sub), 0:cw], x_ref[pl.ds(r0, sub), cw:2 * cw]
            xr = xr + jnp.where(row == first, ar * cr - ai * ci, 0.0)
            xi = xi + jnp.where(row == first, ar * ci + ai * cr, 0.0)
            k = 1
            for pr, pi in powers:
                if asc:
                    sr, si, keep = pltpu.roll(xr, k, 0), pltpu.roll(xi, k, 0), row >= k
                else:
                    sr, si, keep = pltpu.roll(xr, sub - k, 0), pltpu.roll(xi, sub - k, 0), row < sub - k
                sr, si = jnp.where(keep, sr, 0.0), jnp.where(keep, si, 0.0)
                xr, xi = xr + pr * sr - pi * si, xi + pr * si + pi * sr
                k *= 2
            o_ref[pl.ds(r0, sub), 0:cw] = xr
            o_ref[pl.ds(r0, sub), cw:2 * cw] = xi
            out = (jnp.sum(jnp.where(row == last, xr, 0.0), axis=0, keepdims=True),
                   jnp.sum(jnp.where(row == last, xi, 0.0), axis=0, keepdims=True))
            if adj:
                if asc:
                    gr, gi = pltpu.roll(xr, 1, 0), pltpu.roll(xi, 1, 0)
                else:
                    gr, gi = pltpu.roll(xr, sub - 1, 0), pltpu.roll(xi, sub - 1, 0)
                gr, gi = jnp.where(row == first, cr, gr), jnp.where(row == first, ci, gi)
                sr, si = s_ref[pl.ds(r0, sub), 0:cw], s_ref[pl.ds(r0, sub), cw:2 * cw]
                out += (carry[2] + jnp.sum(sr * gr + si * gi, axis=0, keepdims=True),
                        carry[3] + jnp.sum(sr * gi - si * gr, axis=0, keepdims=True))
            return out

        init = (car_ref[:, 0:cw], car_ref[:, cw:2 * cw])
        if adj:
            init += (jnp.zeros((1, cw), F32), jnp.zeros((1, cw), F32))
        done = lax.fori_loop(0, nsub, scan_rows, init)
        car_ref[:, 0:cw] = done[0]
        car_ref[:, cw:2 * cw] = done[1]
        if adj:
            da_ref[:, 0:cw] += done[2]
            da_ref[:, cw:2 * cw] += done[3]

    blk = pl.BlockSpec((L, 2 * cw), lambda j, t: (tile(t), j))
    vec = pl.BlockSpec((1, 2 * cw), lambda j, t: (0, j))
    if adj:
        in_specs, args = [vec, blk, blk], (a, x, s)
        out_specs = [blk, vec]
        out_shape = [jax.ShapeDtypeStruct((M, W), F32), jax.ShapeDtypeStruct((1, W), F32)]
    else:
        in_specs, args = [vec, blk], (a, x)
        out_specs = blk
        out_shape = jax.ShapeDtypeStruct((M, W), F32)
    return pl.pallas_call(body, name=name, grid=(J, nt), in_specs=in_specs, out_specs=out_specs, out_shape=out_shape,
                          scratch_shapes=[pltpu.VMEM((1, 2 * cw), F32)],
                          compiler_params=_params(("parallel", "arbitrary")))(*args)


def _conj_layout(a):
    cw = SSM_CW
    J = a.shape[1] // (2 * cw)
    a4 = a.reshape(1, J, 2, cw)
    return jnp.concatenate([a4[:, :, 0:1], -a4[:, :, 1:2]], axis=2).reshape(a.shape)


@functools.partial(jax.custom_vjp, nondiff_argnums=(2, 3, 4))
def diag_scan(a, x, rev, n_ctx, name):
    return _scan_call(a, x, None, rev=rev, adj=False, n_ctx=n_ctx, name=name)


def _diag_scan_fwd(a, x, rev, n_ctx, name):
    s = _scan_call(a, x, None, rev=rev, adj=False, n_ctx=n_ctx, name=name)
    return s, (a, s)


def _diag_scan_bwd(rev, n_ctx, name, res, ds):
    a, s = res
    g, da = _scan_call(_conj_layout(a), ds, s, rev=rev, adj=True, n_ctx=n_ctx, name=name + "_adj")
    return da, g


diag_scan.defvjp(_diag_scan_fwd, _diag_scan_bwd)


def _loss_call(h, g, target, name):
    M, D = h.shape
    tm = _pick(M, (256, 128, 64, 32, 16, 8))

    def body(h_ref, g_ref, t_ref, loss_ref, dh_ref, dg_ref):
        i = pl.program_id(0)

        @pl.when(i == 0)
        def _():
            loss_ref[...] = jnp.zeros_like(loss_ref)
            dg_ref[...] = jnp.zeros_like(dg_ref)

        xv, gv = h_ref[...], g_ref[...]
        r = lax.rsqrt(jnp.mean(xv * xv, axis=-1, keepdims=True) + EPS)
        xhat = xv * r
        err = xhat * gv - t_ref[...]
        loss_ref[...] += 0.5 * jnp.sum(jnp.mean(err * err, axis=-1, keepdims=True), axis=0, keepdims=True)
        dy = err * (1.0 / D)
        dg_ref[...] += jnp.sum(dy * xhat, axis=0, keepdims=True)
        dxhat = dy * gv
        dh_ref[...] = r * (dxhat - xhat * jnp.mean(dxhat * xhat, axis=-1, keepdims=True))

    row = pl.BlockSpec((tm, D), lambda i: (i, 0))
    vec = pl.BlockSpec((1, D), lambda i: (0, 0))
    one = pl.BlockSpec((1, 1), lambda i: (0, 0))
    return pl.pallas_call(body, name=name, grid=(M // tm,), in_specs=[row, vec, row], out_specs=[one, row, vec],
                          out_shape=[jax.ShapeDtypeStruct((1, 1), F32), jax.ShapeDtypeStruct((M, D), F32),
                                     jax.ShapeDtypeStruct((1, D), F32)],
                          compiler_params=_params(("arbitrary",)))(h, g, target)


@functools.partial(jax.custom_vjp, nondiff_argnums=(3,))
def loss_head(h, g, target, name):
    return _loss_call(h, g, target, name)[0][0, 0]


def _loss_head_fwd(h, g, target, name):
    loss, dh, dg = _loss_call(h, g, target, name)
    return loss[0, 0], (dh, dg, target)


def _loss_head_bwd(name, res, ct):
    dh, dg, target = res
    return ct * dh, ct * dg, jnp.zeros_like(target)


loss_head.defvjp(_loss_head_fwd, _loss_head_bwd)


def _adamw_call(w, gstack, m, v, name):
    R, Cn = w.shape
    n = gstack.shape[0]
    tr = _pick(R, (64, 32, 16, 8))

    def body(w_ref, g_ref, m_ref, v_ref, go_ref, d_ref, mo_ref, vo_ref):
        g = g_ref[0]
        for s in range(1, n):
            g = g + g_ref[s]
        mn = ADAM_B1 * m_ref[...] + (1.0 - ADAM_B1) * g
        vn = ADAM_B2 * v_ref[...] + (1.0 - ADAM_B2) * (g * g)
        m_hat = mn / (1.0 - ADAM_B1 ** ADAM_STEP)
        v_hat = vn / (1.0 - ADAM_B2 ** ADAM_STEP)
        go_ref[...] = g
        d_ref[...] = -ADAM_LR * (m_hat / (jnp.sqrt(v_hat) + ADAM_EPS) + ADAM_WD * w_ref[...])
        mo_ref[...] = mn
        vo_ref[...] = vn

    blk = pl.BlockSpec((tr, Cn), lambda i: (i, 0))
    gblk = pl.BlockSpec((n, tr, Cn), lambda i: (0, i, 0))
    sds = jax.ShapeDtypeStruct((R, Cn), F32)
    return pl.pallas_call(body, name=name, grid=(R // tr,), in_specs=[blk, gblk, blk, blk], out_specs=[blk] * 4,
                          out_shape=[sds] * 4, compiler_params=_params(("parallel",)))(w, gstack, m, v)


MESH = pl.DeviceIdType.MESH
ANY = pl.BlockSpec(memory_space=pl.ANY)


def _place():
    return lax.axis_index("x"), lax.axis_index("y"), lax.axis_index("c")


def _sibling_exchange(v, name):
    n = v.shape[0]

    def body(v_ref, o_ref, send_sems, recv_sems):
        x, y, c = _place()
        copies = [pltpu.make_async_remote_copy(src_ref=v_ref.at[k], dst_ref=o_ref.at[k], send_sem=send_sems.at[k],
                                               recv_sem=recv_sems.at[k], device_id=(x, y, 1 - c), device_id_type=MESH)
                  for k in range(n)]
        for cp in copies:
            cp.start()
        for cp in copies:
            cp.wait()

    return pl.pallas_call(body, name=name, in_specs=[ANY], out_specs=ANY, out_shape=jax.ShapeDtypeStruct(v.shape, v.dtype),
                          scratch_shapes=[pltpu.SemaphoreType.DMA((n,)), pltpu.SemaphoreType.DMA((n,))])(v)


def _sibling_exchange_half(v, name):
    n = v.shape[0]

    def body(v_ref, o_ref, send_sems, recv_sems):
        x, y, c = _place()
        copies = [pltpu.make_async_remote_copy(src_ref=v_ref.at[k, pl.ds(1 - c, 1)], dst_ref=o_ref.at[k],
                                               send_sem=send_sems.at[k], recv_sem=recv_sems.at[k],
                                               device_id=(x, y, 1 - c), device_id_type=MESH)
                  for k in range(n)]
        for cp in copies:
            cp.start()
        for cp in copies:
            cp.wait()

    return pl.pallas_call(body, name=name, in_specs=[ANY], out_specs=ANY,
                          out_shape=jax.ShapeDtypeStruct((n, 1) + v.shape[2:], v.dtype),
                          scratch_shapes=[pltpu.SemaphoreType.DMA((n,)), pltpu.SemaphoreType.DMA((n,))])(v)


def _chip_exchange(v, same, name):
    out_shape = (N_CHIPS,) + (v.shape if same else v.shape[1:])

    def body(v_ref, o_ref, send_sems, recv_sems, local_sem):
        x, y, c = _place()
        me = 2 * x + y
        own = pltpu.make_async_copy(v_ref if same else v_ref.at[me], o_ref.at[me], local_sem)
        own.start()
        copies = []
        for k, (fx, fy) in enumerate(((1, 0), (0, 1), (1, 1))):
            px, py = jnp.where(fx == 1, 1 - x, x), jnp.where(fy == 1, 1 - y, y)
            src = v_ref if same else v_ref.at[2 * px + py]
            copies.append(pltpu.make_async_remote_copy(src_ref=src, dst_ref=o_ref.at[me], send_sem=send_sems.at[k],
                                                       recv_sem=recv_sems.at[k], device_id=(px, py, c), device_id_type=MESH))
        for cp in copies:
            cp.start()
        for cp in copies:
            cp.wait()
        own.wait()

    return pl.pallas_call(body, name=name, in_specs=[ANY], out_specs=ANY, out_shape=jax.ShapeDtypeStruct(out_shape, v.dtype),
                          scratch_shapes=[pltpu.SemaphoreType.DMA((3,)), pltpu.SemaphoreType.DMA((3,)), pltpu.SemaphoreType.DMA])(v)


def _all_gather(v, name):
    def body(v_ref, o_ref, send_sems, recv_sems, local_sem):
        x, y, c = _place()
        me = 4 * x + 2 * y + c
        own = pltpu.make_async_copy(v_ref, o_ref.at[me], local_sem)
        own.start()
        copies = []
        for k in range(1, 8):
            fx, fy, fc = (k >> 2) & 1, (k >> 1) & 1, k & 1
            peer = (jnp.where(fx == 1, 1 - x, x), jnp.where(fy == 1, 1 - y, y), jnp.where(fc == 1, 1 - c, c))
            copies.append(pltpu.make_async_remote_copy(src_ref=v_ref, dst_ref=o_ref.at[me], send_sem=send_sems.at[k - 1],
                                                       recv_sem=recv_sems.at[k - 1], device_id=peer, device_id_type=MESH))
        for cp in copies:
            cp.start()
        for cp in copies:
            cp.wait()
        own.wait()

    return pl.pallas_call(body, name=name, in_specs=[ANY], out_specs=ANY,
                          out_shape=jax.ShapeDtypeStruct((8,) + v.shape, v.dtype),
                          scratch_shapes=[pltpu.SemaphoreType.DMA((7,)), pltpu.SemaphoreType.DMA((7,)), pltpu.SemaphoreType.DMA])(v)


def _add_own_half(g, r, c, name):
    n, _, R, W = g.shape
    tr = _pick(R, (256, 128, 64, 32, 16, 8))

    def body(c_ref, g_ref, r_ref, o_ref):
        o_ref[...] = g_ref[0] + r_ref[0]

    grid_spec = pltpu.PrefetchScalarGridSpec(
        num_scalar_prefetch=1, grid=(n, R // tr),
        in_specs=[pl.BlockSpec((1, 1, tr, W), lambda p, i, c_ref: (p, c_ref[0], i, 0)),
                  pl.BlockSpec((1, 1, tr, W), lambda p, i, c_ref: (p, 0, i, 0))],
        out_specs=pl.BlockSpec((1, tr, W), lambda p, i, c_ref: (p, i, 0)))
    return pl.pallas_call(body, name=name, grid_spec=grid_spec, out_shape=jax.ShapeDtypeStruct((n, R, W), F32),
                          compiler_params=_params(("parallel", "parallel")))(c.reshape(1).astype(jnp.int32), g, r)


def _sum_stack(v, name):
    n, R, W = v.shape
    tr = _pick(R, (256, 128, 64, 32, 16, 8))

    def body(v_ref, o_ref):
        acc = v_ref[0]
        for s in range(1, n):
            acc = acc + v_ref[s]
        o_ref[...] = acc

    return pl.pallas_call(body, name=name, grid=(R // tr,), in_specs=[pl.BlockSpec((n, tr, W), lambda i: (0, i, 0))],
                          out_specs=pl.BlockSpec((tr, W), lambda i: (i, 0)), out_shape=jax.ShapeDtypeStruct((R, W), F32),
                          compiler_params=_params(("parallel",)))(v)


def _flat_rows(n, mult):
    rows = -(-n // FLAT_W)
    return -(-rows // mult) * mult


def _by_core(a, b, c):
    return jnp.where(c == 0, a, b)


def _lane_padded(n):
    return -(-n // LANES) * LANES


def _lane_pad(a):
    pad = _lane_padded(a.shape[-1]) - a.shape[-1]
    return jnp.pad(a, [(0, 0)] * (a.ndim - 1) + [(0, pad)]) if pad else a


def gather_weights(shards):
    _, _, c = _place()
    flat = jnp.concatenate([_lane_pad(shards[n].astype(WIRE_DTYPE)).reshape(-1) for n in BIG_NAMES])
    n_flat = flat.shape[0]
    rh = _flat_rows(n_flat, 32) // 2
    flat = jnp.pad(flat, (0, 2 * rh * FLAT_W - n_flat)).reshape(2, rh, FLAT_W)
    mine = lax.dynamic_index_in_dim(flat, c, axis=0, keepdims=False)
    got = _chip_exchange(mine, True, "gather_chips")
    other = _sibling_exchange(got, "gather_sibling")
    halves = jnp.stack([_by_core(got, other, c), _by_core(other, got, c)], axis=1)
    allflat = halves.reshape(N_CHIPS, 2 * rh * FLAT_W)
    out, off = {}, 0
    for name, axis in BIG:
        shp = shards[name].shape
        padded = shp[:-1] + (_lane_padded(shp[-1]),)
        size = math.prod(padded)
        parts = allflat[:, off:off + size].reshape((N_CHIPS,) + padded)[..., :shp[-1]]
        out[name] = [jnp.concatenate([parts[p, l] for p in range(N_CHIPS)], axis=axis - 1) for l in range(shp[0])]
        off += size
    return out


def scatter_gradients(grads, shards):
    _, _, c = _place()
    cols = []
    for name, axis in SCATTERED:
        parts = [jnp.split(g, N_CHIPS, axis=axis - 1) for g in grads[name]]
        cols.append(jnp.stack([_lane_pad(jnp.stack([per_layer[p] for per_layer in parts])).reshape(-1) for p in range(N_CHIPS)]))
    flat = jnp.concatenate(cols, axis=1)
    n_flat = flat.shape[1]
    rh = _flat_rows(n_flat, 16) // 2
    flat = jnp.pad(flat, ((0, 0), (0, 2 * rh * FLAT_W - n_flat))).reshape(N_CHIPS, 2, rh, FLAT_W)
    theirs = _sibling_exchange_half(flat, "scatter_sibling")
    pair = _add_own_half(flat, theirs, c, "scatter_pair_sum")
    got = _chip_exchange(pair, False, "scatter_chips")
    mine = _sum_stack(got, "scatter_chip_sum")
    other = _sibling_exchange(mine.reshape(1, rh, FLAT_W), "scatter_halves").reshape(rh, FLAT_W)
    full = jnp.stack([_by_core(mine, other, c), _by_core(other, mine, c)]).reshape(-1)
    out, off = {}, 0
    for name, _ in SCATTERED:
        shp = shards[name].shape
        padded = shp[:-1] + (_lane_padded(shp[-1]),)
        size = math.prod(padded)
        out[name] = full[off:off + size].reshape(padded)[..., :shp[-1]]
        off += size
    return out


def _rope_tables(n_ctx, n_lat, n_heads, head_dim, start, n):
    t = jnp.arange(n_lat, dtype=jnp.int32)
    zero = jnp.zeros((n_ctx,), jnp.int32)
    row = jnp.concatenate([zero, t // GRID_W]).astype(F32)
    col = jnp.concatenate([zero, t % GRID_W]).astype(F32)
    half, q = n // 2, n // 4
    j = jnp.arange(n_heads * head_dim, dtype=jnp.int32) % head_dim - start
    inside = (j >= 0) & (j < n)
    u = j % half
    inv_cols = jnp.where(inside, ROPE_BASE ** (-(2 * (u % q)).astype(F32) / half), 0.0)
    ang = (row[:, None] * jnp.where(j // half == 0, inv_cols, 0.0)[None, :]
           + col[:, None] * jnp.where(j // half == 1, inv_cols, 0.0)[None, :])
    first = (inside & (u // q == 0)).astype(F32)[None, :]
    second = (inside & (u // q == 1)).astype(F32)[None, :]
    return jnp.cos(ang), -jnp.sin(ang) * first, jnp.sin(ang) * second, q


def _rope(x, table):
    cos, minus_sin, plus_sin, q = table
    return x * cos + jnp.roll(x, -q, axis=1) * minus_sin + jnp.roll(x, q, axis=1) * plus_sin


def _cmul(ar, ai, br, bi):
    return ar * br - ai * bi, ar * bi + ai * br


def _ssm_discretize(lam_re, lam_im, log_dt, b_re, b_im):
    dt = jnp.exp(log_dt)[:, None]
    mag = jnp.exp(lam_re * dt)
    a_re, a_im = mag * jnp.cos(lam_im * dt), mag * jnp.sin(lam_im * dt)
    den = lam_re * lam_re + lam_im * lam_im
    w_re = ((a_re - 1) * lam_re + a_im * lam_im) / den
    w_im = (a_im * lam_re - (a_re - 1) * lam_im) / den
    bb_re, bb_im = _cmul(w_re[..., None], w_im[..., None], b_re, b_im)
    return a_re, a_im, bb_re, bb_im


def _ssm_layouts(a_re, a_im, bb_re, bb_im, c_re, c_im):
    J, g8, P, Mg = SSM_CHUNKS, SSM_CHUNK_GROUPS, SSM_STATE, SSM_GROUP
    eye = jnp.eye(g8, dtype=F32)
    a = jnp.stack([a_re.reshape(J, g8 * P), a_im.reshape(J, g8 * P)], axis=1).reshape(1, J * 2 * g8 * P)
    bb = jnp.stack([bb_re, bb_im]).reshape(2, J, g8, P, Mg)
    w_drive = jnp.einsum('rjgpm,gh->jgmrhp', bb, eye).reshape(J * g8 * Mg, 2 * g8 * P)
    cc = jnp.stack([c_re, -c_im]).reshape(2, J, g8, Mg, P)
    w_read = jnp.einsum('rjgmp,gh->jrhpgm', cc, eye).reshape(J * 2 * g8 * P, g8 * Mg)
    return a, w_drive, w_read


def _w_in_layout(d_model):
    widths = (("cq", MLA_Q_RANK), ("ckv", MLA_KV_RANK), ("kr", MLA_ROPE), ("u", SSM_WIDTH), ("gq", GQA_HEADS * GQA_HEAD_DIM),
              ("gk", GQA_KV_HEADS * GQA_HEAD_DIM), ("gv", GQA_KV_HEADS * GQA_HEAD_DIM), ("gates", N_BRANCH * d_model))
    out, src, dst = [], 0, 0
    for name, w in widths:
        out.append((name, src, dst, w))
        src += w
        dst += -(-w // LANES) * LANES
    return out, src, dst


def _pad_w_in(w, d_model):
    lay, _, _ = _w_in_layout(d_model)
    parts = []
    for _, src, _, wd in lay:
        seg = w[..., src:src + wd]
        pad = -(-wd // LANES) * LANES - wd
        parts.append(jnp.pad(seg, [(0, 0)] * (w.ndim - 1) + [(0, pad)]) if pad else seg)
    return jnp.concatenate(parts, axis=-1)


def _unpad_w_in(w, d_model):
    lay, _, _ = _w_in_layout(d_model)
    return jnp.concatenate([w[..., dst:dst + wd] for _, _, dst, wd in lay], axis=-1)


@functools.partial(jax.custom_vjp, nondiff_argnums=(1,))
def split_cols(proj, bounds):
    return tuple(proj[:, s:s + w] for s, w in bounds[0])


def _split_cols_fwd(proj, bounds):
    return split_cols(proj, bounds), None


def _split_cols_bwd(bounds, _, cts):
    segments, total = bounds
    rows, pieces, pos = cts[0].shape[0], [], 0
    for (s, w), ct in zip(segments, cts):
        if s > pos:
            pieces.append(jnp.zeros((rows, s - pos), ct.dtype))
        pieces.append(ct)
        pos = s + w
    if pos < total:
        pieces.append(jnp.zeros((rows, total - pos), cts[0].dtype))
    return (jnp.concatenate(pieces, axis=1),)


split_cols.defvjp(_split_cols_fwd, _split_cols_bwd)


def _layer(hall, lw, lz, sp, cs8, n_ctx, ropes):
    M, D = hall.shape
    C = n_ctx

    def lin(x, name):
        return linear(x, lw[name], lz[name], name)

    mod_all = linear_x(cs8, lw["ada_w"], "ada_w")[0:2] + sp["ada_b"][None, :] + lz["ada_tap"]
    mod = [mod_all[:, i * D:(i + 1) * D] for i in range(N_MOD)]

    def ffn(h, tag, norm_g, sh, sc, gate):
        hn = norm_mod(h, norm_g[None, :], sh, sc, C, tag + "_norm")
        act = swiglu_act(lin(hn, tag + "_w13"), tag + "_act")
        return gated_residual(h, lin(act, tag + "_w2"), gate, 0.5, C, tag + "_res")

    hall = ffn(hall, "ffn1", sp["norm_ffn1"], mod[0], mod[1], mod[2])

    xm = norm_mod(hall, sp["norm_mix"][None, :], mod[3], mod[4], C, "mix_norm")
    proj = lin(xm, "w_in")
    lay, _, total = _w_in_layout(D)
    seg = dict(zip([name for name, _, _, _ in lay], split_cols(proj, (tuple((dst, wd) for _, _, dst, wd in lay), total))))

    q = _rope(lin(rmsnorm(seg["cq"], sp["mla_q_norm"][None, :], "mla_q_norm"), "mla_w_uq"), ropes["mla_q"])
    q = (q * ((MLA_NOPE + MLA_ROPE) ** -0.5 * LOG2E)).reshape(M, MLA_HEADS, MLA_NOPE + MLA_ROPE)
    kv = lin(rmsnorm(seg["ckv"], sp["mla_kv_norm"][None, :], "mla_kv_norm"), "mla_w_ukv").reshape(M, MLA_HEADS, MLA_NOPE + MLA_V)
    kr = _rope(seg["kr"], ropes["mla_kr"])[:, None, :]
    k = jnp.concatenate([kv[..., :MLA_NOPE], jnp.broadcast_to(kr, (M, MLA_HEADS, MLA_ROPE))], axis=-1)

    def heads(t, lo, hi):
        return t[lo:hi].transpose(1, 0, 2)

    v = kv[..., MLA_NOPE:]
    k_ctx, v_ctx = heads(k, 0, C), heads(v, 0, C)
    o_lat = flash2(heads(q, C, M), heads(k, C, M), heads(v, C, M), k_ctx, v_ctx, "mla_lat")
    o_ctx = flash1(heads(q, 0, C), k_ctx, v_ctx, "mla_ctx")
    o = jnp.concatenate([o_ctx, o_lat], axis=1).transpose(1, 0, 2).reshape(M, MLA_HEADS * MLA_V)
    mla = lin(o, "mla_w_o")

    u = seg["u"]
    y = u * sp["ssm_d"][None, :]
    for direction in range(2):
        a_re, a_im, bb_re, bb_im = _ssm_discretize(sp["ssm_lambda_re"][direction], sp["ssm_lambda_im"][direction],
                                                   sp["ssm_log_dt"][direction], sp["ssm_b_re"][direction],
                                                   sp["ssm_b_im"][direction])
        a, w_drive, w_read = _ssm_layouts(a_re, a_im, bb_re, bb_im, sp["ssm_c_re"][direction], sp["ssm_c_im"][direction])
        drive = bd_linear(u, w_drive, SSM_CHUNKS, "ssm_drive%d" % direction)
        states = diag_scan(a, drive, direction == 1, C, "ssm_scan%d" % direction)
        y = y + bd_linear(states, w_read, SSM_CHUNKS, "ssm_read%d" % direction)
    zz = lin(jax.nn.gelu(y), "ssm_w_glu")

    gq = _rope(seg["gq"], ropes["gqa_q"]).reshape(M, GQA_HEADS, GQA_HEAD_DIM)
    gk = _rope(seg["gk"], ropes["gqa_k"]).reshape(M, GQA_KV_HEADS, GQA_HEAD_DIM)
    gv = seg["gv"].reshape(M, GQA_KV_HEADS, GQA_HEAD_DIM)
    gqa = lin(gqa_attention(gq, gk, gv, sp["gqa_sink"], C, "gqa"), "gqa_w_o")

    mixed = gated_mix(seg["gates"], mla, zz, gqa, "mix_gate")
    hall = gated_residual(hall, lin(mixed, "w_out"), mod[5], 1.0, C, "mix_res")

    return ffn(hall, "ffn2", sp["norm_ffn2"], mod[6], mod[7], mod[8])


PER_LAYER_SMALL = tuple(n for n in SMALL if n not in ("c_ctx", "final_norm"))


def _loss_fn(diff, x, c, ctx, target, whole):
    zeros, small, x = diff
    T, D = x.shape
    C = ctx.shape[0]
    ropes = {"mla_q": _rope_tables(C, T, MLA_HEADS, MLA_NOPE + MLA_ROPE, MLA_NOPE, MLA_ROPE),
             "mla_kr": _rope_tables(C, T, 1, MLA_ROPE, 0, MLA_ROPE),
             "gqa_q": _rope_tables(C, T, GQA_HEADS, GQA_HEAD_DIM, 0, GQA_HEAD_DIM),
             "gqa_k": _rope_tables(C, T, GQA_KV_HEADS, GQA_HEAD_DIM, 0, GQA_HEAD_DIM)}
    cs8 = jnp.pad(_cond_rows(small["c_ctx"], c), ((0, 6), (0, 0)))
    hall = jnp.concatenate([ctx, x], axis=0)

    for layer in range(len(zeros["ada_tap"])):
        hall = _layer(hall, {n: whole[n][layer] for n in BIG_NAMES}, {n: zeros[n][layer] for n in zeros},
                      {n: small[n][layer] for n in PER_LAYER_SMALL}, cs8, C, ropes)
    return loss_head(hall[C:], small["final_norm"][None, :], target, "loss_head")


def _cond_rows(c_ctx, c):
    return jax.nn.silu(jnp.stack([c_ctx, c]))


def kernel(*args):
    given = dict(zip(ARG_NAMES + ['loss_target'] + ['m_' + n for n in WEIGHTS] + ['v_' + n for n in WEIGHTS], args))
    x, c, ctx, target = given['x'][0], given['c'][0], given['ctx'][0], given['loss_target'][0]
    px, py, _ = _place()
    D = x.shape[-1]
    depth = given['ada_w'].shape[0]
    shards = {n: given[n] for n in BIG_NAMES}
    small = {n: given[n] for n in SMALL}

    whole = gather_weights(shards)
    whole["w_in"] = [_pad_w_in(w, D) for w in whole["w_in"]]
    zeros = {n: [jnp.zeros(w.shape, F32) for w in whole[n]] for n in SCATTERED_NAMES}
    zeros["ada_tap"] = [jnp.zeros((2, N_MOD * D), F32) for _ in range(depth)]

    loss, (gz, gsmall, gx) = jax.value_and_grad(_loss_fn)((zeros, small, x), x, c, ctx, target, whole)
    loss = lax.psum(loss, ("x", "y", "c"))
    taps = jnp.stack(gz.pop("ada_tap"))
    gz["w_in"] = [_unpad_w_in(g, D) for g in gz["w_in"]]
    gbig = scatter_gradients(gz, shards)

    extra = [taps.reshape(-1), _cond_rows(small["c_ctx"], c).reshape(-1)]
    n_small = sum(math.prod(given[n].shape) for n in SMALL)
    n_extra = sum(e.shape[0] for e in extra)

    def flat_small(d, tail=None):
        v = jnp.concatenate([d[n].reshape(-1) for n in SMALL] + (tail or [jnp.zeros((n_extra,), F32)]))
        rows = _flat_rows(v.shape[0], 8)
        return jnp.pad(v, (0, rows * FLAT_W - v.shape[0])).reshape(rows, FLAT_W)

    gathered = _all_gather(flat_small(gsmall, extra), "small_gather")
    outs = _adamw_call(flat_small(small), gathered, flat_small({n: given['m_' + n] for n in SMALL}),
                       flat_small({n: given['v_' + n] for n in SMALL}), "adamw_small")
    res, off = {}, 0
    for name in SMALL:
        shp = given[name].shape
        size = math.prod(shp)
        res[name] = [o.reshape(-1)[off:off + size].reshape(shp) for o in outs]
        off += size

    tails = gathered.reshape(8, -1)[:, n_small:n_small + n_extra]
    all_taps = tails[:, :taps.size].reshape(8, depth, 2, N_MOD * D)
    all_cs = tails[:, taps.size:].reshape(8 * 2, D)
    n_cols = given['ada_w'].shape[2]
    mine = lax.dynamic_slice_in_dim(all_taps, (2 * px + py) * n_cols, n_cols, axis=3)
    gbig["ada_w"] = jnp.stack([_mm_tn(all_cs, mine[:, layer].reshape(8 * 2, n_cols), "ada_w_dw") for layer in range(depth)])

    for name in BIG_NAMES:
        shp = given[name].shape
        two_d = (shp[0] * shp[1], shp[2])
        outs = _adamw_call(given[name].reshape(two_d), gbig[name].reshape((1,) + two_d), given['m_' + name].reshape(two_d),
                           given['v_' + name].reshape(two_d), "adamw_" + name)
        res[name] = [o.reshape(shp) for o in outs]

    return (loss, gx[None], *[res[n][0] for n in WEIGHTS], *[res[n][1] for n in WEIGHTS],
            *[res[n][2] for n in WEIGHTS], *[res[n][3] for n in WEIGHTS])
```

```python
import functools
import math

import jax
import jax.numpy as jnp
from jax import lax
from jax.experimental import pallas as pl
from jax.experimental.pallas import tpu as pltpu

F32 = jnp.float32
MXU_DTYPE = jnp.bfloat16
WIRE_DTYPE = jnp.bfloat16

GRID_W = 64
MLA_HEADS, MLA_NOPE, MLA_ROPE, MLA_V = 8, 64, 32, 64
MLA_Q_RANK, MLA_KV_RANK = 384, 256
SSM_WIDTH, SSM_GROUP, SSM_STATE = 512, 16, 64
SSM_GROUPS = SSM_WIDTH // SSM_GROUP
SSM_CHUNK_GROUPS = 8
SSM_CHUNKS = SSM_GROUPS // SSM_CHUNK_GROUPS
SSM_CW = SSM_CHUNK_GROUPS * SSM_STATE
SCAN_SUB = 32
GQA_HEADS, GQA_KV_HEADS, GQA_HEAD_DIM = 8, 2, 64
WINDOW, BLOCK = 128, 128
N_BRANCH, N_MOD = 3, 9
ROPE_BASE = 10000.0
EPS = 1e-6
NEG_INF = -1e30
LOG2E, LN2 = math.log2(math.e), math.log(2.0)
LANES = 128
FLAT_W = 1024

ADAM_LR, ADAM_B1, ADAM_B2, ADAM_EPS, ADAM_WD, ADAM_STEP = 0.001, 0.9, 0.999, 1e-08, 0.01, 10

VMEM_LIMIT = 48 * 1024 * 1024

ARG_NAMES = ['x', 'c', 'ctx', 'c_ctx', 'ada_w', 'ada_b', 'norm_ffn1', 'norm_mix', 'norm_ffn2', 'ffn1_w13', 'ffn1_w2', 'ffn2_w13', 'ffn2_w2', 'w_in', 'mla_q_norm', 'mla_kv_norm', 'mla_w_uq', 'mla_w_ukv', 'mla_w_o', 'ssm_lambda_re', 'ssm_lambda_im', 'ssm_log_dt', 'ssm_b_re', 'ssm_b_im', 'ssm_c_re', 'ssm_c_im', 'ssm_d', 'ssm_w_glu', 'gqa_sink', 'gqa_w_o', 'w_out', 'final_norm']
WEIGHTS = ARG_NAMES[3:]
BIG = (('ada_w', 2), ('ffn1_w13', 2), ('ffn1_w2', 1), ('ffn2_w13', 2), ('ffn2_w2', 1), ('w_in', 2), ('mla_w_uq', 2),
       ('mla_w_ukv', 2), ('mla_w_o', 2), ('ssm_w_glu', 2), ('gqa_w_o', 2), ('w_out', 1))
BIG_NAMES = tuple(n for n, _ in BIG)
SCATTERED = tuple((n, a) for n, a in BIG if n != 'ada_w')
SCATTERED_NAMES = tuple(n for n, _ in SCATTERED)
SMALL = tuple(n for n in WEIGHTS if n not in BIG_NAMES)
N_CHIPS = 4


def _pick(n, prefs):
    for p in prefs:
        if n % p == 0:
            return p
    return n


def _params(sem=None):
    return pltpu.CompilerParams(dimension_semantics=sem, vmem_limit_bytes=VMEM_LIMIT)


def _mm_call(a, b, *, grid, a_spec, b_spec, o_spec, o_shape, acc_shape, ta, tb, name, out_dtype=F32, keep_a=False):
    nk = grid[2]
    dn = (((0 if ta else 1,), (1 if tb else 0,)), ((), ()))
    assert not keep_a or nk == 1

    def body(a_ref, b_ref, o_ref, scr_ref):
        if keep_a:
            @pl.when(pl.program_id(1) == 0)
            def _():
                scr_ref[...] = a_ref[...].astype(MXU_DTYPE)

            o_ref[...] = lax.dot_general(scr_ref[...], b_ref[...].astype(MXU_DTYPE), dn,
                                         preferred_element_type=F32).astype(o_ref.dtype)
            return
        k = pl.program_id(2)

        @pl.when(k == 0)
        def _():
            scr_ref[...] = jnp.zeros_like(scr_ref)

        scr_ref[...] += lax.dot_general(a_ref[...].astype(MXU_DTYPE), b_ref[...].astype(MXU_DTYPE), dn,
                                        preferred_element_type=F32)

        @pl.when(k == nk - 1)
        def _():
            o_ref[...] = scr_ref[...].astype(o_ref.dtype)

    scratch = pltpu.VMEM(a_spec.block_shape, MXU_DTYPE) if keep_a else pltpu.VMEM(acc_shape, F32)
    return pl.pallas_call(
        body, name=name, grid=grid, in_specs=[a_spec, b_spec], out_specs=o_spec,
        out_shape=jax.ShapeDtypeStruct(o_shape, out_dtype), scratch_shapes=[scratch],
        compiler_params=_params(("parallel", "arbitrary" if keep_a else "parallel", "arbitrary")))(a, b)


_ROWS = (768, 512, 256, 128, 64, 32, 16, 8)
_WIDE = (1408, 1024, 512, 256, 128)
MAX_WHOLE = 2816


def _feat(n):
    return n if n <= _WIDE[0] else _pick(n, _WIDE)


def _mm_nn(x, w, name):
    M, K = x.shape
    N = w.shape[1]
    tm, tn = _pick(M, _ROWS), _pick(N, (512, 256, 128))
    tk = K if K <= MAX_WHOLE else _pick(K, (512, 256, 128))
    return _mm_call(x, w, grid=(M // tm, N // tn, K // tk),
                    a_spec=pl.BlockSpec((tm, tk), lambda i, j, k: (i, k)),
                    b_spec=pl.BlockSpec((tk, tn), lambda i, j, k: (k, j)),
                    o_spec=pl.BlockSpec((tm, tn), lambda i, j, k: (i, j)),
                    o_shape=(M, N), acc_shape=(tm, tn), ta=False, tb=False, name=name, keep_a=tk == K and N > tn)


def _mm_nt(dy, w, name):
    M, N = dy.shape
    K = w.shape[0]
    tm, tn, tk = _pick(M, _ROWS), _feat(K), _feat(N)
    return _mm_call(dy, w, grid=(M // tm, K // tn, N // tk),
                    a_spec=pl.BlockSpec((tm, tk), lambda i, j, k: (i, k)),
                    b_spec=pl.BlockSpec((tn, tk), lambda i, j, k: (j, k)),
                    o_spec=pl.BlockSpec((tm, tn), lambda i, j, k: (i, j)),
                    o_shape=(M, K), acc_shape=(tm, tn), ta=False, tb=True, name=name)


def _mm_tn(x, dy, name):
    M, K = x.shape
    N = dy.shape[1]
    tm, tn, tk = _feat(K), _feat(N), _pick(M, (256, 128, 64, 32, 16, 8))
    return _mm_call(x, dy, grid=(K // tm, N // tn, M // tk),
                    a_spec=pl.BlockSpec((tk, tm), lambda i, j, k: (k, i)),
                    b_spec=pl.BlockSpec((tk, tn), lambda i, j, k: (k, j)),
                    o_spec=pl.BlockSpec((tm, tn), lambda i, j, k: (i, j)),
                    o_shape=(K, N), acc_shape=(tm, tn), ta=True, tb=False, name=name)


@functools.partial(jax.custom_vjp, nondiff_argnums=(3,))
def linear(x, w, wz, name):
    return _mm_nn(x, w, name)


def _linear_fwd(x, w, wz, name):
    return _mm_nn(x, w, name), (x, w)


def _linear_bwd(name, res, dy):
    x, w = res
    return _mm_nt(dy, w, name + "_dx"), jnp.zeros_like(w), _mm_tn(x, dy, name + "_dw")


linear.defvjp(_linear_fwd, _linear_bwd)


@functools.partial(jax.custom_vjp, nondiff_argnums=(2,))
def linear_x(x, w, name):
    return _mm_nn(x, w, name)


def _linear_x_fwd(x, w, name):
    return _mm_nn(x, w, name), (w,)


def _linear_x_bwd(name, res, dy):
    return _mm_nt(dy, res[0], name + "_dx"), jnp.zeros_like(res[0])


linear_x.defvjp(_linear_x_fwd, _linear_x_bwd)


_BD_ROWS = (256, 128, 64, 32, 16, 8)


def _bd_call(a, b, nblk, kind, name):
    M = a.shape[0]
    tm = _pick(M, _BD_ROWS)
    if kind == "tn":
        aj, bj = a.shape[1] // nblk, b.shape[1] // nblk
        o_shape, o_spec = (nblk * aj, bj), pl.BlockSpec((nblk * aj, bj), lambda i: (0, 0))
        b_spec = pl.BlockSpec((tm, b.shape[1]), lambda i: (i, 0))
    else:
        aj = a.shape[1] // nblk
        wj = b.shape[0] // nblk
        oj = b.shape[1] if kind == "nn" else wj
        o_shape, o_spec = (M, nblk * oj), pl.BlockSpec((tm, nblk * oj), lambda i: (i, 0))
        b_spec = pl.BlockSpec(b.shape, lambda i: (0, 0))

    def body(a_ref, b_ref, o_ref):
        if kind == "tn":
            @pl.when(pl.program_id(0) == 0)
            def _():
                o_ref[...] = jnp.zeros_like(o_ref)

        for j in range(nblk):
            if kind == "nn":
                o_ref[:, j * oj:(j + 1) * oj] = _dot(a_ref[:, j * aj:(j + 1) * aj], b_ref[j * wj:(j + 1) * wj, :], _DN_NN)
            elif kind == "nt":
                o_ref[:, j * oj:(j + 1) * oj] = _dot(a_ref[:, j * aj:(j + 1) * aj], b_ref[j * wj:(j + 1) * wj, :], _DN_NT)
            else:
                o_ref[j * aj:(j + 1) * aj, :] += _dot(a_ref[:, j * aj:(j + 1) * aj], b_ref[:, j * bj:(j + 1) * bj], _DN_TN)

    return pl.pallas_call(body, name=name, grid=(M // tm,), in_specs=[pl.BlockSpec((tm, a.shape[1]), lambda i: (i, 0)), b_spec],
                          out_specs=o_spec, out_shape=jax.ShapeDtypeStruct(o_shape, F32),
                          compiler_params=_params(("arbitrary",) if kind == "tn" else ("parallel",)))(a, b)


def _bd_nn(x, w, nblk, name):
    return _bd_call(x, w, nblk, "nn", name)


def _bd_nt(dy, w, nblk, name):
    return _bd_call(dy, w, nblk, "nt", name)


def _bd_tn(x, dy, nblk, name):
    return _bd_call(x, dy, nblk, "tn", name)


@functools.partial(jax.custom_vjp, nondiff_argnums=(2, 3))
def bd_linear(x, w, nblk, name):
    return _bd_nn(x, w, nblk, name)


def _bd_fwd(x, w, nblk, name):
    return _bd_nn(x, w, nblk, name), (x, w)


def _bd_bwd(nblk, name, res, dy):
    x, w = res
    return _bd_nt(dy, w, nblk, name + "_dx"), _bd_tn(x, dy, nblk, name + "_dw")


bd_linear.defvjp(_bd_fwd, _bd_bwd)


def _row_tile(n_ctx, n_all):
    return _pick(math.gcd(n_ctx, n_all), (256, 128, 64, 32, 16, 8))


def _by_group(ref, is_ctx):
    return jnp.where(is_ctx, ref[0:1, :], ref[1:2, :])


def _acc_by_group(ref, is_ctx, part):
    ref[0:1, :] += jnp.where(is_ctx, part, 0.0)
    ref[1:2, :] += jnp.where(is_ctx, 0.0, part)


def _norm_fwd_call(x, g, shift, scale, n_ctx, name):
    M, D = x.shape
    has_mod = shift is not None
    tm = _row_tile(n_ctx, M) if has_mod else _pick(M, (256, 128, 64, 32, 16, 8))
    nct = n_ctx // tm

    def body(*refs):
        if has_mod:
            x_ref, g_ref, sh_ref, sc_ref, o_ref = refs
        else:
            x_ref, g_ref, o_ref = refs
        xv = x_ref[...]
        r = lax.rsqrt(jnp.mean(xv * xv, axis=-1, keepdims=True) + EPS)
        y = xv * r * g_ref[...]
        if has_mod:
            is_ctx = pl.program_id(0) < nct
            y = y * (1.0 + _by_group(sc_ref, is_ctx)) + _by_group(sh_ref, is_ctx)
        o_ref[...] = y

    row = pl.BlockSpec((tm, D), lambda i: (i, 0))
    vec = pl.BlockSpec((1, D), lambda i: (0, 0))
    two = pl.BlockSpec((2, D), lambda i: (0, 0))
    args = (x, g) + ((shift, scale) if has_mod else ())
    return pl.pallas_call(body, name=name, grid=(M // tm,), in_specs=[row, vec] + ([two, two] if has_mod else []),
                          out_specs=row, out_shape=jax.ShapeDtypeStruct((M, D), F32),
                          compiler_params=_params(("parallel",)))(*args)


def _norm_bwd_call(x, g, shift, scale, dy, n_ctx, name):
    M, D = x.shape
    has_mod = shift is not None
    tm = _row_tile(n_ctx, M) if has_mod else _pick(M, (256, 128, 64, 32, 16, 8))
    nct = n_ctx // tm

    def body(*refs):
        if has_mod:
            x_ref, g_ref, sc_ref, dy_ref, dx_ref, dg_ref, dsh_ref, dsc_ref = refs
        else:
            x_ref, g_ref, dy_ref, dx_ref, dg_ref = refs
        i = pl.program_id(0)

        @pl.when(i == 0)
        def _():
            dg_ref[...] = jnp.zeros_like(dg_ref)
            if has_mod:
                dsh_ref[...] = jnp.zeros_like(dsh_ref)
                dsc_ref[...] = jnp.zeros_like(dsc_ref)

        xv, gv, dyv = x_ref[...], g_ref[...], dy_ref[...]
        r = lax.rsqrt(jnp.mean(xv * xv, axis=-1, keepdims=True) + EPS)
        xhat = xv * r
        if has_mod:
            is_ctx = i < nct
            dy0 = dyv * (1.0 + _by_group(sc_ref, is_ctx))
            _acc_by_group(dsc_ref, is_ctx, jnp.sum(dyv * xhat * gv, axis=0, keepdims=True))
            _acc_by_group(dsh_ref, is_ctx, jnp.sum(dyv, axis=0, keepdims=True))
        else:
            dy0 = dyv
        dg_ref[...] += jnp.sum(dy0 * xhat, axis=0, keepdims=True)
        dxhat = dy0 * gv
        dx_ref[...] = r * (dxhat - xhat * jnp.mean(dxhat * xhat, axis=-1, keepdims=True))

    row = pl.BlockSpec((tm, D), lambda i: (i, 0))
    vec = pl.BlockSpec((1, D), lambda i: (0, 0))
    two = pl.BlockSpec((2, D), lambda i: (0, 0))
    if has_mod:
        args, in_specs = (x, g, scale, dy), [row, vec, two, row]
        out_specs = [row, vec, two, two]
        out_shape = [jax.ShapeDtypeStruct((M, D), F32), jax.ShapeDtypeStruct((1, D), F32),
                     jax.ShapeDtypeStruct((2, D), F32), jax.ShapeDtypeStruct((2, D), F32)]
    else:
        args, in_specs = (x, g, dy), [row, vec, row]
        out_specs = [row, vec]
        out_shape = [jax.ShapeDtypeStruct((M, D), F32), jax.ShapeDtypeStruct((1, D), F32)]
    return pl.pallas_call(body, name=name, grid=(M // tm,), in_specs=in_specs, out_specs=out_specs, out_shape=out_shape,
                          compiler_params=_params(("arbitrary",)))(*args)


@functools.partial(jax.custom_vjp, nondiff_argnums=(4, 5))
def norm_mod(x, g, shift, scale, n_ctx, name):
    return _norm_fwd_call(x, g, shift, scale, n_ctx, name)


def _norm_mod_fwd(x, g, shift, scale, n_ctx, name):
    return _norm_fwd_call(x, g, shift, scale, n_ctx, name), (x, g, shift, scale)


def _norm_mod_bwd(n_ctx, name, res, dy):
    x, g, shift, scale = res
    dx, dg, dsh, dsc = _norm_bwd_call(x, g, shift, scale, dy, n_ctx, name + "_bwd")
    return dx, dg, dsh, dsc


norm_mod.defvjp(_norm_mod_fwd, _norm_mod_bwd)


@functools.partial(jax.custom_vjp, nondiff_argnums=(2,))
def rmsnorm(x, g, name):
    return _norm_fwd_call(x, g, None, None, 0, name)


def _rmsnorm_fwd(x, g, name):
    return _norm_fwd_call(x, g, None, None, 0, name), (x, g)


def _rmsnorm_bwd(name, res, dy):
    x, g = res
    dx, dg = _norm_bwd_call(x, g, None, None, dy, 0, name + "_bwd")
    return dx, dg


rmsnorm.defvjp(_rmsnorm_fwd, _rmsnorm_bwd)


def _gres_fwd_call(h, o, gate, coef, n_ctx, name):
    M, D = h.shape
    tm = _row_tile(n_ctx, M)
    nct = n_ctx // tm

    def body(h_ref, o_ref, g_ref, out_ref):
        is_ctx = pl.program_id(0) < nct
        out_ref[...] = h_ref[...] + coef * _by_group(g_ref, is_ctx) * o_ref[...]

    row = pl.BlockSpec((tm, D), lambda i: (i, 0))
    two = pl.BlockSpec((2, D), lambda i: (0, 0))
    return pl.pallas_call(body, name=name, grid=(M // tm,), in_specs=[row, row, two], out_specs=row,
                          out_shape=jax.ShapeDtypeStruct((M, D), F32), compiler_params=_params(("parallel",)))(h, o, gate)


def _gres_bwd_call(o, gate, d, coef, n_ctx, name):
    M, D = o.shape
    tm = _row_tile(n_ctx, M)
    nct = n_ctx // tm

    def body(o_ref, g_ref, d_ref, do_ref, dg_ref):
        i = pl.program_id(0)
        is_ctx = i < nct

        @pl.when(i == 0)
        def _():
            dg_ref[...] = jnp.zeros_like(dg_ref)

        dv = d_ref[...]
        do_ref[...] = coef * _by_group(g_ref, is_ctx) * dv
        _acc_by_group(dg_ref, is_ctx, coef * jnp.sum(dv * o_ref[...], axis=0, keepdims=True))

    row = pl.BlockSpec((tm, D), lambda i: (i, 0))
    two = pl.BlockSpec((2, D), lambda i: (0, 0))
    return pl.pallas_call(body, name=name, grid=(M // tm,), in_specs=[row, two, row], out_specs=[row, two],
                          out_shape=[jax.ShapeDtypeStruct((M, D), F32), jax.ShapeDtypeStruct((2, D), F32)],
                          compiler_params=_params(("arbitrary",)))(o, gate, d)


@functools.partial(jax.custom_vjp, nondiff_argnums=(3, 4, 5))
def gated_residual(h, o, gate, coef, n_ctx, name):
    return _gres_fwd_call(h, o, gate, coef, n_ctx, name)


def _gres_fwd(h, o, gate, coef, n_ctx, name):
    return _gres_fwd_call(h, o, gate, coef, n_ctx, name), (o, gate)


def _gres_bwd(coef, n_ctx, name, res, d):
    o, gate = res
    do, dg = _gres_bwd_call(o, gate, d, coef, n_ctx, name + "_bwd")
    return d, do, dg


gated_residual.defvjp(_gres_fwd, _gres_bwd)


def _swiglu_fwd_call(ab, name):
    M, F2 = ab.shape
    Fh = F2 // 2
    tm, tn = _pick(M, (128, 64, 32, 16, 8)), Fh
    nf = Fh // tn

    def body(a_ref, b_ref, o_ref):
        a = a_ref[...]
        o_ref[...] = a * jax.nn.sigmoid(a) * b_ref[...]

    return pl.pallas_call(body, name=name, grid=(M // tm, nf),
                          in_specs=[pl.BlockSpec((tm, tn), lambda i, j: (i, j)), pl.BlockSpec((tm, tn), lambda i, j: (i, j + nf))],
                          out_specs=pl.BlockSpec((tm, tn), lambda i, j: (i, j)),
                          out_shape=jax.ShapeDtypeStruct((M, Fh), F32), compiler_params=_params(("parallel", "parallel")))(ab, ab)


def _swiglu_bwd_call(ab, dact, name):
    M, F2 = ab.shape
    Fh = F2 // 2
    tm = _pick(M, (128, 64, 32, 16, 8))

    def body(ab_ref, d_ref, o_ref):
        a, b, d = ab_ref[:, 0:Fh], ab_ref[:, Fh:F2], d_ref[...]
        sig = jax.nn.sigmoid(a)
        o_ref[:, 0:Fh] = d * b * sig * (1.0 + a * (1.0 - sig))
        o_ref[:, Fh:F2] = d * a * sig

    return pl.pallas_call(body, name=name, grid=(M // tm,),
                          in_specs=[pl.BlockSpec((tm, F2), lambda i: (i, 0)), pl.BlockSpec((tm, Fh), lambda i: (i, 0))],
                          out_specs=pl.BlockSpec((tm, F2), lambda i: (i, 0)),
                          out_shape=jax.ShapeDtypeStruct((M, F2), F32), compiler_params=_params(("parallel",)))(ab, dact)


@functools.partial(jax.custom_vjp, nondiff_argnums=(1,))
def swiglu_act(ab, name):
    return _swiglu_fwd_call(ab, name)


def _swiglu_fwd(ab, name):
    return _swiglu_fwd_call(ab, name), (ab,)


def _swiglu_bwd(name, res, d):
    return (_swiglu_bwd_call(res[0], d, name + "_bwd"),)


swiglu_act.defvjp(_swiglu_fwd, _swiglu_bwd)


def _mix_call(gl, mla, zz, gqa, d, name):
    M, D = mla.shape
    tm = _pick(M, (128, 64, 32, 16, 8))
    bwd = d is not None

    def body(*refs):
        gl_ref, mla_ref, zz_ref, gqa_ref = refs[:4]
        g0, g1, g2 = (jax.nn.sigmoid(gl_ref[:, i * D:(i + 1) * D]) for i in range(3))
        za, sb = zz_ref[:, 0:D], jax.nn.sigmoid(zz_ref[:, D:2 * D])
        ssm = za * sb
        if not bwd:
            refs[4][...] = g0 * mla_ref[...] + g1 * ssm + g2 * gqa_ref[...]
            return
        d_ref, dgl_ref, dmla_ref, dzz_ref, dgqa_ref = refs[4:]
        dv = d_ref[...]
        dmla_ref[...] = g0 * dv
        dgqa_ref[...] = g2 * dv
        dssm = g1 * dv
        dzz_ref[:, 0:D] = dssm * sb
        dzz_ref[:, D:2 * D] = dssm * ssm * (1.0 - sb)
        dgl_ref[:, 0:D] = dv * mla_ref[...] * g0 * (1.0 - g0)
        dgl_ref[:, D:2 * D] = dv * ssm * g1 * (1.0 - g1)
        dgl_ref[:, 2 * D:3 * D] = dv * gqa_ref[...] * g2 * (1.0 - g2)

    def rows(w):
        return pl.BlockSpec((tm, w), lambda i: (i, 0))

    def sds(w):
        return jax.ShapeDtypeStruct((M, w), F32)

    in_specs, args = [rows(3 * D), rows(D), rows(2 * D), rows(D)], (gl, mla, zz, gqa)
    if bwd:
        return pl.pallas_call(body, name=name, grid=(M // tm,), in_specs=in_specs + [rows(D)],
                              out_specs=[rows(3 * D), rows(D), rows(2 * D), rows(D)],
                              out_shape=[sds(3 * D), sds(D), sds(2 * D), sds(D)],
                              compiler_params=_params(("parallel",)))(*args, d)
    return pl.pallas_call(body, name=name, grid=(M // tm,), in_specs=in_specs, out_specs=rows(D), out_shape=sds(D),
                          compiler_params=_params(("parallel",)))(*args)


@functools.partial(jax.custom_vjp, nondiff_argnums=(4,))
def gated_mix(gl, mla, zz, gqa, name):
    return _mix_call(gl, mla, zz, gqa, None, name)


def _gated_mix_fwd(gl, mla, zz, gqa, name):
    return _mix_call(gl, mla, zz, gqa, None, name), (gl, mla, zz, gqa)


def _gated_mix_bwd(name, res, d):
    return tuple(_mix_call(*res, d, name + "_bwd"))


gated_mix.defvjp(_gated_mix_fwd, _gated_mix_bwd)


_DN_NT = (((1,), (1,)), ((), ()))
_DN_TN = (((0,), (0,)), ((), ()))
_DN_NN = (((1,), (0,)), ((), ()))


def _dot(a, b, dn):
    return lax.dot_general(a.astype(MXU_DTYPE), b.astype(MXU_DTYPE), dn, preferred_element_type=F32)


_TQ = (1024, 512, 256, 128, 64, 32, 16, 8)


def _flash_fwd_call(q, k1, v1, k2, v2, name):
    H, Tq, dk = q.shape
    T1, dv = k1.shape[1], v1.shape[2]
    has2 = k2 is not None
    tq, tk = _pick(Tq, _TQ), _pick(T1, _TQ)
    off = 1 if has2 else 0
    nkv = T1 // tk + off
    C = k2.shape[1] if has2 else 0
    rows = max(tk, C)

    def body(*refs):
        if has2:
            q_ref, k1_ref, v1_ref, k2_ref, v2_ref, o_ref, lse_ref, m_s, acc_s, va_s = refs
        else:
            q_ref, k1_ref, v1_ref, o_ref, lse_ref, m_s, acc_s, va_s = refs
        j = pl.program_id(2)

        @pl.when(j == 0)
        def _():
            m_s[...] = jnp.full_like(m_s, NEG_INF)
            acc_s[...] = jnp.zeros_like(acc_s)
            va_s[:, dv:2 * dv] = jnp.ones((rows, dv), MXU_DTYPE)

        def step(k, v, n):
            va_s[0:n, 0:dv] = v.astype(MXU_DTYPE)
            s = _dot(q_ref[0], k, _DN_NT)
            m_prev = m_s[...]
            m_new = jnp.maximum(m_prev, jnp.max(s, axis=-1, keepdims=True))
            p = jnp.exp2(s - m_new)
            acc_s[...] = jnp.exp2(m_prev - m_new) * acc_s[...] + _dot(p, va_s[0:n, :], _DN_NN)
            m_s[...] = m_new

        if has2:
            @pl.when(j == 0)
            def _():
                step(k2_ref[0], v2_ref[0], C)

            @pl.when(j > 0)
            def _():
                step(k1_ref[0], v1_ref[0], tk)
        else:
            step(k1_ref[0], v1_ref[0], tk)

        @pl.when(j == nkv - 1)
        def _():
            l = acc_s[:, dv:dv + 1]
            o_ref[0] = acc_s[:, 0:dv] / l
            lse_ref[0] = m_s[...] + jnp.log2(l)

    qs = pl.BlockSpec((1, tq, dk), lambda h, i, j: (h, i, 0))
    k1s = pl.BlockSpec((1, tk, dk), lambda h, i, j: (h, jnp.maximum(j - off, 0), 0))
    v1s = pl.BlockSpec((1, tk, dv), lambda h, i, j: (h, jnp.maximum(j - off, 0), 0))
    in_specs, args = [qs, k1s, v1s], [q, k1, v1]
    if has2:
        in_specs += [pl.BlockSpec((1, C, dk), lambda h, i, j: (h, 0, 0)), pl.BlockSpec((1, C, dv), lambda h, i, j: (h, 0, 0))]
        args += [k2, v2]
    return pl.pallas_call(
        body, name=name, grid=(H, Tq // tq, nkv), in_specs=in_specs,
        out_specs=[pl.BlockSpec((1, tq, dv), lambda h, i, j: (h, i, 0)), pl.BlockSpec((1, tq, 1), lambda h, i, j: (h, i, 0))],
        out_shape=[jax.ShapeDtypeStruct((H, Tq, dv), F32), jax.ShapeDtypeStruct((H, Tq, 1), F32)],
        scratch_shapes=[pltpu.VMEM((tq, 1), F32), pltpu.VMEM((tq, 2 * dv), F32), pltpu.VMEM((rows, 2 * dv), MXU_DTYPE)],
        compiler_params=_params(("parallel", "parallel", "arbitrary")))(*args)


def _flash_bwd_call(q, k1, v1, k2, v2, o, lse, do, name):
    H, Tq, dk = q.shape
    T1, dv = k1.shape[1], v1.shape[2]
    has2 = k2 is not None
    tq, tk = _pick(Tq, _TQ), _pick(T1, _TQ)
    off = 1 if has2 else 0
    nkv, nq = T1 // tk + off, Tq // tq
    C = k2.shape[1] if has2 else 0
    rows = max(tk, C)

    def body(*refs):
        if has2:
            (q_ref, k1_ref, v1_ref, k2_ref, v2_ref, o_ref, lse_ref, do_ref,
             dq_ref, dk1_ref, dv1_ref, dk2_ref, dv2_ref, dk_s, dv_s) = refs
        else:
            q_ref, k1_ref, v1_ref, o_ref, lse_ref, do_ref, dq_ref, dk1_ref, dv1_ref, dk_s, dv_s = refs
        j, i = pl.program_id(1), pl.program_id(2)

        @pl.when((j == 0) & (i == 0))
        def _():
            dq_ref[...] = jnp.zeros_like(dq_ref)

        @pl.when(i == 0)
        def _():
            dk_s[...] = jnp.zeros_like(dk_s)
            dv_s[...] = jnp.zeros_like(dv_s)

        def step(k, v, n):
            qb, dob = q_ref[0], do_ref[0]
            p = jnp.exp2(_dot(qb, k, _DN_NT) - lse_ref[0])
            dv_s[0:n, :] += _dot(p, dob, _DN_TN)
            dol = dob * LN2
            ds = p * (_dot(dol, v, _DN_NT) - jnp.sum(dol * o_ref[0], axis=-1, keepdims=True))
            dk_s[0:n, :] += _dot(ds, qb, _DN_TN)
            r0 = pl.multiple_of(i * tq, tq)
            dq_ref[0, pl.ds(r0, tq), :] += _dot(ds, k, _DN_NN)

        if has2:
            @pl.when(j == 0)
            def _():
                step(k2_ref[0], v2_ref[0], C)

            @pl.when(j > 0)
            def _():
                step(k1_ref[0], v1_ref[0], tk)

            @pl.when((i == nq - 1) & (j == 0))
            def _():
                dk2_ref[0] = dk_s[0:C, :]
                dv2_ref[0] = dv_s[0:C, :]

            @pl.when((i == nq - 1) & (j > 0))
            def _():
                dk1_ref[0] = dk_s[0:tk, :]
                dv1_ref[0] = dv_s[0:tk, :]
        else:
            step(k1_ref[0], v1_ref[0], tk)

            @pl.when(i == nq - 1)
            def _():
                dk1_ref[0] = dk_s[...]
                dv1_ref[0] = dv_s[...]

    qs = pl.BlockSpec((1, tq, dk), lambda h, j, i: (h, i, 0))
    os_ = pl.BlockSpec((1, tq, dv), lambda h, j, i: (h, i, 0))
    ls = pl.BlockSpec((1, tq, 1), lambda h, j, i: (h, i, 0))
    k1s = pl.BlockSpec((1, tk, dk), lambda h, j, i: (h, jnp.maximum(j - off, 0), 0))
    v1s = pl.BlockSpec((1, tk, dv), lambda h, j, i: (h, jnp.maximum(j - off, 0), 0))
    in_specs, args = [qs, k1s, v1s], [q, k1, v1]
    out_specs = [pl.BlockSpec((1, Tq, dk), lambda h, j, i: (h, 0, 0)), k1s, v1s]
    out_shape = [jax.ShapeDtypeStruct((H, Tq, dk), F32), jax.ShapeDtypeStruct((H, T1, dk), F32),
                 jax.ShapeDtypeStruct((H, T1, dv), F32)]
    if has2:
        k2s = pl.BlockSpec((1, C, dk), lambda h, j, i: (h, 0, 0))
        v2s = pl.BlockSpec((1, C, dv), lambda h, j, i: (h, 0, 0))
        in_specs += [k2s, v2s]
        args += [k2, v2]
        out_specs += [k2s, v2s]
        out_shape += [jax.ShapeDtypeStruct((H, C, dk), F32), jax.ShapeDtypeStruct((H, C, dv), F32)]
    in_specs += [os_, ls, os_]
    args += [o, lse, do]
    return pl.pallas_call(
        body, name=name, grid=(H, nkv, nq), in_specs=in_specs, out_specs=out_specs, out_shape=out_shape,
        scratch_shapes=[pltpu.VMEM((rows, dk), F32), pltpu.VMEM((rows, dv), F32)],
        compiler_params=_params(("parallel", "arbitrary", "arbitrary")))(*args)


@functools.partial(jax.custom_vjp, nondiff_argnums=(5,))
def flash2(q, k1, v1, k2, v2, name):
    return _flash_fwd_call(q, k1, v1, k2, v2, name)[0]


def _flash2_fwd(q, k1, v1, k2, v2, name):
    o, lse = _flash_fwd_call(q, k1, v1, k2, v2, name)
    return o, (q, k1, v1, k2, v2, o, lse)


def _flash2_bwd(name, res, do):
    q, k1, v1, k2, v2, o, lse = res
    return tuple(_flash_bwd_call(q, k1, v1, k2, v2, o, lse, do, name + "_bwd"))


flash2.defvjp(_flash2_fwd, _flash2_bwd)


@functools.partial(jax.custom_vjp, nondiff_argnums=(3,))
def flash1(q, k, v, name):
    return _flash_fwd_call(q, k, v, None, None, name)[0]


def _flash1_fwd(q, k, v, name):
    o, lse = _flash_fwd_call(q, k, v, None, None, name)
    return o, (q, k, v, o, lse)


def _flash1_bwd(name, res, do):
    q, k, v, o, lse = res
    return tuple(_flash_bwd_call(q, k, v, None, None, o, lse, do, name + "_bwd"))


flash1.defvjp(_flash1_fwd, _flash1_bwd)


def _gqa_call(q, k, v, sink_rows, n_ctx, res, name):
    KV, G, M, d = q.shape
    B, C = BLOCK, n_ctx
    assert WINDOW == BLOCK
    nt, nct, R = M // B, n_ctx // B, G * B
    scale = d ** -0.5
    bwd = res is not None

    def body(*refs):
        q_ref, kc_ref, vc_ref, k0, k1, k2, v0, v1, v2, sink_ref, o_ref, lse_ref = refs[:12]
        i = pl.program_id(1)
        qb = q_ref[0].reshape(R, d)
        a = lax.broadcasted_iota(jnp.int32, (R, B), 0) % B
        b = lax.broadcasted_iota(jnp.int32, (R, B), 1)
        kb, vb, masks = (k0[0], k1[0], k2[0]), (v0[0], v1[0], v2[0]), []
        for r in (-1, 0, 1):
            in_range = (i >= nct) & (i + r >= nct) & (i + r <= nt - 1)
            masks.append(in_range & (a <= b) if r == -1 else (in_range & (a >= b) if r == 1 else in_range & (a >= 0)))
        kc, vc, sink = kc_ref[0], vc_ref[0], sink_ref[0]
        s_c = _dot(qb, kc, _DN_NT) * scale
        s_b = [jnp.where(masks[t], _dot(qb, kb[t], _DN_NT) * scale, NEG_INF) for t in range(3)]
        if not bwd:
            m = jnp.maximum(sink, jnp.max(s_c, axis=-1, keepdims=True))
            for s in s_b:
                m = jnp.maximum(m, jnp.max(s, axis=-1, keepdims=True))
            e_c, e_b = jnp.exp(s_c - m), [jnp.exp(s - m) for s in s_b]
            den = jnp.exp(sink - m) + jnp.sum(e_c, axis=-1, keepdims=True)
            for e in e_b:
                den = den + jnp.sum(e, axis=-1, keepdims=True)
            inv = 1.0 / den
            o = _dot(e_c * inv, vc, _DN_NN)
            for t in range(3):
                o = o + _dot(e_b[t] * inv, vb[t], _DN_NN)
            o_ref[0] = o.reshape(G, B, d)
            lse_ref[0] = (m + jnp.log(den)).reshape(G, B, 1)
            return
        do_ref, dq_ref, dkc_ref, dvc_ref, dkb_ref, dvb_ref, dsink_ref = refs[12:]

        @pl.when(i == 0)
        def _():
            dkc_ref[...] = jnp.zeros_like(dkc_ref)
            dvc_ref[...] = jnp.zeros_like(dvc_ref)
            dsink_ref[...] = jnp.zeros_like(dsink_ref)

        lse, dob = lse_ref[0].reshape(R, 1), do_ref[0].reshape(R, d)
        delta = jnp.sum(dob * o_ref[0].reshape(R, d), axis=-1, keepdims=True)
        p_c = jnp.exp(s_c - lse)
        ds_c = p_c * (_dot(dob, vc, _DN_NT) - delta) * scale
        dq = _dot(ds_c, kc, _DN_NN)
        dkc_ref[0] += _dot(ds_c, qb, _DN_TN)
        dvc_ref[0] += _dot(p_c, dob, _DN_TN)
        for t in range(3):
            p = jnp.exp(s_b[t] - lse)
            ds = p * (_dot(dob, vb[t], _DN_NT) - delta) * scale
            dq = dq + _dot(ds, kb[t], _DN_NN)
            dkb_ref[0, 0, t] = _dot(ds, qb, _DN_TN)
            dvb_ref[0, 0, t] = _dot(p, dob, _DN_TN)
        dq_ref[0] = dq.reshape(G, B, d)
        dsink_ref[0] -= jnp.exp(sink - lse) * delta

    def band(r):
        return lambda h, i: (h, jnp.clip(i + r, nct, nt - 1), 0)

    qs = pl.BlockSpec((1, G, B, d), lambda h, i: (h, 0, i, 0))
    ls = pl.BlockSpec((1, G, B, 1), lambda h, i: (h, 0, i, 0))
    cs = pl.BlockSpec((1, C, d), lambda h, i: (h, 0, 0))
    ss = pl.BlockSpec((1, R, 1), lambda h, i: (h, 0, 0))
    bs = [pl.BlockSpec((1, B, d), band(r)) for r in (-1, 0, 1)]
    in_specs = [qs, cs, cs] + bs + bs + [ss]
    args = [q, k, v, k, k, k, v, v, v, sink_rows]
    if not bwd:
        return pl.pallas_call(body, name=name, grid=(KV, nt), in_specs=in_specs, out_specs=[qs, ls],
                              out_shape=[jax.ShapeDtypeStruct((KV, G, M, d), F32), jax.ShapeDtypeStruct((KV, G, M, 1), F32)],
                              compiler_params=_params(("parallel", "parallel")))(*args)
    o, lse, do = res
    part = pl.BlockSpec((1, 1, 3, B, d), lambda h, i: (h, i, 0, 0, 0))
    part_shape = jax.ShapeDtypeStruct((KV, nt, 3, B, d), F32)
    return pl.pallas_call(body, name=name, grid=(KV, nt), in_specs=in_specs + [qs, ls, qs],
                          out_specs=[qs, cs, cs, part, part, ss],
                          out_shape=[jax.ShapeDtypeStruct((KV, G, M, d), F32), jax.ShapeDtypeStruct((KV, C, d), F32),
                                     jax.ShapeDtypeStruct((KV, C, d), F32), part_shape, part_shape,
                                     jax.ShapeDtypeStruct((KV, R, 1), F32)],
                          compiler_params=_params(("parallel", "arbitrary")))(*args, o, lse, do)


@functools.partial(jax.custom_vjp, nondiff_argnums=(4, 5))
def gqa_core(q, k, v, sink_rows, n_ctx, name):
    return _gqa_call(q, k, v, sink_rows, n_ctx, None, name)[0]


def _gqa_core_fwd(q, k, v, sink_rows, n_ctx, name):
    o, lse = _gqa_call(q, k, v, sink_rows, n_ctx, None, name)
    return o, (q, k, v, sink_rows, o, lse)


def _gqa_core_bwd(n_ctx, name, res, do):
    q, k, v, sink_rows, o, lse = res
    dq, dkc, dvc, dkb, dvb, dsink = _gqa_call(q, k, v, sink_rows, n_ctx, (o, lse, do), name + "_bwd")

    def keys(ctx_part, band_part):
        zero = jnp.zeros_like(band_part[:, :1, 0])
        blocks = (jnp.concatenate([band_part[:, 1:, 0], zero], axis=1) + band_part[:, :, 1]
                  + jnp.concatenate([zero, band_part[:, :-1, 2]], axis=1))
        rows = blocks.reshape(k.shape)
        return jnp.concatenate([rows[:, :n_ctx] + ctx_part, rows[:, n_ctx:]], axis=1)

    return dq, keys(dkc, dkb), keys(dvc, dvb), dsink


gqa_core.defvjp(_gqa_core_fwd, _gqa_core_bwd)


def gqa_attention(gq, gk, gv, sink, n_ctx, name):
    M, H, d = gq.shape
    G = H // GQA_KV_HEADS
    q4 = gq.reshape(M, GQA_KV_HEADS, G, d).transpose(1, 2, 0, 3)
    sink_rows = jnp.repeat(sink.reshape(GQA_KV_HEADS, G), BLOCK, axis=1)[..., None]
    o = gqa_core(q4, gk.transpose(1, 0, 2), gv.transpose(1, 0, 2), sink_rows, n_ctx, name)
    return o.transpose(2, 0, 1, 3).reshape(M, H * d)


def _scan_call(a, x, s, *, rev, adj, n_ctx, name):
    M, W = x.shape
    cw = SSM_CW
    J = W // (2 * cw)
    L = _row_tile(n_ctx, M)
    nt, nc = M // L, n_ctx // L
    asc = rev == adj
    sub = min(SCAN_SUB, L)
    nsub = L // sub
    n_steps = int(math.log2(sub))
    assert 1 << n_steps == sub

    def tile(t):
        if not rev:
            return nt - 1 - t if adj else t
        if not adj:
            return jnp.where(t < nc, nc - 1 - t, nt - 1 - (t - nc))
        return jnp.where(t < nt - nc, nc + t, t - (nt - nc))

    def body(*refs):
        if adj:
            a_ref, x_ref, s_ref, o_ref, da_ref, car_ref = refs
        else:
            a_ref, x_ref, o_ref, car_ref = refs
        t = pl.program_id(1)

        @pl.when(t == 0)
        def _():
            car_ref[...] = jnp.zeros_like(car_ref)
            if adj:
                da_ref[...] = jnp.zeros_like(da_ref)

        ar, ai = a_ref[:, 0:cw], a_ref[:, cw:2 * cw]
        powers, pr, pi = [], ar, ai
        for _ in range(n_steps):
            powers.append((pr, pi))
            pr, pi = pr * pr - pi * pi, 2.0 * pr * pi
        row = lax.broadcasted_iota(jnp.int32, (sub, cw), 0)
        first, last = (0, sub - 1) if asc else (sub - 1, 0)

        def scan_rows(i, carry):
            cr, ci = carry[0], carry[1]
            r0 = pl.multiple_of((i if asc else nsub - 1 - i) * sub, sub)
            xr, xi = x_ref[pl.ds(r0, sub), 0:cw], x_ref[pl.ds(r0, sub), cw:2 * cw]
            xr = xr + jnp.where(row == first, ar * cr - ai * ci, 0.0)
            xi = xi + jnp.where(row == first, ar * ci + ai * cr, 0.0)
            k = 1
            for pr, pi in powers:
                if asc:
                    sr, si, keep = pltpu.roll(xr, k, 0), pltpu.roll(xi, k, 0), row >= k
                else:
                    sr, si, keep = pltpu.roll(xr, sub - k, 0), pltpu.roll(xi, sub - k, 0), row < sub - k
                sr, si = jnp.where(keep, sr, 0.0), jnp.where(keep, si, 0.0)
                xr, xi = xr + pr * sr - pi * si, xi + pr * si + pi * sr
                k *= 2
            o_ref[pl.ds(r0, sub), 0:cw] = xr
            o_ref[pl.ds(r0, sub), cw:2 * cw] = xi
            out = (jnp.sum(jnp.where(row == last, xr, 0.0), axis=0, keepdims=True),
                   jnp.sum(jnp.where(row == last, xi, 0.0), axis=0, keepdims=True))
            if adj:
                if asc:
                    gr, gi = pltpu.roll(xr, 1, 0), pltpu.roll(xi, 1, 0)
                else:
                    gr, gi = pltpu.roll(xr, sub - 1, 0), pltpu.roll(xi, sub - 1, 0)
                gr, gi = jnp.where(row == first, cr, gr), jnp.where(row == first, ci, gi)
                sr, si = s_ref[pl.ds(r0, sub), 0:cw], s_ref[pl.ds(r0, sub), cw:2 * cw]
                out += (carry[2] + jnp.sum(sr * gr + si * gi, axis=0, keepdims=True),
                        carry[3] + jnp.sum(sr * gi - si * gr, axis=0, keepdims=True))
            return out

        init = (car_ref[:, 0:cw], car_ref[:, cw:2 * cw])
        if adj:
            init += (jnp.zeros((1, cw), F32), jnp.zeros((1, cw), F32))
        done = lax.fori_loop(0, nsub, scan_rows, init)
        car_ref[:, 0:cw] = done[0]
        car_ref[:, cw:2 * cw] = done[1]
        if adj:
            da_ref[:, 0:cw] += done[2]
            da_ref[:, cw:2 * cw] += done[3]

    blk = pl.BlockSpec((L, 2 * cw), lambda j, t: (tile(t), j))
    vec = pl.BlockSpec((1, 2 * cw), lambda j, t: (0, j))
    if adj:
        in_specs, args = [vec, blk, blk], (a, x, s)
        out_specs = [blk, vec]
        out_shape = [jax.ShapeDtypeStruct((M, W), F32), jax.ShapeDtypeStruct((1, W), F32)]
    else:
        in_specs, args = [vec, blk], (a, x)
        out_specs = blk
        out_shape = jax.ShapeDtypeStruct((M, W), F32)
    return pl.pallas_call(body, name=name, grid=(J, nt), in_specs=in_specs, out_specs=out_specs, out_shape=out_shape,
                          scratch_shapes=[pltpu.VMEM((1, 2 * cw), F32)],
                          compiler_params=_params(("parallel", "arbitrary")))(*args)


def _conj_layout(a):
    cw = SSM_CW
    J = a.shape[1] // (2 * cw)
    a4 = a.reshape(1, J, 2, cw)
    return jnp.concatenate([a4[:, :, 0:1], -a4[:, :, 1:2]], axis=2).reshape(a.shape)


@functools.partial(jax.custom_vjp, nondiff_argnums=(2, 3, 4))
def diag_scan(a, x, rev, n_ctx, name):
    return _scan_call(a, x, None, rev=rev, adj=False, n_ctx=n_ctx, name=name)


def _diag_scan_fwd(a, x, rev, n_ctx, name):
    s = _scan_call(a, x, None, rev=rev, adj=False, n_ctx=n_ctx, name=name)
    return s, (a, s)


def _diag_scan_bwd(rev, n_ctx, name, res, ds):
    a, s = res
    g, da = _scan_call(_conj_layout(a), ds, s, rev=rev, adj=True, n_ctx=n_ctx, name=name + "_adj")
    return da, g


diag_scan.defvjp(_diag_scan_fwd, _diag_scan_bwd)


def _loss_call(h, g, target, name):
    M, D = h.shape
    tm = _pick(M, (256, 128, 64, 32, 16, 8))

    def body(h_ref, g_ref, t_ref, loss_ref, dh_ref, dg_ref):
        i = pl.program_id(0)

        @pl.when(i == 0)
        def _():
            loss_ref[...] = jnp.zeros_like(loss_ref)
            dg_ref[...] = jnp.zeros_like(dg_ref)

        xv, gv = h_ref[...], g_ref[...]
        r = lax.rsqrt(jnp.mean(xv * xv, axis=-1, keepdims=True) + EPS)
        xhat = xv * r
        err = xhat * gv - t_ref[...]
        loss_ref[...] += 0.5 * jnp.sum(jnp.mean(err * err, axis=-1, keepdims=True), axis=0, keepdims=True)
        dy = err * (1.0 / D)
        dg_ref[...] += jnp.sum(dy * xhat, axis=0, keepdims=True)
        dxhat = dy * gv
        dh_ref[...] = r * (dxhat - xhat * jnp.mean(dxhat * xhat, axis=-1, keepdims=True))

    row = pl.BlockSpec((tm, D), lambda i: (i, 0))
    vec = pl.BlockSpec((1, D), lambda i: (0, 0))
    one = pl.BlockSpec((1, 1), lambda i: (0, 0))
    return pl.pallas_call(body, name=name, grid=(M // tm,), in_specs=[row, vec, row], out_specs=[one, row, vec],
                          out_shape=[jax.ShapeDtypeStruct((1, 1), F32), jax.ShapeDtypeStruct((M, D), F32),
                                     jax.ShapeDtypeStruct((1, D), F32)],
                          compiler_params=_params(("arbitrary",)))(h, g, target)


@functools.partial(jax.custom_vjp, nondiff_argnums=(3,))
def loss_head(h, g, target, name):
    return _loss_call(h, g, target, name)[0][0, 0]


def _loss_head_fwd(h, g, target, name):
    loss, dh, dg = _loss_call(h, g, target, name)
    return loss[0, 0], (dh, dg, target)


def _loss_head_bwd(name, res, ct):
    dh, dg, target = res
    return ct * dh, ct * dg, jnp.zeros_like(target)


loss_head.defvjp(_loss_head_fwd, _loss_head_bwd)


def _adamw_call(w, gstack, m, v, name):
    R, Cn = w.shape
    n = gstack.shape[0]
    tr = _pick(R, (64, 32, 16, 8))

    def body(w_ref, g_ref, m_ref, v_ref, go_ref, d_ref, mo_ref, vo_ref):
        g = g_ref[0]
        for s in range(1, n):
            g = g + g_ref[s]
        mn = ADAM_B1 * m_ref[...] + (1.0 - ADAM_B1) * g
        vn = ADAM_B2 * v_ref[...] + (1.0 - ADAM_B2) * (g * g)
        m_hat = mn / (1.0 - ADAM_B1 ** ADAM_STEP)
        v_hat = vn / (1.0 - ADAM_B2 ** ADAM_STEP)
        go_ref[...] = g
        d_ref[...] = -ADAM_LR * (m_hat / (jnp.sqrt(v_hat) + ADAM_EPS) + ADAM_WD * w_ref[...])
        mo_ref[...] = mn
        vo_ref[...] = vn

    blk = pl.BlockSpec((tr, Cn), lambda i: (i, 0))
    gblk = pl.BlockSpec((n, tr, Cn), lambda i: (0, i, 0))
    sds = jax.ShapeDtypeStruct((R, Cn), F32)
    return pl.pallas_call(body, name=name, grid=(R // tr,), in_specs=[blk, gblk, blk, blk], out_specs=[blk] * 4,
                          out_shape=[sds] * 4, compiler_params=_params(("parallel",)))(w, gstack, m, v)


MESH = pl.DeviceIdType.MESH
ANY = pl.BlockSpec(memory_space=pl.ANY)


def _place():
    return lax.axis_index("x"), lax.axis_index("y"), lax.axis_index("c")


def _sibling_exchange(v, name):
    n = v.shape[0]

    def body(v_ref, o_ref, send_sems, recv_sems):
        x, y, c = _place()
        copies = [pltpu.make_async_remote_copy(src_ref=v_ref.at[k], dst_ref=o_ref.at[k], send_sem=send_sems.at[k],
                                               recv_sem=recv_sems.at[k], device_id=(x, y, 1 - c), device_id_type=MESH)
                  for k in range(n)]
        for cp in copies:
            cp.start()
        for cp in copies:
            cp.wait()

    return pl.pallas_call(body, name=name, in_specs=[ANY], out_specs=ANY, out_shape=jax.ShapeDtypeStruct(v.shape, v.dtype),
                          scratch_shapes=[pltpu.SemaphoreType.DMA((n,)), pltpu.SemaphoreType.DMA((n,))])(v)


def _sibling_exchange_half(v, name):
    n = v.shape[0]

    def body(v_ref, o_ref, send_sems, recv_sems):
        x, y, c = _place()
        copies = [pltpu.make_async_remote_copy(src_ref=v_ref.at[k, pl.ds(1 - c, 1)], dst_ref=o_ref.at[k],
                                               send_sem=send_sems.at[k], recv_sem=recv_sems.at[k],
                                               device_id=(x, y, 1 - c), device_id_type=MESH)
                  for k in range(n)]
        for cp in copies:
            cp.start()
        for cp in copies:
            cp.wait()

    return pl.pallas_call(body, name=name, in_specs=[ANY], out_specs=ANY,
                          out_shape=jax.ShapeDtypeStruct((n, 1) + v.shape[2:], v.dtype),
                          scratch_shapes=[pltpu.SemaphoreType.DMA((n,)), pltpu.SemaphoreType.DMA((n,))])(v)


def _chip_exchange(v, same, name):
    out_shape = (N_CHIPS,) + (v.shape if same else v.shape[1:])

    def body(v_ref, o_ref, send_sems, recv_sems, local_sem):
        x, y, c = _place()
        me = 2 * x + y
        own = pltpu.make_async_copy(v_ref if same else v_ref.at[me], o_ref.at[me], local_sem)
        own.start()
        copies = []
        for k, (fx, fy) in enumerate(((1, 0), (0, 1), (1, 1))):
            px, py = jnp.where(fx == 1, 1 - x, x), jnp.where(fy == 1, 1 - y, y)
            src = v_ref if same else v_ref.at[2 * px + py]
            copies.append(pltpu.make_async_remote_copy(src_ref=src, dst_ref=o_ref.at[me], send_sem=send_sems.at[k],
                                                       recv_sem=recv_sems.at[k], device_id=(px, py, c), device_id_type=MESH))
        for cp in copies:
            cp.start()
        for cp in copies:
            cp.wait()
        own.wait()

    return pl.pallas_call(body, name=name, in_specs=[ANY], out_specs=ANY, out_shape=jax.ShapeDtypeStruct(out_shape, v.dtype),
                          scratch_shapes=[pltpu.SemaphoreType.DMA((3,)), pltpu.SemaphoreType.DMA((3,)), pltpu.SemaphoreType.DMA])(v)


def _all_gather(v, name):
    def body(v_ref, o_ref, send_sems, recv_sems, local_sem):
        x, y, c = _place()
        me = 4 * x + 2 * y + c
        own = pltpu.make_async_copy(v_ref, o_ref.at[me], local_sem)
        own.start()
        copies = []
        for k in range(1, 8):
            fx, fy, fc = (k >> 2) & 1, (k >> 1) & 1, k & 1
            peer = (jnp.where(fx == 1, 1 - x, x), jnp.where(fy == 1, 1 - y, y), jnp.where(fc == 1, 1 - c, c))
            copies.append(pltpu.make_async_remote_copy(src_ref=v_ref, dst_ref=o_ref.at[me], send_sem=send_sems.at[k - 1],
                                                       recv_sem=recv_sems.at[k - 1], device_id=peer, device_id_type=MESH))
        for cp in copies:
            cp.start()
        for cp in copies:
            cp.wait()
        own.wait()

    return pl.pallas_call(body, name=name, in_specs=[ANY], out_specs=ANY,
                          out_shape=jax.ShapeDtypeStruct((8,) + v.shape, v.dtype),
                          scratch_shapes=[pltpu.SemaphoreType.DMA((7,)), pltpu.SemaphoreType.DMA((7,)), pltpu.SemaphoreType.DMA])(v)


def _add_own_half(g, r, c, name):
    n, _, R, W = g.shape
    tr = _pick(R, (256, 128, 64, 32, 16, 8))

    def body(c_ref, g_ref, r_ref, o_ref):
        o_ref[...] = g_ref[0] + r_ref[0]

    grid_spec = pltpu.PrefetchScalarGridSpec(
        num_scalar_prefetch=1, grid=(n, R // tr),
        in_specs=[pl.BlockSpec((1, 1, tr, W), lambda p, i, c_ref: (p, c_ref[0], i, 0)),
                  pl.BlockSpec((1, 1, tr, W), lambda p, i, c_ref: (p, 0, i, 0))],
        out_specs=pl.BlockSpec((1, tr, W), lambda p, i, c_ref: (p, i, 0)))
    return pl.pallas_call(body, name=name, grid_spec=grid_spec, out_shape=jax.ShapeDtypeStruct((n, R, W), F32),
                          compiler_params=_params(("parallel", "parallel")))(c.reshape(1).astype(jnp.int32), g, r)


def _sum_stack(v, name):
    n, R, W = v.shape
    tr = _pick(R, (256, 128, 64, 32, 16, 8))

    def body(v_ref, o_ref):
        acc = v_ref[0]
        for s in range(1, n):
            acc = acc + v_ref[s]
        o_ref[...] = acc

    return pl.pallas_call(body, name=name, grid=(R // tr,), in_specs=[pl.BlockSpec((n, tr, W), lambda i: (0, i, 0))],
                          out_specs=pl.BlockSpec((tr, W), lambda i: (i, 0)), out_shape=jax.ShapeDtypeStruct((R, W), F32),
                          compiler_params=_params(("parallel",)))(v)


def _flat_rows(n, mult):
    rows = -(-n // FLAT_W)
    return -(-rows // mult) * mult


def _by_core(a, b, c):
    return jnp.where(c == 0, a, b)


def _lane_padded(n):
    return -(-n // LANES) * LANES


def _lane_pad(a):
    pad = _lane_padded(a.shape[-1]) - a.shape[-1]
    return jnp.pad(a, [(0, 0)] * (a.ndim - 1) + [(0, pad)]) if pad else a


def gather_weights(shards):
    _, _, c = _place()
    flat = jnp.concatenate([_lane_pad(shards[n].astype(WIRE_DTYPE)).reshape(-1) for n in BIG_NAMES])
    n_flat = flat.shape[0]
    rh = _flat_rows(n_flat, 32) // 2
    flat = jnp.pad(flat, (0, 2 * rh * FLAT_W - n_flat)).reshape(2, rh, FLAT_W)
    mine = lax.dynamic_index_in_dim(flat, c, axis=0, keepdims=False)
    got = _chip_exchange(mine, True, "gather_chips")
    other = _sibling_exchange(got, "gather_sibling")
    halves = jnp.stack([_by_core(got, other, c), _by_core(other, got, c)], axis=1)
    allflat = halves.reshape(N_CHIPS, 2 * rh * FLAT_W)
    out, off = {}, 0
    for name, axis in BIG:
        shp = shards[name].shape
        padded = shp[:-1] + (_lane_padded(shp[-1]),)
        size = math.prod(padded)
        parts = allflat[:, off:off + size].reshape((N_CHIPS,) + padded)[..., :shp[-1]]
        out[name] = [jnp.concatenate([parts[p, l] for p in range(N_CHIPS)], axis=axis - 1) for l in range(shp[0])]
        off += size
    return out


def scatter_gradients(grads, shards):
    _, _, c = _place()
    cols = []
    for name, axis in SCATTERED:
        parts = [jnp.split(g, N_CHIPS, axis=axis - 1) for g in grads[name]]
        cols.append(jnp.stack([_lane_pad(jnp.stack([per_layer[p] for per_layer in parts])).reshape(-1) for p in range(N_CHIPS)]))
    flat = jnp.concatenate(cols, axis=1)
    n_flat = flat.shape[1]
    rh = _flat_rows(n_flat, 16) // 2
    flat = jnp.pad(flat, ((0, 0), (0, 2 * rh * FLAT_W - n_flat))).reshape(N_CHIPS, 2, rh, FLAT_W)
    theirs = _sibling_exchange_half(flat, "scatter_sibling")
    pair = _add_own_half(flat, theirs, c, "scatter_pair_sum")
    got = _chip_exchange(pair, False, "scatter_chips")
    mine = _sum_stack(got, "scatter_chip_sum")
    other = _sibling_exchange(mine.reshape(1, rh, FLAT_W), "scatter_halves").reshape(rh, FLAT_W)
    full = jnp.stack([_by_core(mine, other, c), _by_core(other, mine, c)]).reshape(-1)
    out, off = {}, 0
    for name, _ in SCATTERED:
        shp = shards[name].shape
        padded = shp[:-1] + (_lane_padded(shp[-1]),)
        size = math.prod(padded)
        out[name] = full[off:off + size].reshape(padded)[..., :shp[-1]]
        off += size
    return out


def _rope_tables(n_ctx, n_lat, n_heads, head_dim, start, n):
    t = jnp.arange(n_lat, dtype=jnp.int32)
    zero = jnp.zeros((n_ctx,), jnp.int32)
    row = jnp.concatenate([zero, t // GRID_W]).astype(F32)
    col = jnp.concatenate([zero, t % GRID_W]).astype(F32)
    half, q = n // 2, n // 4
    j = jnp.arange(n_heads * head_dim, dtype=jnp.int32) % head_dim - start
    inside = (j >= 0) & (j < n)
    u = j % half
    inv_cols = jnp.where(inside, ROPE_BASE ** (-(2 * (u % q)).astype(F32) / half), 0.0)
    ang = (row[:, None] * jnp.where(j // half == 0, inv_cols, 0.0)[None, :]
           + col[:, None] * jnp.where(j // half == 1, inv_cols, 0.0)[None, :])
    first = (inside & (u // q == 0)).astype(F32)[None, :]
    second = (inside & (u // q == 1)).astype(F32)[None, :]
    return jnp.cos(ang), -jnp.sin(ang) * first, jnp.sin(ang) * second, q


def _rope(x, table):
    cos, minus_sin, plus_sin, q = table
    return x * cos + jnp.roll(x, -q, axis=1) * minus_sin + jnp.roll(x, q, axis=1) * plus_sin


def _cmul(ar, ai, br, bi):
    return ar * br - ai * bi, ar * bi + ai * br


def _ssm_discretize(lam_re, lam_im, log_dt, b_re, b_im):
    dt = jnp.exp(log_dt)[:, None]
    mag = jnp.exp(lam_re * dt)
    a_re, a_im = mag * jnp.cos(lam_im * dt), mag * jnp.sin(lam_im * dt)
    den = lam_re * lam_re + lam_im * lam_im
    w_re = ((a_re - 1) * lam_re + a_im * lam_im) / den
    w_im = (a_im * lam_re - (a_re - 1) * lam_im) / den
    bb_re, bb_im = _cmul(w_re[..., None], w_im[..., None], b_re, b_im)
    return a_re, a_im, bb_re, bb_im


def _ssm_layouts(a_re, a_im, bb_re, bb_im, c_re, c_im):
    J, g8, P, Mg = SSM_CHUNKS, SSM_CHUNK_GROUPS, SSM_STATE, SSM_GROUP
    eye = jnp.eye(g8, dtype=F32)
    a = jnp.stack([a_re.reshape(J, g8 * P), a_im.reshape(J, g8 * P)], axis=1).reshape(1, J * 2 * g8 * P)
    bb = jnp.stack([bb_re, bb_im]).reshape(2, J, g8, P, Mg)
    w_drive = jnp.einsum('rjgpm,gh->jgmrhp', bb, eye).reshape(J * g8 * Mg, 2 * g8 * P)
    cc = jnp.stack([c_re, -c_im]).reshape(2, J, g8, Mg, P)
    w_read = jnp.einsum('rjgmp,gh->jrhpgm', cc, eye).reshape(J * 2 * g8 * P, g8 * Mg)
    return a, w_drive, w_read


def _w_in_layout(d_model):
    widths = (("cq", MLA_Q_RANK), ("ckv", MLA_KV_RANK), ("kr", MLA_ROPE), ("u", SSM_WIDTH), ("gq", GQA_HEADS * GQA_HEAD_DIM),
              ("gk", GQA_KV_HEADS * GQA_HEAD_DIM), ("gv", GQA_KV_HEADS * GQA_HEAD_DIM), ("gates", N_BRANCH * d_model))
    out, src, dst = [], 0, 0
    for name, w in widths:
        out.append((name, src, dst, w))
        src += w
        dst += -(-w // LANES) * LANES
    return out, src, dst


def _pad_w_in(w, d_model):
    lay, _, _ = _w_in_layout(d_model)
    parts = []
    for _, src, _, wd in lay:
        seg = w[..., src:src + wd]
        pad = -(-wd // LANES) * LANES - wd
        parts.append(jnp.pad(seg, [(0, 0)] * (w.ndim - 1) + [(0, pad)]) if pad else seg)
    return jnp.concatenate(parts, axis=-1)


def _unpad_w_in(w, d_model):
    lay, _, _ = _w_in_layout(d_model)
    return jnp.concatenate([w[..., dst:dst + wd] for _, _, dst, wd in lay], axis=-1)


@functools.partial(jax.custom_vjp, nondiff_argnums=(1,))
def split_cols(proj, bounds):
    return tuple(proj[:, s:s + w] for s, w in bounds[0])


def _split_cols_fwd(proj, bounds):
    return split_cols(proj, bounds), None


def _split_cols_bwd(bounds, _, cts):
    segments, total = bounds
    rows, pieces, pos = cts[0].shape[0], [], 0
    for (s, w), ct in zip(segments, cts):
        if s > pos:
            pieces.append(jnp.zeros((rows, s - pos), ct.dtype))
        pieces.append(ct)
        pos = s + w
    if pos < total:
        pieces.append(jnp.zeros((rows, total - pos), cts[0].dtype))
    return (jnp.concatenate(pieces, axis=1),)


split_cols.defvjp(_split_cols_fwd, _split_cols_bwd)


def _layer(hall, lw, lz, sp, cs8, n_ctx, ropes):
    M, D = hall.shape
    C = n_ctx

    def lin(x, name):
        return linear(x, lw[name], lz[name], name)

    mod_all = linear_x(cs8, lw["ada_w"], "ada_w")[0:2] + sp["ada_b"][None, :] + lz["ada_tap"]
    mod = [mod_all[:, i * D:(i + 1) * D] for i in range(N_MOD)]

    def ffn(h, tag, norm_g, sh, sc, gate):
        hn = norm_mod(h, norm_g[None, :], sh, sc, C, tag + "_norm")
        act = swiglu_act(lin(hn, tag + "_w13"), tag + "_act")
        return gated_residual(h, lin(act, tag + "_w2"), gate, 0.5, C, tag + "_res")

    hall = ffn(hall, "ffn1", sp["norm_ffn1"], mod[0], mod[1], mod[2])

    xm = norm_mod(hall, sp["norm_mix"][None, :], mod[3], mod[4], C, "mix_norm")
    proj = lin(xm, "w_in")
    lay, _, total = _w_in_layout(D)
    seg = dict(zip([name for name, _, _, _ in lay], split_cols(proj, (tuple((dst, wd) for _, _, dst, wd in lay), total))))

    q = _rope(lin(rmsnorm(seg["cq"], sp["mla_q_norm"][None, :], "mla_q_norm"), "mla_w_uq"), ropes["mla_q"])
    q = (q * ((MLA_NOPE + MLA_ROPE) ** -0.5 * LOG2E)).reshape(M, MLA_HEADS, MLA_NOPE + MLA_ROPE)
    kv = lin(rmsnorm(seg["ckv"], sp["mla_kv_norm"][None, :], "mla_kv_norm"), "mla_w_ukv").reshape(M, MLA_HEADS, MLA_NOPE + MLA_V)
    kr = _rope(seg["kr"], ropes["mla_kr"])[:, None, :]
    k = jnp.concatenate([kv[..., :MLA_NOPE], jnp.broadcast_to(kr, (M, MLA_HEADS, MLA_ROPE))], axis=-1)

    def heads(t, lo, hi):
        return t[lo:hi].transpose(1, 0, 2)

    v = kv[..., MLA_NOPE:]
    k_ctx, v_ctx = heads(k, 0, C), heads(v, 0, C)
    o_lat = flash2(heads(q, C, M), heads(k, C, M), heads(v, C, M), k_ctx, v_ctx, "mla_lat")
    o_ctx = flash1(heads(q, 0, C), k_ctx, v_ctx, "mla_ctx")
    o = jnp.concatenate([o_ctx, o_lat], axis=1).transpose(1, 0, 2).reshape(M, MLA_HEADS * MLA_V)
    mla = lin(o, "mla_w_o")

    u = seg["u"]
    y = u * sp["ssm_d"][None, :]
    for direction in range(2):
        a_re, a_im, bb_re, bb_im = _ssm_discretize(sp["ssm_lambda_re"][direction], sp["ssm_lambda_im"][direction],
                                                   sp["ssm_log_dt"][direction], sp["ssm_b_re"][direction],
                                                   sp["ssm_b_im"][direction])
        a, w_drive, w_read = _ssm_layouts(a_re, a_im, bb_re, bb_im, sp["ssm_c_re"][direction], sp["ssm_c_im"][direction])
        drive = bd_linear(u, w_drive, SSM_CHUNKS, "ssm_drive%d" % direction)
        states = diag_scan(a, drive, direction == 1, C, "ssm_scan%d" % direction)
        y = y + bd_linear(states, w_read, SSM_CHUNKS, "ssm_read%d" % direction)
    zz = lin(jax.nn.gelu(y), "ssm_w_glu")

    gq = _rope(seg["gq"], ropes["gqa_q"]).reshape(M, GQA_HEADS, GQA_HEAD_DIM)
    gk = _rope(seg["gk"], ropes["gqa_k"]).reshape(M, GQA_KV_HEADS, GQA_HEAD_DIM)
    gv = seg["gv"].reshape(M, GQA_KV_HEADS, GQA_HEAD_DIM)
    gqa = lin(gqa_attention(gq, gk, gv, sp["gqa_sink"], C, "gqa"), "gqa_w_o")

    mixed = gated_mix(seg["gates"], mla, zz, gqa, "mix_gate")
    hall = gated_residual(hall, lin(mixed, "w_out"), mod[5], 1.0, C, "mix_res")

    return ffn(hall, "ffn2", sp["norm_ffn2"], mod[6], mod[7], mod[8])


PER_LAYER_SMALL = tuple(n for n in SMALL if n not in ("c_ctx", "final_norm"))


def _loss_fn(diff, x, c, ctx, target, whole):
    zeros, small, x = diff
    T, D = x.shape
    C = ctx.shape[0]
    ropes = {"mla_q": _rope_tables(C, T, MLA_HEADS, MLA_NOPE + MLA_ROPE, MLA_NOPE, MLA_ROPE),
             "mla_kr": _rope_tables(C, T, 1, MLA_ROPE, 0, MLA_ROPE),
             "gqa_q": _rope_tables(C, T, GQA_HEADS, GQA_HEAD_DIM, 0, GQA_HEAD_DIM),
             "gqa_k": _rope_tables(C, T, GQA_KV_HEADS, GQA_HEAD_DIM, 0, GQA_HEAD_DIM)}
    cs8 = jnp.pad(_cond_rows(small["c_ctx"], c), ((0, 6), (0, 0)))
    hall = jnp.concatenate([ctx, x], axis=0)

    for layer in range(len(zeros["ada_tap"])):
        hall = _layer(hall, {n: whole[n][layer] for n in BIG_NAMES}, {n: zeros[n][layer] for n in zeros},
                      {n: small[n][layer] for n in PER_LAYER_SMALL}, cs8, C, ropes)
    return loss_head(hall[C:], small["final_norm"][None, :], target, "loss_head")


def _cond_rows(c_ctx, c):
    return jax.nn.silu(jnp.stack([c_ctx, c]))


def kernel(*args):
    given = dict(zip(ARG_NAMES + ['loss_target'] + ['m_' + n for n in WEIGHTS] + ['v_' + n for n in WEIGHTS], args))
    x, c, ctx, target = given['x'][0], given['c'][0], given['ctx'][0], given['loss_target'][0]
    px, py, _ = _place()
    D = x.shape[-1]
    depth = given['ada_w'].shape[0]
    shards = {n: given[n] for n in BIG_NAMES}
    small = {n: given[n] for n in SMALL}

    whole = gather_weights(shards)
    whole["w_in"] = [_pad_w_in(w, D) for w in whole["w_in"]]
    zeros = {n: [jnp.zeros(w.shape, F32) for w in whole[n]] for n in SCATTERED_NAMES}
    zeros["ada_tap"] = [jnp.zeros((2, N_MOD * D), F32) for _ in range(depth)]

    loss, (gz, gsmall, gx) = jax.value_and_grad(_loss_fn)((zeros, small, x), x, c, ctx, target, whole)
    loss = lax.psum(loss, ("x", "y", "c"))
    taps = jnp.stack(gz.pop("ada_tap"))
    gz["w_in"] = [_unpad_w_in(g, D) for g in gz["w_in"]]
    gbig = scatter_gradients(gz, shards)

    extra = [taps.reshape(-1), _cond_rows(small["c_ctx"], c).reshape(-1)]
    n_small = sum(math.prod(given[n].shape) for n in SMALL)
    n_extra = sum(e.shape[0] for e in extra)

    def flat_small(d, tail=None):
        v = jnp.concatenate([d[n].reshape(-1) for n in SMALL] + (tail or [jnp.zeros((n_extra,), F32)]))
        rows = _flat_rows(v.shape[0], 8)
        return jnp.pad(v, (0, rows * FLAT_W - v.shape[0])).reshape(rows, FLAT_W)

    gathered = _all_gather(flat_small(gsmall, extra), "small_gather")
    outs = _adamw_call(flat_small(small), gathered, flat_small({n: given['m_' + n] for n in SMALL}),
                       flat_small({n: given['v_' + n] for n in SMALL}), "adamw_small")
    res, off = {}, 0
    for name in SMALL:
        shp = given[name].shape
        size = math.prod(shp)
        res[name] = [o.reshape(-1)[off:off + size].reshape(shp) for o in outs]
        off += size

    tails = gathered.reshape(8, -1)[:, n_small:n_small + n_extra]
    all_taps = tails[:, :taps.size].reshape(8, depth, 2, N_MOD * D)
    all_cs = tails[:, taps.size:].reshape(8 * 2, D)
    n_cols = given['ada_w'].shape[2]
    mine = lax.dynamic_slice_in_dim(all_taps, (2 * px + py) * n_cols, n_cols, axis=3)
    gbig["ada_w"] = jnp.stack([_mm_tn(all_cs, mine[:, layer].reshape(8 * 2, n_cols), "ada_w_dw") for layer in range(depth)])

    for name in BIG_NAMES:
        shp = given[name].shape
        two_d = (shp[0] * shp[1], shp[2])
        outs = _adamw_call(given[name].reshape(two_d), gbig[name].reshape((1,) + two_d), given['m_' + name].reshape(two_d),
                           given['v_' + name].reshape(two_d), "adamw_" + name)
        res[name] = [o.reshape(shp) for o in outs]

    return (loss, gx[None], *[res[n][0] for n in WEIGHTS], *[res[n][1] for n in WEIGHTS],
            *[res[n][2] for n in WEIGHTS], *[res[n][3] for n in WEIGHTS])
```

```python
import functools
import math

import jax
import jax.numpy as jnp
from jax import lax
from jax.experimental import pallas as pl
from jax.experimental.pallas import tpu as pltpu

F32 = jnp.float32
MXU_DTYPE = jnp.bfloat16
WIRE_DTYPE = jnp.bfloat16

GRID_W = 64
MLA_HEADS, MLA_NOPE, MLA_ROPE, MLA_V = 8, 64, 32, 64
MLA_Q_RANK, MLA_KV_RANK = 384, 256
SSM_WIDTH, SSM_GROUP, SSM_STATE = 512, 16, 64
SSM_GROUPS = SSM_WIDTH // SSM_GROUP
SSM_CHUNK_GROUPS = 8
SSM_CHUNKS = SSM_GROUPS // SSM_CHUNK_GROUPS
SSM_CW = SSM_CHUNK_GROUPS * SSM_STATE
SCAN_SUB = 32
GQA_HEADS, GQA_KV_HEADS, GQA_HEAD_DIM = 8, 2, 64
WINDOW, BLOCK = 128, 128
N_BRANCH, N_MOD = 3, 9
ROPE_BASE = 10000.0
EPS = 1e-6
NEG_INF = -1e30
LOG2E, LN2 = math.log2(math.e), math.log(2.0)
LANES = 128
FLAT_W = 1024

ADAM_LR, ADAM_B1, ADAM_B2, ADAM_EPS, ADAM_WD, ADAM_STEP = 0.001, 0.9, 0.999, 1e-08, 0.01, 10

VMEM_LIMIT = 48 * 1024 * 1024

ARG_NAMES = ['x', 'c', 'ctx', 'c_ctx', 'ada_w', 'ada_b', 'norm_ffn1', 'norm_mix', 'norm_ffn2', 'ffn1_w13', 'ffn1_w2', 'ffn2_w13', 'ffn2_w2', 'w_in', 'mla_q_norm', 'mla_kv_norm', 'mla_w_uq', 'mla_w_ukv', 'mla_w_o', 'ssm_lambda_re', 'ssm_lambda_im', 'ssm_log_dt', 'ssm_b_re', 'ssm_b_im', 'ssm_c_re', 'ssm_c_im', 'ssm_d', 'ssm_w_glu', 'gqa_sink', 'gqa_w_o', 'w_out', 'final_norm']
WEIGHTS = ARG_NAMES[3:]
BIG = (('ada_w', 2), ('ffn1_w13', 2), ('ffn1_w2', 1), ('ffn2_w13', 2), ('ffn2_w2', 1), ('w_in', 2), ('mla_w_uq', 2),
       ('mla_w_ukv', 2), ('mla_w_o', 2), ('ssm_w_glu', 2), ('gqa_w_o', 2), ('w_out', 1))
BIG_NAMES = tuple(n for n, _ in BIG)
SCATTERED = tuple((n, a) for n, a in BIG if n != 'ada_w')
SCATTERED_NAMES = tuple(n for n, _ in SCATTERED)
SMALL = tuple(n for n in WEIGHTS if n not in BIG_NAMES)
N_CHIPS = 4


def _pick(n, prefs):
    for p in prefs:
        if n % p == 0:
            return p
    return n


def _params(sem=None):
    return pltpu.CompilerParams(dimension_semantics=sem, vmem_limit_bytes=VMEM_LIMIT)


def _mm_call(a, b, *, grid, a_spec, b_spec, o_spec, o_shape, acc_shape, ta, tb, name, out_dtype=F32):
    nk = grid[2]
    dn = (((0 if ta else 1,), (1 if tb else 0,)), ((), ()))

    def body(a_ref, b_ref, o_ref, acc_ref):
        k = pl.program_id(2)

        @pl.when(k == 0)
        def _():
            acc_ref[...] = jnp.zeros_like(acc_ref)

        acc_ref[...] += lax.dot_general(a_ref[...].astype(MXU_DTYPE), b_ref[...].astype(MXU_DTYPE), dn,
                                        preferred_element_type=F32)

        @pl.when(k == nk - 1)
        def _():
            o_ref[...] = acc_ref[...].astype(o_ref.dtype)

    return pl.pallas_call(
        body, name=name, grid=grid, in_specs=[a_spec, b_spec], out_specs=o_spec,
        out_shape=jax.ShapeDtypeStruct(o_shape, out_dtype), scratch_shapes=[pltpu.VMEM(acc_shape, F32)],
        compiler_params=_params(("parallel", "parallel", "arbitrary")))(a, b)


_ROWS = (768, 512, 256, 128, 64, 32, 16, 8)
_WIDE = (1408, 1024, 512, 256, 128)
MAX_WHOLE = 2816


def _feat(n):
    return n if n <= _WIDE[0] else _pick(n, _WIDE)


def _mm_nn(x, w, name):
    M, K = x.shape
    N = w.shape[1]
    tm, tn = _pick(M, _ROWS), _pick(N, (512, 256, 128))
    tk = K if K <= MAX_WHOLE else _pick(K, (512, 256, 128))
    return _mm_call(x, w, grid=(M // tm, N // tn, K // tk),
                    a_spec=pl.BlockSpec((tm, tk), lambda i, j, k: (i, k)),
                    b_spec=pl.BlockSpec((tk, tn), lambda i, j, k: (k, j)),
                    o_spec=pl.BlockSpec((tm, tn), lambda i, j, k: (i, j)),
                    o_shape=(M, N), acc_shape=(tm, tn), ta=False, tb=False, name=name)


def _mm_nt(dy, w, name):
    M, N = dy.shape
    K = w.shape[0]
    tm, tn, tk = _pick(M, _ROWS), _feat(K), _feat(N)
    return _mm_call(dy, w, grid=(M // tm, K // tn, N // tk),
                    a_spec=pl.BlockSpec((tm, tk), lambda i, j, k: (i, k)),
                    b_spec=pl.BlockSpec((tn, tk), lambda i, j, k: (j, k)),
                    o_spec=pl.BlockSpec((tm, tn), lambda i, j, k: (i, j)),
                    o_shape=(M, K), acc_shape=(tm, tn), ta=False, tb=True, name=name)


def _mm_tn(x, dy, name):
    M, K = x.shape
    N = dy.shape[1]
    tm, tn, tk = _feat(K), _feat(N), _pick(M, _ROWS)
    return _mm_call(x, dy, grid=(K // tm, N // tn, M // tk),
                    a_spec=pl.BlockSpec((tk, tm), lambda i, j, k: (k, i)),
                    b_spec=pl.BlockSpec((tk, tn), lambda i, j, k: (k, j)),
                    o_spec=pl.BlockSpec((tm, tn), lambda i, j, k: (i, j)),
                    o_shape=(K, N), acc_shape=(tm, tn), ta=True, tb=False, name=name)


@functools.partial(jax.custom_vjp, nondiff_argnums=(3,))
def linear(x, w, wz, name):
    return _mm_nn(x, w, name)


def _linear_fwd(x, w, wz, name):
    return _mm_nn(x, w, name), (x, w)


def _linear_bwd(name, res, dy):
    x, w = res
    return _mm_nt(dy, w, name + "_dx"), jnp.zeros_like(w), _mm_tn(x, dy, name + "_dw")


linear.defvjp(_linear_fwd, _linear_bwd)


@functools.partial(jax.custom_vjp, nondiff_argnums=(2,))
def linear_x(x, w, name):
    return _mm_nn(x, w, name)


def _linear_x_fwd(x, w, name):
    return _mm_nn(x, w, name), (w,)


def _linear_x_bwd(name, res, dy):
    return _mm_nt(dy, res[0], name + "_dx"), jnp.zeros_like(res[0])


linear_x.defvjp(_linear_x_fwd, _linear_x_bwd)


_BD_ROWS = (256, 128, 64, 32, 16, 8)


def _bd_call(a, b, nblk, kind, name):
    M = a.shape[0]
    tm = _pick(M, _BD_ROWS)
    if kind == "tn":
        aj, bj = a.shape[1] // nblk, b.shape[1] // nblk
        o_shape, o_spec = (nblk * aj, bj), pl.BlockSpec((nblk * aj, bj), lambda i: (0, 0))
        b_spec = pl.BlockSpec((tm, b.shape[1]), lambda i: (i, 0))
    else:
        aj = a.shape[1] // nblk
        wj = b.shape[0] // nblk
        oj = b.shape[1] if kind == "nn" else wj
        o_shape, o_spec = (M, nblk * oj), pl.BlockSpec((tm, nblk * oj), lambda i: (i, 0))
        b_spec = pl.BlockSpec(b.shape, lambda i: (0, 0))

    def body(a_ref, b_ref, o_ref):
        if kind == "tn":
            @pl.when(pl.program_id(0) == 0)
            def _():
                o_ref[...] = jnp.zeros_like(o_ref)

        for j in range(nblk):
            if kind == "nn":
                o_ref[:, j * oj:(j + 1) * oj] = _dot(a_ref[:, j * aj:(j + 1) * aj], b_ref[j * wj:(j + 1) * wj, :], _DN_NN)
            elif kind == "nt":
                o_ref[:, j * oj:(j + 1) * oj] = _dot(a_ref[:, j * aj:(j + 1) * aj], b_ref[j * wj:(j + 1) * wj, :], _DN_NT)
            else:
                o_ref[j * aj:(j + 1) * aj, :] += _dot(a_ref[:, j * aj:(j + 1) * aj], b_ref[:, j * bj:(j + 1) * bj], _DN_TN)

    return pl.pallas_call(body, name=name, grid=(M // tm,), in_specs=[pl.BlockSpec((tm, a.shape[1]), lambda i: (i, 0)), b_spec],
                          out_specs=o_spec, out_shape=jax.ShapeDtypeStruct(o_shape, F32),
                          compiler_params=_params(("arbitrary",) if kind == "tn" else ("parallel",)))(a, b)


def _bd_nn(x, w, nblk, name):
    return _bd_call(x, w, nblk, "nn", name)


def _bd_nt(dy, w, nblk, name):
    return _bd_call(dy, w, nblk, "nt", name)


def _bd_tn(x, dy, nblk, name):
    return _bd_call(x, dy, nblk, "tn", name)


@functools.partial(jax.custom_vjp, nondiff_argnums=(2, 3))
def bd_linear(x, w, nblk, name):
    return _bd_nn(x, w, nblk, name)


def _bd_fwd(x, w, nblk, name):
    return _bd_nn(x, w, nblk, name), (x, w)


def _bd_bwd(nblk, name, res, dy):
    x, w = res
    return _bd_nt(dy, w, nblk, name + "_dx"), _bd_tn(x, dy, nblk, name + "_dw")


bd_linear.defvjp(_bd_fwd, _bd_bwd)


def _row_tile(n_ctx, n_all):
    return _pick(math.gcd(n_ctx, n_all), (256, 128, 64, 32, 16, 8))


def _by_group(ref, is_ctx):
    return jnp.where(is_ctx, ref[0:1, :], ref[1:2, :])


def _acc_by_group(ref, is_ctx, part):
    ref[0:1, :] += jnp.where(is_ctx, part, 0.0)
    ref[1:2, :] += jnp.where(is_ctx, 0.0, part)


def _norm_fwd_call(x, g, shift, scale, n_ctx, name):
    M, D = x.shape
    has_mod = shift is not None
    tm = _row_tile(n_ctx, M) if has_mod else _pick(M, (256, 128, 64, 32, 16, 8))
    nct = n_ctx // tm

    def body(*refs):
        if has_mod:
            x_ref, g_ref, sh_ref, sc_ref, o_ref = refs
        else:
            x_ref, g_ref, o_ref = refs
        xv = x_ref[...]
        r = lax.rsqrt(jnp.mean(xv * xv, axis=-1, keepdims=True) + EPS)
        y = xv * r * g_ref[...]
        if has_mod:
            is_ctx = pl.program_id(0) < nct
            y = y * (1.0 + _by_group(sc_ref, is_ctx)) + _by_group(sh_ref, is_ctx)
        o_ref[...] = y

    row = pl.BlockSpec((tm, D), lambda i: (i, 0))
    vec = pl.BlockSpec((1, D), lambda i: (0, 0))
    two = pl.BlockSpec((2, D), lambda i: (0, 0))
    args = (x, g) + ((shift, scale) if has_mod else ())
    return pl.pallas_call(body, name=name, grid=(M // tm,), in_specs=[row, vec] + ([two, two] if has_mod else []),
                          out_specs=row, out_shape=jax.ShapeDtypeStruct((M, D), F32),
                          compiler_params=_params(("parallel",)))(*args)


def _norm_bwd_call(x, g, shift, scale, dy, n_ctx, name):
    M, D = x.shape
    has_mod = shift is not None
    tm = _row_tile(n_ctx, M) if has_mod else _pick(M, (256, 128, 64, 32, 16, 8))
    nct = n_ctx // tm

    def body(*refs):
        if has_mod:
            x_ref, g_ref, sc_ref, dy_ref, dx_ref, dg_ref, dsh_ref, dsc_ref = refs
        else:
            x_ref, g_ref, dy_ref, dx_ref, dg_ref = refs
        i = pl.program_id(0)

        @pl.when(i == 0)
        def _():
            dg_ref[...] = jnp.zeros_like(dg_ref)
            if has_mod:
                dsh_ref[...] = jnp.zeros_like(dsh_ref)
                dsc_ref[...] = jnp.zeros_like(dsc_ref)

        xv, gv, dyv = x_ref[...], g_ref[...], dy_ref[...]
        r = lax.rsqrt(jnp.mean(xv * xv, axis=-1, keepdims=True) + EPS)
        xhat = xv * r
        if has_mod:
            is_ctx = i < nct
            dy0 = dyv * (1.0 + _by_group(sc_ref, is_ctx))
            _acc_by_group(dsc_ref, is_ctx, jnp.sum(dyv * xhat * gv, axis=0, keepdims=True))
            _acc_by_group(dsh_ref, is_ctx, jnp.sum(dyv, axis=0, keepdims=True))
        else:
            dy0 = dyv
        dg_ref[...] += jnp.sum(dy0 * xhat, axis=0, keepdims=True)
        dxhat = dy0 * gv
        dx_ref[...] = r * (dxhat - xhat * jnp.mean(dxhat * xhat, axis=-1, keepdims=True))

    row = pl.BlockSpec((tm, D), lambda i: (i, 0))
    vec = pl.BlockSpec((1, D), lambda i: (0, 0))
    two = pl.BlockSpec((2, D), lambda i: (0, 0))
    if has_mod:
        args, in_specs = (x, g, scale, dy), [row, vec, two, row]
        out_specs = [row, vec, two, two]
        out_shape = [jax.ShapeDtypeStruct((M, D), F32), jax.ShapeDtypeStruct((1, D), F32),
                     jax.ShapeDtypeStruct((2, D), F32), jax.ShapeDtypeStruct((2, D), F32)]
    else:
        args, in_specs = (x, g, dy), [row, vec, row]
        out_specs = [row, vec]
        out_shape = [jax.ShapeDtypeStruct((M, D), F32), jax.ShapeDtypeStruct((1, D), F32)]
    return pl.pallas_call(body, name=name, grid=(M // tm,), in_specs=in_specs, out_specs=out_specs, out_shape=out_shape,
                          compiler_params=_params(("arbitrary",)))(*args)


@functools.partial(jax.custom_vjp, nondiff_argnums=(4, 5))
def norm_mod(x, g, shift, scale, n_ctx, name):
    return _norm_fwd_call(x, g, shift, scale, n_ctx, name)


def _norm_mod_fwd(x, g, shift, scale, n_ctx, name):
    return _norm_fwd_call(x, g, shift, scale, n_ctx, name), (x, g, shift, scale)


def _norm_mod_bwd(n_ctx, name, res, dy):
    x, g, shift, scale = res
    dx, dg, dsh, dsc = _norm_bwd_call(x, g, shift, scale, dy, n_ctx, name + "_bwd")
    return dx, dg, dsh, dsc


norm_mod.defvjp(_norm_mod_fwd, _norm_mod_bwd)


@functools.partial(jax.custom_vjp, nondiff_argnums=(2,))
def rmsnorm(x, g, name):
    return _norm_fwd_call(x, g, None, None, 0, name)


def _rmsnorm_fwd(x, g, name):
    return _norm_fwd_call(x, g, None, None, 0, name), (x, g)


def _rmsnorm_bwd(name, res, dy):
    x, g = res
    dx, dg = _norm_bwd_call(x, g, None, None, dy, 0, name + "_bwd")
    return dx, dg


rmsnorm.defvjp(_rmsnorm_fwd, _rmsnorm_bwd)


def _gres_fwd_call(h, o, gate, coef, n_ctx, name):
    M, D = h.shape
    tm = _row_tile(n_ctx, M)
    nct = n_ctx // tm

    def body(h_ref, o_ref, g_ref, out_ref):
        is_ctx = pl.program_id(0) < nct
        out_ref[...] = h_ref[...] + coef * _by_group(g_ref, is_ctx) * o_ref[...]

    row = pl.BlockSpec((tm, D), lambda i: (i, 0))
    two = pl.BlockSpec((2, D), lambda i: (0, 0))
    return pl.pallas_call(body, name=name, grid=(M // tm,), in_specs=[row, row, two], out_specs=row,
                          out_shape=jax.ShapeDtypeStruct((M, D), F32), compiler_params=_params(("parallel",)))(h, o, gate)


def _gres_bwd_call(o, gate, d, coef, n_ctx, name):
    M, D = o.shape
    tm = _row_tile(n_ctx, M)
    nct = n_ctx // tm

    def body(o_ref, g_ref, d_ref, do_ref, dg_ref):
        i = pl.program_id(0)
        is_ctx = i < nct

        @pl.when(i == 0)
        def _():
            dg_ref[...] = jnp.zeros_like(dg_ref)

        dv = d_ref[...]
        do_ref[...] = coef * _by_group(g_ref, is_ctx) * dv
        _acc_by_group(dg_ref, is_ctx, coef * jnp.sum(dv * o_ref[...], axis=0, keepdims=True))

    row = pl.BlockSpec((tm, D), lambda i: (i, 0))
    two = pl.BlockSpec((2, D), lambda i: (0, 0))
    return pl.pallas_call(body, name=name, grid=(M // tm,), in_specs=[row, two, row], out_specs=[row, two],
                          out_shape=[jax.ShapeDtypeStruct((M, D), F32), jax.ShapeDtypeStruct((2, D), F32)],
                          compiler_params=_params(("arbitrary",)))(o, gate, d)


@functools.partial(jax.custom_vjp, nondiff_argnums=(3, 4, 5))
def gated_residual(h, o, gate, coef, n_ctx, name):
    return _gres_fwd_call(h, o, gate, coef, n_ctx, name)


def _gres_fwd(h, o, gate, coef, n_ctx, name):
    return _gres_fwd_call(h, o, gate, coef, n_ctx, name), (o, gate)


def _gres_bwd(coef, n_ctx, name, res, d):
    o, gate = res
    do, dg = _gres_bwd_call(o, gate, d, coef, n_ctx, name + "_bwd")
    return d, do, dg


gated_residual.defvjp(_gres_fwd, _gres_bwd)


def _swiglu_fwd_call(ab, name):
    M, F2 = ab.shape
    Fh = F2 // 2
    tm, tn = _pick(M, (128, 64, 32, 16, 8)), Fh
    nf = Fh // tn

    def body(a_ref, b_ref, o_ref):
        a = a_ref[...]
        o_ref[...] = a * jax.nn.sigmoid(a) * b_ref[...]

    return pl.pallas_call(body, name=name, grid=(M // tm, nf),
                          in_specs=[pl.BlockSpec((tm, tn), lambda i, j: (i, j)), pl.BlockSpec((tm, tn), lambda i, j: (i, j + nf))],
                          out_specs=pl.BlockSpec((tm, tn), lambda i, j: (i, j)),
                          out_shape=jax.ShapeDtypeStruct((M, Fh), F32), compiler_params=_params(("parallel", "parallel")))(ab, ab)


def _swiglu_bwd_call(ab, dact, name):
    M, F2 = ab.shape
    Fh = F2 // 2
    tm = _pick(M, (128, 64, 32, 16, 8))

    def body(ab_ref, d_ref, o_ref):
        a, b, d = ab_ref[:, 0:Fh], ab_ref[:, Fh:F2], d_ref[...]
        sig = jax.nn.sigmoid(a)
        o_ref[:, 0:Fh] = d * b * sig * (1.0 + a * (1.0 - sig))
        o_ref[:, Fh:F2] = d * a * sig

    return pl.pallas_call(body, name=name, grid=(M // tm,),
                          in_specs=[pl.BlockSpec((tm, F2), lambda i: (i, 0)), pl.BlockSpec((tm, Fh), lambda i: (i, 0))],
                          out_specs=pl.BlockSpec((tm, F2), lambda i: (i, 0)),
                          out_shape=jax.ShapeDtypeStruct((M, F2), F32), compiler_params=_params(("parallel",)))(ab, dact)


@functools.partial(jax.custom_vjp, nondiff_argnums=(1,))
def swiglu_act(ab, name):
    return _swiglu_fwd_call(ab, name)


def _swiglu_fwd(ab, name):
    return _swiglu_fwd_call(ab, name), (ab,)


def _swiglu_bwd(name, res, d):
    return (_swiglu_bwd_call(res[0], d, name + "_bwd"),)


swiglu_act.defvjp(_swiglu_fwd, _swiglu_bwd)


def _mix_call(gl, mla, zz, gqa, d, name):
    M, D = mla.shape
    tm = _pick(M, (128, 64, 32, 16, 8))
    bwd = d is not None

    def body(*refs):
        gl_ref, mla_ref, zz_ref, gqa_ref = refs[:4]
        g0, g1, g2 = (jax.nn.sigmoid(gl_ref[:, i * D:(i + 1) * D]) for i in range(3))
        za, sb = zz_ref[:, 0:D], jax.nn.sigmoid(zz_ref[:, D:2 * D])
        ssm = za * sb
        if not bwd:
            refs[4][...] = g0 * mla_ref[...] + g1 * ssm + g2 * gqa_ref[...]
            return
        d_ref, dgl_ref, dmla_ref, dzz_ref, dgqa_ref = refs[4:]
        dv = d_ref[...]
        dmla_ref[...] = g0 * dv
        dgqa_ref[...] = g2 * dv
        dssm = g1 * dv
        dzz_ref[:, 0:D] = dssm * sb
        dzz_ref[:, D:2 * D] = dssm * ssm * (1.0 - sb)
        dgl_ref[:, 0:D] = dv * mla_ref[...] * g0 * (1.0 - g0)
        dgl_ref[:, D:2 * D] = dv * ssm * g1 * (1.0 - g1)
        dgl_ref[:, 2 * D:3 * D] = dv * gqa_ref[...] * g2 * (1.0 - g2)

    def rows(w):
        return pl.BlockSpec((tm, w), lambda i: (i, 0))

    def sds(w):
        return jax.ShapeDtypeStruct((M, w), F32)

    in_specs, args = [rows(3 * D), rows(D), rows(2 * D), rows(D)], (gl, mla, zz, gqa)
    if bwd:
        return pl.pallas_call(body, name=name, grid=(M // tm,), in_specs=in_specs + [rows(D)],
                              out_specs=[rows(3 * D), rows(D), rows(2 * D), rows(D)],
                              out_shape=[sds(3 * D), sds(D), sds(2 * D), sds(D)],
                              compiler_params=_params(("parallel",)))(*args, d)
    return pl.pallas_call(body, name=name, grid=(M // tm,), in_specs=in_specs, out_specs=rows(D), out_shape=sds(D),
                          compiler_params=_params(("parallel",)))(*args)


@functools.partial(jax.custom_vjp, nondiff_argnums=(4,))
def gated_mix(gl, mla, zz, gqa, name):
    return _mix_call(gl, mla, zz, gqa, None, name)


def _gated_mix_fwd(gl, mla, zz, gqa, name):
    return _mix_call(gl, mla, zz, gqa, None, name), (gl, mla, zz, gqa)


def _gated_mix_bwd(name, res, d):
    return tuple(_mix_call(*res, d, name + "_bwd"))


gated_mix.defvjp(_gated_mix_fwd, _gated_mix_bwd)


_DN_NT = (((1,), (1,)), ((), ()))
_DN_TN = (((0,), (0,)), ((), ()))
_DN_NN = (((1,), (0,)), ((), ()))


def _dot(a, b, dn):
    return lax.dot_general(a.astype(MXU_DTYPE), b.astype(MXU_DTYPE), dn, preferred_element_type=F32)


_TQ = (1024, 512, 256, 128, 64, 32, 16, 8)


def _flash_fwd_call(q, k1, v1, k2, v2, name):
    H, Tq, dk = q.shape
    T1, dv = k1.shape[1], v1.shape[2]
    has2 = k2 is not None
    tq, tk = _pick(Tq, _TQ), _pick(T1, _TQ)
    off = 1 if has2 else 0
    nkv = T1 // tk + off
    C = k2.shape[1] if has2 else 0
    rows = max(tk, C)

    def body(*refs):
        if has2:
            q_ref, k1_ref, v1_ref, k2_ref, v2_ref, o_ref, lse_ref, m_s, acc_s, va_s = refs
        else:
            q_ref, k1_ref, v1_ref, o_ref, lse_ref, m_s, acc_s, va_s = refs
        j = pl.program_id(2)

        @pl.when(j == 0)
        def _():
            m_s[...] = jnp.full_like(m_s, NEG_INF)
            acc_s[...] = jnp.zeros_like(acc_s)
            va_s[:, dv:2 * dv] = jnp.ones((rows, dv), MXU_DTYPE)

        def step(k, v, n):
            va_s[0:n, 0:dv] = v.astype(MXU_DTYPE)
            s = _dot(q_ref[0], k, _DN_NT)
            m_prev = m_s[...]
            m_new = jnp.maximum(m_prev, jnp.max(s, axis=-1, keepdims=True))
            p = jnp.exp2(s - m_new)
            acc_s[...] = jnp.exp2(m_prev - m_new) * acc_s[...] + _dot(p, va_s[0:n, :], _DN_NN)
            m_s[...] = m_new

        if has2:
            @pl.when(j == 0)
            def _():
                step(k2_ref[0], v2_ref[0], C)

            @pl.when(j > 0)
            def _():
                step(k1_ref[0], v1_ref[0], tk)
        else:
            step(k1_ref[0], v1_ref[0], tk)

        @pl.when(j == nkv - 1)
        def _():
            l = acc_s[:, dv:dv + 1]
            o_ref[0] = acc_s[:, 0:dv] / l
            lse_ref[0] = m_s[...] + jnp.log2(l)

    qs = pl.BlockSpec((1, tq, dk), lambda h, i, j: (h, i, 0))
    k1s = pl.BlockSpec((1, tk, dk), lambda h, i, j: (h, jnp.maximum(j - off, 0), 0))
    v1s = pl.BlockSpec((1, tk, dv), lambda h, i, j: (h, jnp.maximum(j - off, 0), 0))
    in_specs, args = [qs, k1s, v1s], [q, k1, v1]
    if has2:
        in_specs += [pl.BlockSpec((1, C, dk), lambda h, i, j: (h, 0, 0)), pl.BlockSpec((1, C, dv), lambda h, i, j: (h, 0, 0))]
        args += [k2, v2]
    return pl.pallas_call(
        body, name=name, grid=(H, Tq // tq, nkv), in_specs=in_specs,
        out_specs=[pl.BlockSpec((1, tq, dv), lambda h, i, j: (h, i, 0)), pl.BlockSpec((1, tq, 1), lambda h, i, j: (h, i, 0))],
        out_shape=[jax.ShapeDtypeStruct((H, Tq, dv), F32), jax.ShapeDtypeStruct((H, Tq, 1), F32)],
        scratch_shapes=[pltpu.VMEM((tq, 1), F32), pltpu.VMEM((tq, 2 * dv), F32), pltpu.VMEM((rows, 2 * dv), MXU_DTYPE)],
        compiler_params=_params(("parallel", "parallel", "arbitrary")))(*args)


def _flash_bwd_call(q, k1, v1, k2, v2, o, lse, do, name):
    H, Tq, dk = q.shape
    T1, dv = k1.shape[1], v1.shape[2]
    has2 = k2 is not None
    tq, tk = _pick(Tq, _TQ), _pick(T1, _TQ)
    off = 1 if has2 else 0
    nkv, nq = T1 // tk + off, Tq // tq
    C = k2.shape[1] if has2 else 0
    rows = max(tk, C)

    def body(*refs):
        if has2:
            (q_ref, k1_ref, v1_ref, k2_ref, v2_ref, o_ref, lse_ref, do_ref,
             dq_ref, dk1_ref, dv1_ref, dk2_ref, dv2_ref, dk_s, dv_s) = refs
        else:
            q_ref, k1_ref, v1_ref, o_ref, lse_ref, do_ref, dq_ref, dk1_ref, dv1_ref, dk_s, dv_s = refs
        j, i = pl.program_id(1), pl.program_id(2)

        @pl.when((j == 0) & (i == 0))
        def _():
            dq_ref[...] = jnp.zeros_like(dq_ref)

        @pl.when(i == 0)
        def _():
            dk_s[...] = jnp.zeros_like(dk_s)
            dv_s[...] = jnp.zeros_like(dv_s)

        def step(k, v, n):
            qb, dob = q_ref[0], do_ref[0]
            p = jnp.exp2(_dot(qb, k, _DN_NT) - lse_ref[0])
            dv_s[0:n, :] += _dot(p, dob, _DN_TN)
            dol = dob * LN2
            ds = p * (_dot(dol, v, _DN_NT) - jnp.sum(dol * o_ref[0], axis=-1, keepdims=True))
            dk_s[0:n, :] += _dot(ds, qb, _DN_TN)
            r0 = pl.multiple_of(i * tq, tq)
            dq_ref[0, pl.ds(r0, tq), :] += _dot(ds, k, _DN_NN)

        if has2:
            @pl.when(j == 0)
            def _():
                step(k2_ref[0], v2_ref[0], C)

            @pl.when(j > 0)
            def _():
                step(k1_ref[0], v1_ref[0], tk)

            @pl.when((i == nq - 1) & (j == 0))
            def _():
                dk2_ref[0] = dk_s[0:C, :]
                dv2_ref[0] = dv_s[0:C, :]

            @pl.when((i == nq - 1) & (j > 0))
            def _():
                dk1_ref[0] = dk_s[0:tk, :]
                dv1_ref[0] = dv_s[0:tk, :]
        else:
            step(k1_ref[0], v1_ref[0], tk)

            @pl.when(i == nq - 1)
            def _():
                dk1_ref[0] = dk_s[...]
                dv1_ref[0] = dv_s[...]

    qs = pl.BlockSpec((1, tq, dk), lambda h, j, i: (h, i, 0))
    os_ = pl.BlockSpec((1, tq, dv), lambda h, j, i: (h, i, 0))
    ls = pl.BlockSpec((1, tq, 1), lambda h, j, i: (h, i, 0))
    k1s = pl.BlockSpec((1, tk, dk), lambda h, j, i: (h, jnp.maximum(j - off, 0), 0))
    v1s = pl.BlockSpec((1, tk, dv), lambda h, j, i: (h, jnp.maximum(j - off, 0), 0))
    in_specs, args = [qs, k1s, v1s], [q, k1, v1]
    out_specs = [pl.BlockSpec((1, Tq, dk), lambda h, j, i: (h, 0, 0)), k1s, v1s]
    out_shape = [jax.ShapeDtypeStruct((H, Tq, dk), F32), jax.ShapeDtypeStruct((H, T1, dk), F32),
                 jax.ShapeDtypeStruct((H, T1, dv), F32)]
    if has2:
        k2s = pl.BlockSpec((1, C, dk), lambda h, j, i: (h, 0, 0))
        v2s = pl.BlockSpec((1, C, dv), lambda h, j, i: (h, 0, 0))
        in_specs += [k2s, v2s]
        args += [k2, v2]
        out_specs += [k2s, v2s]
        out_shape += [jax.ShapeDtypeStruct((H, C, dk), F32), jax.ShapeDtypeStruct((H, C, dv), F32)]
    in_specs += [os_, ls, os_]
    args += [o, lse, do]
    return pl.pallas_call(
        body, name=name, grid=(H, nkv, nq), in_specs=in_specs, out_specs=out_specs, out_shape=out_shape,
        scratch_shapes=[pltpu.VMEM((rows, dk), F32), pltpu.VMEM((rows, dv), F32)],
        compiler_params=_params(("parallel", "arbitrary", "arbitrary")))(*args)


@functools.partial(jax.custom_vjp, nondiff_argnums=(5,))
def flash2(q, k1, v1, k2, v2, name):
    return _flash_fwd_call(q, k1, v1, k2, v2, name)[0]


def _flash2_fwd(q, k1, v1, k2, v2, name):
    o, lse = _flash_fwd_call(q, k1, v1, k2, v2, name)
    return o, (q, k1, v1, k2, v2, o, lse)


def _flash2_bwd(name, res, do):
    q, k1, v1, k2, v2, o, lse = res
    return tuple(_flash_bwd_call(q, k1, v1, k2, v2, o, lse, do, name + "_bwd"))


flash2.defvjp(_flash2_fwd, _flash2_bwd)


@functools.partial(jax.custom_vjp, nondiff_argnums=(3,))
def flash1(q, k, v, name):
    return _flash_fwd_call(q, k, v, None, None, name)[0]


def _flash1_fwd(q, k, v, name):
    o, lse = _flash_fwd_call(q, k, v, None, None, name)
    return o, (q, k, v, o, lse)


def _flash1_bwd(name, res, do):
    q, k, v, o, lse = res
    return tuple(_flash_bwd_call(q, k, v, None, None, o, lse, do, name + "_bwd"))


flash1.defvjp(_flash1_fwd, _flash1_bwd)


def _gqa_call(q, k, v, sink_rows, n_ctx, res, name):
    KV, G, M, d = q.shape
    B, C = BLOCK, n_ctx
    assert WINDOW == BLOCK
    nt, nct, R = M // B, n_ctx // B, G * B
    scale = d ** -0.5
    bwd = res is not None

    def body(*refs):
        q_ref, kc_ref, vc_ref, k0, k1, k2, v0, v1, v2, sink_ref, o_ref, lse_ref = refs[:12]
        i = pl.program_id(1)
        qb = q_ref[0].reshape(R, d)
        a = lax.broadcasted_iota(jnp.int32, (R, B), 0) % B
        b = lax.broadcasted_iota(jnp.int32, (R, B), 1)
        kb, vb, masks = (k0[0], k1[0], k2[0]), (v0[0], v1[0], v2[0]), []
        for r in (-1, 0, 1):
            in_range = (i >= nct) & (i + r >= nct) & (i + r <= nt - 1)
            masks.append(in_range & (a <= b) if r == -1 else (in_range & (a >= b) if r == 1 else in_range & (a >= 0)))
        kc, vc, sink = kc_ref[0], vc_ref[0], sink_ref[0]
        s_c = _dot(qb, kc, _DN_NT) * scale
        s_b = [jnp.where(masks[t], _dot(qb, kb[t], _DN_NT) * scale, NEG_INF) for t in range(3)]
        if not bwd:
            m = jnp.maximum(sink, jnp.max(s_c, axis=-1, keepdims=True))
            for s in s_b:
                m = jnp.maximum(m, jnp.max(s, axis=-1, keepdims=True))
            e_c, e_b = jnp.exp(s_c - m), [jnp.exp(s - m) for s in s_b]
            den = jnp.exp(sink - m) + jnp.sum(e_c, axis=-1, keepdims=True)
            for e in e_b:
                den = den + jnp.sum(e, axis=-1, keepdims=True)
            inv = 1.0 / den
            o = _dot(e_c * inv, vc, _DN_NN)
            for t in range(3):
                o = o + _dot(e_b[t] * inv, vb[t], _DN_NN)
            o_ref[0] = o.reshape(G, B, d)
            lse_ref[0] = (m + jnp.log(den)).reshape(G, B, 1)
            return
        do_ref, dq_ref, dkc_ref, dvc_ref, dkb_ref, dvb_ref, dsink_ref = refs[12:]

        @pl.when(i == 0)
        def _():
            dkc_ref[...] = jnp.zeros_like(dkc_ref)
            dvc_ref[...] = jnp.zeros_like(dvc_ref)
            dsink_ref[...] = jnp.zeros_like(dsink_ref)

        lse, dob = lse_ref[0].reshape(R, 1), do_ref[0].reshape(R, d)
        delta = jnp.sum(dob * o_ref[0].reshape(R, d), axis=-1, keepdims=True)
        p_c = jnp.exp(s_c - lse)
        ds_c = p_c * (_dot(dob, vc, _DN_NT) - delta) * scale
        dq = _dot(ds_c, kc, _DN_NN)
        dkc_ref[0] += _dot(ds_c, qb, _DN_TN)
        dvc_ref[0] += _dot(p_c, dob, _DN_TN)
        for t in range(3):
            p = jnp.exp(s_b[t] - lse)
            ds = p * (_dot(dob, vb[t], _DN_NT) - delta) * scale
            dq = dq + _dot(ds, kb[t], _DN_NN)
            dkb_ref[0, 0, t] = _dot(ds, qb, _DN_TN)
            dvb_ref[0, 0, t] = _dot(p, dob, _DN_TN)
        dq_ref[0] = dq.reshape(G, B, d)
        dsink_ref[0] -= jnp.exp(sink - lse) * delta

    def band(r):
        return lambda h, i: (h, jnp.clip(i + r, nct, nt - 1), 0)

    qs = pl.BlockSpec((1, G, B, d), lambda h, i: (h, 0, i, 0))
    ls = pl.BlockSpec((1, G, B, 1), lambda h, i: (h, 0, i, 0))
    cs = pl.BlockSpec((1, C, d), lambda h, i: (h, 0, 0))
    ss = pl.BlockSpec((1, R, 1), lambda h, i: (h, 0, 0))
    bs = [pl.BlockSpec((1, B, d), band(r)) for r in (-1, 0, 1)]
    in_specs = [qs, cs, cs] + bs + bs + [ss]
    args = [q, k, v, k, k, k, v, v, v, sink_rows]
    if not bwd:
        return pl.pallas_call(body, name=name, grid=(KV, nt), in_specs=in_specs, out_specs=[qs, ls],
                              out_shape=[jax.ShapeDtypeStruct((KV, G, M, d), F32), jax.ShapeDtypeStruct((KV, G, M, 1), F32)],
                              compiler_params=_params(("parallel", "parallel")))(*args)
    o, lse, do = res
    part = pl.BlockSpec((1, 1, 3, B, d), lambda h, i: (h, i, 0, 0, 0))
    part_shape = jax.ShapeDtypeStruct((KV, nt, 3, B, d), F32)
    return pl.pallas_call(body, name=name, grid=(KV, nt), in_specs=in_specs + [qs, ls, qs],
                          out_specs=[qs, cs, cs, part, part, ss],
                          out_shape=[jax.ShapeDtypeStruct((KV, G, M, d), F32), jax.ShapeDtypeStruct((KV, C, d), F32),
                                     jax.ShapeDtypeStruct((KV, C, d), F32), part_shape, part_shape,
                                     jax.ShapeDtypeStruct((KV, R, 1), F32)],
                          compiler_params=_params(("parallel", "arbitrary")))(*args, o, lse, do)


@functools.partial(jax.custom_vjp, nondiff_argnums=(4, 5))
def gqa_core(q, k, v, sink_rows, n_ctx, name):
    return _gqa_call(q, k, v, sink_rows, n_ctx, None, name)[0]


def _gqa_core_fwd(q, k, v, sink_rows, n_ctx, name):
    o, lse = _gqa_call(q, k, v, sink_rows, n_ctx, None, name)
    return o, (q, k, v, sink_rows, o, lse)


def _gqa_core_bwd(n_ctx, name, res, do):
    q, k, v, sink_rows, o, lse = res
    dq, dkc, dvc, dkb, dvb, dsink = _gqa_call(q, k, v, sink_rows, n_ctx, (o, lse, do), name + "_bwd")

    def keys(ctx_part, band_part):
        zero = jnp.zeros_like(band_part[:, :1, 0])
        blocks = (jnp.concatenate([band_part[:, 1:, 0], zero], axis=1) + band_part[:, :, 1]
                  + jnp.concatenate([zero, band_part[:, :-1, 2]], axis=1))
        rows = blocks.reshape(k.shape)
        return jnp.concatenate([rows[:, :n_ctx] + ctx_part, rows[:, n_ctx:]], axis=1)

    return dq, keys(dkc, dkb), keys(dvc, dvb), dsink


gqa_core.defvjp(_gqa_core_fwd, _gqa_core_bwd)


def gqa_attention(gq, gk, gv, sink, n_ctx, name):
    M, H, d = gq.shape
    G = H // GQA_KV_HEADS
    q4 = gq.reshape(M, GQA_KV_HEADS, G, d).transpose(1, 2, 0, 3)
    sink_rows = jnp.repeat(sink.reshape(GQA_KV_HEADS, G), BLOCK, axis=1)[..., None]
    o = gqa_core(q4, gk.transpose(1, 0, 2), gv.transpose(1, 0, 2), sink_rows, n_ctx, name)
    return o.transpose(2, 0, 1, 3).reshape(M, H * d)


def _scan_call(a, x, s, *, rev, adj, n_ctx, name):
    M, W = x.shape
    cw = SSM_CW
    J = W // (2 * cw)
    L = _row_tile(n_ctx, M)
    nt, nc = M // L, n_ctx // L
    asc = rev == adj
    sub = min(SCAN_SUB, L)
    nsub = L // sub
    n_steps = int(math.log2(sub))
    assert 1 << n_steps == sub

    def tile(t):
        if not rev:
            return nt - 1 - t if adj else t
        if not adj:
            return jnp.where(t < nc, nc - 1 - t, nt - 1 - (t - nc))
        return jnp.where(t < nt - nc, nc + t, t - (nt - nc))

    def body(*refs):
        if adj:
            a_ref, x_ref, s_ref, o_ref, da_ref, car_ref = refs
        else:
            a_ref, x_ref, o_ref, car_ref = refs
        t = pl.program_id(1)

        @pl.when(t == 0)
        def _():
            car_ref[...] = jnp.zeros_like(car_ref)
            if adj:
                da_ref[...] = jnp.zeros_like(da_ref)

        ar, ai = a_ref[:, 0:cw], a_ref[:, cw:2 * cw]
        powers, pr, pi = [], ar, ai
        for _ in range(n_steps):
            powers.append((pr, pi))
            pr, pi = pr * pr - pi * pi, 2.0 * pr * pi
        row = lax.broadcasted_iota(jnp.int32, (sub, cw), 0)
        first, last = (0, sub - 1) if asc else (sub - 1, 0)

        def scan_rows(i, carry):
            cr, ci = carry[0], carry[1]
            r0 = pl.multiple_of((i if asc else nsub - 1 - i) * sub, sub)
            xr, xi = x_ref[pl.ds(r0, sub), 0:cw], x_ref[pl.ds(r0, sub), cw:2 * cw]
            xr = xr + jnp.where(row == first, ar * cr - ai * ci, 0.0)
            xi = xi + jnp.where(row == first, ar * ci + ai * cr, 0.0)
            k = 1
            for pr, pi in powers:
                if asc:
                    sr, si, keep = pltpu.roll(xr, k, 0), pltpu.roll(xi, k, 0), row >= k
                else:
                    sr, si, keep = pltpu.roll(xr, sub - k, 0), pltpu.roll(xi, sub - k, 0), row < sub - k
                sr, si = jnp.where(keep, sr, 0.0), jnp.where(keep, si, 0.0)
                xr, xi = xr + pr * sr - pi * si, xi + pr * si + pi * sr
                k *= 2
            o_ref[pl.ds(r0, sub), 0:cw] = xr
            o_ref[pl.ds(r0, sub), cw:2 * cw] = xi
            out = (jnp.sum(jnp.where(row == last, xr, 0.0), axis=0, keepdims=True),
                   jnp.sum(jnp.where(row == last, xi, 0.0), axis=0, keepdims=True))
            if adj:
                if asc:
                    gr, gi = pltpu.roll(xr, 1, 0), pltpu.roll(xi, 1, 0)
                else:
                    gr, gi = pltpu.roll(xr, sub - 1, 0), pltpu.roll(xi, sub - 1, 0)
                gr, gi = jnp.where(row == first, cr, gr), jnp.where(row == first, ci, gi)
                sr, si = s_ref[pl.ds(r0, sub), 0:cw], s_ref[pl.ds(r0, sub), cw:2 * cw]
                out += (carry[2] + jnp.sum(sr * gr + si * gi, axis=0, keepdims=True),
                        carry[3] + jnp.sum(sr * gi - si * gr, axis=0, keepdims=True))
            return out

        init = (car_ref[:, 0:cw], car_ref[:, cw:2 * cw])
        if adj:
            init += (jnp.zeros((1, cw), F32), jnp.zeros((1, cw), F32))
        done = lax.fori_loop(0, nsub, scan_rows, init)
        car_ref[:, 0:cw] = done[0]
        car_ref[:, cw:2 * cw] = done[1]
        if adj:
            da_ref[:, 0:cw] += done[2]
            da_ref[:, cw:2 * cw] += done[3]

    blk = pl.BlockSpec((L, 2 * cw), lambda j, t: (tile(t), j))
    vec = pl.BlockSpec((1, 2 * cw), lambda j, t: (0, j))
    if adj:
        in_specs, args = [vec, blk, blk], (a, x, s)
        out_specs = [blk, vec]
        out_shape = [jax.ShapeDtypeStruct((M, W), F32), jax.ShapeDtypeStruct((1, W), F32)]
    else:
        in_specs, args = [vec, blk], (a, x)
        out_specs = blk
        out_shape = jax.ShapeDtypeStruct((M, W), F32)
    return pl.pallas_call(body, name=name, grid=(J, nt), in_specs=in_specs, out_specs=out_specs, out_shape=out_shape,
                          scratch_shapes=[pltpu.VMEM((1, 2 * cw), F32)],
                          compiler_params=_params(("parallel", "arbitrary")))(*args)


def _conj_layout(a):
    cw = SSM_CW
    J = a.shape[1] // (2 * cw)
    a4 = a.reshape(1, J, 2, cw)
    return jnp.concatenate([a4[:, :, 0:1], -a4[:, :, 1:2]], axis=2).reshape(a.shape)


@functools.partial(jax.custom_vjp, nondiff_argnums=(2, 3, 4))
def diag_scan(a, x, rev, n_ctx, name):
    return _scan_call(a, x, None, rev=rev, adj=False, n_ctx=n_ctx, name=name)


def _diag_scan_fwd(a, x, rev, n_ctx, name):
    s = _scan_call(a, x, None, rev=rev, adj=False, n_ctx=n_ctx, name=name)
    return s, (a, s)


def _diag_scan_bwd(rev, n_ctx, name, res, ds):
    a, s = res
    g, da = _scan_call(_conj_layout(a), ds, s, rev=rev, adj=True, n_ctx=n_ctx, name=name + "_adj")
    return da, g


diag_scan.defvjp(_diag_scan_fwd, _diag_scan_bwd)


def _loss_call(h, g, target, name):
    M, D = h.shape
    tm = _pick(M, (256, 128, 64, 32, 16, 8))

    def body(h_ref, g_ref, t_ref, loss_ref, dh_ref, dg_ref):
        i = pl.program_id(0)

        @pl.when(i == 0)
        def _():
            loss_ref[...] = jnp.zeros_like(loss_ref)
            dg_ref[...] = jnp.zeros_like(dg_ref)

        xv, gv = h_ref[...], g_ref[...]
        r = lax.rsqrt(jnp.mean(xv * xv, axis=-1, keepdims=True) + EPS)
        xhat = xv * r
        err = xhat * gv - t_ref[...]
        loss_ref[...] += 0.5 * jnp.sum(jnp.mean(err * err, axis=-1, keepdims=True), axis=0, keepdims=True)
        dy = err * (1.0 / D)
        dg_ref[...] += jnp.sum(dy * xhat, axis=0, keepdims=True)
        dxhat = dy * gv
        dh_ref[...] = r * (dxhat - xhat * jnp.mean(dxhat * xhat, axis=-1, keepdims=True))

    row = pl.BlockSpec((tm, D), lambda i: (i, 0))
    vec = pl.BlockSpec((1, D), lambda i: (0, 0))
    one = pl.BlockSpec((1, 1), lambda i: (0, 0))
    return pl.pallas_call(body, name=name, grid=(M // tm,), in_specs=[row, vec, row], out_specs=[one, row, vec],
                          out_shape=[jax.ShapeDtypeStruct((1, 1), F32), jax.ShapeDtypeStruct((M, D), F32),
                                     jax.ShapeDtypeStruct((1, D), F32)],
                          compiler_params=_params(("arbitrary",)))(h, g, target)


@functools.partial(jax.custom_vjp, nondiff_argnums=(3,))
def loss_head(h, g, target, name):
    return _loss_call(h, g, target, name)[0][0, 0]


def _loss_head_fwd(h, g, target, name):
    loss, dh, dg = _loss_call(h, g, target, name)
    return loss[0, 0], (dh, dg, target)


def _loss_head_bwd(name, res, ct):
    dh, dg, target = res
    return ct * dh, ct * dg, jnp.zeros_like(target)


loss_head.defvjp(_loss_head_fwd, _loss_head_bwd)


def _adamw_call(w, gstack, m, v, name):
    R, Cn = w.shape
    n = gstack.shape[0]
    tr = _pick(R, (64, 32, 16, 8))

    def body(w_ref, g_ref, m_ref, v_ref, go_ref, d_ref, mo_ref, vo_ref):
        g = g_ref[0]
        for s in range(1, n):
            g = g + g_ref[s]
        mn = ADAM_B1 * m_ref[...] + (1.0 - ADAM_B1) * g
        vn = ADAM_B2 * v_ref[...] + (1.0 - ADAM_B2) * (g * g)
        m_hat = mn / (1.0 - ADAM_B1 ** ADAM_STEP)
        v_hat = vn / (1.0 - ADAM_B2 ** ADAM_STEP)
        go_ref[...] = g
        d_ref[...] = -ADAM_LR * (m_hat / (jnp.sqrt(v_hat) + ADAM_EPS) + ADAM_WD * w_ref[...])
        mo_ref[...] = mn
        vo_ref[...] = vn

    blk = pl.BlockSpec((tr, Cn), lambda i: (i, 0))
    gblk = pl.BlockSpec((n, tr, Cn), lambda i: (0, i, 0))
    sds = jax.ShapeDtypeStruct((R, Cn), F32)
    return pl.pallas_call(body, name=name, grid=(R // tr,), in_specs=[blk, gblk, blk, blk], out_specs=[blk] * 4,
                          out_shape=[sds] * 4, compiler_params=_params(("parallel",)))(w, gstack, m, v)


MESH = pl.DeviceIdType.MESH
ANY = pl.BlockSpec(memory_space=pl.ANY)


def _place():
    return lax.axis_index("x"), lax.axis_index("y"), lax.axis_index("c")


def _sibling_exchange(v, name):
    n = v.shape[0]

    def body(v_ref, o_ref, send_sems, recv_sems):
        x, y, c = _place()
        copies = [pltpu.make_async_remote_copy(src_ref=v_ref.at[k], dst_ref=o_ref.at[k], send_sem=send_sems.at[k],
                                               recv_sem=recv_sems.at[k], device_id=(x, y, 1 - c), device_id_type=MESH)
                  for k in range(n)]
        for cp in copies:
            cp.start()
        for cp in copies:
            cp.wait()

    return pl.pallas_call(body, name=name, in_specs=[ANY], out_specs=ANY, out_shape=jax.ShapeDtypeStruct(v.shape, v.dtype),
                          scratch_shapes=[pltpu.SemaphoreType.DMA((n,)), pltpu.SemaphoreType.DMA((n,))])(v)


def _sibling_exchange_half(v, name):
    n = v.shape[0]

    def body(v_ref, o_ref, send_sems, recv_sems):
        x, y, c = _place()
        copies = [pltpu.make_async_remote_copy(src_ref=v_ref.at[k, pl.ds(1 - c, 1)], dst_ref=o_ref.at[k],
                                               send_sem=send_sems.at[k], recv_sem=recv_sems.at[k],
                                               device_id=(x, y, 1 - c), device_id_type=MESH)
                  for k in range(n)]
        for cp in copies:
            cp.start()
        for cp in copies:
            cp.wait()

    return pl.pallas_call(body, name=name, in_specs=[ANY], out_specs=ANY,
                          out_shape=jax.ShapeDtypeStruct((n, 1) + v.shape[2:], v.dtype),
                          scratch_shapes=[pltpu.SemaphoreType.DMA((n,)), pltpu.SemaphoreType.DMA((n,))])(v)


def _chip_exchange(v, same, name):
    out_shape = (N_CHIPS,) + (v.shape if same else v.shape[1:])

    def body(v_ref, o_ref, send_sems, recv_sems, local_sem):
        x, y, c = _place()
        me = 2 * x + y
        own = pltpu.make_async_copy(v_ref if same else v_ref.at[me], o_ref.at[me], local_sem)
        own.start()
        copies = []
        for k, (fx, fy) in enumerate(((1, 0), (0, 1), (1, 1))):
            px, py = jnp.where(fx == 1, 1 - x, x), jnp.where(fy == 1, 1 - y, y)
            src = v_ref if same else v_ref.at[2 * px + py]
            copies.append(pltpu.make_async_remote_copy(src_ref=src, dst_ref=o_ref.at[me], send_sem=send_sems.at[k],
                                                       recv_sem=recv_sems.at[k], device_id=(px, py, c), device_id_type=MESH))
        for cp in copies:
            cp.start()
        for cp in copies:
            cp.wait()
        own.wait()

    return pl.pallas_call(body, name=name, in_specs=[ANY], out_specs=ANY, out_shape=jax.ShapeDtypeStruct(out_shape, v.dtype),
                          scratch_shapes=[pltpu.SemaphoreType.DMA((3,)), pltpu.SemaphoreType.DMA((3,)), pltpu.SemaphoreType.DMA])(v)


def _all_gather(v, name):
    def body(v_ref, o_ref, send_sems, recv_sems, local_sem):
        x, y, c = _place()
        me = 4 * x + 2 * y + c
        own = pltpu.make_async_copy(v_ref, o_ref.at[me], local_sem)
        own.start()
        copies = []
        for k in range(1, 8):
            fx, fy, fc = (k >> 2) & 1, (k >> 1) & 1, k & 1
            peer = (jnp.where(fx == 1, 1 - x, x), jnp.where(fy == 1, 1 - y, y), jnp.where(fc == 1, 1 - c, c))
            copies.append(pltpu.make_async_remote_copy(src_ref=v_ref, dst_ref=o_ref.at[me], send_sem=send_sems.at[k - 1],
                                                       recv_sem=recv_sems.at[k - 1], device_id=peer, device_id_type=MESH))
        for cp in copies:
            cp.start()
        for cp in copies:
            cp.wait()
        own.wait()

    return pl.pallas_call(body, name=name, in_specs=[ANY], out_specs=ANY,
                          out_shape=jax.ShapeDtypeStruct((8,) + v.shape, v.dtype),
                          scratch_shapes=[pltpu.SemaphoreType.DMA((7,)), pltpu.SemaphoreType.DMA((7,)), pltpu.SemaphoreType.DMA])(v)


def _add_own_half(g, r, c, name):
    n, _, R, W = g.shape
    tr = _pick(R, (256, 128, 64, 32, 16, 8))

    def body(c_ref, g_ref, r_ref, o_ref):
        o_ref[...] = g_ref[0] + r_ref[0]

    grid_spec = pltpu.PrefetchScalarGridSpec(
        num_scalar_prefetch=1, grid=(n, R // tr),
        in_specs=[pl.BlockSpec((1, 1, tr, W), lambda p, i, c_ref: (p, c_ref[0], i, 0)),
                  pl.BlockSpec((1, 1, tr, W), lambda p, i, c_ref: (p, 0, i, 0))],
        out_specs=pl.BlockSpec((1, tr, W), lambda p, i, c_ref: (p, i, 0)))
    return pl.pallas_call(body, name=name, grid_spec=grid_spec, out_shape=jax.ShapeDtypeStruct((n, R, W), F32),
                          compiler_params=_params(("parallel", "parallel")))(c.reshape(1).astype(jnp.int32), g, r)


def _sum_stack(v, name):
    n, R, W = v.shape
    tr = _pick(R, (256, 128, 64, 32, 16, 8))

    def body(v_ref, o_ref):
        acc = v_ref[0]
        for s in range(1, n):
            acc = acc + v_ref[s]
        o_ref[...] = acc

    return pl.pallas_call(body, name=name, grid=(R // tr,), in_specs=[pl.BlockSpec((n, tr, W), lambda i: (0, i, 0))],
                          out_specs=pl.BlockSpec((tr, W), lambda i: (i, 0)), out_shape=jax.ShapeDtypeStruct((R, W), F32),
                          compiler_params=_params(("parallel",)))(v)


def _flat_rows(n, mult):
    rows = -(-n // FLAT_W)
    return -(-rows // mult) * mult


def _by_core(a, b, c):
    return jnp.where(c == 0, a, b)


def _lane_padded(n):
    return -(-n // LANES) * LANES


def _lane_pad(a):
    pad = _lane_padded(a.shape[-1]) - a.shape[-1]
    return jnp.pad(a, [(0, 0)] * (a.ndim - 1) + [(0, pad)]) if pad else a


def gather_weights(shards):
    _, _, c = _place()
    flat = jnp.concatenate([_lane_pad(shards[n].astype(WIRE_DTYPE)).reshape(-1) for n in BIG_NAMES])
    n_flat = flat.shape[0]
    rh = _flat_rows(n_flat, 32) // 2
    flat = jnp.pad(flat, (0, 2 * rh * FLAT_W - n_flat)).reshape(2, rh, FLAT_W)
    mine = lax.dynamic_index_in_dim(flat, c, axis=0, keepdims=False)
    got = _chip_exchange(mine, True, "gather_chips")
    other = _sibling_exchange(got, "gather_sibling")
    halves = jnp.stack([_by_core(got, other, c), _by_core(other, got, c)], axis=1)
    allflat = halves.reshape(N_CHIPS, 2 * rh * FLAT_W)
    out, off = {}, 0
    for name, axis in BIG:
        shp = shards[name].shape
        padded = shp[:-1] + (_lane_padded(shp[-1]),)
        size = math.prod(padded)
        parts = allflat[:, off:off + size].reshape((N_CHIPS,) + padded)[..., :shp[-1]]
        out[name] = [jnp.concatenate([parts[p, l] for p in range(N_CHIPS)], axis=axis - 1) for l in range(shp[0])]
        off += size
    return out


def scatter_gradients(grads, shards):
    _, _, c = _place()
    cols = []
    for name, axis in SCATTERED:
        parts = [jnp.split(g, N_CHIPS, axis=axis - 1) for g in grads[name]]
        cols.append(jnp.stack([_lane_pad(jnp.stack([per_layer[p] for per_layer in parts])).reshape(-1) for p in range(N_CHIPS)]))
    flat = jnp.concatenate(cols, axis=1)
    n_flat = flat.shape[1]
    rh = _flat_rows(n_flat, 16) // 2
    flat = jnp.pad(flat, ((0, 0), (0, 2 * rh * FLAT_W - n_flat))).reshape(N_CHIPS, 2, rh, FLAT_W)
    theirs = _sibling_exchange_half(flat, "scatter_sibling")
    pair = _add_own_half(flat, theirs, c, "scatter_pair_sum")
    got = _chip_exchange(pair, False, "scatter_chips")
    mine = _sum_stack(got, "scatter_chip_sum")
    other = _sibling_exchange(mine.reshape(1, rh, FLAT_W), "scatter_halves").reshape(rh, FLAT_W)
    full = jnp.stack([_by_core(mine, other, c), _by_core(other, mine, c)]).reshape(-1)
    out, off = {}, 0
    for name, _ in SCATTERED:
        shp = shards[name].shape
        padded = shp[:-1] + (_lane_padded(shp[-1]),)
        size = math.prod(padded)
        out[name] = full[off:off + size].reshape(padded)[..., :shp[-1]]
        off += size
    return out


def _rope_tables(n_ctx, n_lat, n_heads, head_dim, start, n):
    t = jnp.arange(n_lat, dtype=jnp.int32)
    zero = jnp.zeros((n_ctx,), jnp.int32)
    row = jnp.concatenate([zero, t // GRID_W]).astype(F32)
    col = jnp.concatenate([zero, t % GRID_W]).astype(F32)
    half, q = n // 2, n // 4
    j = jnp.arange(n_heads * head_dim, dtype=jnp.int32) % head_dim - start
    inside = (j >= 0) & (j < n)
    u = j % half
    inv_cols = jnp.where(inside, ROPE_BASE ** (-(2 * (u % q)).astype(F32) / half), 0.0)
    ang = (row[:, None] * jnp.where(j // half == 0, inv_cols, 0.0)[None, :]
           + col[:, None] * jnp.where(j // half == 1, inv_cols, 0.0)[None, :])
    first = (inside & (u // q == 0)).astype(F32)[None, :]
    second = (inside & (u // q == 1)).astype(F32)[None, :]
    return jnp.cos(ang), -jnp.sin(ang) * first, jnp.sin(ang) * second, q


def _rope(x, table):
    cos, minus_sin, plus_sin, q = table
    return x * cos + jnp.roll(x, -q, axis=1) * minus_sin + jnp.roll(x, q, axis=1) * plus_sin


def _cmul(ar, ai, br, bi):
    return ar * br - ai * bi, ar * bi + ai * br


def _ssm_discretize(lam_re, lam_im, log_dt, b_re, b_im):
    dt = jnp.exp(log_dt)[:, None]
    mag = jnp.exp(lam_re * dt)
    a_re, a_im = mag * jnp.cos(lam_im * dt), mag * jnp.sin(lam_im * dt)
    den = lam_re * lam_re + lam_im * lam_im
    w_re = ((a_re - 1) * lam_re + a_im * lam_im) / den
    w_im = (a_im * lam_re - (a_re - 1) * lam_im) / den
    bb_re, bb_im = _cmul(w_re[..., None], w_im[..., None], b_re, b_im)
    return a_re, a_im, bb_re, bb_im


def _ssm_layouts(a_re, a_im, bb_re, bb_im, c_re, c_im):
    J, g8, P, Mg = SSM_CHUNKS, SSM_CHUNK_GROUPS, SSM_STATE, SSM_GROUP
    eye = jnp.eye(g8, dtype=F32)
    a = jnp.stack([a_re.reshape(J, g8 * P), a_im.reshape(J, g8 * P)], axis=1).reshape(1, J * 2 * g8 * P)
    bb = jnp.stack([bb_re, bb_im]).reshape(2, J, g8, P, Mg)
    w_drive = jnp.einsum('rjgpm,gh->jgmrhp', bb, eye).reshape(J * g8 * Mg, 2 * g8 * P)
    cc = jnp.stack([c_re, -c_im]).reshape(2, J, g8, Mg, P)
    w_read = jnp.einsum('rjgmp,gh->jrhpgm', cc, eye).reshape(J * 2 * g8 * P, g8 * Mg)
    return a, w_drive, w_read


def _w_in_layout(d_model):
    widths = (("cq", MLA_Q_RANK), ("ckv", MLA_KV_RANK), ("kr", MLA_ROPE), ("u", SSM_WIDTH), ("gq", GQA_HEADS * GQA_HEAD_DIM),
              ("gk", GQA_KV_HEADS * GQA_HEAD_DIM), ("gv", GQA_KV_HEADS * GQA_HEAD_DIM), ("gates", N_BRANCH * d_model))
    out, src, dst = [], 0, 0
    for name, w in widths:
        out.append((name, src, dst, w))
        src += w
        dst += -(-w // LANES) * LANES
    return out, src, dst


def _pad_w_in(w, d_model):
    lay, _, _ = _w_in_layout(d_model)
    parts = []
    for _, src, _, wd in lay:
        seg = w[..., src:src + wd]
        pad = -(-wd // LANES) * LANES - wd
        parts.append(jnp.pad(seg, [(0, 0)] * (w.ndim - 1) + [(0, pad)]) if pad else seg)
    return jnp.concatenate(parts, axis=-1)


def _unpad_w_in(w, d_model):
    lay, _, _ = _w_in_layout(d_model)
    return jnp.concatenate([w[..., dst:dst + wd] for _, _, dst, wd in lay], axis=-1)


@functools.partial(jax.custom_vjp, nondiff_argnums=(1,))
def split_cols(proj, bounds):
    return tuple(proj[:, s:s + w] for s, w in bounds[0])


def _split_cols_fwd(proj, bounds):
    return split_cols(proj, bounds), None


def _split_cols_bwd(bounds, _, cts):
    segments, total = bounds
    rows, pieces, pos = cts[0].shape[0], [], 0
    for (s, w), ct in zip(segments, cts):
        if s > pos:
            pieces.append(jnp.zeros((rows, s - pos), ct.dtype))
        pieces.append(ct)
        pos = s + w
    if pos < total:
        pieces.append(jnp.zeros((rows, total - pos), cts[0].dtype))
    return (jnp.concatenate(pieces, axis=1),)


split_cols.defvjp(_split_cols_fwd, _split_cols_bwd)


def _layer(hall, lw, lz, sp, cs8, n_ctx, ropes):
    M, D = hall.shape
    C = n_ctx

    def lin(x, name):
        return linear(x, lw[name], lz[name], name)

    mod_all = linear_x(cs8, lw["ada_w"], "ada_w")[0:2] + sp["ada_b"][None, :] + lz["ada_tap"]
    mod = [mod_all[:, i * D:(i + 1) * D] for i in range(N_MOD)]

    def ffn(h, tag, norm_g, sh, sc, gate):
        hn = norm_mod(h, norm_g[None, :], sh, sc, C, tag + "_norm")
        act = swiglu_act(lin(hn, tag + "_w13"), tag + "_act")
        return gated_residual(h, lin(act, tag + "_w2"), gate, 0.5, C, tag + "_res")

    hall = ffn(hall, "ffn1", sp["norm_ffn1"], mod[0], mod[1], mod[2])

    xm = norm_mod(hall, sp["norm_mix"][None, :], mod[3], mod[4], C, "mix_norm")
    proj = lin(xm, "w_in")
    lay, _, total = _w_in_layout(D)
    seg = dict(zip([name for name, _, _, _ in lay], split_cols(proj, (tuple((dst, wd) for _, _, dst, wd in lay), total))))

    q = _rope(lin(rmsnorm(seg["cq"], sp["mla_q_norm"][None, :], "mla_q_norm"), "mla_w_uq"), ropes["mla_q"])
    q = (q * ((MLA_NOPE + MLA_ROPE) ** -0.5 * LOG2E)).reshape(M, MLA_HEADS, MLA_NOPE + MLA_ROPE)
    kv = lin(rmsnorm(seg["ckv"], sp["mla_kv_norm"][None, :], "mla_kv_norm"), "mla_w_ukv").reshape(M, MLA_HEADS, MLA_NOPE + MLA_V)
    kr = _rope(seg["kr"], ropes["mla_kr"])[:, None, :]
    k = jnp.concatenate([kv[..., :MLA_NOPE], jnp.broadcast_to(kr, (M, MLA_HEADS, MLA_ROPE))], axis=-1)

    def heads(t, lo, hi):
        return t[lo:hi].transpose(1, 0, 2)

    v = kv[..., MLA_NOPE:]
    k_ctx, v_ctx = heads(k, 0, C), heads(v, 0, C)
    o_lat = flash2(heads(q, C, M), heads(k, C, M), heads(v, C, M), k_ctx, v_ctx, "mla_lat")
    o_ctx = flash1(heads(q, 0, C), k_ctx, v_ctx, "mla_ctx")
    o = jnp.concatenate([o_ctx, o_lat], axis=1).transpose(1, 0, 2).reshape(M, MLA_HEADS * MLA_V)
    mla = lin(o, "mla_w_o")

    u = seg["u"]
    y = u * sp["ssm_d"][None, :]
    for direction in range(2):
        a_re, a_im, bb_re, bb_im = _ssm_discretize(sp["ssm_lambda_re"][direction], sp["ssm_lambda_im"][direction],
                                                   sp["ssm_log_dt"][direction], sp["ssm_b_re"][direction],
                                                   sp["ssm_b_im"][direction])
        a, w_drive, w_read = _ssm_layouts(a_re, a_im, bb_re, bb_im, sp["ssm_c_re"][direction], sp["ssm_c_im"][direction])
        drive = bd_linear(u, w_drive, SSM_CHUNKS, "ssm_drive%d" % direction)
        states = diag_scan(a, drive, direction == 1, C, "ssm_scan%d" % direction)
        y = y + bd_linear(states, w_read, SSM_CHUNKS, "ssm_read%d" % direction)
    zz = lin(jax.nn.gelu(y), "ssm_w_glu")

    gq = _rope(seg["gq"], ropes["gqa_q"]).reshape(M, GQA_HEADS, GQA_HEAD_DIM)
    gk = _rope(seg["gk"], ropes["gqa_k"]).reshape(M, GQA_KV_HEADS, GQA_HEAD_DIM)
    gv = seg["gv"].reshape(M, GQA_KV_HEADS, GQA_HEAD_DIM)
    gqa = lin(gqa_attention(gq, gk, gv, sp["gqa_sink"], C, "gqa"), "gqa_w_o")

    mixed = gated_mix(seg["gates"], mla, zz, gqa, "mix_gate")
    hall = gated_residual(hall, lin(mixed, "w_out"), mod[5], 1.0, C, "mix_res")

    return ffn(hall, "ffn2", sp["norm_ffn2"], mod[6], mod[7], mod[8])


PER_LAYER_SMALL = tuple(n for n in SMALL if n not in ("c_ctx", "final_norm"))


def _loss_fn(diff, x, c, ctx, target, whole):
    zeros, small, x = diff
    T, D = x.shape
    C = ctx.shape[0]
    ropes = {"mla_q": _rope_tables(C, T, MLA_HEADS, MLA_NOPE + MLA_ROPE, MLA_NOPE, MLA_ROPE),
             "mla_kr": _rope_tables(C, T, 1, MLA_ROPE, 0, MLA_ROPE),
             "gqa_q": _rope_tables(C, T, GQA_HEADS, GQA_HEAD_DIM, 0, GQA_HEAD_DIM),
             "gqa_k": _rope_tables(C, T, GQA_KV_HEADS, GQA_HEAD_DIM, 0, GQA_HEAD_DIM)}
    cs8 = jnp.pad(_cond_rows(small["c_ctx"], c), ((0, 6), (0, 0)))
    hall = jnp.concatenate([ctx, x], axis=0)

    for layer in range(len(zeros["ada_tap"])):
        hall = _layer(hall, {n: whole[n][layer] for n in BIG_NAMES}, {n: zeros[n][layer] for n in zeros},
                      {n: small[n][layer] for n in PER_LAYER_SMALL}, cs8, C, ropes)
    return loss_head(hall[C:], small["final_norm"][None, :], target, "loss_head")


def _cond_rows(c_ctx, c):
    return jax.nn.silu(jnp.stack([c_ctx, c]))


def kernel(*args):
    given = dict(zip(ARG_NAMES + ['loss_target'] + ['m_' + n for n in WEIGHTS] + ['v_' + n for n in WEIGHTS], args))
    x, c, ctx, target = given['x'][0], given['c'][0], given['ctx'][0], given['loss_target'][0]
    px, py, _ = _place()
    D = x.shape[-1]
    depth = given['ada_w'].shape[0]
    shards = {n: given[n] for n in BIG_NAMES}
    small = {n: given[n] for n in SMALL}

    whole = gather_weights(shards)
    whole["w_in"] = [_pad_w_in(w, D) for w in whole["w_in"]]
    zeros = {n: [jnp.zeros(w.shape, F32) for w in whole[n]] for n in SCATTERED_NAMES}
    zeros["ada_tap"] = [jnp.zeros((2, N_MOD * D), F32) for _ in range(depth)]

    loss, (gz, gsmall, gx) = jax.value_and_grad(_loss_fn)((zeros, small, x), x, c, ctx, target, whole)
    loss = lax.psum(loss, ("x", "y", "c"))
    taps = jnp.stack(gz.pop("ada_tap"))
    gz["w_in"] = [_unpad_w_in(g, D) for g in gz["w_in"]]
    gbig = scatter_gradients(gz, shards)

    extra = [taps.reshape(-1), _cond_rows(small["c_ctx"], c).reshape(-1)]
    n_small = sum(math.prod(given[n].shape) for n in SMALL)
    n_extra = sum(e.shape[0] for e in extra)

    def flat_small(d, tail=None):
        v = jnp.concatenate([d[n].reshape(-1) for n in SMALL] + (tail or [jnp.zeros((n_extra,), F32)]))
        rows = _flat_rows(v.shape[0], 8)
        return jnp.pad(v, (0, rows * FLAT_W - v.shape[0])).reshape(rows, FLAT_W)

    gathered = _all_gather(flat_small(gsmall, extra), "small_gather")
    outs = _adamw_call(flat_small(small), gathered, flat_small({n: given['m_' + n] for n in SMALL}),
                       flat_small({n: given['v_' + n] for n in SMALL}), "adamw_small")
    res, off = {}, 0
    for name in SMALL:
        shp = given[name].shape
        size = math.prod(shp)
        res[name] = [o.reshape(-1)[off:off + size].reshape(shp) for o in outs]
        off += size

    tails = gathered.reshape(8, -1)[:, n_small:n_small + n_extra]
    all_taps = tails[:, :taps.size].reshape(8, depth, 2, N_MOD * D)
    all_cs = tails[:, taps.size:].reshape(8 * 2, D)
    n_cols = given['ada_w'].shape[2]
    mine = lax.dynamic_slice_in_dim(all_taps, (2 * px + py) * n_cols, n_cols, axis=3)
    gbig["ada_w"] = jnp.stack([_mm_tn(all_cs, mine[:, layer].reshape(8 * 2, n_cols), "ada_w_dw") for layer in range(depth)])

    for name in BIG_NAMES:
        shp = given[name].shape
        two_d = (shp[0] * shp[1], shp[2])
        outs = _adamw_call(given[name].reshape(two_d), gbig[name].reshape((1,) + two_d), given['m_' + name].reshape(two_d),
                           given['v_' + name].reshape(two_d), "adamw_" + name)
        res[name] = [o.reshape(shp) for o in outs]

    return (loss, gx[None], *[res[n][0] for n in WEIGHTS], *[res[n][1] for n in WEIGHTS],
            *[res[n][2] for n in WEIGHTS], *[res[n][3] for n in WEIGHTS])
```
